```python
import jax, jax.numpy as jnp
from jax import lax
import numpy as np

D_MODEL = 1024
BATCH = 4
SEQ = 4096
DEPTH = 4

N_META = 16
N_MIXERS = 2
RMS_EPS = 1e-6
ROPE_THETA = 10000.0
ATT_HEADS = 16
ATT_KV_HEADS = 4
ATT_HEAD_DIM = D_MODEL // ATT_HEADS
ATT_GROUP = ATT_HEADS // ATT_KV_HEADS
IDX_HEADS = 8
IDX_DIM = 128
TOPK_MAX = 256
Q_BLOCK = 128
DSA_SPLITS = [ATT_HEADS * ATT_HEAD_DIM, ATT_KV_HEADS * ATT_HEAD_DIM, ATT_KV_HEADS * ATT_HEAD_DIM,
              IDX_HEADS * IDX_DIM, IDX_DIM, IDX_HEADS]
DSA_IN = sum(DSA_SPLITS)
M_HEADS = 4
M_QK_DIM = D_MODEL // (2 * M_HEADS)
M_V_DIM = D_MODEL // M_HEADS
M_CHUNK = 64
GATE_CAP = 15.0
MLSTM_SPLITS = [M_HEADS * M_QK_DIM, M_HEADS * M_QK_DIM, M_HEADS * M_V_DIM, M_HEADS * M_V_DIM,
                M_HEADS, M_HEADS]
MLSTM_IN = sum(MLSTM_SPLITS)
D_FF = 7 * D_MODEL // 2
N_EXPERTS = 8
TOP_K_EXPERTS = 2
N_A = (DEPTH + 1) // 2
N_B = DEPTH // 2

kernel_name = "hybrid_dsa_mlstm_moe_trunk"


def rms_norm(x, g):
    xf = x.astype(jnp.float32)
    y = xf * lax.rsqrt(jnp.mean(xf * xf, axis=-1, keepdims=True) + RMS_EPS)
    return (y * g.astype(jnp.float32)).astype(x.dtype)


def rope(x, pos):
    half = x.shape[-1] // 2
    inv = ROPE_THETA ** (-jnp.arange(half, dtype=jnp.float32) / half)
    ang = pos.astype(jnp.float32)[:, None] * inv[None, :]
    cos = jnp.cos(ang)[:, None, :]
    sin = jnp.sin(ang)[:, None, :]
    x1 = x[..., :half].astype(jnp.float32)
    x2 = x[..., half:].astype(jnp.float32)
    return jnp.concatenate([x1 * cos - x2 * sin, x2 * cos + x1 * sin], axis=-1).astype(x.dtype)


def split_cols(a, sizes):
    return jnp.split(a, list(np.cumsum(sizes)[:-1]), axis=-1)


def dsa_mixer(h, w_in, g_q, g_k, w_out, top_k):
    B, L, _ = h.shape
    q, k, v, iq, ik, iw = split_cols(h @ w_in, DSA_SPLITS)
    pos = jnp.arange(L)
    q = rope(rms_norm(q.reshape(B, L, ATT_HEADS, ATT_HEAD_DIM), g_q), pos)
    k = rope(rms_norm(k.reshape(B, L, ATT_KV_HEADS, ATT_HEAD_DIM), g_k), pos)
    v = v.reshape(B, L, ATT_KV_HEADS, ATT_HEAD_DIM)
    iq = rope(iq.reshape(B, L, IDX_HEADS, IDX_DIM), pos)
    ik = rope(ik[:, :, None, :], pos)[:, :, 0]
    iw = iw * (IDX_HEADS ** -0.5 * IDX_DIM ** -0.5)
    n_blocks = -(-L // Q_BLOCK)
    pad = n_blocks * Q_BLOCK - L
    padq = lambda a: jnp.pad(a, [(0, 0), (0, pad)] + [(0, 0)] * (a.ndim - 2))
    qp = padq(q).reshape(B, n_blocks * Q_BLOCK, ATT_KV_HEADS, ATT_GROUP, ATT_HEAD_DIM)
    iqp, iwp = padq(iq), padq(iw)
    key_pos = jnp.arange(L)
    scale = ATT_HEAD_DIM ** -0.5

    def block(bi):
        start = bi * Q_BLOCK
        qb = lax.dynamic_slice_in_dim(qp, start, Q_BLOCK, axis=1)
        iqb = lax.dynamic_slice_in_dim(iqp, start, Q_BLOCK, axis=1)
        iwb = lax.dynamic_slice_in_dim(iwp, start, Q_BLOCK, axis=1)
        qpos = start + jnp.arange(Q_BLOCK)
        causal = key_pos[None, :] <= qpos[:, None]
        s = jnp.einsum('bqhd,bsd->bqhs', iqb, ik, preferred_element_type=jnp.float32)
        score = jnp.einsum('bqhs,bqh->bqs', jax.nn.relu(s), iwb.astype(jnp.float32))
        score = jnp.where(causal[None], score, -jnp.inf)
        _, sel = lax.top_k(score, top_k)
        valid = sel <= qpos[None, :, None]
        k_sel = jax.vmap(lambda kk, ii: kk[ii])(k, sel)
        v_sel = jax.vmap(lambda vv, ii: vv[ii])(v, sel)
        logits = jnp.einsum('bqgrd,bqkgd->bqgrk', qb, k_sel, preferred_element_type=jnp.float32) * scale
        logits = jnp.where(valid[:, :, None, None, :], logits, -jnp.inf)
        p = jax.nn.softmax(logits, axis=-1)
        return jnp.einsum('bqgrk,bqkgd->bqgrd', p.astype(v.dtype), v_sel)

    out = lax.map(block, jnp.arange(n_blocks))
    out = out.transpose(1, 0, 2, 3, 4, 5).reshape(B, n_blocks * Q_BLOCK, D_MODEL)[:, :L]
    return out @ w_out


def mlstm_mixer(h, w_in, b_i, b_f, g_out, w_out):
    B, L, _ = h.shape
    q, k, v, o, ig, fg = split_cols(h @ w_in, MLSTM_SPLITS)
    f32 = jnp.float32
    q = q.reshape(B, L, M_HEADS, M_QK_DIM).astype(f32)
    k = k.reshape(B, L, M_HEADS, M_QK_DIM).astype(f32) * (M_QK_DIM ** -0.5)
    v = v.reshape(B, L, M_HEADS, M_V_DIM).astype(f32)
    log_i = GATE_CAP * jnp.tanh((ig.astype(f32) + b_i.astype(f32)) / GATE_CAP)
    log_f = jax.nn.log_sigmoid(GATE_CAP * jnp.tanh((fg.astype(f32) + b_f.astype(f32)) / GATE_CAP))
    pf = (-N_META) % M_CHUNK
    pb = (-(pf + L)) % M_CHUNK
    Lp = pf + L + pb
    nc = Lp // M_CHUNK
    padt = lambda a, c=0.0: jnp.pad(a, [(0, 0), (pf, pb)] + [(0, 0)] * (a.ndim - 2), constant_values=c)
    to_chunks = lambda a: a.reshape(B, nc, M_CHUNK, M_HEADS, -1).transpose(1, 0, 3, 2, 4)
    gchunks = lambda a: a.reshape(B, nc, M_CHUNK, M_HEADS).transpose(1, 0, 3, 2)
    qc, kc, vc = to_chunks(padt(q)), to_chunks(padt(k)), to_chunks(padt(v))
    lic = gchunks(padt(log_i, -jnp.inf))
    lfc = gchunks(padt(log_f, 0.0))
    tril = jnp.tril(jnp.ones((M_CHUNK, M_CHUNK), dtype=bool))

    def step(carry, xs):
        C_st, n_st, m_st = carry
        qx, kx, vx, li, lf = xs
        b = jnp.cumsum(lf, axis=-1)
        dmat = jnp.where(tril, b[..., :, None] - b[..., None, :] + li[..., None, :], -jnp.inf)
        inter = b + m_st[..., None]
        m_t = jnp.maximum(inter, jnp.max(dmat, axis=-1))
        w_inter = jnp.exp(inter - m_t)
        s = jnp.einsum('bhtd,bhsd->bhts', qx, kx) * jnp.exp(dmat - m_t[..., None])
        num = w_inter[..., None] * jnp.einsum('bhvd,bhtd->bhtv', C_st, qx) + jnp.einsum('bhts,bhsv->bhtv', s, vx)
        den = w_inter * jnp.einsum('bhd,bhtd->bht', n_st, qx) + jnp.sum(s, axis=-1)
        hout = num / jnp.maximum(jnp.abs(den), jnp.exp(-m_t))[..., None]
        b_last = b[..., -1]
        log_w = b_last[..., None] - b + li
        m_new = jnp.maximum(b_last + m_st, jnp.max(log_w, axis=-1))
        decay = jnp.exp(b_last + m_st - m_new)
        wk = jnp.exp(log_w - m_new[..., None])
        C_new = decay[..., None, None] * C_st + jnp.einsum('bhs,bhsv,bhsd->bhvd', wk, vx, kx)
        n_new = decay[..., None] * n_st + jnp.einsum('bhs,bhsd->bhd', wk, kx)
        return (C_new, n_new, m_new), hout

    init = (jnp.zeros((B, M_HEADS, M_V_DIM, M_QK_DIM), f32), jnp.zeros((B, M_HEADS, M_QK_DIM), f32),
            jnp.zeros((B, M_HEADS), f32))
    _, hs = lax.scan(step, init, (qc, kc, vc, lic, lfc))
    hs = hs.transpose(1, 0, 3, 2, 4).reshape(B, Lp, M_HEADS, M_V_DIM)[:, pf:pf + L]
    hs = rms_norm(hs, g_out.reshape(M_HEADS, M_V_DIM)).reshape(B, L, D_MODEL)
    y = (hs * jax.nn.sigmoid(o.astype(f32))).astype(h.dtype)
    return y @ w_out


def swiglu(h, w_gate, w_up, w_down):
    return (jax.nn.silu(h @ w_gate) * (h @ w_up)) @ w_down


def moe_ffn(h, w_router, w_gate, w_up, w_down):
    logits = (h @ w_router).astype(jnp.float32)
    top_val, top_idx = lax.top_k(logits, TOP_K_EXPERTS)
    gates = jax.nn.softmax(top_val, axis=-1)
    combine = jnp.sum(jax.nn.one_hot(top_idx, N_EXPERTS, dtype=jnp.float32) * gates[..., None], axis=-2)
    y = jnp.zeros_like(h)
    for e in range(N_EXPERTS):
        y = y + combine[..., e:e + 1].astype(h.dtype) * swiglu(h, w_gate[e], w_up[e], w_down[e])
    return y


def setup_inputs(seed: int = 0) -> dict:
    key = jax.random.key(seed)
    ks = jax.random.split(key, 24)
    nrm = lambda k, shape, fan_in: jax.random.normal(k, shape, jnp.float32) * (fan_in ** -0.5)
    gain = lambda k, shape: 1.0 + 0.02 * jax.random.normal(k, shape, jnp.float32)
    out_scale = (2.0 * DEPTH) ** -0.5
    b_f = jnp.linspace(3.0, 6.0, M_HEADS, dtype=jnp.float32)[None, :] + 0.1 * jax.random.normal(ks[9], (N_B, M_HEADS), jnp.float32)
    return {
        "x": jax.random.normal(ks[0], (BATCH, SEQ, D_MODEL), jnp.float32),
        "meta": jax.random.normal(ks[1], (N_META, D_MODEL), jnp.float32),
        "norm_mixer": gain(ks[2], (DEPTH, D_MODEL)),
        "norm_ffn": gain(ks[3], (DEPTH, D_MODEL)),
        "dsa_w_in": nrm(ks[4], (N_A, D_MODEL, DSA_IN), D_MODEL),
        "dsa_q_norm": gain(ks[5], (N_A, ATT_HEAD_DIM)),
        "dsa_k_norm": gain(ks[6], (N_A, ATT_HEAD_DIM)),
        "dsa_w_out": nrm(ks[7], (N_A, D_MODEL, D_MODEL), D_MODEL) * out_scale,
        "mlstm_w_in": nrm(ks[8], (N_B, D_MODEL, MLSTM_IN), D_MODEL),
        "mlstm_b_i": 0.1 * jax.random.normal(ks[10], (N_B, M_HEADS), jnp.float32),
        "mlstm_b_f": b_f,
        "mlstm_out_norm": gain(ks[11], (N_B, D_MODEL)),
        "mlstm_w_out": nrm(ks[12], (N_B, D_MODEL, D_MODEL), D_MODEL) * out_scale,
        "ffn_w_gate": nrm(ks[13], (N_A, D_MODEL, D_FF), D_MODEL),
        "ffn_w_up": nrm(ks[14], (N_A, D_MODEL, D_FF), D_MODEL),
        "ffn_w_down": nrm(ks[15], (N_A, D_FF, D_MODEL), D_FF) * out_scale,
        "moe_router": nrm(ks[16], (N_B, D_MODEL, N_EXPERTS), D_MODEL),
        "moe_w_gate": nrm(ks[17], (N_B, N_EXPERTS, D_MODEL, D_FF), D_MODEL),
        "moe_w_up": nrm(ks[18], (N_B, N_EXPERTS, D_MODEL, D_FF), D_MODEL),
        "moe_w_down": nrm(ks[19], (N_B, N_EXPERTS, D_FF, D_MODEL), D_FF) * out_scale,
    }


def reference(x, meta, norm_mixer, norm_ffn, dsa_w_in, dsa_q_norm, dsa_k_norm, dsa_w_out,
              mlstm_w_in, mlstm_b_i, mlstm_b_f, mlstm_out_norm, mlstm_w_out,
              ffn_w_gate, ffn_w_up, ffn_w_down, moe_router, moe_w_gate, moe_w_up, moe_w_down):
    B, S, D = x.shape
    top_k = min(TOPK_MAX, S // 4)
    h = jnp.concatenate([jnp.broadcast_to(meta[None].astype(x.dtype), (B, N_META, D)), x], axis=1)
    for i in range(DEPTH):
        j = i // N_MIXERS
        hn = rms_norm(h, norm_mixer[i])
        if i % N_MIXERS == 0:
            h = h + dsa_mixer(hn, dsa_w_in[j], dsa_q_norm[j], dsa_k_norm[j], dsa_w_out[j], top_k)
        else:
            h = h + mlstm_mixer(hn, mlstm_w_in[j], mlstm_b_i[j], mlstm_b_f[j], mlstm_out_norm[j], mlstm_w_out[j])
        hn = rms_norm(h, norm_ffn[i])
        if i % 2 == 0:
            h = h + swiglu(hn, ffn_w_gate[j], ffn_w_up[j], ffn_w_down[j])
        else:
            h = h + moe_ffn(hn, moe_router[j], moe_w_gate[j], moe_w_up[j], moe_w_down[j])
    return h[:, N_META:]
```

```python
import functools

import jax
import jax.numpy as jnp
import numpy as np
from jax import lax
from jax.experimental import pallas as pl
from jax.experimental.pallas import tpu as pltpu

F32 = jnp.float32
BF16 = jnp.bfloat16

D_MODEL = 1024
N_META = 16
RMS_EPS = 1e-6
ROPE_THETA = 10000.0
ATT_HEADS = 16
ATT_KV_HEADS = 4
ATT_HEAD_DIM = 64
ATT_GROUP = 4
IDX_HEADS = 8
IDX_DIM = 128
TOPK_MAX = 256
M_HEADS = 4
M_QK_DIM = 128
M_V_DIM = 256
GATE_CAP = 15.0
D_FF = 3584
N_EXPERTS = 8

LANES = 128
Q_BLOCK = 128
KEY_CHUNK = 256
M_CHUNK = 128
VMEM_LIMIT = 52 * 1024 * 1024
INT_MIN = -(2 ** 31)
INT_MAX = 2 ** 31 - 1

DSA_IN_PAD = 2816
MLSTM_IN_PAD = 3200


def _cparams(sem):
    return pltpu.CompilerParams(dimension_semantics=sem, vmem_limit_bytes=VMEM_LIMIT)


def _rms_rows(x, gain):
    ms = jnp.mean(x * x, axis=-1, keepdims=True)
    return x * lax.rsqrt(ms + RMS_EPS) * gain


def _dsa_in_kernel(h_ref, g_ref, w_ref, gq_ref, gk_ref, c64_ref, sa64_ref, sb64_ref, c128_ref, s128_ref,
                   smat_ref, q_ref, k_ref, v_ref, iq_ref, ik_ref, iw_ref):
    xn = _rms_rows(h_ref[...], g_ref[...]).astype(BF16)
    z = jnp.dot(xn, w_ref[...], preferred_element_type=F32)
    c64, sa64, sb64 = c64_ref[...], sa64_ref[...], sb64_ref[...]
    c128, s128 = c128_ref[...], s128_ref[...]
    smat = smat_ref[...]

    def head_norm_rope(zc, gain):
        z2 = zc * zc
        hi = z2.astype(BF16)
        lo = (z2 - hi.astype(F32)).astype(BF16)
        ms = jnp.dot(hi, smat, preferred_element_type=F32) + jnp.dot(lo, smat, preferred_element_type=F32)
        y = zc * lax.rsqrt(ms + RMS_EPS) * gain
        return y * c64 + pltpu.roll(y, 96, 1) * sa64 + pltpu.roll(y, 32, 1) * sb64

    for c in range(8):
        r = head_norm_rope(z[:, c * LANES:(c + 1) * LANES], gq_ref[...])
        q_ref[:, c * LANES:(c + 1) * LANES] = (r * (ATT_HEAD_DIM ** -0.5)).astype(BF16)
    for c in range(2):
        r = head_norm_rope(z[:, 1024 + c * LANES:1024 + (c + 1) * LANES], gk_ref[...])
        k_ref[:, c * LANES:(c + 1) * LANES] = r.astype(BF16)
    v_ref[...] = z[:, 1280:1536].astype(BF16)
    for c in range(9):
        zc = z[:, 1536 + c * LANES:1536 + (c + 1) * LANES]
        r = (zc * c128 + pltpu.roll(zc, 64, 1) * s128).astype(BF16)
        if c < 8:
            iq_ref[:, c * LANES:(c + 1) * LANES] = r
        else:
            ik_ref[...] = r
    iw_ref[...] = z[:, 2688:2816] * (IDX_HEADS ** -0.5 * IDX_DIM ** -0.5)


def _dsa_in_proj(h, gain, w, gq, gk, tabs, seq_pad, tm):
    T = h.shape[0]
    nt = seq_pad // tm
    row = lambda i: (i, 0)
    fixed = lambda i: (0, 0)
    pos = lambda i: (i % nt, 0)
    tab_spec = pl.BlockSpec((tm, LANES), pos)
    smat = jnp.asarray(np.kron(np.eye(2), np.full((64, 64), 1.0 / 64)), BF16)
    outs = [(1024, BF16), (256, BF16), (256, BF16), (1024, BF16), (128, BF16), (128, F32)]
    return pl.pallas_call(
        _dsa_in_kernel,
        grid=(T // tm,),
        in_specs=[pl.BlockSpec((tm, D_MODEL), row), pl.BlockSpec((1, D_MODEL), fixed),
                  pl.BlockSpec((D_MODEL, DSA_IN_PAD), fixed), pl.BlockSpec((1, LANES), fixed),
                  pl.BlockSpec((1, LANES), fixed), tab_spec, tab_spec, tab_spec, tab_spec, tab_spec,
                  pl.BlockSpec((LANES, LANES), fixed)],
        out_specs=[pl.BlockSpec((tm, n), row) for n, _ in outs],
        out_shape=[jax.ShapeDtypeStruct((T, n), dt) for n, dt in outs],
        compiler_params=_cparams(("arbitrary",)),
        name="dsa_in_proj",
    )(h, gain, w, gq, gk, *tabs, smat)


def _dsa_core_kernel(q_ref, k_ref, v_ref, iq_ref, ik_ref, iw_ref, o_ref,
                     iqs_sc, iwb_sc, keys_sc, mask_sc, thr_sc, cut_sc, m_sc, l_sc, acc_sc, *, top_k):
    i = pl.program_id(1)
    nch = (i + 2) // 2
    QB, KC = Q_BLOCK, KEY_CHUNK
    kf = float(top_k)

    for h in range(IDX_HEADS):
        iqs_sc[h * QB:(h + 1) * QB, :] = iq_ref[0, :, h * LANES:(h + 1) * LANES]
        iwb_sc[h] = jnp.broadcast_to(iw_ref[0, :, h:h + 1], (QB, LANES))

    rows = i * QB + lax.broadcasted_iota(jnp.int32, (QB, KC), 0)
    lane = lax.broadcasted_iota(jnp.int32, (QB, KC), 1)

    def score_body(j, carry):
        ikj = ik_ref[0, pl.ds(pl.multiple_of(j * KC, KC), KC), :]
        s = lax.dot_general(iqs_sc[...], ikj, (((1,), (1,)), ((), ())), preferred_element_type=F32)
        acc = jnp.zeros((QB, KC), F32)
        for h in range(IDX_HEADS):
            w = iwb_sc[h]
            acc = acc + jnp.maximum(s[h * QB:(h + 1) * QB, :], 0.0) * jnp.concatenate([w, w], axis=1)
        bits = pltpu.bitcast(acc, jnp.int32)
        key = jnp.where(bits >= 0, bits, bits ^ jnp.int32(INT_MAX))
        keys_sc[j] = jnp.where(j * KC + lane <= rows, key, jnp.int32(INT_MIN))
        return carry

    lax.fori_loop(0, nch, score_body, 0)

    def count(pred):
        def body(j, acc):
            kk = keys_sc[j]
            return (acc + jnp.where(pred(kk[:, :LANES], j * KC), 1.0, 0.0)
                    + jnp.where(pred(kk[:, LANES:], j * KC + LANES), 1.0, 0.0))
        acc = lax.fori_loop(0, nch, body, jnp.zeros((QB, LANES), F32))
        return jnp.sum(acc, axis=1, keepdims=True)

    def count_ge(cand):
        cb = jnp.broadcast_to(cand, (QB, LANES))
        return count(lambda kk, base: kk >= cb)

    v0 = jnp.where(count_ge(jnp.zeros((QB, 1), jnp.int32)) >= kf, jnp.int32(0), jnp.int32(INT_MIN))

    def bit_body(b, v):
        cand = v | lax.shift_left(jnp.int32(1), 30 - b)
        return jnp.where(count_ge(cand) >= kf, cand, v)

    thr = lax.fori_loop(jnp.int32(0), jnp.int32(31), bit_body, v0)
    thr_sc[...] = jnp.broadcast_to(thr, (QB, LANES))
    cut_sc[...] = jnp.full((QB, LANES), INT_MAX, jnp.int32)
    n_ge = count_ge(thr)

    @pl.when(jnp.max(n_ge) > kf)
    def _():
        tb = thr_sc[...]
        n_gt = count(lambda kk, base: kk > tb)
        need = kf - n_gt
        lane1 = lax.broadcasted_iota(jnp.int32, (QB, LANES), 1)

        def cut_body(b, c):
            cand = c | lax.shift_left(jnp.int32(1), 12 - b)
            cb = jnp.broadcast_to(cand, (QB, LANES))
            n = count(lambda kk, base: jnp.where(kk == tb, base + lane1, jnp.int32(INT_MAX)) < cb)
            return jnp.where(n < need, cand, c)

        cut = lax.fori_loop(jnp.int32(0), jnp.int32(13), cut_body, jnp.zeros((QB, 1), jnp.int32))
        cut_sc[...] = jnp.broadcast_to(cut, (QB, LANES))

    def mask_body(j, carry):
        kk = keys_sc[j]
        tb = jnp.concatenate([thr_sc[...]] * 2, axis=1)
        cb = jnp.concatenate([cut_sc[...]] * 2, axis=1)
        kidx = j * KC + lane
        sel = jnp.where(kk > tb, 1.0, jnp.where(kk == tb, jnp.where(kidx <= cb, 1.0, 0.0), 0.0))
        mask_sc[j] = jnp.where(kidx <= rows, sel, 0.0)
        return carry

    lax.fori_loop(0, nch, mask_body, 0)

    for g in range(ATT_KV_HEADS):
        qg = q_ref[0, 0, g]
        m_sc[...] = jnp.full(m_sc.shape, -1e30, F32)
        l_sc[...] = jnp.zeros(l_sc.shape, F32)
        acc_sc[...] = jnp.zeros(acc_sc.shape, F32)

        def att_body(j, carry):
            start = pl.multiple_of(j * KC, KC)
            kj = k_ref[0, g, pl.ds(start, KC), :]
            vj = v_ref[0, g, pl.ds(start, KC), :]
            s = lax.dot_general(qg, kj, (((1,), (1,)), ((), ())), preferred_element_type=F32)
            m_old = m_sc[...]
            m_new = jnp.maximum(m_old, jnp.max(s, axis=1, keepdims=True))
            alpha = jnp.exp(m_old - m_new)
            p = jnp.exp(s - m_new) * jnp.concatenate([mask_sc[j]] * ATT_GROUP, axis=0)
            l_sc[...] = alpha * l_sc[...] + jnp.sum(p, axis=1, keepdims=True)
            acc_sc[...] = alpha * acc_sc[...] + jnp.dot(p.astype(BF16), vj, preferred_element_type=F32)
            m_sc[...] = m_new
            return carry

        lax.fori_loop(0, nch, att_body, 0)
        og = acc_sc[...] / l_sc[...]
        o_ref[0, :, g * 256:(g + 1) * 256] = jnp.concatenate(
            [og[r * QB:(r + 1) * QB] for r in range(ATT_GROUP)], axis=1).astype(BF16)


def _dsa_core(q5, k4, v4, iq, ik, iw, top_k):
    B, nqb = q5.shape[0], q5.shape[1]
    seq_pad = nqb * Q_BLOCK
    key_pad = k4.shape[2]
    nkc = key_pad // KEY_CHUNK
    rows = ATT_GROUP * Q_BLOCK
    return pl.pallas_call(
        functools.partial(_dsa_core_kernel, top_k=top_k),
        grid=(B, nqb),
        in_specs=[
            pl.BlockSpec((1, 1, ATT_KV_HEADS, rows, ATT_HEAD_DIM), lambda b, i: (b, i, 0, 0, 0)),
            pl.BlockSpec((1, ATT_KV_HEADS, key_pad, ATT_HEAD_DIM), lambda b, i: (b, 0, 0, 0)),
            pl.BlockSpec((1, ATT_KV_HEADS, key_pad, ATT_HEAD_DIM), lambda b, i: (b, 0, 0, 0)),
            pl.BlockSpec((1, Q_BLOCK, IDX_HEADS * IDX_DIM), lambda b, i: (b, i, 0)),
            pl.BlockSpec((1, key_pad, IDX_DIM), lambda b, i: (b, 0, 0)),
            pl.BlockSpec((1, Q_BLOCK, LANES), lambda b, i: (b, i, 0)),
        ],
        out_specs=pl.BlockSpec((1, Q_BLOCK, D_MODEL), lambda b, i: (b, i, 0)),
        out_shape=jax.ShapeDtypeStruct((B, seq_pad, D_MODEL), BF16),
        scratch_shapes=[
            pltpu.VMEM((IDX_HEADS * Q_BLOCK, IDX_DIM), BF16),
            pltpu.VMEM((IDX_HEADS, Q_BLOCK, LANES), F32),
            pltpu.VMEM((nkc, Q_BLOCK, KEY_CHUNK), jnp.int32),
            pltpu.VMEM((nkc, Q_BLOCK, KEY_CHUNK), F32),
            pltpu.VMEM((Q_BLOCK, LANES), jnp.int32),
            pltpu.VMEM((Q_BLOCK, LANES), jnp.int32),
            pltpu.VMEM((rows, 1), F32),
            pltpu.VMEM((rows, 1), F32),
            pltpu.VMEM((rows, ATT_HEAD_DIM), F32),
        ],
        compiler_params=_cparams(("arbitrary", "arbitrary")),
        name="dsa_core",
    )(q5, k4, v4, iq, ik, iw)


def _out_proj_kernel(h_ref, y_ref, w_ref, o_ref):
    o_ref[...] = h_ref[...] + jnp.dot(y_ref[...], w_ref[...], preferred_element_type=F32)


def _out_proj(h, y, w, tm):
    T = h.shape[0]
    row = lambda i: (i, 0)
    return pl.pallas_call(
        _out_proj_kernel,
        grid=(T // tm,),
        in_specs=[pl.BlockSpec((tm, D_MODEL), row), pl.BlockSpec((tm, D_MODEL), row),
                  pl.BlockSpec((D_MODEL, D_MODEL), lambda i: (0, 0))],
        out_specs=pl.BlockSpec((tm, D_MODEL), row),
        out_shape=jax.ShapeDtypeStruct((T, D_MODEL), F32),
        compiler_params=_cparams(("arbitrary",)),
        name="out_proj",
    )(h, y, w)


def _mlstm_in_kernel(h_ref, g_ref, w_ref, q_ref, k_ref, v_ref, o_ref, gate_ref):
    xn = _rms_rows(h_ref[...], g_ref[...]).astype(BF16)
    z = jnp.dot(xn, w_ref[...], preferred_element_type=F32)
    q_ref[...] = z[:, 0:512].astype(BF16)
    k_ref[...] = (z[:, 512:1024] * (M_QK_DIM ** -0.5)).astype(BF16)
    v_ref[...] = z[:, 1024:2048].astype(BF16)
    o_ref[...] = z[:, 2048:3072]
    gate_ref[...] = z[:, 3072:3200]


def _mlstm_in_proj(h, gain, w, tm):
    T = h.shape[0]
    row = lambda i: (i, 0)
    fixed = lambda i: (0, 0)
    outs = [(512, BF16), (512, BF16), (1024, BF16), (1024, F32), (128, F32)]
    return pl.pallas_call(
        _mlstm_in_kernel,
        grid=(T // tm,),
        in_specs=[pl.BlockSpec((tm, D_MODEL), row), pl.BlockSpec((1, D_MODEL), fixed),
                  pl.BlockSpec((D_MODEL, MLSTM_IN_PAD), fixed)],
        out_specs=[pl.BlockSpec((tm, n), row) for n, _ in outs],
        out_shape=[jax.ShapeDtypeStruct((T, n), dt) for n, dt in outs],
        compiler_params=_cparams(("arbitrary",)),
        name="mlstm_in_proj",
    )(h, gain, w)


def _mlstm_kernel(q_ref, k_ref, v_ref, o_ref, gate_ref, bias_ref, gout_ref, y_ref, ct_sc, n_sc, m_sc):
    C = M_CHUNK

    @pl.when(pl.program_id(1) == 0)
    def _():
        ct_sc[...] = jnp.zeros(ct_sc.shape, F32)
        n_sc[...] = jnp.zeros(n_sc.shape, F32)
        m_sc[...] = jnp.zeros(m_sc.shape, F32)

    pre = gate_ref[...].T[0:8, :] + bias_ref[...]
    capped = GATE_CAP * jnp.tanh(pre / GATE_CAP)
    log_f = -(jnp.maximum(-capped, 0.0) + jnp.log1p(jnp.exp(-jnp.abs(capped))))
    lane8 = lax.broadcasted_iota(jnp.int32, (8, C), 1)
    b = log_f
    sh = 1
    while sh < C:
        b = b + jnp.where(lane8 >= sh, pltpu.roll(b, sh, 1), 0.0)
        sh *= 2
    stacked = jnp.concatenate([b[4:8], capped[0:4]], axis=0)
    cols = jnp.concatenate([stacked, jnp.zeros((C - 8, C), F32)], axis=0).T

    t_idx = lax.broadcasted_iota(jnp.int32, (C, C), 0)
    s_idx = lax.broadcasted_iota(jnp.int32, (C, C), 1)
    for h in range(M_HEADS):
        qh = q_ref[:, h * M_QK_DIM:(h + 1) * M_QK_DIM]
        kh = k_ref[:, h * M_QK_DIM:(h + 1) * M_QK_DIM]
        vh = v_ref[:, h * M_V_DIM:(h + 1) * M_V_DIM]
        b_row, li_row = stacked[h:h + 1, :], stacked[4 + h:5 + h, :]
        b_col, li_col = cols[:, h:h + 1], cols[:, 4 + h:5 + h]
        m_st = m_sc[h:h + 1, 0:1]
        dmat = jnp.where(s_idx <= t_idx, b_col - b_row + li_row, -jnp.inf)
        inter = b_col + m_st
        m_t = jnp.maximum(inter, jnp.max(dmat, axis=1, keepdims=True))
        w_inter = jnp.exp(inter - m_t)
        qk = lax.dot_general(qh, kh, (((1,), (1,)), ((), ())), preferred_element_type=F32)
        s = qk * jnp.exp(dmat - m_t)
        ct = ct_sc[h]
        num = (w_inter * jnp.dot(qh, ct.astype(BF16), preferred_element_type=F32)
               + jnp.dot(s.astype(BF16), vh, preferred_element_type=F32))
        qn = jnp.sum(qh.astype(F32) * n_sc[h:h + 1, :], axis=1, keepdims=True)
        den = w_inter * qn + jnp.sum(s, axis=1, keepdims=True)
        hout = num / jnp.maximum(jnp.abs(den), jnp.exp(-m_t))

        b_last = b_row[:, C - 1:C]
        m_new = jnp.maximum(b_last + m_st, jnp.max(b_last - b_row + li_row, axis=1, keepdims=True))
        decay = jnp.exp(b_last + m_st - m_new)
        wk = jnp.exp(b_last - b_col + li_col - m_new)
        kw = kh.astype(F32) * wk
        ct_sc[h] = decay * ct + jnp.dot(kw.T.astype(BF16), vh, preferred_element_type=F32)
        n_sc[h:h + 1, :] = decay * n_sc[h:h + 1, :] + jnp.sum(kw, axis=0, keepdims=True)
        m_sc[h:h + 1, :] = jnp.broadcast_to(m_new, (1, LANES))

        cs = slice(h * M_V_DIM, (h + 1) * M_V_DIM)
        hn = _rms_rows(hout, gout_ref[:, cs])
        y_ref[:, cs] = (hn * jax.nn.sigmoid(o_ref[:, cs])).astype(BF16)


def _mlstm_core(q, k, v, o, gates, bias8, gout, batch):
    T = q.shape[0]
    nc = T // batch // M_CHUNK
    row = lambda b, c: (b * nc + c, 0)
    fixed = lambda b, c: (0, 0)
    C = M_CHUNK
    return pl.pallas_call(
        _mlstm_kernel,
        grid=(batch, nc),
        in_specs=[pl.BlockSpec((C, 512), row), pl.BlockSpec((C, 512), row), pl.BlockSpec((C, 1024), row),
                  pl.BlockSpec((C, 1024), row), pl.BlockSpec((C, LANES), row),
                  pl.BlockSpec((8, 1), fixed), pl.BlockSpec((1, D_MODEL), fixed)],
        out_specs=pl.BlockSpec((C, D_MODEL), row),
        out_shape=jax.ShapeDtypeStruct((T, D_MODEL), BF16),
        scratch_shapes=[pltpu.VMEM((M_HEADS, M_QK_DIM, M_V_DIM), F32),
                        pltpu.VMEM((8, M_QK_DIM), F32),
                        pltpu.VMEM((8, LANES), F32)],
        compiler_params=_cparams(("arbitrary", "arbitrary")),
        name="mlstm_core",
    )(q, k, v, o, gates, bias8, gout)


def _ffn_kernel(h_ref, g_ref, wg_ref, wu_ref, wd_ref, o_ref, xn_sc, acc_sc):
    f = pl.program_id(1)

    @pl.when(f == 0)
    def _():
        xn_sc[...] = _rms_rows(h_ref[...], g_ref[...]).astype(BF16)
        acc_sc[...] = h_ref[...]

    xn = xn_sc[...]
    gate = jnp.dot(xn, wg_ref[...], preferred_element_type=F32)
    up = jnp.dot(xn, wu_ref[...], preferred_element_type=F32)
    act = (gate * jax.nn.sigmoid(gate) * up).astype(BF16)
    acc_sc[...] += jnp.dot(act, wd_ref[...], preferred_element_type=F32)

    @pl.when(f == pl.num_programs(1) - 1)
    def _():
        o_ref[...] = acc_sc[...]


def _ffn(h, gain, wg, wu, wd, tm, tf):
    T = h.shape[0]
    return pl.pallas_call(
        _ffn_kernel,
        grid=(T // tm, D_FF // tf),
        in_specs=[pl.BlockSpec((tm, D_MODEL), lambda i, f: (i, 0)),
                  pl.BlockSpec((1, D_MODEL), lambda i, f: (0, 0)),
                  pl.BlockSpec((D_MODEL, tf), lambda i, f: (0, f)),
                  pl.BlockSpec((D_MODEL, tf), lambda i, f: (0, f)),
                  pl.BlockSpec((tf, D_MODEL), lambda i, f: (f, 0))],
        out_specs=pl.BlockSpec((tm, D_MODEL), lambda i, f: (i, 0)),
        out_shape=jax.ShapeDtypeStruct((T, D_MODEL), F32),
        scratch_shapes=[pltpu.VMEM((tm, D_MODEL), BF16), pltpu.VMEM((tm, D_MODEL), F32)],
        compiler_params=_cparams(("arbitrary", "arbitrary")),
        name="ffn_dense",
    )(h, gain, wg, wu, wd)


def _moe_kernel(h_ref, g_ref, wr_ref, wg_ref, wu_ref, wd_ref, o_ref, xn_sc, acc_sc, comb_sc, ce_sc):
    e = pl.program_id(1)
    f = pl.program_id(2)
    tm = h_ref.shape[0]
    lane = lax.broadcasted_iota(jnp.int32, (tm, LANES), 1)

    @pl.when((e == 0) & (f == 0))
    def _():
        xn = _rms_rows(h_ref[...], g_ref[...])
        xn_sc[...] = xn.astype(BF16)
        acc_sc[...] = h_ref[...]
        logits = jnp.dot(xn, wr_ref[...], preferred_element_type=F32, precision=lax.Precision.HIGHEST)
        logits = jnp.where(lane < N_EXPERTS, logits, -jnp.inf)
        m1 = jnp.max(logits, axis=1, keepdims=True)
        i1 = jnp.min(jnp.where(logits == m1, lane, LANES), axis=1, keepdims=True)
        rest = jnp.where(lane == i1, -jnp.inf, logits)
        m2 = jnp.max(rest, axis=1, keepdims=True)
        i2 = jnp.min(jnp.where(rest == m2, lane, LANES), axis=1, keepdims=True)
        e2 = jnp.exp(m2 - m1)
        g1 = 1.0 / (1.0 + e2)
        comb_sc[...] = jnp.where(lane == i1, g1, jnp.where(lane == i2, e2 * g1, 0.0))

    @pl.when(f == 0)
    def _():
        ce_sc[...] = jnp.sum(jnp.where(lane == e, comb_sc[...], 0.0), axis=1, keepdims=True)

    xn = xn_sc[...]
    gate = jnp.dot(xn, wg_ref[0], preferred_element_type=F32)
    up = jnp.dot(xn, wu_ref[0], preferred_element_type=F32)
    act = (gate * jax.nn.sigmoid(gate) * up * ce_sc[...]).astype(BF16)
    acc_sc[...] += jnp.dot(act, wd_ref[0], preferred_element_type=F32)

    @pl.when((e == pl.num_programs(1) - 1) & (f == pl.num_programs(2) - 1))
    def _():
        o_ref[...] = acc_sc[...]


def _moe(h, gain, wr, wg, wu, wd, tm, tf):
    T = h.shape[0]
    return pl.pallas_call(
        _moe_kernel,
        grid=(T // tm, N_EXPERTS, D_FF // tf),
        in_specs=[pl.BlockSpec((tm, D_MODEL), lambda i, e, f: (i, 0)),
                  pl.BlockSpec((1, D_MODEL), lambda i, e, f: (0, 0)),
                  pl.BlockSpec((D_MODEL, LANES), lambda i, e, f: (0, 0)),
                  pl.BlockSpec((1, D_MODEL, tf), lambda i, e, f: (e, 0, f)),
                  pl.BlockSpec((1, D_MODEL, tf), lambda i, e, f: (e, 0, f)),
                  pl.BlockSpec((1, tf, D_MODEL), lambda i, e, f: (e, f, 0))],
        out_specs=pl.BlockSpec((tm, D_MODEL), lambda i, e, f: (i, 0)),
        out_shape=jax.ShapeDtypeStruct((T, D_MODEL), F32),
        scratch_shapes=[pltpu.VMEM((tm, D_MODEL), BF16), pltpu.VMEM((tm, D_MODEL), F32),
                        pltpu.VMEM((tm, LANES), F32), pltpu.VMEM((tm, 1), F32)],
        compiler_params=_cparams(("arbitrary", "arbitrary", "arbitrary")),
        name="moe_dense",
    )(h, gain, wr, wg, wu, wd)


def _rope_tables(seq_pad):
    pos = jnp.arange(seq_pad, dtype=F32)[:, None]
    lane = np.arange(LANES)

    def table(head_dim):
        half = head_dim // 2
        inv = ROPE_THETA ** (-jnp.arange(half, dtype=F32) / half)
        d = lane % head_dim
        ang = pos * inv[d % half][None, :]
        return jnp.cos(ang), jnp.sin(ang), jnp.asarray(d < half)[None, :]

    c64, s64, lo64 = table(ATT_HEAD_DIM)
    c128, s128, lo128 = table(IDX_DIM)
    return (c64, jnp.where(lo64, -s64, 0.0), jnp.where(lo64, 0.0, s64), c128, jnp.where(lo128, -s128, s128))


def _pad_cols(w, n):
    return jnp.pad(w, ((0, 0), (0, n - w.shape[1])))


def kernel(x, meta, norm_mixer, norm_ffn, dsa_w_in, dsa_q_norm, dsa_k_norm, dsa_w_out, mlstm_w_in, mlstm_b_i,
           mlstm_b_f, mlstm_out_norm, mlstm_w_out, ffn_w_gate, ffn_w_up, ffn_w_down, moe_router, moe_w_gate,
           moe_w_up, moe_w_down):
    B, S, D = x.shape
    L = S + N_META
    top_k = min(TOPK_MAX, S // 4)
    seq_pad = -(-L // Q_BLOCK) * Q_BLOCK
    key_pad = -(-seq_pad // KEY_CHUNK) * KEY_CHUNK
    nqb = seq_pad // Q_BLOCK
    T = B * seq_pad
    tm_proj = 384 if seq_pad % 384 == 0 else Q_BLOCK
    tm_ffn = 768 if T % 768 == 0 else Q_BLOCK
    tf = 512

    h = jnp.concatenate([jnp.broadcast_to(meta[None].astype(x.dtype), (B, N_META, D)), x,
                         jnp.zeros((B, seq_pad - L, D), x.dtype)], axis=1).reshape(T, D)
    tabs = _rope_tables(seq_pad)
    depth = norm_mixer.shape[0]

    for i in range(depth):
        j = i // 2
        gain_m = norm_mixer[i][None, :]
        gain_f = norm_ffn[i][None, :]
        if i % 2 == 0:
            w_in = _pad_cols(dsa_w_in[j], DSA_IN_PAD).astype(BF16)
            gq = jnp.tile(dsa_q_norm[j], 2)[None, :]
            gk = jnp.tile(dsa_k_norm[j], 2)[None, :]
            q, k, v, iq, ik, iw = _dsa_in_proj(h, gain_m, w_in, gq, gk, tabs, seq_pad, tm_proj)
            q5 = q.reshape(B, nqb, Q_BLOCK, ATT_KV_HEADS, ATT_GROUP, ATT_HEAD_DIM).transpose(0, 1, 3, 4, 2, 5)
            q5 = q5.reshape(B, nqb, ATT_KV_HEADS, ATT_GROUP * Q_BLOCK, ATT_HEAD_DIM)
            kpad = ((0, 0), (0, key_pad - seq_pad), (0, 0), (0, 0))
            k4 = jnp.pad(k.reshape(B, seq_pad, ATT_KV_HEADS, ATT_HEAD_DIM), kpad).transpose(0, 2, 1, 3)
            v4 = jnp.pad(v.reshape(B, seq_pad, ATT_KV_HEADS, ATT_HEAD_DIM), kpad).transpose(0, 2, 1, 3)
            ik3 = jnp.pad(ik.reshape(B, seq_pad, IDX_DIM), ((0, 0), (0, key_pad - seq_pad), (0, 0)))
            att = _dsa_core(q5, k4, v4, iq.reshape(B, seq_pad, -1), ik3, iw.reshape(B, seq_pad, LANES), top_k)
            h = _out_proj(h, att.reshape(T, D), dsa_w_out[j].astype(BF16), tm_proj)
            h = _ffn(h, gain_f, ffn_w_gate[j].astype(BF16), ffn_w_up[j].astype(BF16),
                     ffn_w_down[j].astype(BF16), tm_ffn, tf)
        else:
            w_in = _pad_cols(mlstm_w_in[j], MLSTM_IN_PAD).astype(BF16)
            q, k, v, o, gates = _mlstm_in_proj(h, gain_m, w_in, tm_proj)
            bias8 = jnp.concatenate([mlstm_b_i[j], mlstm_b_f[j]])[:, None]
            y = _mlstm_core(q, k, v, o, gates, bias8, mlstm_out_norm[j][None, :], B)
            h = _out_proj(h, y, mlstm_w_out[j].astype(BF16), tm_proj)
            wr = _pad_cols(moe_router[j], LANES)
            h = _moe(h, gain_f, wr, moe_w_gate[j].astype(BF16), moe_w_up[j].astype(BF16),
                     moe_w_down[j].astype(BF16), tm_ffn, tf)
    return h.reshape(B, seq_pad, D)[:, N_META:L]
```

```python
import functools

import jax
import jax.numpy as jnp
import numpy as np
from jax import lax
from jax.experimental import pallas as pl
from jax.experimental.pallas import tpu as pltpu

F32 = jnp.float32
BF16 = jnp.bfloat16

D_MODEL = 1024
N_META = 16
RMS_EPS = 1e-6
ROPE_THETA = 10000.0
ATT_HEADS = 16
ATT_KV_HEADS = 4
ATT_HEAD_DIM = 64
ATT_GROUP = 4
IDX_HEADS = 8
IDX_DIM = 128
TOPK_MAX = 256
M_HEADS = 4
M_QK_DIM = 128
M_V_DIM = 256
GATE_CAP = 15.0
D_FF = 3584
N_EXPERTS = 8

LANES = 128
Q_BLOCK = 128
KEY_CHUNK = 256
M_CHUNK = 128
VMEM_LIMIT = 52 * 1024 * 1024
INT_MIN = -(2 ** 31)
INT_MAX = 2 ** 31 - 1
LOG2E = 1.4426950408889634

DSA_IN_PAD = 2816
MLSTM_IN_PAD = 3200


def _cparams(sem):
    return pltpu.CompilerParams(dimension_semantics=sem, vmem_limit_bytes=VMEM_LIMIT)


def _rms_rows(x, gain):
    ms = jnp.mean(x * x, axis=-1, keepdims=True)
    return x * lax.rsqrt(ms + RMS_EPS) * gain


def _dsa_in_kernel(h_ref, g_ref, w_ref, gq_ref, gk_ref, c64_ref, sa64_ref, sb64_ref, c128_ref, s128_ref,
                   smat_ref, q_ref, qsw_ref, k_ref, v_ref, iq_ref, ik_ref, iw_ref):
    xn = _rms_rows(h_ref[...], g_ref[...]).astype(BF16)
    z = jnp.dot(xn, w_ref[...], preferred_element_type=F32)
    c64, sa64, sb64 = c64_ref[...], sa64_ref[...], sb64_ref[...]
    c128, s128 = c128_ref[...], s128_ref[...]
    smat = smat_ref[...]

    def head_norm_rope(zc, gain):
        z2 = zc * zc
        hi = z2.astype(BF16)
        lo = (z2 - hi.astype(F32)).astype(BF16)
        ms = jnp.dot(hi, smat, preferred_element_type=F32) + jnp.dot(lo, smat, preferred_element_type=F32)
        y = zc * lax.rsqrt(ms + RMS_EPS) * gain
        return y * c64 + pltpu.roll(y, 96, 1) * sa64 + pltpu.roll(y, 32, 1) * sb64

    for c in range(8):
        r = head_norm_rope(z[:, c * LANES:(c + 1) * LANES], gq_ref[...]) * (ATT_HEAD_DIM ** -0.5 * LOG2E)
        q_ref[:, c * LANES:(c + 1) * LANES] = r.astype(BF16)
        qsw_ref[:, c * LANES:(c + 1) * LANES] = pltpu.roll(r, 64, 1).astype(BF16)
    for c in range(2):
        r = head_norm_rope(z[:, 1024 + c * LANES:1024 + (c + 1) * LANES], gk_ref[...])
        k_ref[:, c * LANES:(c + 1) * LANES] = r.astype(BF16)
    ones = jnp.ones((z.shape[0], LANES), BF16)
    for c in range(2):
        v_ref[:, 2 * c * LANES:(2 * c + 1) * LANES] = z[:, 1280 + c * LANES:1280 + (c + 1) * LANES].astype(BF16)
        v_ref[:, (2 * c + 1) * LANES:(2 * c + 2) * LANES] = ones
    for c in range(9):
        zc = z[:, 1536 + c * LANES:1536 + (c + 1) * LANES]
        r = (zc * c128 + pltpu.roll(zc, 64, 1) * s128).astype(BF16)
        if c < 8:
            iq_ref[:, c * LANES:(c + 1) * LANES] = r
        else:
            ik_ref[...] = r
    iw_ref[...] = z[:, 2688:2816] * (IDX_HEADS ** -0.5 * IDX_DIM ** -0.5)


def _dsa_in_proj(h, gain, w, gq, gk, tabs, seq_pad, tm):
    T = h.shape[0]
    nt = seq_pad // tm
    row = lambda i: (i, 0)
    fixed = lambda i: (0, 0)
    pos = lambda i: (i % nt, 0)
    tab_spec = pl.BlockSpec((tm, LANES), pos)
    smat = jnp.asarray(np.kron(np.eye(2), np.full((64, 64), 1.0 / 64)), BF16)
    outs = [(1024, BF16), (1024, BF16), (256, BF16), (512, BF16), (1024, BF16), (128, BF16), (128, F32)]
    return pl.pallas_call(
        _dsa_in_kernel,
        grid=(T // tm,),
        in_specs=[pl.BlockSpec((tm, D_MODEL), row), pl.BlockSpec((1, D_MODEL), fixed),
                  pl.BlockSpec((D_MODEL, DSA_IN_PAD), fixed), pl.BlockSpec((1, LANES), fixed),
                  pl.BlockSpec((1, LANES), fixed), tab_spec, tab_spec, tab_spec, tab_spec, tab_spec,
                  pl.BlockSpec((LANES, LANES), fixed)],
        out_specs=[pl.BlockSpec((tm, n), row) for n, _ in outs],
        out_shape=[jax.ShapeDtypeStruct((T, n), dt) for n, dt in outs],
        compiler_params=_cparams(("arbitrary",)),
        name="dsa_in_proj",
    )(h, gain, w, gq, gk, *tabs, smat)


def _dsa_core_kernel(q_ref, qsw_ref, k_ref, v_ref, iq_ref, ik_ref, iw_ref, o_ref,
                     iqs_sc, iwb_sc, qs_sc, keys_sc, bias_sc, thr_sc, cut_sc, m_sc, l_sc, acc_sc,
                     *, top_k, seq_pad):
    i = pl.program_id(1)
    QB, KC = Q_BLOCK, KEY_CHUNK
    nch = (i * QB + QB + KC - 1) // KC
    kf = float(top_k)
    lane1 = lax.broadcasted_iota(jnp.int32, (QB, LANES), 1)

    for h in range(IDX_HEADS):
        iqs_sc[h * QB:(h + 1) * QB, :] = iq_ref[0, :, h * LANES:(h + 1) * LANES]
        iwb_sc[h] = jnp.broadcast_to(iw_ref[0, :, h:h + 1], (QB, LANES))

    for g in range(ATT_KV_HEADS):
        for r in range(ATT_GROUP):
            h = g * ATT_GROUP + r
            src = q_ref if h % 2 == g % 2 else qsw_ref
            chunk = src[0, :, (h // 2) * LANES:(h // 2 + 1) * LANES].astype(F32)
            keep = (lane1 >= 64) if g % 2 else (lane1 < 64)
            qs_sc[g, r * QB:(r + 1) * QB, :] = jnp.where(keep, chunk, 0.0).astype(BF16)

    rows = i * QB + lax.broadcasted_iota(jnp.int32, (QB, KC), 0)
    lane = lax.broadcasted_iota(jnp.int32, (QB, KC), 1)

    def chunk_start(j):
        return pl.multiple_of(jnp.minimum(j * KC, seq_pad - KC), LANES)

    def key_valid(j, start):
        kidx = start + lane
        return (kidx <= rows) & (kidx >= j * KC)

    def score_body(j, carry):
        start = chunk_start(j)
        ikj = ik_ref[0, pl.ds(start, KC), :]
        s = lax.dot_general(iqs_sc[...], ikj, (((1,), (1,)), ((), ())), preferred_element_type=F32)
        acc = jnp.zeros((QB, KC), F32)
        for h in range(IDX_HEADS):
            w = iwb_sc[h]
            acc = acc + jnp.maximum(s[h * QB:(h + 1) * QB, :], 0.0) * jnp.concatenate([w, w], axis=1)
        bits = pltpu.bitcast(acc, jnp.int32)
        key = jnp.where(bits >= 0, bits, bits ^ jnp.int32(INT_MAX))
        keys_sc[j] = jnp.where(key_valid(j, start), key, jnp.int32(INT_MIN))
        return carry

    lax.fori_loop(0, nch, score_body, 0)

    def count(pred):
        def body(j, acc):
            kk = keys_sc[j]
            base = chunk_start(j)
            return (acc + jnp.where(pred(kk[:, :LANES], base), 1.0, 0.0)
                    + jnp.where(pred(kk[:, LANES:], base + LANES), 1.0, 0.0))
        acc = lax.fori_loop(0, nch, body, jnp.zeros((QB, LANES), F32))
        return jnp.sum(acc, axis=1, keepdims=True)

    zero = jnp.zeros((QB, LANES), jnp.int32)
    v0 = jnp.where(count(lambda kk, base: kk >= zero) >= kf, zero, jnp.int32(INT_MIN))

    def bit_body(b, v):
        cand = v | lax.shift_left(jnp.int32(1), (30 - b).astype(jnp.int32))
        return jnp.where(count(lambda kk, base: kk >= cand) >= kf, cand, v)

    thr = lax.fori_loop(jnp.int32(0), jnp.int32(31), bit_body, v0)
    thr_sc[...] = thr
    cut_sc[...] = jnp.full((QB, LANES), INT_MAX, jnp.int32)
    n_ge = count(lambda kk, base: kk >= thr)

    @pl.when(jnp.max(n_ge) > kf)
    def _():
        tb = thr_sc[...]
        need = kf - count(lambda kk, base: kk > tb)

        def cut_body(b, c):
            cand = c | lax.shift_left(jnp.int32(1), (12 - b).astype(jnp.int32))
            n = count(lambda kk, base: jnp.where(kk == tb, base + lane1, jnp.int32(INT_MAX)) < cand)
            return jnp.where(n < need, cand, c)

        cut_sc[...] = lax.fori_loop(jnp.int32(0), jnp.int32(13), cut_body, zero)

    def bias_body(j, carry):
        kk = keys_sc[j]
        start = chunk_start(j)
        tb = jnp.concatenate([thr_sc[...]] * 2, axis=1)
        cb = jnp.concatenate([cut_sc[...]] * 2, axis=1)
        sel = (kk > tb) | ((kk == tb) & (start + lane <= cb))
        bias_sc[j] = jnp.where(sel & key_valid(j, start), 0.0, -1e30)
        return carry

    lax.fori_loop(0, nch, bias_body, 0)

    m_sc[...] = jnp.full(m_sc.shape, -1e9, F32)
    l_sc[...] = jnp.zeros(l_sc.shape, F32)
    acc_sc[...] = jnp.zeros(acc_sc.shape, F32)

    def att_body(j, carry):
        start = chunk_start(j)
        bias = jnp.concatenate([bias_sc[j]] * ATT_GROUP, axis=0)
        for g in range(ATT_KV_HEADS):
            cg = g // 2
            kj = k_ref[0, pl.ds(start, KC), cg * LANES:(cg + 1) * LANES]
            vj = v_ref[0, pl.ds(start, KC), cg * 2 * LANES:(cg + 1) * 2 * LANES]
            s = lax.dot_general(qs_sc[g], kj, (((1,), (1,)), ((), ())), preferred_element_type=F32) + bias
            m_old = m_sc[g]
            m_new = jnp.maximum(m_old, jnp.max(s, axis=1, keepdims=True))
            alpha = jnp.exp2(m_old - m_new)
            p = jnp.exp2(s - jnp.concatenate([m_new, m_new], axis=1)).astype(BF16)
            pv = jnp.dot(p, vj, preferred_element_type=F32)
            l_sc[g] = alpha * l_sc[g] + pv[:, LANES:]
            acc_sc[g] = alpha * acc_sc[g] + pv[:, :LANES]
            m_sc[g] = m_new
        return carry

    lax.fori_loop(0, nch, att_body, 0)

    for g in range(ATT_KV_HEADS):
        og = acc_sc[g] / l_sc[g]
        for pair in range(2):
            even = og[(2 * pair) * QB:(2 * pair + 1) * QB]
            odd = og[(2 * pair + 1) * QB:(2 * pair + 2) * QB]
            if g % 2:
                even = pltpu.roll(even, 64, 1)
            else:
                odd = pltpu.roll(odd, 64, 1)
            c = 2 * g + pair
            o_ref[0, :, c * LANES:(c + 1) * LANES] = jnp.where(lane1 < 64, even, odd).astype(BF16)


def _dsa_core(q, qsw, k, v, iq, ik, iw, top_k):
    B, seq_pad = q.shape[0], q.shape[1]
    nqb = seq_pad // Q_BLOCK
    nkc = -(-seq_pad // KEY_CHUNK)
    rows = ATT_GROUP * Q_BLOCK
    qblock = lambda n: pl.BlockSpec((1, Q_BLOCK, n), lambda b, i: (b, i, 0))
    whole = lambda n: pl.BlockSpec((1, seq_pad, n), lambda b, i: (b, 0, 0))
    return pl.pallas_call(
        functools.partial(_dsa_core_kernel, top_k=top_k, seq_pad=seq_pad),
        grid=(B, nqb),
        in_specs=[qblock(D_MODEL), qblock(D_MODEL), whole(256), whole(512),
                  qblock(IDX_HEADS * IDX_DIM), whole(IDX_DIM), qblock(LANES)],
        out_specs=qblock(D_MODEL),
        out_shape=jax.ShapeDtypeStruct((B, seq_pad, D_MODEL), BF16),
        scratch_shapes=[
            pltpu.VMEM((IDX_HEADS * Q_BLOCK, IDX_DIM), BF16),
            pltpu.VMEM((IDX_HEADS, Q_BLOCK, LANES), F32),
            pltpu.VMEM((ATT_KV_HEADS, rows, LANES), BF16),
            pltpu.VMEM((nkc, Q_BLOCK, KEY_CHUNK), jnp.int32),
            pltpu.VMEM((nkc, Q_BLOCK, KEY_CHUNK), F32),
            pltpu.VMEM((Q_BLOCK, LANES), jnp.int32),
            pltpu.VMEM((Q_BLOCK, LANES), jnp.int32),
            pltpu.VMEM((ATT_KV_HEADS, rows, LANES), F32),
            pltpu.VMEM((ATT_KV_HEADS, rows, LANES), F32),
            pltpu.VMEM((ATT_KV_HEADS, rows, LANES), F32),
        ],
        compiler_params=_cparams(("arbitrary", "arbitrary")),
        name="dsa_core",
    )(q, qsw, k, v, iq, ik, iw)


def _out_proj_kernel(h_ref, y_ref, w_ref, o_ref):
    o_ref[...] = h_ref[...] + jnp.dot(y_ref[...], w_ref[...], preferred_element_type=F32)


def _out_proj(h, y, w, tm):
    T = h.shape[0]
    row = lambda i: (i, 0)
    return pl.pallas_call(
        _out_proj_kernel,
        grid=(T // tm,),
        in_specs=[pl.BlockSpec((tm, D_MODEL), row), pl.BlockSpec((tm, D_MODEL), row),
                  pl.BlockSpec((D_MODEL, D_MODEL), lambda i: (0, 0))],
        out_specs=pl.BlockSpec((tm, D_MODEL), row),
        out_shape=jax.ShapeDtypeStruct((T, D_MODEL), F32),
        compiler_params=_cparams(("arbitrary",)),
        name="out_proj",
    )(h, y, w)


def _mlstm_in_kernel(h_ref, g_ref, w_ref, q_ref, k_ref, v_ref, o_ref, gate_ref):
    xn = _rms_rows(h_ref[...], g_ref[...]).astype(BF16)
    z = jnp.dot(xn, w_ref[...], preferred_element_type=F32)
    q_ref[...] = z[:, 0:512].astype(BF16)
    k_ref[...] = (z[:, 512:1024] * (M_QK_DIM ** -0.5)).astype(BF16)
    v_ref[...] = z[:, 1024:2048].astype(BF16)
    o_ref[...] = z[:, 2048:3072]
    gate_ref[...] = z[:, 3072:3200]


def _mlstm_in_proj(h, gain, w, tm):
    T = h.shape[0]
    row = lambda i: (i, 0)
    fixed = lambda i: (0, 0)
    outs = [(512, BF16), (512, BF16), (1024, BF16), (1024, F32), (128, F32)]
    return pl.pallas_call(
        _mlstm_in_kernel,
        grid=(T // tm,),
        in_specs=[pl.BlockSpec((tm, D_MODEL), row), pl.BlockSpec((1, D_MODEL), fixed),
                  pl.BlockSpec((D_MODEL, MLSTM_IN_PAD), fixed)],
        out_specs=[pl.BlockSpec((tm, n), row) for n, _ in outs],
        out_shape=[jax.ShapeDtypeStruct((T, n), dt) for n, dt in outs],
        compiler_params=_cparams(("arbitrary",)),
        name="mlstm_in_proj",
    )(h, gain, w)


def _mlstm_kernel(q_ref, k_ref, v_ref, o_ref, gate_ref, bias_ref, gout_ref, y_ref, ct_sc, n_sc, m_sc):
    C = M_CHUNK

    @pl.when(pl.program_id(1) == 0)
    def _():
        ct_sc[...] = jnp.zeros(ct_sc.shape, F32)
        n_sc[...] = jnp.zeros(n_sc.shape, F32)
        m_sc[...] = jnp.zeros(m_sc.shape, F32)

    pre = gate_ref[...].T[0:8, :] + bias_ref[...]
    capped = GATE_CAP * jnp.tanh(pre / GATE_CAP)
    log_f = -(jnp.maximum(-capped, 0.0) + jnp.log1p(jnp.exp(-jnp.abs(capped))))
    lane8 = lax.broadcasted_iota(jnp.int32, (8, C), 1)
    b = log_f
    sh = 1
    while sh < C:
        b = b + jnp.where(lane8 >= sh, pltpu.roll(b, sh, 1), 0.0)
        sh *= 2
    stacked = jnp.concatenate([b[4:8], capped[0:4]], axis=0)
    cols = jnp.concatenate([stacked, jnp.zeros((C - 8, C), F32)], axis=0).T

    t_idx = lax.broadcasted_iota(jnp.int32, (C, C), 0)
    s_idx = lax.broadcasted_iota(jnp.int32, (C, C), 1)
    for h in range(M_HEADS):
        qh = q_ref[:, h * M_QK_DIM:(h + 1) * M_QK_DIM]
        kh = k_ref[:, h * M_QK_DIM:(h + 1) * M_QK_DIM]
        vh = v_ref[:, h * M_V_DIM:(h + 1) * M_V_DIM]
        b_row, li_row = stacked[h:h + 1, :], stacked[4 + h:5 + h, :]
        b_col, li_col = cols[:, h:h + 1], cols[:, 4 + h:5 + h]
        m_st = m_sc[h:h + 1, 0:1]
        dmat = jnp.where(s_idx <= t_idx, b_col - b_row + li_row, -jnp.inf)
        inter = b_col + m_st
        m_t = jnp.maximum(inter, jnp.max(dmat, axis=1, keepdims=True))
        w_inter = jnp.exp(inter - m_t)
        qk = lax.dot_general(qh, kh, (((1,), (1,)), ((), ())), preferred_element_type=F32)
        s = qk * jnp.exp(dmat - m_t)
        ct = ct_sc[h]
        num = (w_inter * jnp.dot(qh, ct.astype(BF16), preferred_element_type=F32)
               + jnp.dot(s.astype(BF16), vh, preferred_element_type=F32))
        qn = jnp.sum(qh.astype(F32) * n_sc[h:h + 1, :], axis=1, keepdims=True)
        den = w_inter * qn + jnp.sum(s, axis=1, keepdims=True)
        hout = num / jnp.maximum(jnp.abs(den), jnp.exp(-m_t))

        b_last = b_row[:, C - 1:C]
        m_new = jnp.maximum(b_last + m_st, jnp.max(b_last - b_row + li_row, axis=1, keepdims=True))
        decay = jnp.exp(b_last + m_st - m_new)
        wk = jnp.exp(b_last - b_col + li_col - m_new)
        kw = kh.astype(F32) * wk
        ct_sc[h] = decay * ct + jnp.dot(kw.T.astype(BF16), vh, preferred_element_type=F32)
        n_sc[h:h + 1, :] = decay * n_sc[h:h + 1, :] + jnp.sum(kw, axis=0, keepdims=True)
        m_sc[h:h + 1, :] = jnp.broadcast_to(m_new, (1, LANES))

        cs = slice(h * M_V_DIM, (h + 1) * M_V_DIM)
        hn = _rms_rows(hout, gout_ref[:, cs])
        y_ref[:, cs] = (hn * jax.nn.sigmoid(o_ref[:, cs])).astype(BF16)


def _mlstm_core(q, k, v, o, gates, bias8, gout, batch):
    T = q.shape[0]
    nc = T // batch // M_CHUNK
    row = lambda b, c: (b * nc + c, 0)
    fixed = lambda b, c: (0, 0)
    C = M_CHUNK
    return pl.pallas_call(
        _mlstm_kernel,
        grid=(batch, nc),
        in_specs=[pl.BlockSpec((C, 512), row), pl.BlockSpec((C, 512), row), pl.BlockSpec((C, 1024), row),
                  pl.BlockSpec((C, 1024), row), pl.BlockSpec((C, LANES), row),
                  pl.BlockSpec((8, 1), fixed), pl.BlockSpec((1, D_MODEL), fixed)],
        out_specs=pl.BlockSpec((C, D_MODEL), row),
        out_shape=jax.ShapeDtypeStruct((T, D_MODEL), BF16),
        scratch_shapes=[pltpu.VMEM((M_HEADS, M_QK_DIM, M_V_DIM), F32),
                        pltpu.VMEM((8, M_QK_DIM), F32),
                        pltpu.VMEM((8, LANES), F32)],
        compiler_params=_cparams(("arbitrary", "arbitrary")),
        name="mlstm_core",
    )(q, k, v, o, gates, bias8, gout)


def _ffn_kernel(h_ref, g_ref, wg_ref, wu_ref, wd_ref, o_ref, xn_sc, acc_sc):
    f = pl.program_id(1)

    @pl.when(f == 0)
    def _():
        xn_sc[...] = _rms_rows(h_ref[...], g_ref[...]).astype(BF16)
        acc_sc[...] = h_ref[...]

    xn = xn_sc[...]
    gate = jnp.dot(xn, wg_ref[...], preferred_element_type=F32)
    up = jnp.dot(xn, wu_ref[...], preferred_element_type=F32)
    act = (gate * jax.nn.sigmoid(gate) * up).astype(BF16)
    acc_sc[...] += jnp.dot(act, wd_ref[...], preferred_element_type=F32)

    @pl.when(f == pl.num_programs(1) - 1)
    def _():
        o_ref[...] = acc_sc[...]


def _ffn(h, gain, wg, wu, wd, tm, tf):
    T = h.shape[0]
    return pl.pallas_call(
        _ffn_kernel,
        grid=(T // tm, D_FF // tf),
        in_specs=[pl.BlockSpec((tm, D_MODEL), lambda i, f: (i, 0)),
                  pl.BlockSpec((1, D_MODEL), lambda i, f: (0, 0)),
                  pl.BlockSpec((D_MODEL, tf), lambda i, f: (0, f)),
                  pl.BlockSpec((D_MODEL, tf), lambda i, f: (0, f)),
                  pl.BlockSpec((tf, D_MODEL), lambda i, f: (f, 0))],
        out_specs=pl.BlockSpec((tm, D_MODEL), lambda i, f: (i, 0)),
        out_shape=jax.ShapeDtypeStruct((T, D_MODEL), F32),
        scratch_shapes=[pltpu.VMEM((tm, D_MODEL), BF16), pltpu.VMEM((tm, D_MODEL), F32)],
        compiler_params=_cparams(("arbitrary", "arbitrary")),
        name="ffn_dense",
    )(h, gain, wg, wu, wd)


def _moe_kernel(h_ref, g_ref, wr_ref, wg_ref, wu_ref, wd_ref, o_ref, xn_sc, acc_sc, comb_sc, ce_sc):
    e = pl.program_id(1)
    f = pl.program_id(2)
    tm = h_ref.shape[0]
    lane = lax.broadcasted_iota(jnp.int32, (tm, LANES), 1)

    @pl.when((e == 0) & (f == 0))
    def _():
        xn = _rms_rows(h_ref[...], g_ref[...])
        xn_sc[...] = xn.astype(BF16)
        acc_sc[...] = h_ref[...]
        logits = jnp.dot(xn, wr_ref[...], preferred_element_type=F32, precision=lax.Precision.HIGHEST)
        logits = jnp.where(lane < N_EXPERTS, logits, -jnp.inf)
        m1 = jnp.max(logits, axis=1, keepdims=True)
        i1 = jnp.min(jnp.where(logits == m1, lane, LANES), axis=1, keepdims=True)
        rest = jnp.where(lane == i1, -jnp.inf, logits)
        m2 = jnp.max(rest, axis=1, keepdims=True)
        i2 = jnp.min(jnp.where(rest == m2, lane, LANES), axis=1, keepdims=True)
        e2 = jnp.exp(m2 - m1)
        g1 = 1.0 / (1.0 + e2)
        comb_sc[...] = jnp.where(lane == i1, g1, jnp.where(lane == i2, e2 * g1, 0.0))

    @pl.when(f == 0)
    def _():
        ce_sc[...] = jnp.sum(jnp.where(lane == e, comb_sc[...], 0.0), axis=1, keepdims=True)

    xn = xn_sc[...]
    gate = jnp.dot(xn, wg_ref[0], preferred_element_type=F32)
    up = jnp.dot(xn, wu_ref[0], preferred_element_type=F32)
    act = (gate * jax.nn.sigmoid(gate) * up * ce_sc[...]).astype(BF16)
    acc_sc[...] += jnp.dot(act, wd_ref[0], preferred_element_type=F32)

    @pl.when((e == pl.num_programs(1) - 1) & (f == pl.num_programs(2) - 1))
    def _():
        o_ref[...] = acc_sc[...]


def _moe(h, gain, wr, wg, wu, wd, tm, tf):
    T = h.shape[0]
    return pl.pallas_call(
        _moe_kernel,
        grid=(T // tm, N_EXPERTS, D_FF // tf),
        in_specs=[pl.BlockSpec((tm, D_MODEL), lambda i, e, f: (i, 0)),
                  pl.BlockSpec((1, D_MODEL), lambda i, e, f: (0, 0)),
                  pl.BlockSpec((D_MODEL, LANES), lambda i, e, f: (0, 0)),
                  pl.BlockSpec((1, D_MODEL, tf), lambda i, e, f: (e, 0, f)),
                  pl.BlockSpec((1, D_MODEL, tf), lambda i, e, f: (e, 0, f)),
                  pl.BlockSpec((1, tf, D_MODEL), lambda i, e, f: (e, f, 0))],
        out_specs=pl.BlockSpec((tm, D_MODEL), lambda i, e, f: (i, 0)),
        out_shape=jax.ShapeDtypeStruct((T, D_MODEL), F32),
        scratch_shapes=[pltpu.VMEM((tm, D_MODEL), BF16), pltpu.VMEM((tm, D_MODEL), F32),
                        pltpu.VMEM((tm, LANES), F32), pltpu.VMEM((tm, 1), F32)],
        compiler_params=_cparams(("arbitrary", "arbitrary", "arbitrary")),
        name="moe_dense",
    )(h, gain, wr, wg, wu, wd)


def _rope_tables(seq_pad):
    pos = jnp.arange(seq_pad, dtype=F32)[:, None]
    lane = np.arange(LANES)

    def table(head_dim):
        half = head_dim // 2
        inv = ROPE_THETA ** (-jnp.arange(half, dtype=F32) / half)
        d = lane % head_dim
        ang = pos * inv[d % half][None, :]
        return jnp.cos(ang), jnp.sin(ang), jnp.asarray(d < half)[None, :]

    c64, s64, lo64 = table(ATT_HEAD_DIM)
    c128, s128, lo128 = table(IDX_DIM)
    return (c64, jnp.where(lo64, -s64, 0.0), jnp.where(lo64, 0.0, s64), c128, jnp.where(lo128, -s128, s128))


def _pad_cols(w, n):
    return jnp.pad(w, ((0, 0), (0, n - w.shape[1])))


def kernel(x, meta, norm_mixer, norm_ffn, dsa_w_in, dsa_q_norm, dsa_k_norm, dsa_w_out, mlstm_w_in, mlstm_b_i,
           mlstm_b_f, mlstm_out_norm, mlstm_w_out, ffn_w_gate, ffn_w_up, ffn_w_down, moe_router, moe_w_gate,
           moe_w_up, moe_w_down):
    B, S, D = x.shape
    L = S + N_META
    top_k = min(TOPK_MAX, S // 4)
    seq_pad = max(-(-L // Q_BLOCK) * Q_BLOCK, KEY_CHUNK)
    T = B * seq_pad
    tm_proj = 384 if seq_pad % 384 == 0 else Q_BLOCK
    tm_ffn = 768 if T % 768 == 0 else Q_BLOCK
    tf = 512

    h = jnp.concatenate([jnp.broadcast_to(meta[None].astype(x.dtype), (B, N_META, D)), x,
                         jnp.zeros((B, seq_pad - L, D), x.dtype)], axis=1).reshape(T, D)
    tabs = _rope_tables(seq_pad)
    depth = norm_mixer.shape[0]

    for i in range(depth):
        j = i // 2
        gain_m = norm_mixer[i][None, :]
        gain_f = norm_ffn[i][None, :]
        if i % 2 == 0:
            w_in = _pad_cols(dsa_w_in[j], DSA_IN_PAD).astype(BF16)
            gq = jnp.tile(dsa_q_norm[j], 2)[None, :]
            gk = jnp.tile(dsa_k_norm[j], 2)[None, :]
            proj = _dsa_in_proj(h, gain_m, w_in, gq, gk, tabs, seq_pad, tm_proj)
            att = _dsa_core(*[a.reshape(B, seq_pad, a.shape[-1]) for a in proj], top_k)
            h = _out_proj(h, att.reshape(T, D), dsa_w_out[j].astype(BF16), tm_proj)
            h = _ffn(h, gain_f, ffn_w_gate[j].astype(BF16), ffn_w_up[j].astype(BF16),
                     ffn_w_down[j].astype(BF16), tm_ffn, tf)
        else:
            w_in = _pad_cols(mlstm_w_in[j], MLSTM_IN_PAD).astype(BF16)
            q, k, v, o, gates = _mlstm_in_proj(h, gain_m, w_in, tm_proj)
            bias8 = jnp.concatenate([mlstm_b_i[j], mlstm_b_f[j]])[:, None]
            y = _mlstm_core(q, k, v, o, gates, bias8, mlstm_out_norm[j][None, :], B)
            h = _out_proj(h, y, mlstm_w_out[j].astype(BF16), tm_proj)
            wr = _pad_cols(moe_router[j], LANES)
            h = _moe(h, gain_f, wr, moe_w_gate[j].astype(BF16), moe_w_up[j].astype(BF16),
                     moe_w_down[j].astype(BF16), tm_ffn, tf)
    return h.reshape(B, seq_pad, D)[:, N_META:L]
```

```python
import functools

import jax
import jax.numpy as jnp
import numpy as np
from jax import lax
from jax.experimental import pallas as pl
from jax.experimental.pallas import tpu as pltpu

F32 = jnp.float32
BF16 = jnp.bfloat16

D_MODEL = 1024
N_META = 16
RMS_EPS = 1e-6
ROPE_THETA = 10000.0
ATT_HEADS = 16
ATT_KV_HEADS = 4
ATT_HEAD_DIM = 64
ATT_GROUP = 4
IDX_HEADS = 8
IDX_DIM = 128
TOPK_MAX = 256
M_HEADS = 4
M_QK_DIM = 128
M_V_DIM = 256
GATE_CAP = 15.0
D_FF = 3584
N_EXPERTS = 8

LANES = 128
Q_BLOCK = 128
KEY_CHUNK = 256
M_CHUNK = 128
MOE_SLOT_BLOCK = 256
VMEM_LIMIT = 52 * 1024 * 1024
INT_MIN = -(2 ** 31)
INT_MAX = 2 ** 31 - 1
LOG2E = 1.4426950408889634

DSA_IN_PAD = 2816
MLSTM_IN_PAD = 3200


def _cparams(sem):
    return pltpu.CompilerParams(dimension_semantics=sem, vmem_limit_bytes=VMEM_LIMIT)


def _rms_rows(x, gain):
    ms = jnp.mean(x * x, axis=-1, keepdims=True)
    return x * lax.rsqrt(ms + RMS_EPS) * gain


def _dsa_in_kernel(h_ref, g_ref, w_ref, gq_ref, gk_ref, c64_ref, sa64_ref, sb64_ref, c128_ref, s128_ref,
                   smat_ref, q_ref, qsw_ref, k_ref, v_ref, iq_ref, ik_ref, iw_ref):
    xn = _rms_rows(h_ref[...], g_ref[...]).astype(BF16)
    z = jnp.dot(xn, w_ref[...], preferred_element_type=F32)
    c64, sa64, sb64 = c64_ref[...], sa64_ref[...], sb64_ref[...]
    c128, s128 = c128_ref[...], s128_ref[...]
    smat = smat_ref[...]

    def head_norm_rope(zc, gain):
        z2 = zc * zc
        hi = z2.astype(BF16)
        lo = (z2 - hi.astype(F32)).astype(BF16)
        ms = jnp.dot(hi, smat, preferred_element_type=F32) + jnp.dot(lo, smat, preferred_element_type=F32)
        y = zc * lax.rsqrt(ms + RMS_EPS) * gain
        return y * c64 + pltpu.roll(y, 96, 1) * sa64 + pltpu.roll(y, 32, 1) * sb64

    for c in range(8):
        r = head_norm_rope(z[:, c * LANES:(c + 1) * LANES], gq_ref[...]) * (ATT_HEAD_DIM ** -0.5 * LOG2E)
        q_ref[:, c * LANES:(c + 1) * LANES] = r.astype(BF16)
        qsw_ref[:, c * LANES:(c + 1) * LANES] = pltpu.roll(r, 64, 1).astype(BF16)
    for c in range(2):
        r = head_norm_rope(z[:, 1024 + c * LANES:1024 + (c + 1) * LANES], gk_ref[...])
        k_ref[:, c * LANES:(c + 1) * LANES] = r.astype(BF16)
    ones = jnp.ones((z.shape[0], LANES), BF16)
    for c in range(2):
        v_ref[:, 2 * c * LANES:(2 * c + 1) * LANES] = z[:, 1280 + c * LANES:1280 + (c + 1) * LANES].astype(BF16)
        v_ref[:, (2 * c + 1) * LANES:(2 * c + 2) * LANES] = ones
    for c in range(9):
        zc = z[:, 1536 + c * LANES:1536 + (c + 1) * LANES]
        r = (zc * c128 + pltpu.roll(zc, 64, 1) * s128).astype(BF16)
        if c < 8:
            iq_ref[:, c * LANES:(c + 1) * LANES] = r
        else:
            ik_ref[...] = r
    iw_ref[...] = z[:, 2688:2816] * (IDX_HEADS ** -0.5 * IDX_DIM ** -0.5)


def _dsa_in_proj(h, gain, w, gq, gk, tabs, seq_pad, tm):
    T = h.shape[0]
    nt = seq_pad // tm
    row = lambda i: (i, 0)
    fixed = lambda i: (0, 0)
    pos = lambda i: (i % nt, 0)
    tab_spec = pl.BlockSpec((tm, LANES), pos)
    smat = jnp.asarray(np.kron(np.eye(2), np.full((64, 64), 1.0 / 64)), BF16)
    outs = [(1024, BF16), (1024, BF16), (256, BF16), (512, BF16), (1024, BF16), (128, BF16), (128, F32)]
    return pl.pallas_call(
        _dsa_in_kernel,
        grid=(T // tm,),
        in_specs=[pl.BlockSpec((tm, D_MODEL), row), pl.BlockSpec((1, D_MODEL), fixed),
                  pl.BlockSpec((D_MODEL, DSA_IN_PAD), fixed), pl.BlockSpec((1, LANES), fixed),
                  pl.BlockSpec((1, LANES), fixed), tab_spec, tab_spec, tab_spec, tab_spec, tab_spec,
                  pl.BlockSpec((LANES, LANES), fixed)],
        out_specs=[pl.BlockSpec((tm, n), row) for n, _ in outs],
        out_shape=[jax.ShapeDtypeStruct((T, n), dt) for n, dt in outs],
        compiler_params=_cparams(("arbitrary",)),
        name="dsa_in_proj",
    )(h, gain, w, gq, gk, *tabs, smat)


def _dsa_core_kernel(q_ref, qsw_ref, k_ref, v_ref, iq_ref, ik_ref, iw_ref, o_ref,
                     iqs_sc, iwb_sc, qs_sc, keys_sc, bias_sc, thr_sc, cut_sc, m_sc, l_sc, acc_sc,
                     *, top_k, seq_pad):
    i = pl.program_id(1)
    QB, KC = Q_BLOCK, KEY_CHUNK
    nch = (i * QB + QB + KC - 1) // KC
    kf = float(top_k)
    lane1 = lax.broadcasted_iota(jnp.int32, (QB, LANES), 1)

    for h in range(IDX_HEADS):
        iqs_sc[h * QB:(h + 1) * QB, :] = iq_ref[0, :, h * LANES:(h + 1) * LANES]
        iwb_sc[h] = jnp.broadcast_to(iw_ref[0, :, h:h + 1], (QB, LANES))

    for g in range(ATT_KV_HEADS):
        for r in range(ATT_GROUP):
            h = g * ATT_GROUP + r
            src = q_ref if h % 2 == g % 2 else qsw_ref
            chunk = src[0, :, (h // 2) * LANES:(h // 2 + 1) * LANES].astype(F32)
            keep = (lane1 >= 64) if g % 2 else (lane1 < 64)
            qs_sc[g, r * QB:(r + 1) * QB, :] = jnp.where(keep, chunk, 0.0).astype(BF16)

    rows = i * QB + lax.broadcasted_iota(jnp.int32, (QB, KC), 0)
    lane = lax.broadcasted_iota(jnp.int32, (QB, KC), 1)

    def chunk_start(j):
        return pl.multiple_of(jnp.minimum(j * KC, seq_pad - KC), LANES)

    def key_valid(j, start):
        kidx = start + lane
        return (kidx <= rows) & (kidx >= j * KC)

    def score_body(j, carry):
        start = chunk_start(j)
        ikj = ik_ref[0, pl.ds(start, KC), :]
        s = lax.dot_general(iqs_sc[...], ikj, (((1,), (1,)), ((), ())), preferred_element_type=F32)
        acc = jnp.zeros((QB, KC), F32)
        for h in range(IDX_HEADS):
            w = iwb_sc[h]
            acc = acc + jnp.maximum(s[h * QB:(h + 1) * QB, :], 0.0) * jnp.concatenate([w, w], axis=1)
        bits = pltpu.bitcast(acc, jnp.int32)
        key = jnp.where(bits >= 0, bits, bits ^ jnp.int32(INT_MAX))
        keys_sc[j] = jnp.where(key_valid(j, start), key, jnp.int32(INT_MIN))
        return carry

    lax.fori_loop(0, nch, score_body, 0)

    def count(pred):
        def body(j, acc):
            kk = keys_sc[j]
            base = chunk_start(j)
            return (acc + jnp.where(pred(kk[:, :LANES], base), 1.0, 0.0)
                    + jnp.where(pred(kk[:, LANES:], base + LANES), 1.0, 0.0))
        acc = lax.fori_loop(0, nch, body, jnp.zeros((QB, LANES), F32))
        return jnp.sum(acc, axis=1, keepdims=True)

    zero = jnp.zeros((QB, LANES), jnp.int32)
    v0 = jnp.where(count(lambda kk, base: kk >= zero) >= kf, zero, jnp.int32(INT_MIN))

    def bit_body(b, v):
        cand = v | lax.shift_left(jnp.int32(1), (30 - b).astype(jnp.int32))
        return jnp.where(count(lambda kk, base: kk >= cand) >= kf, cand, v)

    thr = lax.fori_loop(jnp.int32(0), jnp.int32(31), bit_body, v0)
    thr_sc[...] = thr
    cut_sc[...] = jnp.full((QB, LANES), INT_MAX, jnp.int32)
    n_ge = count(lambda kk, base: kk >= thr)

    @pl.when(jnp.max(n_ge) > kf)
    def _():
        tb = thr_sc[...]
        need = kf - count(lambda kk, base: kk > tb)

        def cut_body(b, c):
            cand = c | lax.shift_left(jnp.int32(1), (12 - b).astype(jnp.int32))
            n = count(lambda kk, base: jnp.where(kk == tb, base + lane1, jnp.int32(INT_MAX)) < cand)
            return jnp.where(n < need, cand, c)

        cut_sc[...] = lax.fori_loop(jnp.int32(0), jnp.int32(13), cut_body, zero)

    def bias_body(j, carry):
        kk = keys_sc[j]
        start = chunk_start(j)
        tb = jnp.concatenate([thr_sc[...]] * 2, axis=1)
        cb = jnp.concatenate([cut_sc[...]] * 2, axis=1)
        sel = (kk > tb) | ((kk == tb) & (start + lane <= cb))
        bias_sc[j] = jnp.where(sel & key_valid(j, start), 0.0, -1e30)
        return carry

    lax.fori_loop(0, nch, bias_body, 0)

    m_sc[...] = jnp.full(m_sc.shape, -1e9, F32)
    l_sc[...] = jnp.zeros(l_sc.shape, F32)
    acc_sc[...] = jnp.zeros(acc_sc.shape, F32)

    def att_body(j, carry):
        start = chunk_start(j)
        bias = jnp.concatenate([bias_sc[j]] * ATT_GROUP, axis=0)
        for g in range(ATT_KV_HEADS):
            cg = g // 2
            kj = k_ref[0, pl.ds(start, KC), cg * LANES:(cg + 1) * LANES]
            vj = v_ref[0, pl.ds(start, KC), cg * 2 * LANES:(cg + 1) * 2 * LANES]
            s = lax.dot_general(qs_sc[g], kj, (((1,), (1,)), ((), ())), preferred_element_type=F32) + bias
            m_old = m_sc[g]
            m_new = jnp.maximum(m_old, jnp.max(s, axis=1, keepdims=True))
            alpha = jnp.exp2(m_old - m_new)
            p = jnp.exp2(s - jnp.concatenate([m_new, m_new], axis=1)).astype(BF16)
            pv = jnp.dot(p, vj, preferred_element_type=F32)
            l_sc[g] = alpha * l_sc[g] + pv[:, LANES:]
            acc_sc[g] = alpha * acc_sc[g] + pv[:, :LANES]
            m_sc[g] = m_new
        return carry

    lax.fori_loop(0, nch, att_body, 0)

    for g in range(ATT_KV_HEADS):
        og = acc_sc[g] / l_sc[g]
        for pair in range(2):
            even = og[(2 * pair) * QB:(2 * pair + 1) * QB]
            odd = og[(2 * pair + 1) * QB:(2 * pair + 2) * QB]
            if g % 2:
                even = pltpu.roll(even, 64, 1)
            else:
                odd = pltpu.roll(odd, 64, 1)
            c = 2 * g + pair
            o_ref[0, :, c * LANES:(c + 1) * LANES] = jnp.where(lane1 < 64, even, odd).astype(BF16)


def _dsa_core(q, qsw, k, v, iq, ik, iw, top_k):
    B, seq_pad = q.shape[0], q.shape[1]
    nqb = seq_pad // Q_BLOCK
    nkc = -(-seq_pad // KEY_CHUNK)
    rows = ATT_GROUP * Q_BLOCK
    qblock = lambda n: pl.BlockSpec((1, Q_BLOCK, n), lambda b, i: (b, i, 0))
    whole = lambda n: pl.BlockSpec((1, seq_pad, n), lambda b, i: (b, 0, 0))
    return pl.pallas_call(
        functools.partial(_dsa_core_kernel, top_k=top_k, seq_pad=seq_pad),
        grid=(B, nqb),
        in_specs=[qblock(D_MODEL), qblock(D_MODEL), whole(256), whole(512),
                  qblock(IDX_HEADS * IDX_DIM), whole(IDX_DIM), qblock(LANES)],
        out_specs=qblock(D_MODEL),
        out_shape=jax.ShapeDtypeStruct((B, seq_pad, D_MODEL), BF16),
        scratch_shapes=[
            pltpu.VMEM((IDX_HEADS * Q_BLOCK, IDX_DIM), BF16),
            pltpu.VMEM((IDX_HEADS, Q_BLOCK, LANES), F32),
            pltpu.VMEM((ATT_KV_HEADS, rows, LANES), BF16),
            pltpu.VMEM((nkc, Q_BLOCK, KEY_CHUNK), jnp.int32),
            pltpu.VMEM((nkc, Q_BLOCK, KEY_CHUNK), F32),
            pltpu.VMEM((Q_BLOCK, LANES), jnp.int32),
            pltpu.VMEM((Q_BLOCK, LANES), jnp.int32),
            pltpu.VMEM((ATT_KV_HEADS, rows, LANES), F32),
            pltpu.VMEM((ATT_KV_HEADS, rows, LANES), F32),
            pltpu.VMEM((ATT_KV_HEADS, rows, LANES), F32),
        ],
        compiler_params=_cparams(("arbitrary", "arbitrary")),
        name="dsa_core",
    )(q, qsw, k, v, iq, ik, iw)


def _out_proj_kernel(h_ref, y_ref, w_ref, o_ref):
    o_ref[...] = h_ref[...] + jnp.dot(y_ref[...], w_ref[...], preferred_element_type=F32)


def _out_proj(h, y, w, tm):
    T = h.shape[0]
    row = lambda i: (i, 0)
    return pl.pallas_call(
        _out_proj_kernel,
        grid=(T // tm,),
        in_specs=[pl.BlockSpec((tm, D_MODEL), row), pl.BlockSpec((tm, D_MODEL), row),
                  pl.BlockSpec((D_MODEL, D_MODEL), lambda i: (0, 0))],
        out_specs=pl.BlockSpec((tm, D_MODEL), row),
        out_shape=jax.ShapeDtypeStruct((T, D_MODEL), F32),
        compiler_params=_cparams(("arbitrary",)),
        name="out_proj",
    )(h, y, w)


def _mlstm_in_kernel(h_ref, g_ref, w_ref, q_ref, k_ref, v_ref, o_ref, gate_ref):
    xn = _rms_rows(h_ref[...], g_ref[...]).astype(BF16)
    z = jnp.dot(xn, w_ref[...], preferred_element_type=F32)
    q_ref[...] = z[:, 0:512].astype(BF16)
    k_ref[...] = (z[:, 512:1024] * (M_QK_DIM ** -0.5)).astype(BF16)
    v_ref[...] = z[:, 1024:2048].astype(BF16)
    o_ref[...] = z[:, 2048:3072]
    gate_ref[...] = z[:, 3072:3200]


def _mlstm_in_proj(h, gain, w, tm):
    T = h.shape[0]
    row = lambda i: (i, 0)
    fixed = lambda i: (0, 0)
    outs = [(512, BF16), (512, BF16), (1024, BF16), (1024, F32), (128, F32)]
    return pl.pallas_call(
        _mlstm_in_kernel,
        grid=(T // tm,),
        in_specs=[pl.BlockSpec((tm, D_MODEL), row), pl.BlockSpec((1, D_MODEL), fixed),
                  pl.BlockSpec((D_MODEL, MLSTM_IN_PAD), fixed)],
        out_specs=[pl.BlockSpec((tm, n), row) for n, _ in outs],
        out_shape=[jax.ShapeDtypeStruct((T, n), dt) for n, dt in outs],
        compiler_params=_cparams(("arbitrary",)),
        name="mlstm_in_proj",
    )(h, gain, w)


def _mlstm_kernel(q_ref, k_ref, v_ref, o_ref, gate_ref, bias_ref, gout_ref, y_ref, ct_sc, n_sc, m_sc):
    C = M_CHUNK

    @pl.when(pl.program_id(1) == 0)
    def _():
        ct_sc[...] = jnp.zeros(ct_sc.shape, F32)
        n_sc[...] = jnp.zeros(n_sc.shape, F32)
        m_sc[...] = jnp.zeros(m_sc.shape, F32)

    pre = gate_ref[...].T[0:8, :] + bias_ref[...]
    capped = GATE_CAP * jnp.tanh(pre / GATE_CAP)
    log_f = -(jnp.maximum(-capped, 0.0) + jnp.log1p(jnp.exp(-jnp.abs(capped))))
    lane8 = lax.broadcasted_iota(jnp.int32, (8, C), 1)
    b = log_f
    sh = 1
    while sh < C:
        b = b + jnp.where(lane8 >= sh, pltpu.roll(b, sh, 1), 0.0)
        sh *= 2
    stacked = jnp.concatenate([b[4:8], capped[0:4]], axis=0)
    cols = jnp.concatenate([stacked, jnp.zeros((C - 8, C), F32)], axis=0).T

    t_idx = lax.broadcasted_iota(jnp.int32, (C, C), 0)
    s_idx = lax.broadcasted_iota(jnp.int32, (C, C), 1)
    for h in range(M_HEADS):
        qh = q_ref[:, h * M_QK_DIM:(h + 1) * M_QK_DIM]
        kh = k_ref[:, h * M_QK_DIM:(h + 1) * M_QK_DIM]
        vh = v_ref[:, h * M_V_DIM:(h + 1) * M_V_DIM]
        b_row, li_row = stacked[h:h + 1, :], stacked[4 + h:5 + h, :]
        b_col, li_col = cols[:, h:h + 1], cols[:, 4 + h:5 + h]
        m_st = m_sc[h:h + 1, 0:1]
        dmat = jnp.where(s_idx <= t_idx, b_col - b_row + li_row, -jnp.inf)
        inter = b_col + m_st
        m_t = jnp.maximum(inter, jnp.max(dmat, axis=1, keepdims=True))
        w_inter = jnp.exp(inter - m_t)
        qk = lax.dot_general(qh, kh, (((1,), (1,)), ((), ())), preferred_element_type=F32)
        s = qk * jnp.exp(dmat - m_t)
        ct = ct_sc[h]
        num = (w_inter * jnp.dot(qh, ct.astype(BF16), preferred_element_type=F32)
               + jnp.dot(s.astype(BF16), vh, preferred_element_type=F32))
        qn = jnp.sum(qh.astype(F32) * n_sc[h:h + 1, :], axis=1, keepdims=True)
        den = w_inter * qn + jnp.sum(s, axis=1, keepdims=True)
        hout = num / jnp.maximum(jnp.abs(den), jnp.exp(-m_t))

        b_last = b_row[:, C - 1:C]
        m_new = jnp.maximum(b_last + m_st, jnp.max(b_last - b_row + li_row, axis=1, keepdims=True))
        decay = jnp.exp(b_last + m_st - m_new)
        wk = jnp.exp(b_last - b_col + li_col - m_new)
        kw = kh.astype(F32) * wk
        ct_sc[h] = decay * ct + jnp.dot(kw.T.astype(BF16), vh, preferred_element_type=F32)
        n_sc[h:h + 1, :] = decay * n_sc[h:h + 1, :] + jnp.sum(kw, axis=0, keepdims=True)
        m_sc[h:h + 1, :] = jnp.broadcast_to(m_new, (1, LANES))

        cs = slice(h * M_V_DIM, (h + 1) * M_V_DIM)
        hn = _rms_rows(hout, gout_ref[:, cs])
        y_ref[:, cs] = (hn * jax.nn.sigmoid(o_ref[:, cs])).astype(BF16)


def _mlstm_core(q, k, v, o, gates, bias8, gout, batch):
    T = q.shape[0]
    nc = T // batch // M_CHUNK
    row = lambda b, c: (b * nc + c, 0)
    fixed = lambda b, c: (0, 0)
    C = M_CHUNK
    return pl.pallas_call(
        _mlstm_kernel,
        grid=(batch, nc),
        in_specs=[pl.BlockSpec((C, 512), row), pl.BlockSpec((C, 512), row), pl.BlockSpec((C, 1024), row),
                  pl.BlockSpec((C, 1024), row), pl.BlockSpec((C, LANES), row),
                  pl.BlockSpec((8, 1), fixed), pl.BlockSpec((1, D_MODEL), fixed)],
        out_specs=pl.BlockSpec((C, D_MODEL), row),
        out_shape=jax.ShapeDtypeStruct((T, D_MODEL), BF16),
        scratch_shapes=[pltpu.VMEM((M_HEADS, M_QK_DIM, M_V_DIM), F32),
                        pltpu.VMEM((8, M_QK_DIM), F32),
                        pltpu.VMEM((8, LANES), F32)],
        compiler_params=_cparams(("arbitrary", "arbitrary")),
        name="mlstm_core",
    )(q, k, v, o, gates, bias8, gout)


def _ffn_kernel(h_ref, g_ref, wg_ref, wu_ref, wd_ref, o_ref, xn_sc, acc_sc):
    f = pl.program_id(1)

    @pl.when(f == 0)
    def _():
        xn_sc[...] = _rms_rows(h_ref[...], g_ref[...]).astype(BF16)
        acc_sc[...] = h_ref[...]

    xn = xn_sc[...]
    gate = jnp.dot(xn, wg_ref[...], preferred_element_type=F32)
    up = jnp.dot(xn, wu_ref[...], preferred_element_type=F32)
    act = (gate * jax.nn.sigmoid(gate) * up).astype(BF16)
    acc_sc[...] += jnp.dot(act, wd_ref[...], preferred_element_type=F32)

    @pl.when(f == pl.num_programs(1) - 1)
    def _():
        o_ref[...] = acc_sc[...]


def _ffn(h, gain, wg, wu, wd, tm, tf):
    T = h.shape[0]
    return pl.pallas_call(
        _ffn_kernel,
        grid=(T // tm, D_FF // tf),
        in_specs=[pl.BlockSpec((tm, D_MODEL), lambda i, f: (i, 0)),
                  pl.BlockSpec((1, D_MODEL), lambda i, f: (0, 0)),
                  pl.BlockSpec((D_MODEL, tf), lambda i, f: (0, f)),
                  pl.BlockSpec((D_MODEL, tf), lambda i, f: (0, f)),
                  pl.BlockSpec((tf, D_MODEL), lambda i, f: (f, 0))],
        out_specs=pl.BlockSpec((tm, D_MODEL), lambda i, f: (i, 0)),
        out_shape=jax.ShapeDtypeStruct((T, D_MODEL), F32),
        scratch_shapes=[pltpu.VMEM((tm, D_MODEL), BF16), pltpu.VMEM((tm, D_MODEL), F32)],
        compiler_params=_cparams(("arbitrary", "arbitrary")),
        name="ffn_dense",
    )(h, gain, wg, wu, wd)


def _moe_kernel(h_ref, g_ref, wr_ref, wg_ref, wu_ref, wd_ref, o_ref,
                xn_sc, xs_sc, ys_sc, posc_sc, posr_sc, comb_sc, cnt_sc):
    e = pl.program_id(1)
    f = pl.program_id(2)
    tm = h_ref.shape[0]
    SB = min(MOE_SLOT_BLOCK, tm)
    lane = lax.broadcasted_iota(jnp.int32, (tm, LANES), 1)

    @pl.when((e == 0) & (f == 0))
    def _():
        x = h_ref[...]
        xn = _rms_rows(x, g_ref[...])
        xn_sc[...] = xn.astype(BF16)
        o_ref[...] = x
        logits = jnp.dot(xn, wr_ref[...], preferred_element_type=F32, precision=lax.Precision.HIGHEST)
        logits = jnp.where(lane < N_EXPERTS, logits, -jnp.inf)
        m1 = jnp.max(logits, axis=1, keepdims=True)
        i1 = jnp.min(jnp.where(logits == m1, lane, LANES), axis=1, keepdims=True)
        rest = jnp.where(lane == i1, -jnp.inf, logits)
        m2 = jnp.max(rest, axis=1, keepdims=True)
        i2 = jnp.min(jnp.where(rest == m2, lane, LANES), axis=1, keepdims=True)
        e2 = jnp.exp(m2 - m1)
        g1 = 1.0 / (1.0 + e2)
        comb_sc[...] = jnp.where(lane == i1, g1, jnp.where(lane == i2, e2 * g1, 0.0))
        member = jnp.where(lane == i1, 1.0, jnp.where(lane == i2, 1.0, 0.0))
        tri = jnp.where(lax.broadcasted_iota(jnp.int32, (SB, SB), 0) >= lax.broadcasted_iota(jnp.int32, (SB, SB), 1),
                        1.0, 0.0).astype(BF16)
        carry = jnp.zeros((1, LANES), F32)
        for b in range(tm // SB):
            mb = member[b * SB:(b + 1) * SB]
            incl = jnp.dot(tri, mb.astype(BF16), preferred_element_type=F32)
            posc_sc[b * SB:(b + 1) * SB, :] = jnp.where(mb > 0.0, incl - 1.0 + carry, -1.0)
            carry = carry + incl[SB - 1:SB, :]
        cnt_sc[...] = carry
        for b in range(tm // LANES):
            posr_sc[:, b * LANES:(b + 1) * LANES] = posc_sc[b * LANES:(b + 1) * LANES, :].T

    lane_row = lax.broadcasted_iota(jnp.int32, (1, LANES), 1)
    count = jnp.sum(jnp.where(lane_row == e, cnt_sc[...], 0.0))
    nblk = ((count + (SB - 1.0)) * (1.0 / SB)).astype(jnp.int32)

    @pl.when(f == 0)
    def _():
        slots_of_tokens = posr_sc[pl.ds(e, 1), :]

        def body(r, carry):
            row0 = pl.multiple_of(r * SB, SB)
            slot = (r * SB + lax.broadcasted_iota(jnp.int32, (SB, tm), 0)).astype(F32)
            onehot = jnp.where(slots_of_tokens == slot, 1.0, 0.0).astype(BF16)
            xs_sc[pl.ds(row0, SB), :] = jnp.dot(onehot, xn_sc[...], preferred_element_type=F32).astype(BF16)
            ys_sc[pl.ds(row0, SB), :] = jnp.zeros((SB, D_MODEL), F32)
            return carry

        lax.fori_loop(0, nblk, body, 0)

    def ffn_body(r, carry):
        row0 = pl.multiple_of(r * SB, SB)
        xs = xs_sc[pl.ds(row0, SB), :]
        gate = jnp.dot(xs, wg_ref[0], preferred_element_type=F32)
        up = jnp.dot(xs, wu_ref[0], preferred_element_type=F32)
        act = (gate * jax.nn.sigmoid(gate) * up).astype(BF16)
        ys_sc[pl.ds(row0, SB), :] += jnp.dot(act, wd_ref[0], preferred_element_type=F32)
        return carry

    lax.fori_loop(0, nblk, ffn_body, 0)

    @pl.when(f == pl.num_programs(2) - 1)
    def _():
        on_e = lane == e
        gate_e = jnp.sum(jnp.where(on_e, comb_sc[...], 0.0), axis=1, keepdims=True)
        slot_e = jnp.sum(jnp.where(on_e, posc_sc[...], 0.0), axis=1, keepdims=True)

        def body(r, carry):
            row0 = pl.multiple_of(r * SB, SB)
            ys = ys_sc[pl.ds(row0, SB), :].astype(BF16)
            slot = (r * SB + lax.broadcasted_iota(jnp.int32, (SB, SB), 1)).astype(F32)
            for tb in range(tm // SB):
                rows = slice(tb * SB, (tb + 1) * SB)
                onehot = jnp.where(slot_e[rows] == slot, 1.0, 0.0).astype(BF16)
                o_ref[rows, :] += gate_e[rows] * jnp.dot(onehot, ys, preferred_element_type=F32)
            return carry

        lax.fori_loop(0, nblk, body, 0)


def _moe(h, gain, wr, wg, wu, wd, tm, tf):
    T = h.shape[0]
    return pl.pallas_call(
        _moe_kernel,
        grid=(T // tm, N_EXPERTS, D_FF // tf),
        in_specs=[pl.BlockSpec((tm, D_MODEL), lambda i, e, f: (i, 0), pipeline_mode=pl.Buffered(1)),
                  pl.BlockSpec((1, D_MODEL), lambda i, e, f: (0, 0)),
                  pl.BlockSpec((D_MODEL, LANES), lambda i, e, f: (0, 0)),
                  pl.BlockSpec((1, D_MODEL, tf), lambda i, e, f: (e, 0, f)),
                  pl.BlockSpec((1, D_MODEL, tf), lambda i, e, f: (e, 0, f)),
                  pl.BlockSpec((1, tf, D_MODEL), lambda i, e, f: (e, f, 0))],
        out_specs=pl.BlockSpec((tm, D_MODEL), lambda i, e, f: (i, 0)),
        out_shape=jax.ShapeDtypeStruct((T, D_MODEL), F32),
        scratch_shapes=[pltpu.VMEM((tm, D_MODEL), BF16), pltpu.VMEM((tm, D_MODEL), BF16),
                        pltpu.VMEM((tm, D_MODEL), F32), pltpu.VMEM((tm, LANES), F32),
                        pltpu.VMEM((LANES, tm), F32), pltpu.VMEM((tm, LANES), F32),
                        pltpu.VMEM((1, LANES), F32)],
        compiler_params=_cparams(("arbitrary", "arbitrary", "arbitrary")),
        name="moe_sparse",
    )(h, gain, wr, wg, wu, wd)


def _rope_tables(seq_pad):
    pos = jnp.arange(seq_pad, dtype=F32)[:, None]
    lane = np.arange(LANES)

    def table(head_dim):
        half = head_dim // 2
        inv = ROPE_THETA ** (-jnp.arange(half, dtype=F32) / half)
        d = lane % head_dim
        ang = pos * inv[d % half][None, :]
        return jnp.cos(ang), jnp.sin(ang), jnp.asarray(d < half)[None, :]

    c64, s64, lo64 = table(ATT_HEAD_DIM)
    c128, s128, lo128 = table(IDX_DIM)
    return (c64, jnp.where(lo64, -s64, 0.0), jnp.where(lo64, 0.0, s64), c128, jnp.where(lo128, -s128, s128))


def _pad_cols(w, n):
    return jnp.pad(w, ((0, 0), (0, n - w.shape[1])))


def kernel(x, meta, norm_mixer, norm_ffn, dsa_w_in, dsa_q_norm, dsa_k_norm, dsa_w_out, mlstm_w_in, mlstm_b_i,
           mlstm_b_f, mlstm_out_norm, mlstm_w_out, ffn_w_gate, ffn_w_up, ffn_w_down, moe_router, moe_w_gate,
           moe_w_up, moe_w_down):
    B, S, D = x.shape
    L = S + N_META
    top_k = min(TOPK_MAX, S // 4)
    seq_pad = max(-(-L // Q_BLOCK) * Q_BLOCK, KEY_CHUNK)
    T = B * seq_pad
    tm_proj = 384 if seq_pad % 384 == 0 else Q_BLOCK
    tm_ffn = 768 if T % 768 == 0 else Q_BLOCK
    tm_moe = 1536 if T % 1536 == 0 else Q_BLOCK
    tf = 512

    h = jnp.concatenate([jnp.broadcast_to(meta[None].astype(x.dtype), (B, N_META, D)), x,
                         jnp.zeros((B, seq_pad - L, D), x.dtype)], axis=1).reshape(T, D)
    tabs = _rope_tables(seq_pad)
    depth = norm_mixer.shape[0]

    for i in range(depth):
        j = i // 2
        gain_m = norm_mixer[i][None, :]
        gain_f = norm_ffn[i][None, :]
        if i % 2 == 0:
            w_in = _pad_cols(dsa_w_in[j], DSA_IN_PAD).astype(BF16)
            gq = jnp.tile(dsa_q_norm[j], 2)[None, :]
            gk = jnp.tile(dsa_k_norm[j], 2)[None, :]
            proj = _dsa_in_proj(h, gain_m, w_in, gq, gk, tabs, seq_pad, tm_proj)
            att = _dsa_core(*[a.reshape(B, seq_pad, a.shape[-1]) for a in proj], top_k)
            h = _out_proj(h, att.reshape(T, D), dsa_w_out[j].astype(BF16), tm_proj)
            h = _ffn(h, gain_f, ffn_w_gate[j].astype(BF16), ffn_w_up[j].astype(BF16),
                     ffn_w_down[j].astype(BF16), tm_ffn, tf)
        else:
            w_in = _pad_cols(mlstm_w_in[j], MLSTM_IN_PAD).astype(BF16)
            q, k, v, o, gates = _mlstm_in_proj(h, gain_m, w_in, tm_proj)
            bias8 = jnp.concatenate([mlstm_b_i[j], mlstm_b_f[j]])[:, None]
            y = _mlstm_core(q, k, v, o, gates, bias8, mlstm_out_norm[j][None, :], B)
            h = _out_proj(h, y, mlstm_w_out[j].astype(BF16), tm_proj)
            wr = _pad_cols(moe_router[j], LANES)
            h = _moe(h, gain_f, wr, moe_w_gate[j].astype(BF16), moe_w_up[j].astype(BF16),
                     moe_w_down[j].astype(BF16), tm_moe, tf)
    return h.reshape(B, seq_pad, D)[:, N_META:L]
```

```python
import functools

import jax
import jax.numpy as jnp
import numpy as np
from jax import lax
from jax.experimental import pallas as pl
from jax.experimental.pallas import tpu as pltpu

F32 = jnp.float32
BF16 = jnp.bfloat16

D_MODEL = 1024
N_META = 16
RMS_EPS = 1e-6
ROPE_THETA = 10000.0
ATT_HEADS = 16
ATT_KV_HEADS = 4
ATT_HEAD_DIM = 64
ATT_GROUP = 4
IDX_HEADS = 8
IDX_DIM = 128
TOPK_MAX = 256
M_HEADS = 4
M_QK_DIM = 128
M_V_DIM = 256
GATE_CAP = 15.0
D_FF = 3584
N_EXPERTS = 8

LANES = 128
Q_BLOCK = 128
KEY_CHUNK = 256
M_CHUNK = 128
MOE_SLOT_BLOCK = 224
MOE_TOKEN_BLOCK = 256
VMEM_LIMIT = 52 * 1024 * 1024
INT_MIN = -(2 ** 31)
INT_MAX = 2 ** 31 - 1
LOG2E = 1.4426950408889634

DSA_IN_PAD = 2816
MLSTM_IN_PAD = 3200


def _cparams(sem):
    return pltpu.CompilerParams(dimension_semantics=sem, vmem_limit_bytes=VMEM_LIMIT)


def _rms_rows(x, gain):
    ms = jnp.mean(x * x, axis=-1, keepdims=True)
    return x * lax.rsqrt(ms + RMS_EPS) * gain


def _dsa_in_kernel(h_ref, g_ref, w_ref, gq_ref, gk_ref, c64_ref, sa64_ref, sb64_ref, c128_ref, s128_ref,
                   smat_ref, q_ref, qsw_ref, k_ref, v_ref, iq_ref, ik_ref, iw_ref):
    xn = _rms_rows(h_ref[...], g_ref[...]).astype(BF16)
    z = jnp.dot(xn, w_ref[...], preferred_element_type=F32)
    c64, sa64, sb64 = c64_ref[...], sa64_ref[...], sb64_ref[...]
    c128, s128 = c128_ref[...], s128_ref[...]
    smat = smat_ref[...]

    def head_norm_rope(zc, gain):
        z2 = zc * zc
        hi = z2.astype(BF16)
        lo = (z2 - hi.astype(F32)).astype(BF16)
        ms = jnp.dot(hi, smat, preferred_element_type=F32) + jnp.dot(lo, smat, preferred_element_type=F32)
        y = zc * lax.rsqrt(ms + RMS_EPS) * gain
        return y * c64 + pltpu.roll(y, 96, 1) * sa64 + pltpu.roll(y, 32, 1) * sb64

    for c in range(8):
        r = head_norm_rope(z[:, c * LANES:(c + 1) * LANES], gq_ref[...]) * (ATT_HEAD_DIM ** -0.5 * LOG2E)
        q_ref[:, c * LANES:(c + 1) * LANES] = r.astype(BF16)
        qsw_ref[:, c * LANES:(c + 1) * LANES] = pltpu.roll(r, 64, 1).astype(BF16)
    for c in range(2):
        r = head_norm_rope(z[:, 1024 + c * LANES:1024 + (c + 1) * LANES], gk_ref[...])
        k_ref[:, c * LANES:(c + 1) * LANES] = r.astype(BF16)
    ones = jnp.ones((z.shape[0], LANES), BF16)
    for c in range(2):
        v_ref[:, 2 * c * LANES:(2 * c + 1) * LANES] = z[:, 1280 + c * LANES:1280 + (c + 1) * LANES].astype(BF16)
        v_ref[:, (2 * c + 1) * LANES:(2 * c + 2) * LANES] = ones
    for c in range(9):
        zc = z[:, 1536 + c * LANES:1536 + (c + 1) * LANES]
        r = (zc * c128 + pltpu.roll(zc, 64, 1) * s128).astype(BF16)
        if c < 8:
            iq_ref[:, c * LANES:(c + 1) * LANES] = r
        else:
            ik_ref[...] = r
    iw_ref[...] = z[:, 2688:2816] * (IDX_HEADS ** -0.5 * IDX_DIM ** -0.5)


def _dsa_in_proj(h, gain, w, gq, gk, tabs, seq_pad, tm):
    T = h.shape[0]
    nt = seq_pad // tm
    row = lambda i: (i, 0)
    fixed = lambda i: (0, 0)
    pos = lambda i: (i % nt, 0)
    tab_spec = pl.BlockSpec((tm, LANES), pos)
    smat = jnp.asarray(np.kron(np.eye(2), np.full((64, 64), 1.0 / 64)), BF16)
    outs = [(1024, BF16), (1024, BF16), (256, BF16), (512, BF16), (1024, BF16), (128, BF16), (128, F32)]
    return pl.pallas_call(
        _dsa_in_kernel,
        grid=(T // tm,),
        in_specs=[pl.BlockSpec((tm, D_MODEL), row), pl.BlockSpec((1, D_MODEL), fixed),
                  pl.BlockSpec((D_MODEL, DSA_IN_PAD), fixed), pl.BlockSpec((1, LANES), fixed),
                  pl.BlockSpec((1, LANES), fixed), tab_spec, tab_spec, tab_spec, tab_spec, tab_spec,
                  pl.BlockSpec((LANES, LANES), fixed)],
        out_specs=[pl.BlockSpec((tm, n), row) for n, _ in outs],
        out_shape=[jax.ShapeDtypeStruct((T, n), dt) for n, dt in outs],
        compiler_params=_cparams(("arbitrary",)),
        name="dsa_in_proj",
    )(h, gain, w, gq, gk, *tabs, smat)


def _dsa_core_kernel(q_ref, qsw_ref, k_ref, v_ref, iq_ref, ik_ref, iw_ref, o_ref,
                     iqt_sc, iwt_sc, qs_sc, keys_sc, bias_sc, cut_sc, m_sc, l_sc, acc_sc,
                     *, top_k, seq_pad):
    i = pl.program_id(1)
    QB, KC = Q_BLOCK, KEY_CHUNK
    nch = (i * QB + QB + KC - 1) // KC
    kf = float(top_k)
    lane1 = lax.broadcasted_iota(jnp.int32, (QB, LANES), 1)

    for h in range(IDX_HEADS):
        iqt_sc[:, h * QB:(h + 1) * QB] = iq_ref[0, :, h * LANES:(h + 1) * LANES].astype(F32).T.astype(BF16)
    iwt_sc[...] = iw_ref[0].T

    for g in range(ATT_KV_HEADS):
        for r in range(ATT_GROUP):
            h = g * ATT_GROUP + r
            src = q_ref if h % 2 == g % 2 else qsw_ref
            chunk = src[0, :, (h // 2) * LANES:(h // 2 + 1) * LANES].astype(F32)
            keep = (lane1 >= 64) if g % 2 else (lane1 < 64)
            qs_sc[g, r * QB:(r + 1) * QB, :] = jnp.where(keep, chunk, 0.0).astype(BF16)

    qpos = i * QB + lax.broadcasted_iota(jnp.int32, (KC, QB), 1)
    krow = lax.broadcasted_iota(jnp.int32, (KC, QB), 0)

    def chunk_start(j):
        return pl.multiple_of(jnp.minimum(j * KC, seq_pad - KC), LANES)

    def key_valid(j, start):
        kidx = start + krow
        return (kidx <= qpos) & (kidx >= j * KC)

    def score_body(j, carry):
        start = chunk_start(j)
        s = jnp.dot(ik_ref[0, pl.ds(start, KC), :], iqt_sc[...], preferred_element_type=F32)
        acc = jnp.zeros((KC, QB), F32)
        for h in range(IDX_HEADS):
            acc = acc + jnp.maximum(s[:, h * QB:(h + 1) * QB], 0.0) * iwt_sc[h:h + 1, :]
        bits = pltpu.bitcast(acc, jnp.int32)
        key = jnp.where(bits >= 0, bits, bits ^ jnp.int32(INT_MAX))
        keys_sc[j] = jnp.where(key_valid(j, start), key, jnp.int32(INT_MIN))
        return carry

    lax.fori_loop(0, nch, score_body, 0)

    def count(pred):
        def body(j, acc):
            return acc + jnp.where(pred(keys_sc[j], chunk_start(j)), 1.0, 0.0)
        acc = lax.fori_loop(0, nch, body, jnp.zeros((KC, QB), F32))
        return jnp.sum(acc, axis=0, keepdims=True)

    zero = jnp.zeros((1, QB), jnp.int32)
    v0 = jnp.where(count(lambda kk, base: kk >= zero) >= kf, zero, jnp.int32(INT_MIN))

    def bit_body(b, v):
        cand = v | lax.shift_left(jnp.int32(1), (30 - b).astype(jnp.int32))
        return jnp.where(count(lambda kk, base: kk >= cand) >= kf, cand, v)

    thr = lax.fori_loop(jnp.int32(0), jnp.int32(31), bit_body, v0)
    cut_sc[...] = jnp.full((1, QB), INT_MAX, jnp.int32)
    n_ge = count(lambda kk, base: kk >= thr)

    @pl.when(jnp.max(n_ge) > kf)
    def _():
        need = kf - count(lambda kk, base: kk > thr)

        def cut_body(b, c):
            cand = c | lax.shift_left(jnp.int32(1), (12 - b).astype(jnp.int32))
            n = count(lambda kk, base: jnp.where(kk == thr, base + krow, jnp.int32(INT_MAX)) < cand)
            return jnp.where(n < need, cand, c)

        cut_sc[...] = lax.fori_loop(jnp.int32(0), jnp.int32(13), cut_body, zero)

    cut = cut_sc[...]

    def bias_body(j, carry):
        kk = keys_sc[j]
        start = chunk_start(j)
        sel = (kk > thr) | ((kk == thr) & (start + krow <= cut))
        bt = jnp.where(sel & key_valid(j, start), 0.0, -1e30)
        bias_sc[j] = jnp.concatenate([bt[c * QB:(c + 1) * QB].T for c in range(KC // QB)], axis=1)
        return carry

    lax.fori_loop(0, nch, bias_body, 0)

    m_sc[...] = jnp.full(m_sc.shape, -1e9, F32)
    l_sc[...] = jnp.zeros(l_sc.shape, F32)
    acc_sc[...] = jnp.zeros(acc_sc.shape, F32)

    def att_body(j, carry):
        start = chunk_start(j)
        bias = jnp.concatenate([bias_sc[j]] * ATT_GROUP, axis=0)
        for g in range(ATT_KV_HEADS):
            cg = g // 2
            kj = k_ref[0, pl.ds(start, KC), cg * LANES:(cg + 1) * LANES]
            vj = v_ref[0, pl.ds(start, KC), cg * 2 * LANES:(cg + 1) * 2 * LANES]
            s = lax.dot_general(qs_sc[g], kj, (((1,), (1,)), ((), ())), preferred_element_type=F32) + bias
            m_old = m_sc[g]
            m_new = jnp.maximum(m_old, jnp.max(s, axis=1, keepdims=True))
            alpha = jnp.exp2(m_old - m_new)
            p = jnp.exp2(s - jnp.concatenate([m_new, m_new], axis=1)).astype(BF16)
            pv = jnp.dot(p, vj, preferred_element_type=F32)
            l_sc[g] = alpha * l_sc[g] + pv[:, LANES:]
            acc_sc[g] = alpha * acc_sc[g] + pv[:, :LANES]
            m_sc[g] = m_new
        return carry

    lax.fori_loop(0, nch, att_body, 0)

    for g in range(ATT_KV_HEADS):
        og = acc_sc[g] / l_sc[g]
        for pair in range(2):
            even = og[(2 * pair) * QB:(2 * pair + 1) * QB]
            odd = og[(2 * pair + 1) * QB:(2 * pair + 2) * QB]
            if g % 2:
                even = pltpu.roll(even, 64, 1)
            else:
                odd = pltpu.roll(odd, 64, 1)
            c = 2 * g + pair
            o_ref[0, :, c * LANES:(c + 1) * LANES] = jnp.where(lane1 < 64, even, odd).astype(BF16)


def _dsa_core(q, qsw, k, v, iq, ik, iw, top_k):
    B, seq_pad = q.shape[0], q.shape[1]
    nqb = seq_pad // Q_BLOCK
    nkc = -(-seq_pad // KEY_CHUNK)
    rows = ATT_GROUP * Q_BLOCK
    qblock = lambda n: pl.BlockSpec((1, Q_BLOCK, n), lambda b, i: (b, i, 0))
    whole = lambda n: pl.BlockSpec((1, seq_pad, n), lambda b, i: (b, 0, 0))
    return pl.pallas_call(
        functools.partial(_dsa_core_kernel, top_k=top_k, seq_pad=seq_pad),
        grid=(B, nqb),
        in_specs=[qblock(D_MODEL), qblock(D_MODEL), whole(256), whole(512),
                  qblock(IDX_HEADS * IDX_DIM), whole(IDX_DIM), qblock(LANES)],
        out_specs=qblock(D_MODEL),
        out_shape=jax.ShapeDtypeStruct((B, seq_pad, D_MODEL), BF16),
        scratch_shapes=[
            pltpu.VMEM((IDX_DIM, IDX_HEADS * Q_BLOCK), BF16),
            pltpu.VMEM((LANES, Q_BLOCK), F32),
            pltpu.VMEM((ATT_KV_HEADS, rows, LANES), BF16),
            pltpu.VMEM((nkc, KEY_CHUNK, Q_BLOCK), jnp.int32),
            pltpu.VMEM((nkc, Q_BLOCK, KEY_CHUNK), F32),
            pltpu.VMEM((1, Q_BLOCK), jnp.int32),
            pltpu.VMEM((ATT_KV_HEADS, rows, LANES), F32),
            pltpu.VMEM((ATT_KV_HEADS, rows, LANES), F32),
            pltpu.VMEM((ATT_KV_HEADS, rows, LANES), F32),
        ],
        compiler_params=_cparams(("arbitrary", "arbitrary")),
        name="dsa_core",
    )(q, qsw, k, v, iq, ik, iw)


def _out_proj_kernel(h_ref, y_ref, w_ref, o_ref):
    o_ref[...] = h_ref[...] + jnp.dot(y_ref[...], w_ref[...], preferred_element_type=F32)


def _out_proj(h, y, w, tm):
    T = h.shape[0]
    row = lambda i: (i, 0)
    return pl.pallas_call(
        _out_proj_kernel,
        grid=(T // tm,),
        in_specs=[pl.BlockSpec((tm, D_MODEL), row), pl.BlockSpec((tm, D_MODEL), row),
                  pl.BlockSpec((D_MODEL, D_MODEL), lambda i: (0, 0))],
        out_specs=pl.BlockSpec((tm, D_MODEL), row),
        out_shape=jax.ShapeDtypeStruct((T, D_MODEL), F32),
        compiler_params=_cparams(("arbitrary",)),
        name="out_proj",
    )(h, y, w)


def _mlstm_in_kernel(h_ref, g_ref, w_ref, q_ref, k_ref, v_ref, o_ref, gate_ref):
    xn = _rms_rows(h_ref[...], g_ref[...]).astype(BF16)
    z = jnp.dot(xn, w_ref[...], preferred_element_type=F32)
    q_ref[...] = z[:, 0:512].astype(BF16)
    k_ref[...] = (z[:, 512:1024] * (M_QK_DIM ** -0.5)).astype(BF16)
    v_ref[...] = z[:, 1024:2048].astype(BF16)
    o_ref[...] = z[:, 2048:3072]
    gate_ref[...] = z[:, 3072:3200]


def _mlstm_in_proj(h, gain, w, tm):
    T = h.shape[0]
    row = lambda i: (i, 0)
    fixed = lambda i: (0, 0)
    outs = [(512, BF16), (512, BF16), (1024, BF16), (1024, F32), (128, F32)]
    return pl.pallas_call(
        _mlstm_in_kernel,
        grid=(T // tm,),
        in_specs=[pl.BlockSpec((tm, D_MODEL), row), pl.BlockSpec((1, D_MODEL), fixed),
                  pl.BlockSpec((D_MODEL, MLSTM_IN_PAD), fixed)],
        out_specs=[pl.BlockSpec((tm, n), row) for n, _ in outs],
        out_shape=[jax.ShapeDtypeStruct((T, n), dt) for n, dt in outs],
        compiler_params=_cparams(("arbitrary",)),
        name="mlstm_in_proj",
    )(h, gain, w)


def _mlstm_kernel(q_ref, k_ref, v_ref, o_ref, gate_ref, bias_ref, gout_ref, y_ref, ct_sc, n_sc, m_sc):
    C = M_CHUNK

    @pl.when(pl.program_id(1) == 0)
    def _():
        ct_sc[...] = jnp.zeros(ct_sc.shape, F32)
        n_sc[...] = jnp.zeros(n_sc.shape, F32)
        m_sc[...] = jnp.zeros(m_sc.shape, F32)

    pre = gate_ref[...].T[0:8, :] + bias_ref[...]
    capped = GATE_CAP * jnp.tanh(pre / GATE_CAP)
    log_f = -(jnp.maximum(-capped, 0.0) + jnp.log1p(jnp.exp(-jnp.abs(capped))))
    lane8 = lax.broadcasted_iota(jnp.int32, (8, C), 1)
    b = log_f
    sh = 1
    while sh < C:
        b = b + jnp.where(lane8 >= sh, pltpu.roll(b, sh, 1), 0.0)
        sh *= 2
    stacked = jnp.concatenate([b[4:8], capped[0:4]], axis=0)
    cols = jnp.concatenate([stacked, jnp.zeros((C - 8, C), F32)], axis=0).T

    t_idx = lax.broadcasted_iota(jnp.int32, (C, C), 0)
    s_idx = lax.broadcasted_iota(jnp.int32, (C, C), 1)
    for h in range(M_HEADS):
        qh = q_ref[:, h * M_QK_DIM:(h + 1) * M_QK_DIM]
        kh = k_ref[:, h * M_QK_DIM:(h + 1) * M_QK_DIM]
        vh = v_ref[:, h * M_V_DIM:(h + 1) * M_V_DIM]
        b_row, li_row = stacked[h:h + 1, :], stacked[4 + h:5 + h, :]
        b_col, li_col = cols[:, h:h + 1], cols[:, 4 + h:5 + h]
        m_st = m_sc[h:h + 1, 0:1]
        dmat = jnp.where(s_idx <= t_idx, b_col - b_row + li_row, -jnp.inf)
        inter = b_col + m_st
        m_t = jnp.maximum(inter, jnp.max(dmat, axis=1, keepdims=True))
        w_inter = jnp.exp(inter - m_t)
        qk = lax.dot_general(qh, kh, (((1,), (1,)), ((), ())), preferred_element_type=F32)
        s = qk * jnp.exp(dmat - m_t)
        ct = ct_sc[h]
        num = (w_inter * jnp.dot(qh, ct.astype(BF16), preferred_element_type=F32)
               + jnp.dot(s.astype(BF16), vh, preferred_element_type=F32))
        qn = jnp.sum(qh.astype(F32) * n_sc[h:h + 1, :], axis=1, keepdims=True)
        den = w_inter * qn + jnp.sum(s, axis=1, keepdims=True)
        hout = num / jnp.maximum(jnp.abs(den), jnp.exp(-m_t))

        b_last = b_row[:, C - 1:C]
        m_new = jnp.maximum(b_last + m_st, jnp.max(b_last - b_row + li_row, axis=1, keepdims=True))
        decay = jnp.exp(b_last + m_st - m_new)
        wk = jnp.exp(b_last - b_col + li_col - m_new)
        kw = kh.astype(F32) * wk
        ct_sc[h] = decay * ct + jnp.dot(kw.T.astype(BF16), vh, preferred_element_type=F32)
        n_sc[h:h + 1, :] = decay * n_sc[h:h + 1, :] + jnp.sum(kw, axis=0, keepdims=True)
        m_sc[h:h + 1, :] = jnp.broadcast_to(m_new, (1, LANES))

        cs = slice(h * M_V_DIM, (h + 1) * M_V_DIM)
        hn = _rms_rows(hout, gout_ref[:, cs])
        y_ref[:, cs] = (hn * jax.nn.sigmoid(o_ref[:, cs])).astype(BF16)


def _mlstm_core(q, k, v, o, gates, bias8, gout, batch):
    T = q.shape[0]
    nc = T // batch // M_CHUNK
    row = lambda b, c: (b * nc + c, 0)
    fixed = lambda b, c: (0, 0)
    C = M_CHUNK
    return pl.pallas_call(
        _mlstm_kernel,
        grid=(batch, nc),
        in_specs=[pl.BlockSpec((C, 512), row), pl.BlockSpec((C, 512), row), pl.BlockSpec((C, 1024), row),
                  pl.BlockSpec((C, 1024), row), pl.BlockSpec((C, LANES), row),
                  pl.BlockSpec((8, 1), fixed), pl.BlockSpec((1, D_MODEL), fixed)],
        out_specs=pl.BlockSpec((C, D_MODEL), row),
        out_shape=jax.ShapeDtypeStruct((T, D_MODEL), BF16),
        scratch_shapes=[pltpu.VMEM((M_HEADS, M_QK_DIM, M_V_DIM), F32),
                        pltpu.VMEM((8, M_QK_DIM), F32),
                        pltpu.VMEM((8, LANES), F32)],
        compiler_params=_cparams(("arbitrary", "arbitrary")),
        name="mlstm_core",
    )(q, k, v, o, gates, bias8, gout)


def _ffn_kernel(h_ref, g_ref, wg_ref, wu_ref, wd_ref, o_ref, xn_sc, acc_sc):
    f = pl.program_id(1)

    @pl.when(f == 0)
    def _():
        xn_sc[...] = _rms_rows(h_ref[...], g_ref[...]).astype(BF16)
        acc_sc[...] = h_ref[...]

    xn = xn_sc[...]
    gate = jnp.dot(xn, wg_ref[...], preferred_element_type=F32)
    up = jnp.dot(xn, wu_ref[...], preferred_element_type=F32)
    act = (gate * jax.nn.sigmoid(gate) * up).astype(BF16)
    acc_sc[...] += jnp.dot(act, wd_ref[...], preferred_element_type=F32)

    @pl.when(f == pl.num_programs(1) - 1)
    def _():
        o_ref[...] = acc_sc[...]


def _ffn(h, gain, wg, wu, wd, tm, tf):
    T = h.shape[0]
    return pl.pallas_call(
        _ffn_kernel,
        grid=(T // tm, D_FF // tf),
        in_specs=[pl.BlockSpec((tm, D_MODEL), lambda i, f: (i, 0)),
                  pl.BlockSpec((1, D_MODEL), lambda i, f: (0, 0)),
                  pl.BlockSpec((D_MODEL, tf), lambda i, f: (0, f)),
                  pl.BlockSpec((D_MODEL, tf), lambda i, f: (0, f)),
                  pl.BlockSpec((tf, D_MODEL), lambda i, f: (f, 0))],
        out_specs=pl.BlockSpec((tm, D_MODEL), lambda i, f: (i, 0)),
        out_shape=jax.ShapeDtypeStruct((T, D_MODEL), F32),
        scratch_shapes=[pltpu.VMEM((tm, D_MODEL), BF16), pltpu.VMEM((tm, D_MODEL), F32)],
        compiler_params=_cparams(("arbitrary", "arbitrary")),
        name="ffn_dense",
    )(h, gain, wg, wu, wd)


def _moe_kernel(h_ref, g_ref, wr_ref, wg_ref, wu_ref, wd_ref, o_ref,
                xn_sc, xs_sc, ys_sc, posc_sc, posr_sc, comb_sc, cnt_sc):
    e = pl.program_id(1)
    f = pl.program_id(2)
    tm = h_ref.shape[0]
    SB = _moe_slot_block(tm)
    TB = _moe_token_block(tm)
    lane = lax.broadcasted_iota(jnp.int32, (tm, LANES), 1)

    @pl.when((e == 0) & (f == 0))
    def _():
        x = h_ref[...]
        xn = _rms_rows(x, g_ref[...])
        xn_sc[...] = xn.astype(BF16)
        o_ref[...] = x
        logits = jnp.dot(xn, wr_ref[...], preferred_element_type=F32, precision=lax.Precision.HIGHEST)
        logits = jnp.where(lane < N_EXPERTS, logits, -jnp.inf)
        m1 = jnp.max(logits, axis=1, keepdims=True)
        i1 = jnp.min(jnp.where(logits == m1, lane, LANES), axis=1, keepdims=True)
        rest = jnp.where(lane == i1, -jnp.inf, logits)
        m2 = jnp.max(rest, axis=1, keepdims=True)
        i2 = jnp.min(jnp.where(rest == m2, lane, LANES), axis=1, keepdims=True)
        e2 = jnp.exp(m2 - m1)
        g1 = 1.0 / (1.0 + e2)
        comb_sc[...] = jnp.where(lane == i1, g1, jnp.where(lane == i2, e2 * g1, 0.0))
        member = jnp.where(lane == i1, 1.0, jnp.where(lane == i2, 1.0, 0.0))
        tri = jnp.where(lax.broadcasted_iota(jnp.int32, (TB, TB), 0) >= lax.broadcasted_iota(jnp.int32, (TB, TB), 1),
                        1.0, 0.0).astype(BF16)
        carry = jnp.zeros((1, LANES), F32)
        for b in range(tm // TB):
            mb = member[b * TB:(b + 1) * TB]
            incl = jnp.dot(tri, mb.astype(BF16), preferred_element_type=F32)
            posc_sc[b * TB:(b + 1) * TB, :] = jnp.where(mb > 0.0, incl - 1.0 + carry, -1.0)
            carry = carry + incl[TB - 1:TB, :]
        cnt_sc[...] = carry
        for b in range(tm // LANES):
            posr_sc[:, b * LANES:(b + 1) * LANES] = posc_sc[b * LANES:(b + 1) * LANES, :].T

    lane_row = lax.broadcasted_iota(jnp.int32, (1, LANES), 1)
    count = jnp.sum(jnp.where(lane_row == e, cnt_sc[...], 0.0))
    nblk = ((count + (SB - 1.0)) * (1.0 / SB)).astype(jnp.int32)

    @pl.when(f == 0)
    def _():
        slots_of_tokens = posr_sc[pl.ds(e, 1), :]

        def body(r, carry):
            row0 = pl.multiple_of(r * SB, 16)
            slot = (r * SB + lax.broadcasted_iota(jnp.int32, (SB, tm), 0)).astype(F32)
            onehot = jnp.where(slots_of_tokens == slot, 1.0, 0.0).astype(BF16)
            xs_sc[pl.ds(row0, SB), :] = jnp.dot(onehot, xn_sc[...], preferred_element_type=F32).astype(BF16)
            ys_sc[pl.ds(row0, SB), :] = jnp.zeros((SB, D_MODEL), F32)
            return carry

        lax.fori_loop(0, nblk, body, 0)

    def ffn_body(r, carry):
        row0 = pl.multiple_of(r * SB, 16)
        xs = xs_sc[pl.ds(row0, SB), :]
        gate = jnp.dot(xs, wg_ref[0], preferred_element_type=F32)
        up = jnp.dot(xs, wu_ref[0], preferred_element_type=F32)
        act = (gate * jax.nn.sigmoid(gate) * up).astype(BF16)
        ys_sc[pl.ds(row0, SB), :] += jnp.dot(act, wd_ref[0], preferred_element_type=F32)
        return carry

    lax.fori_loop(0, nblk, ffn_body, 0)

    @pl.when(f == pl.num_programs(2) - 1)
    def _():
        on_e = lane == e
        gate_e = jnp.sum(jnp.where(on_e, comb_sc[...], 0.0), axis=1, keepdims=True)
        slot_e = jnp.sum(jnp.where(on_e, posc_sc[...], 0.0), axis=1, keepdims=True)

        def body(r, carry):
            row0 = pl.multiple_of(r * SB, 16)
            ys = ys_sc[pl.ds(row0, SB), :].astype(BF16)
            slot = (r * SB + lax.broadcasted_iota(jnp.int32, (TB, SB), 1)).astype(F32)
            for tb in range(tm // TB):
                rows = slice(tb * TB, (tb + 1) * TB)
                onehot = jnp.where(slot_e[rows] == slot, 1.0, 0.0).astype(BF16)
                o_ref[rows, :] += gate_e[rows] * jnp.dot(onehot, ys, preferred_element_type=F32)
            return carry

        lax.fori_loop(0, nblk, body, 0)


def _moe_slot_block(tm):
    return MOE_SLOT_BLOCK if tm > MOE_SLOT_BLOCK else tm


def _moe_token_block(tm):
    return MOE_TOKEN_BLOCK if tm % MOE_TOKEN_BLOCK == 0 else tm


def _moe(h, gain, wr, wg, wu, wd, tm, tf):
    T = h.shape[0]
    sb = _moe_slot_block(tm)
    cap = -(-tm // sb) * sb
    return pl.pallas_call(
        _moe_kernel,
        grid=(T // tm, N_EXPERTS, D_FF // tf),
        in_specs=[pl.BlockSpec((tm, D_MODEL), lambda i, e, f: (i, 0), pipeline_mode=pl.Buffered(1)),
                  pl.BlockSpec((1, D_MODEL), lambda i, e, f: (0, 0)),
                  pl.BlockSpec((D_MODEL, LANES), lambda i, e, f: (0, 0)),
                  pl.BlockSpec((1, D_MODEL, tf), lambda i, e, f: (e, 0, f)),
                  pl.BlockSpec((1, D_MODEL, tf), lambda i, e, f: (e, 0, f)),
                  pl.BlockSpec((1, tf, D_MODEL), lambda i, e, f: (e, f, 0))],
        out_specs=pl.BlockSpec((tm, D_MODEL), lambda i, e, f: (i, 0)),
        out_shape=jax.ShapeDtypeStruct((T, D_MODEL), F32),
        scratch_shapes=[pltpu.VMEM((tm, D_MODEL), BF16), pltpu.VMEM((cap, D_MODEL), BF16),
                        pltpu.VMEM((cap, D_MODEL), F32), pltpu.VMEM((tm, LANES), F32),
                        pltpu.VMEM((LANES, tm), F32), pltpu.VMEM((tm, LANES), F32),
                        pltpu.VMEM((1, LANES), F32)],
        compiler_params=_cparams(("arbitrary", "arbitrary", "arbitrary")),
        name="moe_sparse",
    )(h, gain, wr, wg, wu, wd)


def _rope_tables(seq_pad):
    pos = jnp.arange(seq_pad, dtype=F32)[:, None]
    lane = np.arange(LANES)

    def table(head_dim):
        half = head_dim // 2
        inv = ROPE_THETA ** (-jnp.arange(half, dtype=F32) / half)
        d = lane % head_dim
        ang = pos * inv[d % half][None, :]
        return jnp.cos(ang), jnp.sin(ang), jnp.asarray(d < half)[None, :]

    c64, s64, lo64 = table(ATT_HEAD_DIM)
    c128, s128, lo128 = table(IDX_DIM)
    return (c64, jnp.where(lo64, -s64, 0.0), jnp.where(lo64, 0.0, s64), c128, jnp.where(lo128, -s128, s128))


def _pad_cols(w, n):
    return jnp.pad(w, ((0, 0), (0, n - w.shape[1])))


def kernel(x, meta, norm_mixer, norm_ffn, dsa_w_in, dsa_q_norm, dsa_k_norm, dsa_w_out, mlstm_w_in, mlstm_b_i,
           mlstm_b_f, mlstm_out_norm, mlstm_w_out, ffn_w_gate, ffn_w_up, ffn_w_down, moe_router, moe_w_gate,
           moe_w_up, moe_w_down):
    B, S, D = x.shape
    L = S + N_META
    top_k = min(TOPK_MAX, S // 4)
    seq_pad = max(-(-L // Q_BLOCK) * Q_BLOCK, KEY_CHUNK)
    T = B * seq_pad
    tm_proj = 384 if seq_pad % 384 == 0 else Q_BLOCK
    tm_ffn = 768 if T % 768 == 0 else Q_BLOCK
    tm_moe = 1536 if T % 1536 == 0 else Q_BLOCK
    tf = 512

    h = jnp.concatenate([jnp.broadcast_to(meta[None].astype(x.dtype), (B, N_META, D)), x,
                         jnp.zeros((B, seq_pad - L, D), x.dtype)], axis=1).reshape(T, D)
    tabs = _rope_tables(seq_pad)
    depth = norm_mixer.shape[0]

    for i in range(depth):
        j = i // 2
        gain_m = norm_mixer[i][None, :]
        gain_f = norm_ffn[i][None, :]
        if i % 2 == 0:
            w_in = _pad_cols(dsa_w_in[j], DSA_IN_PAD).astype(BF16)
            gq = jnp.tile(dsa_q_norm[j], 2)[None, :]
            gk = jnp.tile(dsa_k_norm[j], 2)[None, :]
            proj = _dsa_in_proj(h, gain_m, w_in, gq, gk, tabs, seq_pad, tm_proj)
            att = _dsa_core(*[a.reshape(B, seq_pad, a.shape[-1]) for a in proj], top_k)
            h = _out_proj(h, att.reshape(T, D), dsa_w_out[j].astype(BF16), tm_proj)
            h = _ffn(h, gain_f, ffn_w_gate[j].astype(BF16), ffn_w_up[j].astype(BF16),
                     ffn_w_down[j].astype(BF16), tm_ffn, tf)
        else:
            w_in = _pad_cols(mlstm_w_in[j], MLSTM_IN_PAD).astype(BF16)
            q, k, v, o, gates = _mlstm_in_proj(h, gain_m, w_in, tm_proj)
            bias8 = jnp.concatenate([mlstm_b_i[j], mlstm_b_f[j]])[:, None]
            y = _mlstm_core(q, k, v, o, gates, bias8, mlstm_out_norm[j][None, :], B)
            h = _out_proj(h, y, mlstm_w_out[j].astype(BF16), tm_proj)
            wr = _pad_cols(moe_router[j], LANES)
            h = _moe(h, gain_f, wr, moe_w_gate[j].astype(BF16), moe_w_up[j].astype(BF16),
                     moe_w_down[j].astype(BF16), tm_moe, tf)
    return h.reshape(B, seq_pad, D)[:, N_META:L]
```

```python
import functools

import jax
import jax.numpy as jnp
import numpy as np
from jax import lax
from jax.experimental import pallas as pl
from jax.experimental.pallas import tpu as pltpu

F32 = jnp.float32
BF16 = jnp.bfloat16

D_MODEL = 1024
N_META = 16
RMS_EPS = 1e-6
ROPE_THETA = 10000.0
ATT_HEADS = 16
ATT_KV_HEADS = 4
ATT_HEAD_DIM = 64
ATT_GROUP = 4
IDX_HEADS = 8
IDX_DIM = 128
TOPK_MAX = 256
M_HEADS = 4
M_QK_DIM = 128
M_V_DIM = 256
GATE_CAP = 15.0
D_FF = 3584
N_EXPERTS = 8

LANES = 128
Q_BLOCK = 128
KEY_CHUNK = 256
M_CHUNK = 128
M_BATCH_ROWS = 1
MOE_SLOT_BLOCK = 224
MOE_TOKEN_BLOCK = 256
VMEM_LIMIT = 52 * 1024 * 1024
INT_MIN = -(2 ** 31)
INT_MAX = 2 ** 31 - 1
LOG2E = 1.4426950408889634

DSA_IN_PAD = 2816
MLSTM_IN_PAD = 3200


def _cparams(sem):
    return pltpu.CompilerParams(dimension_semantics=sem, vmem_limit_bytes=VMEM_LIMIT)


def _rms_rows(x, gain):
    ms = jnp.mean(x * x, axis=-1, keepdims=True)
    return x * lax.rsqrt(ms + RMS_EPS) * gain


def _dsa_in_kernel(h_ref, g_ref, w_ref, gq_ref, gk_ref, c64_ref, sa64_ref, sb64_ref, c128_ref, s128_ref,
                   smat_ref, q_ref, qsw_ref, k_ref, v_ref, iq_ref, ik_ref, iw_ref):
    xn = _rms_rows(h_ref[...], g_ref[...]).astype(BF16)
    z = jnp.dot(xn, w_ref[...], preferred_element_type=F32)
    c64, sa64, sb64 = c64_ref[...], sa64_ref[...], sb64_ref[...]
    c128, s128 = c128_ref[...], s128_ref[...]
    smat = smat_ref[...]

    def head_norm_rope(zc, gain):
        z2 = zc * zc
        hi = z2.astype(BF16)
        lo = (z2 - hi.astype(F32)).astype(BF16)
        ms = jnp.dot(hi, smat, preferred_element_type=F32) + jnp.dot(lo, smat, preferred_element_type=F32)
        y = zc * lax.rsqrt(ms + RMS_EPS) * gain
        return y * c64 + pltpu.roll(y, 96, 1) * sa64 + pltpu.roll(y, 32, 1) * sb64

    for c in range(8):
        r = head_norm_rope(z[:, c * LANES:(c + 1) * LANES], gq_ref[...]) * (ATT_HEAD_DIM ** -0.5 * LOG2E)
        q_ref[:, c * LANES:(c + 1) * LANES] = r.astype(BF16)
        qsw_ref[:, c * LANES:(c + 1) * LANES] = pltpu.roll(r, 64, 1).astype(BF16)
    for c in range(2):
        r = head_norm_rope(z[:, 1024 + c * LANES:1024 + (c + 1) * LANES], gk_ref[...])
        k_ref[:, c * LANES:(c + 1) * LANES] = r.astype(BF16)
    ones = jnp.ones((z.shape[0], LANES), BF16)
    for c in range(2):
        v_ref[:, 2 * c * LANES:(2 * c + 1) * LANES] = z[:, 1280 + c * LANES:1280 + (c + 1) * LANES].astype(BF16)
        v_ref[:, (2 * c + 1) * LANES:(2 * c + 2) * LANES] = ones
    for c in range(9):
        zc = z[:, 1536 + c * LANES:1536 + (c + 1) * LANES]
        r = (zc * c128 + pltpu.roll(zc, 64, 1) * s128).astype(BF16)
        if c < 8:
            iq_ref[:, c * LANES:(c + 1) * LANES] = r
        else:
            ik_ref[...] = r
    iw_ref[...] = z[:, 2688:2816] * (IDX_HEADS ** -0.5 * IDX_DIM ** -0.5)


def _dsa_in_proj(h, gain, w, gq, gk, tabs, seq_pad, tm):
    T = h.shape[0]
    nt = seq_pad // tm
    row = lambda i: (i, 0)
    fixed = lambda i: (0, 0)
    pos = lambda i: (i % nt, 0)
    tab_spec = pl.BlockSpec((tm, LANES), pos)
    smat = jnp.asarray(np.kron(np.eye(2), np.full((64, 64), 1.0 / 64)), BF16)
    outs = [(1024, BF16), (1024, BF16), (256, BF16), (512, BF16), (1024, BF16), (128, BF16), (128, F32)]
    return pl.pallas_call(
        _dsa_in_kernel,
        grid=(T // tm,),
        in_specs=[pl.BlockSpec((tm, D_MODEL), row), pl.BlockSpec((1, D_MODEL), fixed),
                  pl.BlockSpec((D_MODEL, DSA_IN_PAD), fixed), pl.BlockSpec((1, LANES), fixed),
                  pl.BlockSpec((1, LANES), fixed), tab_spec, tab_spec, tab_spec, tab_spec, tab_spec,
                  pl.BlockSpec((LANES, LANES), fixed)],
        out_specs=[pl.BlockSpec((tm, n), row) for n, _ in outs],
        out_shape=[jax.ShapeDtypeStruct((T, n), dt) for n, dt in outs],
        compiler_params=_cparams(("arbitrary",)),
        name="dsa_in_proj",
    )(h, gain, w, gq, gk, *tabs, smat)


def _dsa_core_kernel(q_ref, qsw_ref, k_ref, v_ref, iq_ref, ik_ref, iw_ref, o_ref,
                     iqt_sc, iwt_sc, qs_sc, keys_sc, bias_sc, cut_sc, m_sc, l_sc, acc_sc,
                     *, top_k, seq_pad):
    i = pl.program_id(1)
    QB, KC = Q_BLOCK, KEY_CHUNK
    nch = (i * QB + QB + KC - 1) // KC
    kf = float(top_k)
    lane1 = lax.broadcasted_iota(jnp.int32, (QB, LANES), 1)

    for h in range(IDX_HEADS):
        iqt_sc[:, h * QB:(h + 1) * QB] = iq_ref[0, :, h * LANES:(h + 1) * LANES].astype(F32).T.astype(BF16)
    iwt_sc[...] = iw_ref[0].T

    for g in range(ATT_KV_HEADS):
        for r in range(ATT_GROUP):
            h = g * ATT_GROUP + r
            src = q_ref if h % 2 == g % 2 else qsw_ref
            chunk = src[0, :, (h // 2) * LANES:(h // 2 + 1) * LANES].astype(F32)
            keep = (lane1 >= 64) if g % 2 else (lane1 < 64)
            qs_sc[g, r * QB:(r + 1) * QB, :] = jnp.where(keep, chunk, 0.0).astype(BF16)

    qpos = i * QB + lax.broadcasted_iota(jnp.int32, (KC, QB), 1)
    krow = lax.broadcasted_iota(jnp.int32, (KC, QB), 0)

    def chunk_start(j):
        return pl.multiple_of(jnp.minimum(j * KC, seq_pad - KC), LANES)

    def key_valid(j, start):
        kidx = start + krow
        return (kidx <= qpos) & (kidx >= j * KC)

    def score_body(j, carry):
        start = chunk_start(j)
        s = jnp.dot(ik_ref[0, pl.ds(start, KC), :], iqt_sc[...], preferred_element_type=F32)
        acc = jnp.zeros((KC, QB), F32)
        for h in range(IDX_HEADS):
            acc = acc + jnp.maximum(s[:, h * QB:(h + 1) * QB], 0.0) * iwt_sc[h:h + 1, :]
        bits = pltpu.bitcast(acc, jnp.int32)
        key = jnp.where(bits >= 0, bits, bits ^ jnp.int32(INT_MAX))
        keys_sc[j] = jnp.where(key_valid(j, start), key, jnp.int32(INT_MIN))
        return carry

    lax.fori_loop(0, nch, score_body, 0)

    def count(pred):
        def body(j, acc):
            return acc + jnp.where(pred(keys_sc[j], chunk_start(j)), 1.0, 0.0)
        acc = lax.fori_loop(0, nch, body, jnp.zeros((KC, QB), F32))
        return jnp.sum(acc, axis=0, keepdims=True)

    zero = jnp.zeros((1, QB), jnp.int32)
    v0 = jnp.where(count(lambda kk, base: kk >= zero) >= kf, zero, jnp.int32(INT_MIN))

    def bit_body(b, v):
        cand = v | lax.shift_left(jnp.int32(1), (30 - b).astype(jnp.int32))
        return jnp.where(count(lambda kk, base: kk >= cand) >= kf, cand, v)

    thr = lax.fori_loop(jnp.int32(0), jnp.int32(31), bit_body, v0)
    cut_sc[...] = jnp.full((1, QB), INT_MAX, jnp.int32)
    n_ge = count(lambda kk, base: kk >= thr)

    @pl.when(jnp.max(n_ge) > kf)
    def _():
        need = kf - count(lambda kk, base: kk > thr)

        def cut_body(b, c):
            cand = c | lax.shift_left(jnp.int32(1), (12 - b).astype(jnp.int32))
            n = count(lambda kk, base: jnp.where(kk == thr, base + krow, jnp.int32(INT_MAX)) < cand)
            return jnp.where(n < need, cand, c)

        cut_sc[...] = lax.fori_loop(jnp.int32(0), jnp.int32(13), cut_body, zero)

    cut = cut_sc[...]

    def bias_body(j, carry):
        kk = keys_sc[j]
        start = chunk_start(j)
        sel = (kk > thr) | ((kk == thr) & (start + krow <= cut))
        bt = jnp.where(sel & key_valid(j, start), 0.0, -1e30)
        bias_sc[j] = jnp.concatenate([bt[c * QB:(c + 1) * QB].T for c in range(KC // QB)], axis=1)
        return carry

    lax.fori_loop(0, nch, bias_body, 0)

    m_sc[...] = jnp.full(m_sc.shape, -1e9, F32)
    l_sc[...] = jnp.zeros(l_sc.shape, F32)
    acc_sc[...] = jnp.zeros(acc_sc.shape, F32)

    def att_step(start, width, bias):
        bias = jnp.concatenate([bias] * ATT_GROUP, axis=0)
        for g in range(ATT_KV_HEADS):
            cg = g // 2
            kj = k_ref[0, pl.ds(start, width), cg * LANES:(cg + 1) * LANES]
            vj = v_ref[0, pl.ds(start, width), cg * 2 * LANES:(cg + 1) * 2 * LANES]
            s = lax.dot_general(qs_sc[g], kj, (((1,), (1,)), ((), ())), preferred_element_type=F32) + bias
            m_old = m_sc[g]
            m_new = jnp.maximum(m_old, jnp.max(s, axis=1, keepdims=True))
            alpha = jnp.exp2(m_old - m_new)
            p = jnp.exp2(s - jnp.concatenate([m_new] * (width // LANES), axis=1)).astype(BF16)
            pv = jnp.dot(p, vj, preferred_element_type=F32)
            l_sc[g] = alpha * l_sc[g] + pv[:, LANES:]
            acc_sc[g] = alpha * acc_sc[g] + pv[:, :LANES]
            m_sc[g] = m_new

    npair = jnp.minimum(nch // 2, seq_pad // (2 * KC))

    def pair_body(j, carry):
        bias = jnp.concatenate([bias_sc[2 * j], bias_sc[2 * j + 1]], axis=1)
        att_step(pl.multiple_of(j * 2 * KC, 2 * KC), 2 * KC, bias)
        return carry

    def single_body(j, carry):
        att_step(chunk_start(j), KC, bias_sc[j])
        return carry

    lax.fori_loop(0, npair, pair_body, 0)
    lax.fori_loop(2 * npair, nch, single_body, 0)

    for g in range(ATT_KV_HEADS):
        og = acc_sc[g] / l_sc[g]
        for pair in range(2):
            even = og[(2 * pair) * QB:(2 * pair + 1) * QB]
            odd = og[(2 * pair + 1) * QB:(2 * pair + 2) * QB]
            if g % 2:
                even = pltpu.roll(even, 64, 1)
            else:
                odd = pltpu.roll(odd, 64, 1)
            c = 2 * g + pair
            o_ref[0, :, c * LANES:(c + 1) * LANES] = jnp.where(lane1 < 64, even, odd).astype(BF16)


def _dsa_core(q, qsw, k, v, iq, ik, iw, top_k):
    B, seq_pad = q.shape[0], q.shape[1]
    nqb = seq_pad // Q_BLOCK
    nkc = -(-seq_pad // KEY_CHUNK)
    rows = ATT_GROUP * Q_BLOCK
    qblock = lambda n: pl.BlockSpec((1, Q_BLOCK, n), lambda b, i: (b, i, 0))
    whole = lambda n: pl.BlockSpec((1, seq_pad, n), lambda b, i: (b, 0, 0))
    return pl.pallas_call(
        functools.partial(_dsa_core_kernel, top_k=top_k, seq_pad=seq_pad),
        grid=(B, nqb),
        in_specs=[qblock(D_MODEL), qblock(D_MODEL), whole(256), whole(512),
                  qblock(IDX_HEADS * IDX_DIM), whole(IDX_DIM), qblock(LANES)],
        out_specs=qblock(D_MODEL),
        out_shape=jax.ShapeDtypeStruct((B, seq_pad, D_MODEL), BF16),
        scratch_shapes=[
            pltpu.VMEM((IDX_DIM, IDX_HEADS * Q_BLOCK), BF16),
            pltpu.VMEM((LANES, Q_BLOCK), F32),
            pltpu.VMEM((ATT_KV_HEADS, rows, LANES), BF16),
            pltpu.VMEM((nkc, KEY_CHUNK, Q_BLOCK), jnp.int32),
            pltpu.VMEM((nkc, Q_BLOCK, KEY_CHUNK), F32),
            pltpu.VMEM((1, Q_BLOCK), jnp.int32),
            pltpu.VMEM((ATT_KV_HEADS, rows, LANES), F32),
            pltpu.VMEM((ATT_KV_HEADS, rows, LANES), F32),
            pltpu.VMEM((ATT_KV_HEADS, rows, LANES), F32),
        ],
        compiler_params=_cparams(("arbitrary", "arbitrary")),
        name="dsa_core",
    )(q, qsw, k, v, iq, ik, iw)


def _out_proj_kernel(h_ref, y_ref, w_ref, o_ref):
    o_ref[...] = h_ref[...] + jnp.dot(y_ref[...], w_ref[...], preferred_element_type=F32)


def _out_proj(h, y, w, tm):
    T = h.shape[0]
    row = lambda i: (i, 0)
    return pl.pallas_call(
        _out_proj_kernel,
        grid=(T // tm,),
        in_specs=[pl.BlockSpec((tm, D_MODEL), row), pl.BlockSpec((tm, D_MODEL), row),
                  pl.BlockSpec((D_MODEL, D_MODEL), lambda i: (0, 0))],
        out_specs=pl.BlockSpec((tm, D_MODEL), row),
        out_shape=jax.ShapeDtypeStruct((T, D_MODEL), F32),
        compiler_params=_cparams(("arbitrary",)),
        name="out_proj",
    )(h, y, w)


def _mlstm_in_kernel(h_ref, g_ref, w_ref, q_ref, k_ref, v_ref, o_ref, gate_ref):
    xn = _rms_rows(h_ref[...], g_ref[...]).astype(BF16)
    z = jnp.dot(xn, w_ref[...], preferred_element_type=F32)
    q_ref[...] = z[:, 0:512].astype(BF16)
    k_ref[...] = (z[:, 512:1024] * (M_QK_DIM ** -0.5)).astype(BF16)
    v_ref[...] = z[:, 1024:2048].astype(BF16)
    o_ref[...] = z[:, 2048:3072]
    gate_ref[...] = z[:, 3072:3200]


def _mlstm_in_proj(h, gain, w, tm):
    T = h.shape[0]
    row = lambda i: (i, 0)
    fixed = lambda i: (0, 0)
    outs = [(512, BF16), (512, BF16), (1024, BF16), (1024, F32), (128, F32)]
    return pl.pallas_call(
        _mlstm_in_kernel,
        grid=(T // tm,),
        in_specs=[pl.BlockSpec((tm, D_MODEL), row), pl.BlockSpec((1, D_MODEL), fixed),
                  pl.BlockSpec((D_MODEL, MLSTM_IN_PAD), fixed)],
        out_specs=[pl.BlockSpec((tm, n), row) for n, _ in outs],
        out_shape=[jax.ShapeDtypeStruct((T, n), dt) for n, dt in outs],
        compiler_params=_cparams(("arbitrary",)),
        name="mlstm_in_proj",
    )(h, gain, w)


def _mlstm_kernel(q_ref, k_ref, v_ref, o_ref, gate_ref, bias_ref, gout_ref, y_ref, ct_sc, n_sc, m_sc):
    C = M_CHUNK

    @pl.when(pl.program_id(1) == 0)
    def _():
        ct_sc[...] = jnp.zeros(ct_sc.shape, F32)
        n_sc[...] = jnp.zeros(n_sc.shape, F32)
        m_sc[...] = jnp.zeros(m_sc.shape, F32)

    lane8 = lax.broadcasted_iota(jnp.int32, (8, C), 1)
    t_idx = lax.broadcasted_iota(jnp.int32, (C, C), 0)
    s_idx = lax.broadcasted_iota(jnp.int32, (C, C), 1)
    for bi in range(q_ref.shape[0]):
        pre = gate_ref[bi].T[0:8, :] + bias_ref[...]
        capped = GATE_CAP * jnp.tanh(pre / GATE_CAP)
        log_f = -(jnp.maximum(-capped, 0.0) + jnp.log1p(jnp.exp(-jnp.abs(capped))))
        b = log_f
        sh = 1
        while sh < C:
            b = b + jnp.where(lane8 >= sh, pltpu.roll(b, sh, 1), 0.0)
            sh *= 2
        stacked = jnp.concatenate([b[4:8], capped[0:4]], axis=0)
        cols = jnp.concatenate([stacked, jnp.zeros((C - 8, C), F32)], axis=0).T

        for h in range(M_HEADS):
            st = bi * M_HEADS + h
            qh = q_ref[bi, :, h * M_QK_DIM:(h + 1) * M_QK_DIM]
            kh = k_ref[bi, :, h * M_QK_DIM:(h + 1) * M_QK_DIM]
            vh = v_ref[bi, :, h * M_V_DIM:(h + 1) * M_V_DIM]
            b_row, li_row = stacked[h:h + 1, :], stacked[4 + h:5 + h, :]
            b_col, li_col = cols[:, h:h + 1], cols[:, 4 + h:5 + h]
            m_st = m_sc[st:st + 1, 0:1]
            dmat = jnp.where(s_idx <= t_idx, b_col - b_row + li_row, -jnp.inf)
            inter = b_col + m_st
            m_t = jnp.maximum(inter, jnp.max(dmat, axis=1, keepdims=True))
            w_inter = jnp.exp(inter - m_t)
            qk = lax.dot_general(qh, kh, (((1,), (1,)), ((), ())), preferred_element_type=F32)
            s = qk * jnp.exp(dmat - m_t)
            ct = ct_sc[st]
            num = (w_inter * jnp.dot(qh, ct.astype(BF16), preferred_element_type=F32)
                   + jnp.dot(s.astype(BF16), vh, preferred_element_type=F32))
            qn = jnp.sum(qh.astype(F32) * n_sc[st:st + 1, :], axis=1, keepdims=True)
            den = w_inter * qn + jnp.sum(s, axis=1, keepdims=True)
            hout = num / jnp.maximum(jnp.abs(den), jnp.exp(-m_t))

            b_last = b_row[:, C - 1:C]
            m_new = jnp.maximum(b_last + m_st, jnp.max(b_last - b_row + li_row, axis=1, keepdims=True))
            decay = jnp.exp(b_last + m_st - m_new)
            wk = jnp.exp(b_last - b_col + li_col - m_new)
            kw = kh.astype(F32) * wk
            ct_sc[st] = decay * ct + jnp.dot(kw.T.astype(BF16), vh, preferred_element_type=F32)
            n_sc[st:st + 1, :] = decay * n_sc[st:st + 1, :] + jnp.sum(kw, axis=0, keepdims=True)
            m_sc[st:st + 1, :] = jnp.broadcast_to(m_new, (1, LANES))

            cs = slice(h * M_V_DIM, (h + 1) * M_V_DIM)
            hn = _rms_rows(hout, gout_ref[:, cs])
            y_ref[bi, :, cs] = (hn * jax.nn.sigmoid(o_ref[bi, :, cs])).astype(BF16)


def _mlstm_core(q, k, v, o, gates, bias8, gout):
    B, seq_pad = q.shape[0], q.shape[1]
    nb = M_BATCH_ROWS
    C = M_CHUNK
    blk = lambda n: pl.BlockSpec((nb, C, n), lambda b, c: (b, c, 0))
    fixed = lambda b, c: (0, 0)
    return pl.pallas_call(
        _mlstm_kernel,
        grid=(B // nb, seq_pad // C),
        in_specs=[blk(512), blk(512), blk(1024), blk(1024), blk(LANES),
                  pl.BlockSpec((8, 1), fixed), pl.BlockSpec((1, D_MODEL), fixed)],
        out_specs=blk(D_MODEL),
        out_shape=jax.ShapeDtypeStruct((B, seq_pad, D_MODEL), BF16),
        scratch_shapes=[pltpu.VMEM((nb * M_HEADS, M_QK_DIM, M_V_DIM), F32),
                        pltpu.VMEM((nb * M_HEADS, M_QK_DIM), F32),
                        pltpu.VMEM((nb * M_HEADS, LANES), F32)],
        compiler_params=_cparams(("arbitrary", "arbitrary")),
        name="mlstm_core",
    )(q, k, v, o, gates, bias8, gout)


def _ffn_kernel(h_ref, y_ref, wo_ref, g_ref, wg_ref, wu_ref, wd_ref, o_ref, xn_sc, acc_sc):
    f = pl.program_id(1)

    @pl.when(f == 0)
    def _():
        x = h_ref[...] + jnp.dot(y_ref[...], wo_ref[...], preferred_element_type=F32)
        xn_sc[...] = _rms_rows(x, g_ref[...]).astype(BF16)
        acc_sc[...] = x

    xn = xn_sc[...]
    gate = jnp.dot(xn, wg_ref[...], preferred_element_type=F32)
    up = jnp.dot(xn, wu_ref[...], preferred_element_type=F32)
    act = (gate * jax.nn.sigmoid(gate) * up).astype(BF16)
    acc_sc[...] += jnp.dot(act, wd_ref[...], preferred_element_type=F32)

    @pl.when(f == pl.num_programs(1) - 1)
    def _():
        o_ref[...] = acc_sc[...]


def _ffn(h, y, wo, gain, wg, wu, wd, tm, tf):
    T = h.shape[0]
    return pl.pallas_call(
        _ffn_kernel,
        grid=(T // tm, D_FF // tf),
        in_specs=[pl.BlockSpec((tm, D_MODEL), lambda i, f: (i, 0)),
                  pl.BlockSpec((tm, D_MODEL), lambda i, f: (i, 0)),
                  pl.BlockSpec((D_MODEL, D_MODEL), lambda i, f: (0, 0)),
                  pl.BlockSpec((1, D_MODEL), lambda i, f: (0, 0)),
                  pl.BlockSpec((D_MODEL, tf), lambda i, f: (0, f)),
                  pl.BlockSpec((D_MODEL, tf), lambda i, f: (0, f)),
                  pl.BlockSpec((tf, D_MODEL), lambda i, f: (f, 0))],
        out_specs=pl.BlockSpec((tm, D_MODEL), lambda i, f: (i, 0)),
        out_shape=jax.ShapeDtypeStruct((T, D_MODEL), F32),
        scratch_shapes=[pltpu.VMEM((tm, D_MODEL), BF16), pltpu.VMEM((tm, D_MODEL), F32)],
        compiler_params=_cparams(("arbitrary", "arbitrary")),
        name="ffn_dense",
    )(h, y, wo, gain, wg, wu, wd)


def _moe_kernel(h_ref, g_ref, wr_ref, wg_ref, wu_ref, wd_ref, o_ref,
                xn_sc, xs_sc, ys_sc, posc_sc, posr_sc, comb_sc, cnt_sc):
    e = pl.program_id(1)
    f = pl.program_id(2)
    tm = h_ref.shape[0]
    SB = _moe_slot_block(tm)
    TB = _moe_token_block(tm)
    lane = lax.broadcasted_iota(jnp.int32, (tm, LANES), 1)

    @pl.when((e == 0) & (f == 0))
    def _():
        x = h_ref[...]
        xn = _rms_rows(x, g_ref[...])
        xn_sc[...] = xn.astype(BF16)
        o_ref[...] = x
        logits = jnp.dot(xn, wr_ref[...], preferred_element_type=F32, precision=lax.Precision.HIGHEST)
        logits = jnp.where(lane < N_EXPERTS, logits, -jnp.inf)
        m1 = jnp.max(logits, axis=1, keepdims=True)
        i1 = jnp.min(jnp.where(logits == m1, lane, LANES), axis=1, keepdims=True)
        rest = jnp.where(lane == i1, -jnp.inf, logits)
        m2 = jnp.max(rest, axis=1, keepdims=True)
        i2 = jnp.min(jnp.where(rest == m2, lane, LANES), axis=1, keepdims=True)
        e2 = jnp.exp(m2 - m1)
        g1 = 1.0 / (1.0 + e2)
        comb_sc[...] = jnp.where(lane == i1, g1, jnp.where(lane == i2, e2 * g1, 0.0))
        member = jnp.where(lane == i1, 1.0, jnp.where(lane == i2, 1.0, 0.0))
        tri = jnp.where(lax.broadcasted_iota(jnp.int32, (TB, TB), 0) >= lax.broadcasted_iota(jnp.int32, (TB, TB), 1),
                        1.0, 0.0).astype(BF16)
        carry = jnp.zeros((1, LANES), F32)
        for b in range(tm // TB):
            mb = member[b * TB:(b + 1) * TB]
            incl = jnp.dot(tri, mb.astype(BF16), preferred_element_type=F32)
            posc_sc[b * TB:(b + 1) * TB, :] = jnp.where(mb > 0.0, incl - 1.0 + carry, -1.0)
            carry = carry + incl[TB - 1:TB, :]
        cnt_sc[...] = carry
        for b in range(tm // LANES):
            posr_sc[:, b * LANES:(b + 1) * LANES] = posc_sc[b * LANES:(b + 1) * LANES, :].T

    lane_row = lax.broadcasted_iota(jnp.int32, (1, LANES), 1)
    count = jnp.sum(jnp.where(lane_row == e, cnt_sc[...], 0.0))
    nblk = ((count + (SB - 1.0)) * (1.0 / SB)).astype(jnp.int32)

    @pl.when(f == 0)
    def _():
        slots_of_tokens = posr_sc[pl.ds(e, 1), :]

        def body(r, carry):
            row0 = pl.multiple_of(r * SB, 16)
            slot = (r * SB + lax.broadcasted_iota(jnp.int32, (SB, tm), 0)).astype(F32)
            onehot = jnp.where(slots_of_tokens == slot, 1.0, 0.0).astype(BF16)
            xs_sc[pl.ds(row0, SB), :] = jnp.dot(onehot, xn_sc[...], preferred_element_type=F32).astype(BF16)
            ys_sc[pl.ds(row0, SB), :] = jnp.zeros((SB, D_MODEL), F32)
            return carry

        lax.fori_loop(0, nblk, body, 0)

    def ffn_rows(row0, nrows):
        xs = xs_sc[pl.ds(row0, nrows), :]
        gate = jnp.dot(xs, wg_ref[0], preferred_element_type=F32)
        up = jnp.dot(xs, wu_ref[0], preferred_element_type=F32)
        act = (gate * jax.nn.sigmoid(gate) * up).astype(BF16)
        ys_sc[pl.ds(row0, nrows), :] += jnp.dot(act, wd_ref[0], preferred_element_type=F32)

    def pair_body(r, carry):
        ffn_rows(pl.multiple_of(r * 2 * SB, 16), 2 * SB)
        return carry

    lax.fori_loop(0, nblk // 2, pair_body, 0)

    @pl.when(nblk % 2 == 1)
    def _():
        ffn_rows(pl.multiple_of((nblk - 1) * SB, 16), SB)

    @pl.when(f == pl.num_programs(2) - 1)
    def _():
        on_e = lane == e
        gate_e = jnp.sum(jnp.where(on_e, comb_sc[...], 0.0), axis=1, keepdims=True)
        slot_e = jnp.sum(jnp.where(on_e, posc_sc[...], 0.0), axis=1, keepdims=True)

        def body(r, carry):
            row0 = pl.multiple_of(r * SB, 16)
            ys = ys_sc[pl.ds(row0, SB), :].astype(BF16)
            slot = (r * SB + lax.broadcasted_iota(jnp.int32, (TB, SB), 1)).astype(F32)
            for tb in range(tm // TB):
                rows = slice(tb * TB, (tb + 1) * TB)
                onehot = jnp.where(slot_e[rows] == slot, 1.0, 0.0).astype(BF16)
                o_ref[rows, :] += gate_e[rows] * jnp.dot(onehot, ys, preferred_element_type=F32)
            return carry

        lax.fori_loop(0, nblk, body, 0)


def _moe_slot_block(tm):
    return MOE_SLOT_BLOCK if tm > MOE_SLOT_BLOCK else tm


def _moe_token_block(tm):
    return MOE_TOKEN_BLOCK if tm % MOE_TOKEN_BLOCK == 0 else tm


def _moe(h, gain, wr, wg, wu, wd, tm, tf):
    T = h.shape[0]
    sb = _moe_slot_block(tm)
    cap = -(-tm // sb) * sb
    return pl.pallas_call(
        _moe_kernel,
        grid=(T // tm, N_EXPERTS, D_FF // tf),
        in_specs=[pl.BlockSpec((tm, D_MODEL), lambda i, e, f: (i, 0), pipeline_mode=pl.Buffered(1)),
                  pl.BlockSpec((1, D_MODEL), lambda i, e, f: (0, 0)),
                  pl.BlockSpec((D_MODEL, LANES), lambda i, e, f: (0, 0)),
                  pl.BlockSpec((1, D_MODEL, tf), lambda i, e, f: (e, 0, f)),
                  pl.BlockSpec((1, D_MODEL, tf), lambda i, e, f: (e, 0, f)),
                  pl.BlockSpec((1, tf, D_MODEL), lambda i, e, f: (e, f, 0))],
        out_specs=pl.BlockSpec((tm, D_MODEL), lambda i, e, f: (i, 0)),
        out_shape=jax.ShapeDtypeStruct((T, D_MODEL), F32),
        scratch_shapes=[pltpu.VMEM((tm, D_MODEL), BF16), pltpu.VMEM((cap, D_MODEL), BF16),
                        pltpu.VMEM((cap, D_MODEL), F32), pltpu.VMEM((tm, LANES), F32),
                        pltpu.VMEM((LANES, tm), F32), pltpu.VMEM((tm, LANES), F32),
                        pltpu.VMEM((1, LANES), F32)],
        compiler_params=_cparams(("arbitrary", "arbitrary", "arbitrary")),
        name="moe_sparse",
    )(h, gain, wr, wg, wu, wd)


def _rope_tables(seq_pad):
    pos = jnp.arange(seq_pad, dtype=F32)[:, None]
    lane = np.arange(LANES)

    def table(head_dim):
        half = head_dim // 2
        inv = ROPE_THETA ** (-jnp.arange(half, dtype=F32) / half)
        d = lane % head_dim
        ang = pos * inv[d % half][None, :]
        return jnp.cos(ang), jnp.sin(ang), jnp.asarray(d < half)[None, :]

    c64, s64, lo64 = table(ATT_HEAD_DIM)
    c128, s128, lo128 = table(IDX_DIM)
    return (c64, jnp.where(lo64, -s64, 0.0), jnp.where(lo64, 0.0, s64), c128, jnp.where(lo128, -s128, s128))


def _pad_cols(w, n):
    return jnp.pad(w, ((0, 0), (0, n - w.shape[1])))


def kernel(x, meta, norm_mixer, norm_ffn, dsa_w_in, dsa_q_norm, dsa_k_norm, dsa_w_out, mlstm_w_in, mlstm_b_i,
           mlstm_b_f, mlstm_out_norm, mlstm_w_out, ffn_w_gate, ffn_w_up, ffn_w_down, moe_router, moe_w_gate,
           moe_w_up, moe_w_down):
    B, S, D = x.shape
    L = S + N_META
    top_k = min(TOPK_MAX, S // 4)
    seq_pad = max(-(-L // Q_BLOCK) * Q_BLOCK, KEY_CHUNK)
    T = B * seq_pad
    tm_proj = 384 if seq_pad % 384 == 0 else Q_BLOCK
    tm_ffn = 768 if T % 768 == 0 else Q_BLOCK
    tm_moe = 1536 if T % 1536 == 0 else Q_BLOCK
    tf = 512

    h = jnp.concatenate([jnp.broadcast_to(meta[None].astype(x.dtype), (B, N_META, D)), x,
                         jnp.zeros((B, seq_pad - L, D), x.dtype)], axis=1).reshape(T, D)
    tabs = _rope_tables(seq_pad)
    depth = norm_mixer.shape[0]

    for i in range(depth):
        j = i // 2
        gain_m = norm_mixer[i][None, :]
        gain_f = norm_ffn[i][None, :]
        if i % 2 == 0:
            w_in = _pad_cols(dsa_w_in[j], DSA_IN_PAD).astype(BF16)
            gq = jnp.tile(dsa_q_norm[j], 2)[None, :]
            gk = jnp.tile(dsa_k_norm[j], 2)[None, :]
            proj = _dsa_in_proj(h, gain_m, w_in, gq, gk, tabs, seq_pad, tm_proj)
            att = _dsa_core(*[a.reshape(B, seq_pad, a.shape[-1]) for a in proj], top_k)
            h = _ffn(h, att.reshape(T, D), dsa_w_out[j].astype(BF16), gain_f, ffn_w_gate[j].astype(BF16),
                     ffn_w_up[j].astype(BF16), ffn_w_down[j].astype(BF16), tm_ffn, tf)
        else:
            w_in = _pad_cols(mlstm_w_in[j], MLSTM_IN_PAD).astype(BF16)
            q, k, v, o, gates = _mlstm_in_proj(h, gain_m, w_in, tm_proj)
            bias8 = jnp.concatenate([mlstm_b_i[j], mlstm_b_f[j]])[:, None]
            y = _mlstm_core(*[a.reshape(B, seq_pad, a.shape[-1]) for a in (q, k, v, o, gates)], bias8,
                            mlstm_out_norm[j][None, :])
            h = _out_proj(h, y.reshape(T, D), mlstm_w_out[j].astype(BF16), tm_proj)
            wr = _pad_cols(moe_router[j], LANES)
            h = _moe(h, gain_f, wr, moe_w_gate[j].astype(BF16), moe_w_up[j].astype(BF16),
                     moe_w_down[j].astype(BF16), tm_moe, tf)
    return h.reshape(B, seq_pad, D)[:, N_META:L]
```

```python
import functools

import jax
import jax.numpy as jnp
import numpy as np
from jax import lax
from jax.experimental import pallas as pl
from jax.experimental.pallas import tpu as pltpu

F32 = jnp.float32
BF16 = jnp.bfloat16

D_MODEL = 1024
N_META = 16
RMS_EPS = 1e-6
ROPE_THETA = 10000.0
ATT_HEADS = 16
ATT_KV_HEADS = 4
ATT_HEAD_DIM = 64
ATT_GROUP = 4
IDX_HEADS = 8
IDX_DIM = 128
TOPK_MAX = 256
M_HEADS = 4
M_QK_DIM = 128
M_V_DIM = 256
GATE_CAP = 15.0
D_FF = 3584
N_EXPERTS = 8

LANES = 128
Q_BLOCK = 128
KEY_CHUNK = 256
ATT_PAIR_STEP_GROUPS = ((0, 1, 2, 3),)
M_CHUNK = 128
M_BATCH_ROWS = 1
MOE_SLOT_BLOCK = 224
MOE_TOKEN_BLOCK = 256
VMEM_LIMIT = 52 * 1024 * 1024
INT_MIN = -(2 ** 31)
INT_MAX = 2 ** 31 - 1
LOG2E = 1.4426950408889634

DSA_IN_PAD = 2816
MLSTM_IN_PAD = 3200


def _cparams(sem):
    return pltpu.CompilerParams(dimension_semantics=sem, vmem_limit_bytes=VMEM_LIMIT)


def _rms_rows(x, gain):
    ms = jnp.mean(x * x, axis=-1, keepdims=True)
    return x * lax.rsqrt(ms + RMS_EPS) * gain


def _dsa_in_kernel(h_ref, g_ref, w_ref, gq_ref, gk_ref, c64_ref, sa64_ref, sb64_ref, c128_ref, s128_ref,
                   smat_ref, q_ref, qsw_ref, k_ref, v_ref, iq_ref, ik_ref, iw_ref):
    xn = _rms_rows(h_ref[...], g_ref[...]).astype(BF16)
    z = jnp.dot(xn, w_ref[...], preferred_element_type=F32)
    c64, sa64, sb64 = c64_ref[...], sa64_ref[...], sb64_ref[...]
    c128, s128 = c128_ref[...], s128_ref[...]
    smat = smat_ref[...]

    def head_norm_rope(zc, gain):
        z2 = zc * zc
        hi = z2.astype(BF16)
        lo = (z2 - hi.astype(F32)).astype(BF16)
        ms = jnp.dot(hi, smat, preferred_element_type=F32) + jnp.dot(lo, smat, preferred_element_type=F32)
        y = zc * lax.rsqrt(ms + RMS_EPS) * gain
        return y * c64 + pltpu.roll(y, 96, 1) * sa64 + pltpu.roll(y, 32, 1) * sb64

    for c in range(8):
        r = head_norm_rope(z[:, c * LANES:(c + 1) * LANES], gq_ref[...]) * (ATT_HEAD_DIM ** -0.5 * LOG2E)
        q_ref[:, c * LANES:(c + 1) * LANES] = r.astype(BF16)
        qsw_ref[:, c * LANES:(c + 1) * LANES] = pltpu.roll(r, 64, 1).astype(BF16)
    for c in range(2):
        r = head_norm_rope(z[:, 1024 + c * LANES:1024 + (c + 1) * LANES], gk_ref[...])
        k_ref[:, c * LANES:(c + 1) * LANES] = r.astype(BF16)
    ones = jnp.ones((z.shape[0], LANES), BF16)
    for c in range(2):
        v_ref[:, 2 * c * LANES:(2 * c + 1) * LANES] = z[:, 1280 + c * LANES:1280 + (c + 1) * LANES].astype(BF16)
        v_ref[:, (2 * c + 1) * LANES:(2 * c + 2) * LANES] = ones
    for c in range(9):
        zc = z[:, 1536 + c * LANES:1536 + (c + 1) * LANES]
        r = (zc * c128 + pltpu.roll(zc, 64, 1) * s128).astype(BF16)
        if c < 8:
            iq_ref[:, c * LANES:(c + 1) * LANES] = r
        else:
            ik_ref[...] = r
    iw_ref[...] = z[:, 2688:2816] * (IDX_HEADS ** -0.5 * IDX_DIM ** -0.5)


def _dsa_in_proj(h, gain, w, gq, gk, tabs, seq_pad, tm):
    T = h.shape[0]
    nt = seq_pad // tm
    row = lambda i: (i, 0)
    fixed = lambda i: (0, 0)
    pos = lambda i: (i % nt, 0)
    tab_spec = pl.BlockSpec((tm, LANES), pos)
    smat = jnp.asarray(np.kron(np.eye(2), np.full((64, 64), 1.0 / 64)), BF16)
    outs = [(1024, BF16), (1024, BF16), (256, BF16), (512, BF16), (1024, BF16), (128, BF16), (128, F32)]
    return pl.pallas_call(
        _dsa_in_kernel,
        grid=(T // tm,),
        in_specs=[pl.BlockSpec((tm, D_MODEL), row), pl.BlockSpec((1, D_MODEL), fixed),
                  pl.BlockSpec((D_MODEL, DSA_IN_PAD), fixed), pl.BlockSpec((1, LANES), fixed),
                  pl.BlockSpec((1, LANES), fixed), tab_spec, tab_spec, tab_spec, tab_spec, tab_spec,
                  pl.BlockSpec((LANES, LANES), fixed)],
        out_specs=[pl.BlockSpec((tm, n), row) for n, _ in outs],
        out_shape=[jax.ShapeDtypeStruct((T, n), dt) for n, dt in outs],
        compiler_params=_cparams(("arbitrary",)),
        name="dsa_in_proj",
    )(h, gain, w, gq, gk, *tabs, smat)


def _dsa_core_kernel(q_ref, qsw_ref, k_ref, v_ref, iq_ref, ik_ref, iw_ref, o_ref,
                     iqt_sc, iwt_sc, qs_sc, keys_sc, bias_sc, cut_sc, m_sc, l_sc, acc_sc,
                     *, top_k, seq_pad):
    i = pl.program_id(1)
    QB, KC = Q_BLOCK, KEY_CHUNK
    nch = (i * QB + QB + KC - 1) // KC
    kf = float(top_k)
    lane1 = lax.broadcasted_iota(jnp.int32, (QB, LANES), 1)

    for h in range(IDX_HEADS):
        iqt_sc[:, h * QB:(h + 1) * QB] = iq_ref[0, :, h * LANES:(h + 1) * LANES].astype(F32).T.astype(BF16)
    iwt_sc[...] = iw_ref[0].T

    for g in range(ATT_KV_HEADS):
        for r in range(ATT_GROUP):
            h = g * ATT_GROUP + r
            src = q_ref if h % 2 == g % 2 else qsw_ref
            chunk = src[0, :, (h // 2) * LANES:(h // 2 + 1) * LANES].astype(F32)
            keep = (lane1 >= 64) if g % 2 else (lane1 < 64)
            qs_sc[g, r * QB:(r + 1) * QB, :] = jnp.where(keep, chunk, 0.0).astype(BF16)

    qpos = i * QB + lax.broadcasted_iota(jnp.int32, (KC, QB), 1)
    krow = lax.broadcasted_iota(jnp.int32, (KC, QB), 0)

    def chunk_start(j):
        return pl.multiple_of(jnp.minimum(j * KC, seq_pad - KC), LANES)

    def key_valid(j, start):
        kidx = start + krow
        return (kidx <= qpos) & (kidx >= j * KC)

    def for_chunks(fn):
        def body(j2, carry):
            fn(2 * j2)
            fn(2 * j2 + 1)
            return carry

        lax.fori_loop(0, nch // 2, body, 0)

        @pl.when(nch % 2 == 1)
        def _():
            fn(nch - 1)

    def score_chunk(j):
        start = chunk_start(j)
        s = jnp.dot(ik_ref[0, pl.ds(start, KC), :], iqt_sc[...], preferred_element_type=F32)
        acc = jnp.zeros((KC, QB), F32)
        for h in range(IDX_HEADS):
            acc = acc + jnp.maximum(s[:, h * QB:(h + 1) * QB], 0.0) * iwt_sc[h:h + 1, :]
        bits = pltpu.bitcast(acc, jnp.int32)
        key = jnp.where(bits >= 0, bits, bits ^ jnp.int32(INT_MAX))
        keys_sc[j] = jnp.where(key_valid(j, start), key, jnp.int32(INT_MIN))

    for_chunks(score_chunk)

    def count(pred):
        def body(j, acc):
            return acc + jnp.where(pred(keys_sc[j], chunk_start(j)), 1.0, 0.0)
        acc = lax.fori_loop(0, nch, body, jnp.zeros((KC, QB), F32))
        return jnp.sum(acc, axis=0, keepdims=True)

    zero = jnp.zeros((1, QB), jnp.int32)
    v0 = jnp.where(count(lambda kk, base: kk >= zero) >= kf, zero, jnp.int32(INT_MIN))

    def bit_body(b, v):
        cand = v | lax.shift_left(jnp.int32(1), (30 - b).astype(jnp.int32))
        return jnp.where(count(lambda kk, base: kk >= cand) >= kf, cand, v)

    thr = lax.fori_loop(jnp.int32(0), jnp.int32(31), bit_body, v0)
    cut_sc[...] = jnp.full((1, QB), INT_MAX, jnp.int32)
    n_ge = count(lambda kk, base: kk >= thr)

    @pl.when(jnp.max(n_ge) > kf)
    def _():
        need = kf - count(lambda kk, base: kk > thr)

        def cut_body(b, c):
            cand = c | lax.shift_left(jnp.int32(1), (12 - b).astype(jnp.int32))
            n = count(lambda kk, base: jnp.where(kk == thr, base + krow, jnp.int32(INT_MAX)) < cand)
            return jnp.where(n < need, cand, c)

        cut_sc[...] = lax.fori_loop(jnp.int32(0), jnp.int32(13), cut_body, zero)

    cut = cut_sc[...]

    def bias_chunk(j):
        kk = keys_sc[j]
        start = chunk_start(j)
        sel = (kk > thr) | ((kk == thr) & (start + krow <= cut))
        bt = jnp.where(sel & key_valid(j, start), 0.0, -1e30)
        bias_sc[j] = jnp.concatenate([bt[c * QB:(c + 1) * QB].T for c in range(KC // QB)], axis=1)

    for_chunks(bias_chunk)

    m_sc[...] = jnp.full(m_sc.shape, -1e9, F32)
    l_sc[...] = jnp.zeros(l_sc.shape, F32)
    acc_sc[...] = jnp.zeros(acc_sc.shape, F32)

    def att_step(start, width, bias, groups):
        bias = jnp.concatenate([bias] * ATT_GROUP, axis=0)
        for g in groups:
            cg = g // 2
            kj = k_ref[0, pl.ds(start, width), cg * LANES:(cg + 1) * LANES]
            vj = v_ref[0, pl.ds(start, width), cg * 2 * LANES:(cg + 1) * 2 * LANES]
            s = lax.dot_general(qs_sc[g], kj, (((1,), (1,)), ((), ())), preferred_element_type=F32) + bias
            m_old = m_sc[g]
            m_new = jnp.maximum(m_old, jnp.max(s, axis=1, keepdims=True))
            alpha = jnp.exp2(m_old - m_new)
            p = jnp.exp2(s - jnp.concatenate([m_new] * (width // LANES), axis=1)).astype(BF16)
            pv = jnp.dot(p, vj, preferred_element_type=F32)
            l_sc[g] = alpha * l_sc[g] + pv[:, LANES:]
            acc_sc[g] = alpha * acc_sc[g] + pv[:, :LANES]
            m_sc[g] = m_new

    npair = jnp.minimum(nch // 2, seq_pad // (2 * KC))

    def single_body(j, carry):
        att_step(chunk_start(j), KC, bias_sc[j], range(ATT_KV_HEADS))
        return carry

    for groups in ATT_PAIR_STEP_GROUPS:
        def pair_body(j, carry, groups=groups):
            bias = jnp.concatenate([bias_sc[2 * j], bias_sc[2 * j + 1]], axis=1)
            att_step(pl.multiple_of(j * 2 * KC, 2 * KC), 2 * KC, bias, groups)
            return carry

        lax.fori_loop(0, npair, pair_body, 0)
    lax.fori_loop(2 * npair, nch, single_body, 0)

    for g in range(ATT_KV_HEADS):
        og = acc_sc[g] / l_sc[g]
        for pair in range(2):
            even = og[(2 * pair) * QB:(2 * pair + 1) * QB]
            odd = og[(2 * pair + 1) * QB:(2 * pair + 2) * QB]
            if g % 2:
                even = pltpu.roll(even, 64, 1)
            else:
                odd = pltpu.roll(odd, 64, 1)
            c = 2 * g + pair
            o_ref[0, :, c * LANES:(c + 1) * LANES] = jnp.where(lane1 < 64, even, odd).astype(BF16)


def _dsa_core(q, qsw, k, v, iq, ik, iw, top_k):
    B, seq_pad = q.shape[0], q.shape[1]
    nqb = seq_pad // Q_BLOCK
    nkc = -(-seq_pad // KEY_CHUNK)
    rows = ATT_GROUP * Q_BLOCK
    qblock = lambda n: pl.BlockSpec((1, Q_BLOCK, n), lambda b, i: (b, i, 0))
    whole = lambda n: pl.BlockSpec((1, seq_pad, n), lambda b, i: (b, 0, 0))
    return pl.pallas_call(
        functools.partial(_dsa_core_kernel, top_k=top_k, seq_pad=seq_pad),
        grid=(B, nqb),
        in_specs=[qblock(D_MODEL), qblock(D_MODEL), whole(256), whole(512),
                  qblock(IDX_HEADS * IDX_DIM), whole(IDX_DIM), qblock(LANES)],
        out_specs=qblock(D_MODEL),
        out_shape=jax.ShapeDtypeStruct((B, seq_pad, D_MODEL), BF16),
        scratch_shapes=[
            pltpu.VMEM((IDX_DIM, IDX_HEADS * Q_BLOCK), BF16),
            pltpu.VMEM((LANES, Q_BLOCK), F32),
            pltpu.VMEM((ATT_KV_HEADS, rows, LANES), BF16),
            pltpu.VMEM((nkc, KEY_CHUNK, Q_BLOCK), jnp.int32),
            pltpu.VMEM((nkc, Q_BLOCK, KEY_CHUNK), F32),
            pltpu.VMEM((1, Q_BLOCK), jnp.int32),
            pltpu.VMEM((ATT_KV_HEADS, rows, LANES), F32),
            pltpu.VMEM((ATT_KV_HEADS, rows, LANES), F32),
            pltpu.VMEM((ATT_KV_HEADS, rows, LANES), F32),
        ],
        compiler_params=_cparams(("arbitrary", "arbitrary")),
        name="dsa_core",
    )(q, qsw, k, v, iq, ik, iw)


def _out_proj_kernel(h_ref, y_ref, w_ref, o_ref):
    o_ref[...] = h_ref[...] + jnp.dot(y_ref[...], w_ref[...], preferred_element_type=F32)


def _out_proj(h, y, w, tm):
    T = h.shape[0]
    row = lambda i: (i, 0)
    return pl.pallas_call(
        _out_proj_kernel,
        grid=(T // tm,),
        in_specs=[pl.BlockSpec((tm, D_MODEL), row), pl.BlockSpec((tm, D_MODEL), row),
                  pl.BlockSpec((D_MODEL, D_MODEL), lambda i: (0, 0))],
        out_specs=pl.BlockSpec((tm, D_MODEL), row),
        out_shape=jax.ShapeDtypeStruct((T, D_MODEL), F32),
        compiler_params=_cparams(("arbitrary",)),
        name="out_proj",
    )(h, y, w)


def _mlstm_in_kernel(h_ref, g_ref, w_ref, q_ref, k_ref, v_ref, o_ref, gate_ref):
    xn = _rms_rows(h_ref[...], g_ref[...]).astype(BF16)
    z = jnp.dot(xn, w_ref[...], preferred_element_type=F32)
    q_ref[...] = z[:, 0:512].astype(BF16)
    k_ref[...] = (z[:, 512:1024] * (M_QK_DIM ** -0.5)).astype(BF16)
    v_ref[...] = z[:, 1024:2048].astype(BF16)
    o_ref[...] = z[:, 2048:3072]
    gate_ref[...] = z[:, 3072:3200]


def _mlstm_in_proj(h, gain, w, tm):
    T = h.shape[0]
    row = lambda i: (i, 0)
    fixed = lambda i: (0, 0)
    outs = [(512, BF16), (512, BF16), (1024, BF16), (1024, F32), (128, F32)]
    return pl.pallas_call(
        _mlstm_in_kernel,
        grid=(T // tm,),
        in_specs=[pl.BlockSpec((tm, D_MODEL), row), pl.BlockSpec((1, D_MODEL), fixed),
                  pl.BlockSpec((D_MODEL, MLSTM_IN_PAD), fixed)],
        out_specs=[pl.BlockSpec((tm, n), row) for n, _ in outs],
        out_shape=[jax.ShapeDtypeStruct((T, n), dt) for n, dt in outs],
        compiler_params=_cparams(("arbitrary",)),
        name="mlstm_in_proj",
    )(h, gain, w)


def _mlstm_kernel(q_ref, k_ref, v_ref, o_ref, gate_ref, bias_ref, gout_ref, y_ref, ct_sc, n_sc, m_sc):
    C = M_CHUNK

    @pl.when(pl.program_id(1) == 0)
    def _():
        ct_sc[...] = jnp.zeros(ct_sc.shape, F32)
        n_sc[...] = jnp.zeros(n_sc.shape, F32)
        m_sc[...] = jnp.zeros(m_sc.shape, F32)

    lane8 = lax.broadcasted_iota(jnp.int32, (8, C), 1)
    t_idx = lax.broadcasted_iota(jnp.int32, (C, C), 0)
    s_idx = lax.broadcasted_iota(jnp.int32, (C, C), 1)
    for bi in range(q_ref.shape[0]):
        pre = gate_ref[bi].T[0:8, :] + bias_ref[...]
        capped = GATE_CAP * jnp.tanh(pre / GATE_CAP)
        log_f = -(jnp.maximum(-capped, 0.0) + jnp.log1p(jnp.exp(-jnp.abs(capped))))
        b = log_f
        sh = 1
        while sh < C:
            b = b + jnp.where(lane8 >= sh, pltpu.roll(b, sh, 1), 0.0)
            sh *= 2
        stacked = jnp.concatenate([b[4:8], capped[0:4]], axis=0)
        cols = jnp.concatenate([stacked, jnp.zeros((C - 8, C), F32)], axis=0).T

        for h in range(M_HEADS):
            st = bi * M_HEADS + h
            qh = q_ref[bi, :, h * M_QK_DIM:(h + 1) * M_QK_DIM]
            kh = k_ref[bi, :, h * M_QK_DIM:(h + 1) * M_QK_DIM]
            vh = v_ref[bi, :, h * M_V_DIM:(h + 1) * M_V_DIM]
            b_row, li_row = stacked[h:h + 1, :], stacked[4 + h:5 + h, :]
            b_col, li_col = cols[:, h:h + 1], cols[:, 4 + h:5 + h]
            m_st = m_sc[st:st + 1, 0:1]
            dmat = jnp.where(s_idx <= t_idx, b_col - b_row + li_row, -jnp.inf)
            inter = b_col + m_st
            m_t = jnp.maximum(inter, jnp.max(dmat, axis=1, keepdims=True))
            w_inter = jnp.exp(inter - m_t)
            qk = lax.dot_general(qh, kh, (((1,), (1,)), ((), ())), preferred_element_type=F32)
            s = qk * jnp.exp(dmat - m_t)
            ct = ct_sc[st]
            num = (w_inter * jnp.dot(qh, ct.astype(BF16), preferred_element_type=F32)
                   + jnp.dot(s.astype(BF16), vh, preferred_element_type=F32))
            qn = jnp.sum(qh.astype(F32) * n_sc[st:st + 1, :], axis=1, keepdims=True)
            den = w_inter * qn + jnp.sum(s, axis=1, keepdims=True)
            hout = num / jnp.maximum(jnp.abs(den), jnp.exp(-m_t))

            b_last = b_row[:, C - 1:C]
            m_new = jnp.maximum(b_last + m_st, jnp.max(b_last - b_row + li_row, axis=1, keepdims=True))
            decay = jnp.exp(b_last + m_st - m_new)
            wk = jnp.exp(b_last - b_col + li_col - m_new)
            kw = kh.astype(F32) * wk
            ct_sc[st] = decay * ct + jnp.dot(kw.T.astype(BF16), vh, preferred_element_type=F32)
            n_sc[st:st + 1, :] = decay * n_sc[st:st + 1, :] + jnp.sum(kw, axis=0, keepdims=True)
            m_sc[st:st + 1, :] = jnp.broadcast_to(m_new, (1, LANES))

            cs = slice(h * M_V_DIM, (h + 1) * M_V_DIM)
            hn = _rms_rows(hout, gout_ref[:, cs])
            y_ref[bi, :, cs] = (hn * jax.nn.sigmoid(o_ref[bi, :, cs])).astype(BF16)


def _mlstm_core(q, k, v, o, gates, bias8, gout):
    B, seq_pad = q.shape[0], q.shape[1]
    nb = M_BATCH_ROWS
    C = M_CHUNK
    blk = lambda n: pl.BlockSpec((nb, C, n), lambda b, c: (b, c, 0))
    fixed = lambda b, c: (0, 0)
    return pl.pallas_call(
        _mlstm_kernel,
        grid=(B // nb, seq_pad // C),
        in_specs=[blk(512), blk(512), blk(1024), blk(1024), blk(LANES),
                  pl.BlockSpec((8, 1), fixed), pl.BlockSpec((1, D_MODEL), fixed)],
        out_specs=blk(D_MODEL),
        out_shape=jax.ShapeDtypeStruct((B, seq_pad, D_MODEL), BF16),
        scratch_shapes=[pltpu.VMEM((nb * M_HEADS, M_QK_DIM, M_V_DIM), F32),
                        pltpu.VMEM((nb * M_HEADS, M_QK_DIM), F32),
                        pltpu.VMEM((nb * M_HEADS, LANES), F32)],
        compiler_params=_cparams(("arbitrary", "arbitrary")),
        name="mlstm_core",
    )(q, k, v, o, gates, bias8, gout)


def _ffn_kernel(h_ref, y_ref, wo_ref, g_ref, wg_ref, wu_ref, wd_ref, o_ref, xn_sc, acc_sc):
    f = pl.program_id(1)

    @pl.when(f == 0)
    def _():
        x = h_ref[...] + jnp.dot(y_ref[...], wo_ref[...], preferred_element_type=F32)
        xn_sc[...] = _rms_rows(x, g_ref[...]).astype(BF16)
        acc_sc[...] = x

    xn = xn_sc[...]
    gate = jnp.dot(xn, wg_ref[...], preferred_element_type=F32)
    up = jnp.dot(xn, wu_ref[...], preferred_element_type=F32)
    act = (gate * jax.nn.sigmoid(gate) * up).astype(BF16)
    acc_sc[...] += jnp.dot(act, wd_ref[...], preferred_element_type=F32)

    @pl.when(f == pl.num_programs(1) - 1)
    def _():
        o_ref[...] = acc_sc[...]


def _ffn(h, y, wo, gain, wg, wu, wd, tm, tf):
    T = h.shape[0]
    return pl.pallas_call(
        _ffn_kernel,
        grid=(T // tm, D_FF // tf),
        in_specs=[pl.BlockSpec((tm, D_MODEL), lambda i, f: (i, 0)),
                  pl.BlockSpec((tm, D_MODEL), lambda i, f: (i, 0)),
                  pl.BlockSpec((D_MODEL, D_MODEL), lambda i, f: (0, 0)),
                  pl.BlockSpec((1, D_MODEL), lambda i, f: (0, 0)),
                  pl.BlockSpec((D_MODEL, tf), lambda i, f: (0, f)),
                  pl.BlockSpec((D_MODEL, tf), lambda i, f: (0, f)),
                  pl.BlockSpec((tf, D_MODEL), lambda i, f: (f, 0))],
        out_specs=pl.BlockSpec((tm, D_MODEL), lambda i, f: (i, 0)),
        out_shape=jax.ShapeDtypeStruct((T, D_MODEL), F32),
        scratch_shapes=[pltpu.VMEM((tm, D_MODEL), BF16), pltpu.VMEM((tm, D_MODEL), F32)],
        compiler_params=_cparams(("arbitrary", "arbitrary")),
        name="ffn_dense",
    )(h, y, wo, gain, wg, wu, wd)


def _moe_kernel(h_ref, g_ref, wr_ref, wg_ref, wu_ref, wd_ref, o_ref,
                xn_sc, xs_sc, ys_sc, posc_sc, posr_sc, comb_sc, cnt_sc):
    e = pl.program_id(1)
    f = pl.program_id(2)
    tm = h_ref.shape[0]
    SB = _moe_slot_block(tm)
    TB = _moe_token_block(tm)
    lane = lax.broadcasted_iota(jnp.int32, (tm, LANES), 1)

    @pl.when((e == 0) & (f == 0))
    def _():
        x = h_ref[...]
        xn = _rms_rows(x, g_ref[...])
        xn_sc[...] = xn.astype(BF16)
        o_ref[...] = x
        logits = jnp.dot(xn, wr_ref[...], preferred_element_type=F32, precision=lax.Precision.HIGHEST)
        logits = jnp.where(lane < N_EXPERTS, logits, -jnp.inf)
        m1 = jnp.max(logits, axis=1, keepdims=True)
        i1 = jnp.min(jnp.where(logits == m1, lane, LANES), axis=1, keepdims=True)
        rest = jnp.where(lane == i1, -jnp.inf, logits)
        m2 = jnp.max(rest, axis=1, keepdims=True)
        i2 = jnp.min(jnp.where(rest == m2, lane, LANES), axis=1, keepdims=True)
        e2 = jnp.exp(m2 - m1)
        g1 = 1.0 / (1.0 + e2)
        comb_sc[...] = jnp.where(lane == i1, g1, jnp.where(lane == i2, e2 * g1, 0.0))
        member = jnp.where(lane == i1, 1.0, jnp.where(lane == i2, 1.0, 0.0))
        tri = jnp.where(lax.broadcasted_iota(jnp.int32, (TB, TB), 0) >= lax.broadcasted_iota(jnp.int32, (TB, TB), 1),
                        1.0, 0.0).astype(BF16)
        carry = jnp.zeros((1, LANES), F32)
        for b in range(tm // TB):
            mb = member[b * TB:(b + 1) * TB]
            incl = jnp.dot(tri, mb.astype(BF16), preferred_element_type=F32)
            posc_sc[b * TB:(b + 1) * TB, :] = jnp.where(mb > 0.0, incl - 1.0 + carry, -1.0)
            carry = carry + incl[TB - 1:TB, :]
        cnt_sc[...] = carry
        for b in range(tm // LANES):
            posr_sc[:, b * LANES:(b + 1) * LANES] = posc_sc[b * LANES:(b + 1) * LANES, :].T

    lane_row = lax.broadcasted_iota(jnp.int32, (1, LANES), 1)
    count = jnp.sum(jnp.where(lane_row == e, cnt_sc[...], 0.0))
    nblk = ((count + (SB - 1.0)) * (1.0 / SB)).astype(jnp.int32)

    @pl.when(f == 0)
    def _():
        slots_of_tokens = posr_sc[pl.ds(e, 1), :]

        def body(r, carry):
            row0 = pl.multiple_of(r * SB, 16)
            slot = (r * SB + lax.broadcasted_iota(jnp.int32, (SB, tm), 0)).astype(F32)
            onehot = jnp.where(slots_of_tokens == slot, 1.0, 0.0).astype(BF16)
            xs_sc[pl.ds(row0, SB), :] = jnp.dot(onehot, xn_sc[...], preferred_element_type=F32).astype(BF16)
            ys_sc[pl.ds(row0, SB), :] = jnp.zeros((SB, D_MODEL), F32)
            return carry

        lax.fori_loop(0, nblk, body, 0)

    def ffn_rows(row0, nrows):
        xs = xs_sc[pl.ds(row0, nrows), :]
        gate = jnp.dot(xs, wg_ref[0], preferred_element_type=F32)
        up = jnp.dot(xs, wu_ref[0], preferred_element_type=F32)
        act = (gate * jax.nn.sigmoid(gate) * up).astype(BF16)
        ys_sc[pl.ds(row0, nrows), :] += jnp.dot(act, wd_ref[0], preferred_element_type=F32)

    def pair_body(r, carry):
        ffn_rows(pl.multiple_of(r * 2 * SB, 16), 2 * SB)
        return carry

    lax.fori_loop(0, nblk // 2, pair_body, 0)

    @pl.when(nblk % 2 == 1)
    def _():
        ffn_rows(pl.multiple_of((nblk - 1) * SB, 16), SB)

    @pl.when(f == pl.num_programs(2) - 1)
    def _():
        on_e = lane == e
        gate_e = jnp.sum(jnp.where(on_e, comb_sc[...], 0.0), axis=1, keepdims=True)
        slot_e = jnp.sum(jnp.where(on_e, posc_sc[...], 0.0), axis=1, keepdims=True)

        def body(r, carry):
            row0 = pl.multiple_of(r * SB, 16)
            ys = ys_sc[pl.ds(row0, SB), :].astype(BF16)
            slot = (r * SB + lax.broadcasted_iota(jnp.int32, (TB, SB), 1)).astype(F32)
            for tb in range(tm // TB):
                rows = slice(tb * TB, (tb + 1) * TB)
                onehot = jnp.where(slot_e[rows] == slot, 1.0, 0.0).astype(BF16)
                o_ref[rows, :] += gate_e[rows] * jnp.dot(onehot, ys, preferred_element_type=F32)
            return carry

        lax.fori_loop(0, nblk, body, 0)


def _moe_slot_block(tm):
    return MOE_SLOT_BLOCK if tm > MOE_SLOT_BLOCK else tm


def _moe_token_block(tm):
    return MOE_TOKEN_BLOCK if tm % MOE_TOKEN_BLOCK == 0 else tm


def _moe(h, gain, wr, wg, wu, wd, tm, tf):
    T = h.shape[0]
    sb = _moe_slot_block(tm)
    cap = -(-tm // sb) * sb
    return pl.pallas_call(
        _moe_kernel,
        grid=(T // tm, N_EXPERTS, D_FF // tf),
        in_specs=[pl.BlockSpec((tm, D_MODEL), lambda i, e, f: (i, 0), pipeline_mode=pl.Buffered(1)),
                  pl.BlockSpec((1, D_MODEL), lambda i, e, f: (0, 0)),
                  pl.BlockSpec((D_MODEL, LANES), lambda i, e, f: (0, 0)),
                  pl.BlockSpec((1, D_MODEL, tf), lambda i, e, f: (e, 0, f)),
                  pl.BlockSpec((1, D_MODEL, tf), lambda i, e, f: (e, 0, f)),
                  pl.BlockSpec((1, tf, D_MODEL), lambda i, e, f: (e, f, 0))],
        out_specs=pl.BlockSpec((tm, D_MODEL), lambda i, e, f: (i, 0)),
        out_shape=jax.ShapeDtypeStruct((T, D_MODEL), F32),
        scratch_shapes=[pltpu.VMEM((tm, D_MODEL), BF16), pltpu.VMEM((cap, D_MODEL), BF16),
                        pltpu.VMEM((cap, D_MODEL), F32), pltpu.VMEM((tm, LANES), F32),
                        pltpu.VMEM((LANES, tm), F32), pltpu.VMEM((tm, LANES), F32),
                        pltpu.VMEM((1, LANES), F32)],
        compiler_params=_cparams(("arbitrary", "arbitrary", "arbitrary")),
        name="moe_sparse",
    )(h, gain, wr, wg, wu, wd)


def _rope_tables(seq_pad):
    pos = jnp.arange(seq_pad, dtype=F32)[:, None]
    lane = np.arange(LANES)

    def table(head_dim):
        half = head_dim // 2
        inv = ROPE_THETA ** (-jnp.arange(half, dtype=F32) / half)
        d = lane % head_dim
        ang = pos * inv[d % half][None, :]
        return jnp.cos(ang), jnp.sin(ang), jnp.asarray(d < half)[None, :]

    c64, s64, lo64 = table(ATT_HEAD_DIM)
    c128, s128, lo128 = table(IDX_DIM)
    return (c64, jnp.where(lo64, -s64, 0.0), jnp.where(lo64, 0.0, s64), c128, jnp.where(lo128, -s128, s128))


def _pad_cols(w, n):
    return jnp.pad(w, ((0, 0), (0, n - w.shape[1])))


def kernel(x, meta, norm_mixer, norm_ffn, dsa_w_in, dsa_q_norm, dsa_k_norm, dsa_w_out, mlstm_w_in, mlstm_b_i,
           mlstm_b_f, mlstm_out_norm, mlstm_w_out, ffn_w_gate, ffn_w_up, ffn_w_down, moe_router, moe_w_gate,
           moe_w_up, moe_w_down):
    B, S, D = x.shape
    L = S + N_META
    top_k = min(TOPK_MAX, S // 4)
    seq_pad = max(-(-L // Q_BLOCK) * Q_BLOCK, KEY_CHUNK)
    T = B * seq_pad
    tm_proj = 384 if seq_pad % 384 == 0 else Q_BLOCK
    tm_ffn = 768 if T % 768 == 0 else Q_BLOCK
    tm_moe = 1536 if T % 1536 == 0 else Q_BLOCK
    tf = 512

    h = jnp.concatenate([jnp.broadcast_to(meta[None].astype(x.dtype), (B, N_META, D)), x,
                         jnp.zeros((B, seq_pad - L, D), x.dtype)], axis=1).reshape(T, D)
    tabs = _rope_tables(seq_pad)
    depth = norm_mixer.shape[0]

    for i in range(depth):
        j = i // 2
        gain_m = norm_mixer[i][None, :]
        gain_f = norm_ffn[i][None, :]
        if i % 2 == 0:
            w_in = _pad_cols(dsa_w_in[j], DSA_IN_PAD).astype(BF16)
            gq = jnp.tile(dsa_q_norm[j], 2)[None, :]
            gk = jnp.tile(dsa_k_norm[j], 2)[None, :]
            proj = _dsa_in_proj(h, gain_m, w_in, gq, gk, tabs, seq_pad, tm_proj)
            att = _dsa_core(*[a.reshape(B, seq_pad, a.shape[-1]) for a in proj], top_k)
            h = _ffn(h, att.reshape(T, D), dsa_w_out[j].astype(BF16), gain_f, ffn_w_gate[j].astype(BF16),
                     ffn_w_up[j].astype(BF16), ffn_w_down[j].astype(BF16), tm_ffn, tf)
        else:
            w_in = _pad_cols(mlstm_w_in[j], MLSTM_IN_PAD).astype(BF16)
            q, k, v, o, gates = _mlstm_in_proj(h, gain_m, w_in, tm_proj)
            bias8 = jnp.concatenate([mlstm_b_i[j], mlstm_b_f[j]])[:, None]
            y = _mlstm_core(*[a.reshape(B, seq_pad, a.shape[-1]) for a in (q, k, v, o, gates)], bias8,
                            mlstm_out_norm[j][None, :])
            h = _out_proj(h, y.reshape(T, D), mlstm_w_out[j].astype(BF16), tm_proj)
            wr = _pad_cols(moe_router[j], LANES)
            h = _moe(h, gain_f, wr, moe_w_gate[j].astype(BF16), moe_w_up[j].astype(BF16),
                     moe_w_down[j].astype(BF16), tm_moe, tf)
    return h.reshape(B, seq_pad, D)[:, N_META:L]
```

```python
import functools

import jax
import jax.numpy as jnp
import numpy as np
from jax import lax
from jax.experimental import pallas as pl
from jax.experimental.pallas import tpu as pltpu

F32 = jnp.float32
BF16 = jnp.bfloat16

D_MODEL = 1024
N_META = 16
RMS_EPS = 1e-6
ROPE_THETA = 10000.0
ATT_HEADS = 16
ATT_KV_HEADS = 4
ATT_HEAD_DIM = 64
ATT_GROUP = 4
IDX_HEADS = 8
IDX_DIM = 128
TOPK_MAX = 256
M_HEADS = 4
M_QK_DIM = 128
M_V_DIM = 256
GATE_CAP = 15.0
D_FF = 3584
N_EXPERTS = 8

LANES = 128
SUBLANES = 8
Q_BLOCK = 128
KEY_CHUNK = 256
ATT_PAIR_STEP_GROUPS = ((0, 1, 2, 3),)
M_CHUNK = 128
M_BATCH_ROWS = 1
MOE_SLOT_BLOCK = 224
MOE_TOKEN_BLOCK = 256
VMEM_LIMIT = 52 * 1024 * 1024
INT_MIN = -(2 ** 31)
INT_MAX = 2 ** 31 - 1
LOG2E = 1.4426950408889634

DSA_IN_PAD = 2816
MLSTM_IN_PAD = 3200


def _cparams(sem):
    return pltpu.CompilerParams(dimension_semantics=sem, vmem_limit_bytes=VMEM_LIMIT)


def _rms_rows(x, gain):
    ms = jnp.mean(x * x, axis=-1, keepdims=True)
    return x * lax.rsqrt(ms + RMS_EPS) * gain


def _dsa_in_kernel(h_ref, g_ref, w_ref, gq_ref, gk_ref, c64_ref, sa64_ref, sb64_ref, c128_ref, s128_ref,
                   smat_ref, q_ref, qsw_ref, k_ref, v_ref, iq_ref, ik_ref, iw_ref):
    xn = _rms_rows(h_ref[...], g_ref[...]).astype(BF16)
    z = jnp.dot(xn, w_ref[...], preferred_element_type=F32)
    c64, sa64, sb64 = c64_ref[...], sa64_ref[...], sb64_ref[...]
    c128, s128 = c128_ref[...], s128_ref[...]
    smat = smat_ref[...]

    def head_norm_rope(zc, gain):
        z2 = zc * zc
        hi = z2.astype(BF16)
        lo = (z2 - hi.astype(F32)).astype(BF16)
        ms = jnp.dot(hi, smat, preferred_element_type=F32) + jnp.dot(lo, smat, preferred_element_type=F32)
        y = zc * lax.rsqrt(ms + RMS_EPS) * gain
        return y * c64 + pltpu.roll(y, 96, 1) * sa64 + pltpu.roll(y, 32, 1) * sb64

    for c in range(8):
        r = head_norm_rope(z[:, c * LANES:(c + 1) * LANES], gq_ref[...]) * (ATT_HEAD_DIM ** -0.5 * LOG2E)
        q_ref[:, c * LANES:(c + 1) * LANES] = r.astype(BF16)
        qsw_ref[:, c * LANES:(c + 1) * LANES] = pltpu.roll(r, 64, 1).astype(BF16)
    for c in range(2):
        r = head_norm_rope(z[:, 1024 + c * LANES:1024 + (c + 1) * LANES], gk_ref[...])
        k_ref[:, c * LANES:(c + 1) * LANES] = r.astype(BF16)
    ones = jnp.ones((z.shape[0], LANES), BF16)
    for c in range(2):
        v_ref[:, 2 * c * LANES:(2 * c + 1) * LANES] = z[:, 1280 + c * LANES:1280 + (c + 1) * LANES].astype(BF16)
        v_ref[:, (2 * c + 1) * LANES:(2 * c + 2) * LANES] = ones
    for c in range(9):
        zc = z[:, 1536 + c * LANES:1536 + (c + 1) * LANES]
        r = (zc * c128 + pltpu.roll(zc, 64, 1) * s128).astype(BF16)
        if c < 8:
            iq_ref[:, c * LANES:(c + 1) * LANES] = r
        else:
            ik_ref[...] = r
    iw_ref[...] = z[:, 2688:2816] * (IDX_HEADS ** -0.5 * IDX_DIM ** -0.5)


def _dsa_in_proj(h, gain, w, gq, gk, tabs, seq_pad, tm):
    T = h.shape[0]
    nt = seq_pad // tm
    row = lambda i: (i, 0)
    fixed = lambda i: (0, 0)
    pos = lambda i: (i % nt, 0)
    tab_spec = pl.BlockSpec((tm, LANES), pos)
    smat = jnp.asarray(np.kron(np.eye(2), np.full((64, 64), 1.0 / 64)), BF16)
    outs = [(1024, BF16), (1024, BF16), (256, BF16), (512, BF16), (1024, BF16), (128, BF16), (128, F32)]
    return pl.pallas_call(
        _dsa_in_kernel,
        grid=(T // tm,),
        in_specs=[pl.BlockSpec((tm, D_MODEL), row), pl.BlockSpec((1, D_MODEL), fixed),
                  pl.BlockSpec((D_MODEL, DSA_IN_PAD), fixed), pl.BlockSpec((1, LANES), fixed),
                  pl.BlockSpec((1, LANES), fixed), tab_spec, tab_spec, tab_spec, tab_spec, tab_spec,
                  pl.BlockSpec((LANES, LANES), fixed)],
        out_specs=[pl.BlockSpec((tm, n), row) for n, _ in outs],
        out_shape=[jax.ShapeDtypeStruct((T, n), dt) for n, dt in outs],
        compiler_params=_cparams(("arbitrary",)),
        name="dsa_in_proj",
    )(h, gain, w, gq, gk, *tabs, smat)


def _dsa_core_kernel(q_ref, qsw_ref, k_ref, v_ref, iq_ref, ik_ref, iw_ref, o_ref,
                     iqt_sc, iwt_sc, qs_sc, keys_sc, bias_sc, cut_sc, m_sc, l_sc, acc_sc,
                     *, top_k, seq_pad):
    i = pl.program_id(1)
    QB, KC = Q_BLOCK, KEY_CHUNK
    nch = (i * QB + QB + KC - 1) // KC
    kf = float(top_k)
    lane1 = lax.broadcasted_iota(jnp.int32, (QB, LANES), 1)

    for h in range(IDX_HEADS):
        iqt_sc[:, h * QB:(h + 1) * QB] = iq_ref[0, :, h * LANES:(h + 1) * LANES].astype(F32).T.astype(BF16)
    iwt_sc[...] = iw_ref[0].T

    for g in range(ATT_KV_HEADS):
        for r in range(ATT_GROUP):
            h = g * ATT_GROUP + r
            src = q_ref if h % 2 == g % 2 else qsw_ref
            chunk = src[0, :, (h // 2) * LANES:(h // 2 + 1) * LANES].astype(F32)
            keep = (lane1 >= 64) if g % 2 else (lane1 < 64)
            qs_sc[g, r * QB:(r + 1) * QB, :] = jnp.where(keep, chunk, 0.0).astype(BF16)

    qpos = i * QB + lax.broadcasted_iota(jnp.int32, (KC, QB), 1)
    krow = lax.broadcasted_iota(jnp.int32, (KC, QB), 0)

    def chunk_start(j):
        return pl.multiple_of(jnp.minimum(j * KC, seq_pad - KC), LANES)

    def key_valid(j, start):
        kidx = start + krow
        return (kidx <= qpos) & (kidx >= j * KC)

    def for_chunks(fn):
        def body(j2, carry):
            fn(2 * j2)
            fn(2 * j2 + 1)
            return carry

        lax.fori_loop(0, nch // 2, body, 0)

        @pl.when(nch % 2 == 1)
        def _():
            fn(nch - 1)

    def score_chunk(j):
        start = chunk_start(j)
        s = jnp.dot(ik_ref[0, pl.ds(start, KC), :], iqt_sc[...], preferred_element_type=F32)
        acc = jnp.zeros((KC, QB), F32)
        for h in range(IDX_HEADS):
            acc = acc + jnp.maximum(s[:, h * QB:(h + 1) * QB], 0.0) * iwt_sc[h:h + 1, :]
        bits = pltpu.bitcast(acc, jnp.int32)
        key = jnp.where(bits >= 0, bits, bits ^ jnp.int32(INT_MAX))
        keys_sc[j] = jnp.where(key_valid(j, start), key, jnp.int32(INT_MIN))

    for_chunks(score_chunk)

    tiles = (KC // SUBLANES, SUBLANES, QB)
    krow3 = (lax.broadcasted_iota(jnp.int32, tiles, 0) * SUBLANES + lax.broadcasted_iota(jnp.int32, tiles, 1))

    def count(pred):
        def body(j, acc):
            kk = keys_sc[j].reshape(tiles)
            hit = jnp.where(pred(kk, chunk_start(j)), jnp.float32(1.0), jnp.float32(0.0))
            while hit.shape[0] > 1:
                half = hit.shape[0] // 2
                hit = hit[:half] + hit[half:]
            return acc + hit[0]
        acc = lax.fori_loop(0, nch, body, jnp.zeros((SUBLANES, QB), F32))
        for shift in (4, 2, 1):
            acc = acc + pltpu.roll(acc, shift, 0)
        return acc

    zero = jnp.zeros((SUBLANES, QB), jnp.int32)
    v0 = jnp.where(count(lambda kk, base: kk >= zero) >= kf, zero, jnp.int32(INT_MIN))

    def bit_body(b, v):
        cand = v | lax.shift_left(jnp.int32(1), (30 - b).astype(jnp.int32))
        return jnp.where(count(lambda kk, base: kk >= cand) >= kf, cand, v)

    thr = lax.fori_loop(jnp.int32(0), jnp.int32(31), bit_body, v0)
    cut_sc[...] = jnp.full((SUBLANES, QB), INT_MAX, jnp.int32)
    n_ge = count(lambda kk, base: kk >= thr)

    @pl.when(jnp.max(n_ge) > kf)
    def _():
        need = kf - count(lambda kk, base: kk > thr)

        def cut_body(b, c):
            cand = c | lax.shift_left(jnp.int32(1), (12 - b).astype(jnp.int32))
            n = count(lambda kk, base: jnp.where(kk == thr, base + krow3, jnp.int32(INT_MAX)) < cand)
            return jnp.where(n < need, cand, c)

        cut_sc[...] = lax.fori_loop(jnp.int32(0), jnp.int32(13), cut_body, zero)

    thr1 = thr[0:1, :]
    cut1 = cut_sc[0:1, :]

    def bias_chunk(j):
        kk = keys_sc[j]
        start = chunk_start(j)
        sel = (kk > thr1) | ((kk == thr1) & (start + krow <= cut1))
        bt = jnp.where(sel & key_valid(j, start), 0.0, -1e30)
        bias_sc[j] = jnp.concatenate([bt[c * QB:(c + 1) * QB].T for c in range(KC // QB)], axis=1)

    for_chunks(bias_chunk)

    m_sc[...] = jnp.full(m_sc.shape, -1e9, F32)
    l_sc[...] = jnp.zeros(l_sc.shape, F32)
    acc_sc[...] = jnp.zeros(acc_sc.shape, F32)

    def att_step(start, width, bias, groups):
        bias = jnp.concatenate([bias] * ATT_GROUP, axis=0)
        for g in groups:
            cg = g // 2
            kj = k_ref[0, pl.ds(start, width), cg * LANES:(cg + 1) * LANES]
            vj = v_ref[0, pl.ds(start, width), cg * 2 * LANES:(cg + 1) * 2 * LANES]
            s = lax.dot_general(qs_sc[g], kj, (((1,), (1,)), ((), ())), preferred_element_type=F32) + bias
            m_old = m_sc[g]
            m_new = jnp.maximum(m_old, jnp.max(s, axis=1, keepdims=True))
            alpha = jnp.exp2(m_old - m_new)
            p = jnp.exp2(s - jnp.concatenate([m_new] * (width // LANES), axis=1)).astype(BF16)
            pv = jnp.dot(p, vj, preferred_element_type=F32)
            l_sc[g] = alpha * l_sc[g] + pv[:, LANES:]
            acc_sc[g] = alpha * acc_sc[g] + pv[:, :LANES]
            m_sc[g] = m_new

    npair = jnp.minimum(nch // 2, seq_pad // (2 * KC))

    def single_body(j, carry):
        att_step(chunk_start(j), KC, bias_sc[j], range(ATT_KV_HEADS))
        return carry

    for groups in ATT_PAIR_STEP_GROUPS:
        def pair_body(j, carry, groups=groups):
            bias = jnp.concatenate([bias_sc[2 * j], bias_sc[2 * j + 1]], axis=1)
            att_step(pl.multiple_of(j * 2 * KC, 2 * KC), 2 * KC, bias, groups)
            return carry

        lax.fori_loop(0, npair, pair_body, 0)
    lax.fori_loop(2 * npair, nch, single_body, 0)

    for g in range(ATT_KV_HEADS):
        og = acc_sc[g] / l_sc[g]
        for pair in range(2):
            even = og[(2 * pair) * QB:(2 * pair + 1) * QB]
            odd = og[(2 * pair + 1) * QB:(2 * pair + 2) * QB]
            if g % 2:
                even = pltpu.roll(even, 64, 1)
            else:
                odd = pltpu.roll(odd, 64, 1)
            c = 2 * g + pair
            o_ref[0, :, c * LANES:(c + 1) * LANES] = jnp.where(lane1 < 64, even, odd).astype(BF16)


def _dsa_core(q, qsw, k, v, iq, ik, iw, top_k):
    B, seq_pad = q.shape[0], q.shape[1]
    nqb = seq_pad // Q_BLOCK
    nkc = -(-seq_pad // KEY_CHUNK)
    rows = ATT_GROUP * Q_BLOCK
    qblock = lambda n: pl.BlockSpec((1, Q_BLOCK, n), lambda b, i: (b, i, 0))
    whole = lambda n: pl.BlockSpec((1, seq_pad, n), lambda b, i: (b, 0, 0))
    return pl.pallas_call(
        functools.partial(_dsa_core_kernel, top_k=top_k, seq_pad=seq_pad),
        grid=(B, nqb),
        in_specs=[qblock(D_MODEL), qblock(D_MODEL), whole(256), whole(512),
                  qblock(IDX_HEADS * IDX_DIM), whole(IDX_DIM), qblock(LANES)],
        out_specs=qblock(D_MODEL),
        out_shape=jax.ShapeDtypeStruct((B, seq_pad, D_MODEL), BF16),
        scratch_shapes=[
            pltpu.VMEM((IDX_DIM, IDX_HEADS * Q_BLOCK), BF16),
            pltpu.VMEM((LANES, Q_BLOCK), F32),
            pltpu.VMEM((ATT_KV_HEADS, rows, LANES), BF16),
            pltpu.VMEM((nkc, KEY_CHUNK, Q_BLOCK), jnp.int32),
            pltpu.VMEM((nkc, Q_BLOCK, KEY_CHUNK), F32),
            pltpu.VMEM((SUBLANES, Q_BLOCK), jnp.int32),
            pltpu.VMEM((ATT_KV_HEADS, rows, LANES), F32),
            pltpu.VMEM((ATT_KV_HEADS, rows, LANES), F32),
            pltpu.VMEM((ATT_KV_HEADS, rows, LANES), F32),
        ],
        compiler_params=_cparams(("arbitrary", "arbitrary")),
        name="dsa_core",
    )(q, qsw, k, v, iq, ik, iw)


def _out_proj_kernel(h_ref, y_ref, w_ref, o_ref):
    o_ref[...] = h_ref[...] + jnp.dot(y_ref[...], w_ref[...], preferred_element_type=F32)


def _out_proj(h, y, w, tm):
    T = h.shape[0]
    row = lambda i: (i, 0)
    return pl.pallas_call(
        _out_proj_kernel,
        grid=(T // tm,),
        in_specs=[pl.BlockSpec((tm, D_MODEL), row), pl.BlockSpec((tm, D_MODEL), row),
                  pl.BlockSpec((D_MODEL, D_MODEL), lambda i: (0, 0))],
        out_specs=pl.BlockSpec((tm, D_MODEL), row),
        out_shape=jax.ShapeDtypeStruct((T, D_MODEL), F32),
        compiler_params=_cparams(("arbitrary",)),
        name="out_proj",
    )(h, y, w)


def _mlstm_in_kernel(h_ref, g_ref, w_ref, q_ref, k_ref, v_ref, o_ref, gate_ref):
    xn = _rms_rows(h_ref[...], g_ref[...]).astype(BF16)
    z = jnp.dot(xn, w_ref[...], preferred_element_type=F32)
    q_ref[...] = z[:, 0:512].astype(BF16)
    k_ref[...] = (z[:, 512:1024] * (M_QK_DIM ** -0.5)).astype(BF16)
    v_ref[...] = z[:, 1024:2048].astype(BF16)
    o_ref[...] = z[:, 2048:3072]
    gate_ref[...] = z[:, 3072:3200]


def _mlstm_in_proj(h, gain, w, tm):
    T = h.shape[0]
    row = lambda i: (i, 0)
    fixed = lambda i: (0, 0)
    outs = [(512, BF16), (512, BF16), (1024, BF16), (1024, F32), (128, F32)]
    return pl.pallas_call(
        _mlstm_in_kernel,
        grid=(T // tm,),
        in_specs=[pl.BlockSpec((tm, D_MODEL), row), pl.BlockSpec((1, D_MODEL), fixed),
                  pl.BlockSpec((D_MODEL, MLSTM_IN_PAD), fixed)],
        out_specs=[pl.BlockSpec((tm, n), row) for n, _ in outs],
        out_shape=[jax.ShapeDtypeStruct((T, n), dt) for n, dt in outs],
        compiler_params=_cparams(("arbitrary",)),
        name="mlstm_in_proj",
    )(h, gain, w)


def _mlstm_kernel(q_ref, k_ref, v_ref, o_ref, gate_ref, bias_ref, gout_ref, y_ref, ct_sc, n_sc, m_sc):
    C = M_CHUNK

    @pl.when(pl.program_id(1) == 0)
    def _():
        ct_sc[...] = jnp.zeros(ct_sc.shape, F32)
        n_sc[...] = jnp.zeros(n_sc.shape, F32)
        m_sc[...] = jnp.zeros(m_sc.shape, F32)

    lane8 = lax.broadcasted_iota(jnp.int32, (8, C), 1)
    t_idx = lax.broadcasted_iota(jnp.int32, (C, C), 0)
    s_idx = lax.broadcasted_iota(jnp.int32, (C, C), 1)
    for bi in range(q_ref.shape[0]):
        pre = gate_ref[bi].T[0:8, :] + bias_ref[...]
        capped = GATE_CAP * jnp.tanh(pre / GATE_CAP)
        log_f = -(jnp.maximum(-capped, 0.0) + jnp.log1p(jnp.exp(-jnp.abs(capped))))
        b = log_f
        sh = 1
        while sh < C:
            b = b + jnp.where(lane8 >= sh, pltpu.roll(b, sh, 1), 0.0)
            sh *= 2
        stacked = jnp.concatenate([b[4:8], capped[0:4]], axis=0)
        cols = jnp.concatenate([stacked, jnp.zeros((C - 8, C), F32)], axis=0).T

        for h in range(M_HEADS):
            st = bi * M_HEADS + h
            qh = q_ref[bi, :, h * M_QK_DIM:(h + 1) * M_QK_DIM]
            kh = k_ref[bi, :, h * M_QK_DIM:(h + 1) * M_QK_DIM]
            vh = v_ref[bi, :, h * M_V_DIM:(h + 1) * M_V_DIM]
            b_row, li_row = stacked[h:h + 1, :], stacked[4 + h:5 + h, :]
            b_col, li_col = cols[:, h:h + 1], cols[:, 4 + h:5 + h]
            m_st = m_sc[st:st + 1, 0:1]
            dmat = jnp.where(s_idx <= t_idx, b_col - b_row + li_row, -jnp.inf)
            inter = b_col + m_st
            m_t = jnp.maximum(inter, jnp.max(dmat, axis=1, keepdims=True))
            w_inter = jnp.exp(inter - m_t)
            qk = lax.dot_general(qh, kh, (((1,), (1,)), ((), ())), preferred_element_type=F32)
            s = qk * jnp.exp(dmat - m_t)
            ct = ct_sc[st]
            num = (w_inter * jnp.dot(qh, ct.astype(BF16), preferred_element_type=F32)
                   + jnp.dot(s.astype(BF16), vh, preferred_element_type=F32))
            qn = jnp.sum(qh.astype(F32) * n_sc[st:st + 1, :], axis=1, keepdims=True)
            den = w_inter * qn + jnp.sum(s, axis=1, keepdims=True)
            hout = num / jnp.maximum(jnp.abs(den), jnp.exp(-m_t))

            b_last = b_row[:, C - 1:C]
            m_new = jnp.maximum(b_last + m_st, jnp.max(b_last - b_row + li_row, axis=1, keepdims=True))
            decay = jnp.exp(b_last + m_st - m_new)
            wk = jnp.exp(b_last - b_col + li_col - m_new)
            kw = kh.astype(F32) * wk
            ct_sc[st] = decay * ct + jnp.dot(kw.T.astype(BF16), vh, preferred_element_type=F32)
            n_sc[st:st + 1, :] = decay * n_sc[st:st + 1, :] + jnp.sum(kw, axis=0, keepdims=True)
            m_sc[st:st + 1, :] = jnp.broadcast_to(m_new, (1, LANES))

            cs = slice(h * M_V_DIM, (h + 1) * M_V_DIM)
            hn = _rms_rows(hout, gout_ref[:, cs])
            y_ref[bi, :, cs] = (hn * jax.nn.sigmoid(o_ref[bi, :, cs])).astype(BF16)


def _mlstm_core(q, k, v, o, gates, bias8, gout):
    B, seq_pad = q.shape[0], q.shape[1]
    nb = M_BATCH_ROWS
    C = M_CHUNK
    blk = lambda n: pl.BlockSpec((nb, C, n), lambda b, c: (b, c, 0))
    fixed = lambda b, c: (0, 0)
    return pl.pallas_call(
        _mlstm_kernel,
        grid=(B // nb, seq_pad // C),
        in_specs=[blk(512), blk(512), blk(1024), blk(1024), blk(LANES),
                  pl.BlockSpec((8, 1), fixed), pl.BlockSpec((1, D_MODEL), fixed)],
        out_specs=blk(D_MODEL),
        out_shape=jax.ShapeDtypeStruct((B, seq_pad, D_MODEL), BF16),
        scratch_shapes=[pltpu.VMEM((nb * M_HEADS, M_QK_DIM, M_V_DIM), F32),
                        pltpu.VMEM((nb * M_HEADS, M_QK_DIM), F32),
                        pltpu.VMEM((nb * M_HEADS, LANES), F32)],
        compiler_params=_cparams(("arbitrary", "arbitrary")),
        name="mlstm_core",
    )(q, k, v, o, gates, bias8, gout)


def _ffn_kernel(h_ref, y_ref, wo_ref, g_ref, wg_ref, wu_ref, wd_ref, o_ref, xn_sc, acc_sc):
    f = pl.program_id(1)

    @pl.when(f == 0)
    def _():
        x = h_ref[...] + jnp.dot(y_ref[...], wo_ref[...], preferred_element_type=F32)
        xn_sc[...] = _rms_rows(x, g_ref[...]).astype(BF16)
        acc_sc[...] = x

    xn = xn_sc[...]
    gate = jnp.dot(xn, wg_ref[...], preferred_element_type=F32)
    up = jnp.dot(xn, wu_ref[...], preferred_element_type=F32)
    act = (gate * jax.nn.sigmoid(gate) * up).astype(BF16)
    acc_sc[...] += jnp.dot(act, wd_ref[...], preferred_element_type=F32)

    @pl.when(f == pl.num_programs(1) - 1)
    def _():
        o_ref[...] = acc_sc[...]


def _ffn(h, y, wo, gain, wg, wu, wd, tm, tf):
    T = h.shape[0]
    return pl.pallas_call(
        _ffn_kernel,
        grid=(T // tm, D_FF // tf),
        in_specs=[pl.BlockSpec((tm, D_MODEL), lambda i, f: (i, 0)),
                  pl.BlockSpec((tm, D_MODEL), lambda i, f: (i, 0)),
                  pl.BlockSpec((D_MODEL, D_MODEL), lambda i, f: (0, 0)),
                  pl.BlockSpec((1, D_MODEL), lambda i, f: (0, 0)),
                  pl.BlockSpec((D_MODEL, tf), lambda i, f: (0, f)),
                  pl.BlockSpec((D_MODEL, tf), lambda i, f: (0, f)),
                  pl.BlockSpec((tf, D_MODEL), lambda i, f: (f, 0))],
        out_specs=pl.BlockSpec((tm, D_MODEL), lambda i, f: (i, 0)),
        out_shape=jax.ShapeDtypeStruct((T, D_MODEL), F32),
        scratch_shapes=[pltpu.VMEM((tm, D_MODEL), BF16), pltpu.VMEM((tm, D_MODEL), F32)],
        compiler_params=_cparams(("arbitrary", "arbitrary")),
        name="ffn_dense",
    )(h, y, wo, gain, wg, wu, wd)


def _moe_kernel(h_ref, g_ref, wr_ref, wg_ref, wu_ref, wd_ref, o_ref,
                xn_sc, xs_sc, ys_sc, posc_sc, posr_sc, comb_sc, cnt_sc):
    e = pl.program_id(1)
    f = pl.program_id(2)
    tm = h_ref.shape[0]
    SB = _moe_slot_block(tm)
    TB = _moe_token_block(tm)
    lane = lax.broadcasted_iota(jnp.int32, (tm, LANES), 1)

    @pl.when((e == 0) & (f == 0))
    def _():
        x = h_ref[...]
        xn = _rms_rows(x, g_ref[...])
        xn_sc[...] = xn.astype(BF16)
        o_ref[...] = x
        logits = jnp.dot(xn, wr_ref[...], preferred_element_type=F32, precision=lax.Precision.HIGHEST)
        logits = jnp.where(lane < N_EXPERTS, logits, -jnp.inf)
        m1 = jnp.max(logits, axis=1, keepdims=True)
        i1 = jnp.min(jnp.where(logits == m1, lane, LANES), axis=1, keepdims=True)
        rest = jnp.where(lane == i1, -jnp.inf, logits)
        m2 = jnp.max(rest, axis=1, keepdims=True)
        i2 = jnp.min(jnp.where(rest == m2, lane, LANES), axis=1, keepdims=True)
        e2 = jnp.exp(m2 - m1)
        g1 = 1.0 / (1.0 + e2)
        comb_sc[...] = jnp.where(lane == i1, g1, jnp.where(lane == i2, e2 * g1, 0.0))
        member = jnp.where(lane == i1, 1.0, jnp.where(lane == i2, 1.0, 0.0))
        tri = jnp.where(lax.broadcasted_iota(jnp.int32, (TB, TB), 0) >= lax.broadcasted_iota(jnp.int32, (TB, TB), 1),
                        1.0, 0.0).astype(BF16)
        carry = jnp.zeros((1, LANES), F32)
        for b in range(tm // TB):
            mb = member[b * TB:(b + 1) * TB]
            incl = jnp.dot(tri, mb.astype(BF16), preferred_element_type=F32)
            posc_sc[b * TB:(b + 1) * TB, :] = jnp.where(mb > 0.0, incl - 1.0 + carry, -1.0)
            carry = carry + incl[TB - 1:TB, :]
        cnt_sc[...] = carry
        for b in range(tm // LANES):
            posr_sc[:, b * LANES:(b + 1) * LANES] = posc_sc[b * LANES:(b + 1) * LANES, :].T

    lane_row = lax.broadcasted_iota(jnp.int32, (1, LANES), 1)
    count = jnp.sum(jnp.where(lane_row == e, cnt_sc[...], 0.0))
    nblk = ((count + (SB - 1.0)) * (1.0 / SB)).astype(jnp.int32)

    @pl.when(f == 0)
    def _():
        slots_of_tokens = posr_sc[pl.ds(e, 1), :]

        def body(r, carry):
            row0 = pl.multiple_of(r * SB, 16)
            slot = (r * SB + lax.broadcasted_iota(jnp.int32, (SB, tm), 0)).astype(F32)
            onehot = jnp.where(slots_of_tokens == slot, 1.0, 0.0).astype(BF16)
            xs_sc[pl.ds(row0, SB), :] = jnp.dot(onehot, xn_sc[...], preferred_element_type=F32).astype(BF16)
            ys_sc[pl.ds(row0, SB), :] = jnp.zeros((SB, D_MODEL), F32)
            return carry

        lax.fori_loop(0, nblk, body, 0)

    def ffn_rows(row0, nrows):
        xs = xs_sc[pl.ds(row0, nrows), :]
        gate = jnp.dot(xs, wg_ref[0], preferred_element_type=F32)
        up = jnp.dot(xs, wu_ref[0], preferred_element_type=F32)
        act = (gate * jax.nn.sigmoid(gate) * up).astype(BF16)
        ys_sc[pl.ds(row0, nrows), :] += jnp.dot(act, wd_ref[0], preferred_element_type=F32)

    def pair_body(r, carry):
        ffn_rows(pl.multiple_of(r * 2 * SB, 16), 2 * SB)
        return carry

    lax.fori_loop(0, nblk // 2, pair_body, 0)

    @pl.when(nblk % 2 == 1)
    def _():
        ffn_rows(pl.multiple_of((nblk - 1) * SB, 16), SB)

    @pl.when(f == pl.num_programs(2) - 1)
    def _():
        on_e = lane == e
        gate_e = jnp.sum(jnp.where(on_e, comb_sc[...], 0.0), axis=1, keepdims=True)
        slot_e = jnp.sum(jnp.where(on_e, posc_sc[...], 0.0), axis=1, keepdims=True)

        def body(r, carry):
            row0 = pl.multiple_of(r * SB, 16)
            ys = ys_sc[pl.ds(row0, SB), :].astype(BF16)
            slot = (r * SB + lax.broadcasted_iota(jnp.int32, (TB, SB), 1)).astype(F32)
            for tb in range(tm // TB):
                rows = slice(tb * TB, (tb + 1) * TB)
                onehot = jnp.where(slot_e[rows] == slot, 1.0, 0.0).astype(BF16)
                o_ref[rows, :] += gate_e[rows] * jnp.dot(onehot, ys, preferred_element_type=F32)
            return carry

        lax.fori_loop(0, nblk, body, 0)


def _moe_slot_block(tm):
    return MOE_SLOT_BLOCK if tm > MOE_SLOT_BLOCK else tm


def _moe_token_block(tm):
    return MOE_TOKEN_BLOCK if tm % MOE_TOKEN_BLOCK == 0 else tm


def _moe(h, gain, wr, wg, wu, wd, tm, tf):
    T = h.shape[0]
    sb = _moe_slot_block(tm)
    cap = -(-tm // sb) * sb
    return pl.pallas_call(
        _moe_kernel,
        grid=(T // tm, N_EXPERTS, D_FF // tf),
        in_specs=[pl.BlockSpec((tm, D_MODEL), lambda i, e, f: (i, 0), pipeline_mode=pl.Buffered(1)),
                  pl.BlockSpec((1, D_MODEL), lambda i, e, f: (0, 0)),
                  pl.BlockSpec((D_MODEL, LANES), lambda i, e, f: (0, 0)),
                  pl.BlockSpec((1, D_MODEL, tf), lambda i, e, f: (e, 0, f)),
                  pl.BlockSpec((1, D_MODEL, tf), lambda i, e, f: (e, 0, f)),
                  pl.BlockSpec((1, tf, D_MODEL), lambda i, e, f: (e, f, 0))],
        out_specs=pl.BlockSpec((tm, D_MODEL), lambda i, e, f: (i, 0)),
        out_shape=jax.ShapeDtypeStruct((T, D_MODEL), F32),
        scratch_shapes=[pltpu.VMEM((tm, D_MODEL), BF16), pltpu.VMEM((cap, D_MODEL), BF16),
                        pltpu.VMEM((cap, D_MODEL), F32), pltpu.VMEM((tm, LANES), F32),
                        pltpu.VMEM((LANES, tm), F32), pltpu.VMEM((tm, LANES), F32),
                        pltpu.VMEM((1, LANES), F32)],
        compiler_params=_cparams(("arbitrary", "arbitrary", "arbitrary")),
        name="moe_sparse",
    )(h, gain, wr, wg, wu, wd)


def _rope_tables(seq_pad):
    pos = jnp.arange(seq_pad, dtype=F32)[:, None]
    lane = np.arange(LANES)

    def table(head_dim):
        half = head_dim // 2
        inv = ROPE_THETA ** (-jnp.arange(half, dtype=F32) / half)
        d = lane % head_dim
        ang = pos * inv[d % half][None, :]
        return jnp.cos(ang), jnp.sin(ang), jnp.asarray(d < half)[None, :]

    c64, s64, lo64 = table(ATT_HEAD_DIM)
    c128, s128, lo128 = table(IDX_DIM)
    return (c64, jnp.where(lo64, -s64, 0.0), jnp.where(lo64, 0.0, s64), c128, jnp.where(lo128, -s128, s128))


def _pad_cols(w, n):
    return jnp.pad(w, ((0, 0), (0, n - w.shape[1])))


def kernel(x, meta, norm_mixer, norm_ffn, dsa_w_in, dsa_q_norm, dsa_k_norm, dsa_w_out, mlstm_w_in, mlstm_b_i,
           mlstm_b_f, mlstm_out_norm, mlstm_w_out, ffn_w_gate, ffn_w_up, ffn_w_down, moe_router, moe_w_gate,
           moe_w_up, moe_w_down):
    B, S, D = x.shape
    L = S + N_META
    top_k = min(TOPK_MAX, S // 4)
    seq_pad = max(-(-L // Q_BLOCK) * Q_BLOCK, KEY_CHUNK)
    T = B * seq_pad
    tm_proj = 384 if seq_pad % 384 == 0 else Q_BLOCK
    tm_ffn = 768 if T % 768 == 0 else Q_BLOCK
    tm_moe = 1536 if T % 1536 == 0 else Q_BLOCK
    tf = 512

    h = jnp.concatenate([jnp.broadcast_to(meta[None].astype(x.dtype), (B, N_META, D)), x,
                         jnp.zeros((B, seq_pad - L, D), x.dtype)], axis=1).reshape(T, D)
    tabs = _rope_tables(seq_pad)
    depth = norm_mixer.shape[0]

    for i in range(depth):
        j = i // 2
        gain_m = norm_mixer[i][None, :]
        gain_f = norm_ffn[i][None, :]
        if i % 2 == 0:
            w_in = _pad_cols(dsa_w_in[j], DSA_IN_PAD).astype(BF16)
            gq = jnp.tile(dsa_q_norm[j], 2)[None, :]
            gk = jnp.tile(dsa_k_norm[j], 2)[None, :]
            proj = _dsa_in_proj(h, gain_m, w_in, gq, gk, tabs, seq_pad, tm_proj)
            att = _dsa_core(*[a.reshape(B, seq_pad, a.shape[-1]) for a in proj], top_k)
            h = _ffn(h, att.reshape(T, D), dsa_w_out[j].astype(BF16), gain_f, ffn_w_gate[j].astype(BF16),
                     ffn_w_up[j].astype(BF16), ffn_w_down[j].astype(BF16), tm_ffn, tf)
        else:
            w_in = _pad_cols(mlstm_w_in[j], MLSTM_IN_PAD).astype(BF16)
            q, k, v, o, gates = _mlstm_in_proj(h, gain_m, w_in, tm_proj)
            bias8 = jnp.concatenate([mlstm_b_i[j], mlstm_b_f[j]])[:, None]
            y = _mlstm_core(*[a.reshape(B, seq_pad, a.shape[-1]) for a in (q, k, v, o, gates)], bias8,
                            mlstm_out_norm[j][None, :])
            h = _out_proj(h, y.reshape(T, D), mlstm_w_out[j].astype(BF16), tm_proj)
            wr = _pad_cols(moe_router[j], LANES)
            h = _moe(h, gain_f, wr, moe_w_gate[j].astype(BF16), moe_w_up[j].astype(BF16),
                     moe_w_down[j].astype(BF16), tm_moe, tf)
    return h.reshape(B, seq_pad, D)[:, N_META:L]
```

```python
import functools

import jax
import jax.numpy as jnp
import numpy as np
from jax import lax
from jax.experimental import pallas as pl
from jax.experimental.pallas import tpu as pltpu

F32 = jnp.float32
BF16 = jnp.bfloat16

D_MODEL = 1024
N_META = 16
RMS_EPS = 1e-6
ROPE_THETA = 10000.0
ATT_HEADS = 16
ATT_KV_HEADS = 4
ATT_HEAD_DIM = 64
ATT_GROUP = 4
IDX_HEADS = 8
IDX_DIM = 128
TOPK_MAX = 256
M_HEADS = 4
M_QK_DIM = 128
M_V_DIM = 256
GATE_CAP = 15.0
D_FF = 3584
N_EXPERTS = 8

LANES = 128
SUBLANES = 8
Q_BLOCK = 128
KEY_CHUNK = 256
ATT_PAIR_STEP_GROUPS = ((0, 1, 2, 3),)
M_CHUNK = 128
M_BATCH_ROWS = 1
MOE_SLOT_BLOCK = 208
MOE_TOKEN_BLOCK = 256
VMEM_LIMIT = 52 * 1024 * 1024
INT_MIN = -(2 ** 31)
INT_MAX = 2 ** 31 - 1
LOG2E = 1.4426950408889634

DSA_IN_PAD = 2816
MLSTM_IN_PAD = 3200


def _cparams(sem):
    return pltpu.CompilerParams(dimension_semantics=sem, vmem_limit_bytes=VMEM_LIMIT)


def _rms_rows(x, gain):
    ms = jnp.mean(x * x, axis=-1, keepdims=True)
    return x * lax.rsqrt(ms + RMS_EPS) * gain


def _dsa_in_kernel(h_ref, g_ref, w_ref, gq_ref, gk_ref, c64_ref, sa64_ref, sb64_ref, c128_ref, s128_ref,
                   smat_ref, q_ref, qsw_ref, k_ref, v_ref, iq_ref, ik_ref, iw_ref):
    xn = _rms_rows(h_ref[...], g_ref[...]).astype(BF16)
    z = jnp.dot(xn, w_ref[...], preferred_element_type=F32)
    c64, sa64, sb64 = c64_ref[...], sa64_ref[...], sb64_ref[...]
    c128, s128 = c128_ref[...], s128_ref[...]
    smat = smat_ref[...]

    def head_norm_rope(zc, gain):
        z2 = zc * zc
        hi = z2.astype(BF16)
        lo = (z2 - hi.astype(F32)).astype(BF16)
        ms = jnp.dot(hi, smat, preferred_element_type=F32) + jnp.dot(lo, smat, preferred_element_type=F32)
        y = zc * lax.rsqrt(ms + RMS_EPS) * gain
        return y * c64 + pltpu.roll(y, 96, 1) * sa64 + pltpu.roll(y, 32, 1) * sb64

    for c in range(8):
        r = head_norm_rope(z[:, c * LANES:(c + 1) * LANES], gq_ref[...]) * (ATT_HEAD_DIM ** -0.5 * LOG2E)
        q_ref[:, c * LANES:(c + 1) * LANES] = r.astype(BF16)
        qsw_ref[:, c * LANES:(c + 1) * LANES] = pltpu.roll(r, 64, 1).astype(BF16)
    for c in range(2):
        r = head_norm_rope(z[:, 1024 + c * LANES:1024 + (c + 1) * LANES], gk_ref[...])
        k_ref[:, c * LANES:(c + 1) * LANES] = r.astype(BF16)
    ones = jnp.ones((z.shape[0], LANES), BF16)
    for c in range(2):
        v_ref[:, 2 * c * LANES:(2 * c + 1) * LANES] = z[:, 1280 + c * LANES:1280 + (c + 1) * LANES].astype(BF16)
        v_ref[:, (2 * c + 1) * LANES:(2 * c + 2) * LANES] = ones
    for c in range(9):
        zc = z[:, 1536 + c * LANES:1536 + (c + 1) * LANES]
        r = (zc * c128 + pltpu.roll(zc, 64, 1) * s128).astype(BF16)
        if c < 8:
            iq_ref[:, c * LANES:(c + 1) * LANES] = r
        else:
            ik_ref[...] = r
    iw_ref[...] = z[:, 2688:2816] * (IDX_HEADS ** -0.5 * IDX_DIM ** -0.5)


def _dsa_in_proj(h, gain, w, gq, gk, tabs, seq_pad, tm):
    T = h.shape[0]
    nt = seq_pad // tm
    row = lambda i: (i, 0)
    fixed = lambda i: (0, 0)
    pos = lambda i: (i % nt, 0)
    tab_spec = pl.BlockSpec((tm, LANES), pos)
    smat = jnp.asarray(np.kron(np.eye(2), np.full((64, 64), 1.0 / 64)), BF16)
    outs = [(1024, BF16), (1024, BF16), (256, BF16), (512, BF16), (1024, BF16), (128, BF16), (128, F32)]
    return pl.pallas_call(
        _dsa_in_kernel,
        grid=(T // tm,),
        in_specs=[pl.BlockSpec((tm, D_MODEL), row), pl.BlockSpec((1, D_MODEL), fixed),
                  pl.BlockSpec((D_MODEL, DSA_IN_PAD), fixed), pl.BlockSpec((1, LANES), fixed),
                  pl.BlockSpec((1, LANES), fixed), tab_spec, tab_spec, tab_spec, tab_spec, tab_spec,
                  pl.BlockSpec((LANES, LANES), fixed)],
        out_specs=[pl.BlockSpec((tm, n), row) for n, _ in outs],
        out_shape=[jax.ShapeDtypeStruct((T, n), dt) for n, dt in outs],
        compiler_params=_cparams(("arbitrary",)),
        name="dsa_in_proj",
    )(h, gain, w, gq, gk, *tabs, smat)


def _dsa_core_kernel(q_ref, qsw_ref, k_ref, v_ref, iq_ref, ik_ref, iw_ref, o_ref,
                     iqt_sc, iwt_sc, qs_sc, keys_sc, bias_sc, cut_sc, m_sc, l_sc, acc_sc,
                     *, top_k, seq_pad):
    i = pl.program_id(1)
    QB, KC = Q_BLOCK, KEY_CHUNK
    nch = (i * QB + QB + KC - 1) // KC
    kf = float(top_k)
    lane1 = lax.broadcasted_iota(jnp.int32, (QB, LANES), 1)

    for h in range(IDX_HEADS):
        iqt_sc[:, h * QB:(h + 1) * QB] = iq_ref[0, :, h * LANES:(h + 1) * LANES].astype(F32).T.astype(BF16)
    iwt_sc[...] = iw_ref[0].T

    for g in range(ATT_KV_HEADS):
        for r in range(ATT_GROUP):
            h = g * ATT_GROUP + r
            src = q_ref if h % 2 == g % 2 else qsw_ref
            chunk = src[0, :, (h // 2) * LANES:(h // 2 + 1) * LANES].astype(F32)
            keep = (lane1 >= 64) if g % 2 else (lane1 < 64)
            qs_sc[g, r * QB:(r + 1) * QB, :] = jnp.where(keep, chunk, 0.0).astype(BF16)

    qpos = i * QB + lax.broadcasted_iota(jnp.int32, (KC, QB), 1)
    krow = lax.broadcasted_iota(jnp.int32, (KC, QB), 0)

    def chunk_start(j):
        return pl.multiple_of(jnp.minimum(j * KC, seq_pad - KC), LANES)

    def key_valid(j, start):
        kidx = start + krow
        return (kidx <= qpos) & (kidx >= j * KC)

    def for_chunks(fn):
        def body(j2, carry):
            fn(2 * j2)
            fn(2 * j2 + 1)
            return carry

        lax.fori_loop(0, nch // 2, body, 0)

        @pl.when(nch % 2 == 1)
        def _():
            fn(nch - 1)

    def score_chunk(j):
        start = chunk_start(j)
        s = jnp.dot(ik_ref[0, pl.ds(start, KC), :], iqt_sc[...], preferred_element_type=F32)
        acc = jnp.zeros((KC, QB), F32)
        for h in range(IDX_HEADS):
            acc = acc + jnp.maximum(s[:, h * QB:(h + 1) * QB], 0.0) * iwt_sc[h:h + 1, :]
        bits = pltpu.bitcast(acc, jnp.int32)
        key = jnp.where(bits >= 0, bits, bits ^ jnp.int32(INT_MAX))
        keys_sc[j] = jnp.where(key_valid(j, start), key, jnp.int32(INT_MIN))

    for_chunks(score_chunk)

    tiles = (KC // SUBLANES, SUBLANES, QB)
    krow3 = (lax.broadcasted_iota(jnp.int32, tiles, 0) * SUBLANES + lax.broadcasted_iota(jnp.int32, tiles, 1))

    keys_sc[nch] = jnp.full((KC, QB), INT_MIN, jnp.int32)

    def count(pred):
        def body(j2, acc):
            for j in (2 * j2, 2 * j2 + 1):
                kk = keys_sc[j].reshape(tiles)
                hit = jnp.where(pred(kk, chunk_start(j)), jnp.float32(1.0), jnp.float32(0.0))
                while hit.shape[0] > 1:
                    half = hit.shape[0] // 2
                    hit = hit[:half] + hit[half:]
                acc = acc + hit[0]
            return acc
        acc = lax.fori_loop(0, (nch + 1) // 2, body, jnp.zeros((SUBLANES, QB), F32))
        for shift in (4, 2, 1):
            acc = acc + pltpu.roll(acc, shift, 0)
        return acc

    zero = jnp.zeros((SUBLANES, QB), jnp.int32)
    v0 = jnp.where(count(lambda kk, base: kk >= zero) >= kf, zero, jnp.int32(INT_MIN))

    def bit_body(b, v):
        cand = v | lax.shift_left(jnp.int32(1), (30 - b).astype(jnp.int32))
        return jnp.where(count(lambda kk, base: kk >= cand) >= kf, cand, v)

    thr = lax.fori_loop(jnp.int32(0), jnp.int32(31), bit_body, v0)
    cut_sc[...] = jnp.full((SUBLANES, QB), INT_MAX, jnp.int32)
    n_ge = count(lambda kk, base: kk >= thr)

    @pl.when(jnp.max(n_ge) > kf)
    def _():
        need = kf - count(lambda kk, base: kk > thr)

        def cut_body(b, c):
            cand = c | lax.shift_left(jnp.int32(1), (12 - b).astype(jnp.int32))
            n = count(lambda kk, base: jnp.where(kk == thr, base + krow3, jnp.int32(INT_MAX)) < cand)
            return jnp.where(n < need, cand, c)

        cut_sc[...] = lax.fori_loop(jnp.int32(0), jnp.int32(13), cut_body, zero)

    thr1 = thr[0:1, :]
    cut1 = cut_sc[0:1, :]

    def bias_chunk(j):
        kk = keys_sc[j]
        start = chunk_start(j)
        sel = (kk > thr1) | ((kk == thr1) & (start + krow <= cut1))
        bt = jnp.where(sel & key_valid(j, start), 0.0, -1e30)
        bias_sc[j] = jnp.concatenate([bt[c * QB:(c + 1) * QB].T for c in range(KC // QB)], axis=1)

    for_chunks(bias_chunk)

    m_sc[...] = jnp.full(m_sc.shape, -1e9, F32)
    l_sc[...] = jnp.zeros(l_sc.shape, F32)
    acc_sc[...] = jnp.zeros(acc_sc.shape, F32)

    def att_step(start, width, bias, groups):
        bias = jnp.concatenate([bias] * ATT_GROUP, axis=0)
        for g in groups:
            cg = g // 2
            kj = k_ref[0, pl.ds(start, width), cg * LANES:(cg + 1) * LANES]
            vj = v_ref[0, pl.ds(start, width), cg * 2 * LANES:(cg + 1) * 2 * LANES]
            s = lax.dot_general(qs_sc[g], kj, (((1,), (1,)), ((), ())), preferred_element_type=F32) + bias
            m_old = m_sc[g]
            m_new = jnp.maximum(m_old, jnp.max(s, axis=1, keepdims=True))
            alpha = jnp.exp2(m_old - m_new)
            p = jnp.exp2(s - jnp.concatenate([m_new] * (width // LANES), axis=1)).astype(BF16)
            pv = jnp.dot(p, vj, preferred_element_type=F32)
            l_sc[g] = alpha * l_sc[g] + pv[:, LANES:]
            acc_sc[g] = alpha * acc_sc[g] + pv[:, :LANES]
            m_sc[g] = m_new

    npair = jnp.minimum(nch // 2, seq_pad // (2 * KC))

    def single_body(j, carry):
        att_step(chunk_start(j), KC, bias_sc[j], range(ATT_KV_HEADS))
        return carry

    for groups in ATT_PAIR_STEP_GROUPS:
        def pair_body(j, carry, groups=groups):
            bias = jnp.concatenate([bias_sc[2 * j], bias_sc[2 * j + 1]], axis=1)
            att_step(pl.multiple_of(j * 2 * KC, 2 * KC), 2 * KC, bias, groups)
            return carry

        lax.fori_loop(0, npair, pair_body, 0)
    lax.fori_loop(2 * npair, nch, single_body, 0)

    for g in range(ATT_KV_HEADS):
        og = acc_sc[g] / l_sc[g]
        for pair in range(2):
            even = og[(2 * pair) * QB:(2 * pair + 1) * QB]
            odd = og[(2 * pair + 1) * QB:(2 * pair + 2) * QB]
            if g % 2:
                even = pltpu.roll(even, 64, 1)
            else:
                odd = pltpu.roll(odd, 64, 1)
            c = 2 * g + pair
            o_ref[0, :, c * LANES:(c + 1) * LANES] = jnp.where(lane1 < 64, even, odd).astype(BF16)


def _dsa_core(q, qsw, k, v, iq, ik, iw, top_k):
    B, seq_pad = q.shape[0], q.shape[1]
    nqb = seq_pad // Q_BLOCK
    nkc = -(-seq_pad // KEY_CHUNK)
    rows = ATT_GROUP * Q_BLOCK
    qblock = lambda n: pl.BlockSpec((1, Q_BLOCK, n), lambda b, i: (b, i, 0))
    whole = lambda n: pl.BlockSpec((1, seq_pad, n), lambda b, i: (b, 0, 0))
    return pl.pallas_call(
        functools.partial(_dsa_core_kernel, top_k=top_k, seq_pad=seq_pad),
        grid=(B, nqb),
        in_specs=[qblock(D_MODEL), qblock(D_MODEL), whole(256), whole(512),
                  qblock(IDX_HEADS * IDX_DIM), whole(IDX_DIM), qblock(LANES)],
        out_specs=qblock(D_MODEL),
        out_shape=jax.ShapeDtypeStruct((B, seq_pad, D_MODEL), BF16),
        scratch_shapes=[
            pltpu.VMEM((IDX_DIM, IDX_HEADS * Q_BLOCK), BF16),
            pltpu.VMEM((LANES, Q_BLOCK), F32),
            pltpu.VMEM((ATT_KV_HEADS, rows, LANES), BF16),
            pltpu.VMEM((nkc + 1, KEY_CHUNK, Q_BLOCK), jnp.int32),
            pltpu.VMEM((nkc, Q_BLOCK, KEY_CHUNK), F32),
            pltpu.VMEM((SUBLANES, Q_BLOCK), jnp.int32),
            pltpu.VMEM((ATT_KV_HEADS, rows, LANES), F32),
            pltpu.VMEM((ATT_KV_HEADS, rows, LANES), F32),
            pltpu.VMEM((ATT_KV_HEADS, rows, LANES), F32),
        ],
        compiler_params=_cparams(("arbitrary", "arbitrary")),
        name="dsa_core",
    )(q, qsw, k, v, iq, ik, iw)


def _out_proj_kernel(h_ref, y_ref, w_ref, o_ref):
    o_ref[...] = h_ref[...] + jnp.dot(y_ref[...], w_ref[...], preferred_element_type=F32)


def _out_proj(h, y, w, tm):
    T = h.shape[0]
    row = lambda i: (i, 0)
    return pl.pallas_call(
        _out_proj_kernel,
        grid=(T // tm,),
        in_specs=[pl.BlockSpec((tm, D_MODEL), row), pl.BlockSpec((tm, D_MODEL), row),
                  pl.BlockSpec((D_MODEL, D_MODEL), lambda i: (0, 0))],
        out_specs=pl.BlockSpec((tm, D_MODEL), row),
        out_shape=jax.ShapeDtypeStruct((T, D_MODEL), F32),
        compiler_params=_cparams(("arbitrary",)),
        name="out_proj",
    )(h, y, w)


def _mlstm_in_kernel(h_ref, g_ref, w_ref, q_ref, k_ref, v_ref, o_ref, gate_ref):
    xn = _rms_rows(h_ref[...], g_ref[...]).astype(BF16)
    z = jnp.dot(xn, w_ref[...], preferred_element_type=F32)
    q_ref[...] = z[:, 0:512].astype(BF16)
    k_ref[...] = (z[:, 512:1024] * (M_QK_DIM ** -0.5)).astype(BF16)
    v_ref[...] = z[:, 1024:2048].astype(BF16)
    o_ref[...] = z[:, 2048:3072]
    gate_ref[...] = z[:, 3072:3200]


def _mlstm_in_proj(h, gain, w, tm):
    T = h.shape[0]
    row = lambda i: (i, 0)
    fixed = lambda i: (0, 0)
    outs = [(512, BF16), (512, BF16), (1024, BF16), (1024, F32), (128, F32)]
    return pl.pallas_call(
        _mlstm_in_kernel,
        grid=(T // tm,),
        in_specs=[pl.BlockSpec((tm, D_MODEL), row), pl.BlockSpec((1, D_MODEL), fixed),
                  pl.BlockSpec((D_MODEL, MLSTM_IN_PAD), fixed)],
        out_specs=[pl.BlockSpec((tm, n), row) for n, _ in outs],
        out_shape=[jax.ShapeDtypeStruct((T, n), dt) for n, dt in outs],
        compiler_params=_cparams(("arbitrary",)),
        name="mlstm_in_proj",
    )(h, gain, w)


def _mlstm_kernel(q_ref, k_ref, v_ref, o_ref, gate_ref, bias_ref, gout_ref, y_ref, ct_sc, n_sc, m_sc):
    C = M_CHUNK

    @pl.when(pl.program_id(1) == 0)
    def _():
        ct_sc[...] = jnp.zeros(ct_sc.shape, F32)
        n_sc[...] = jnp.zeros(n_sc.shape, F32)
        m_sc[...] = jnp.zeros(m_sc.shape, F32)

    lane8 = lax.broadcasted_iota(jnp.int32, (8, C), 1)
    t_idx = lax.broadcasted_iota(jnp.int32, (C, C), 0)
    s_idx = lax.broadcasted_iota(jnp.int32, (C, C), 1)
    for bi in range(q_ref.shape[0]):
        pre = gate_ref[bi].T[0:8, :] + bias_ref[...]
        capped = GATE_CAP * jnp.tanh(pre / GATE_CAP)
        log_f = -(jnp.maximum(-capped, 0.0) + jnp.log1p(jnp.exp(-jnp.abs(capped))))
        b = log_f
        sh = 1
        while sh < C:
            b = b + jnp.where(lane8 >= sh, pltpu.roll(b, sh, 1), 0.0)
            sh *= 2
        stacked = jnp.concatenate([b[4:8], capped[0:4]], axis=0)
        cols = jnp.concatenate([stacked, jnp.zeros((C - 8, C), F32)], axis=0).T

        for h in range(M_HEADS):
            st = bi * M_HEADS + h
            qh = q_ref[bi, :, h * M_QK_DIM:(h + 1) * M_QK_DIM]
            kh = k_ref[bi, :, h * M_QK_DIM:(h + 1) * M_QK_DIM]
            vh = v_ref[bi, :, h * M_V_DIM:(h + 1) * M_V_DIM]
            b_row, li_row = stacked[h:h + 1, :], stacked[4 + h:5 + h, :]
            b_col, li_col = cols[:, h:h + 1], cols[:, 4 + h:5 + h]
            m_st = m_sc[st:st + 1, 0:1]
            dmat = jnp.where(s_idx <= t_idx, b_col - b_row + li_row, -jnp.inf)
            inter = b_col + m_st
            m_t = jnp.maximum(inter, jnp.max(dmat, axis=1, keepdims=True))
            w_inter = jnp.exp(inter - m_t)
            qk = lax.dot_general(qh, kh, (((1,), (1,)), ((), ())), preferred_element_type=F32)
            s = qk * jnp.exp(dmat - m_t)
            ct = ct_sc[st]
            num = (w_inter * jnp.dot(qh, ct.astype(BF16), preferred_element_type=F32)
                   + jnp.dot(s.astype(BF16), vh, preferred_element_type=F32))
            qn = jnp.sum(qh.astype(F32) * n_sc[st:st + 1, :], axis=1, keepdims=True)
            den = w_inter * qn + jnp.sum(s, axis=1, keepdims=True)
            hout = num / jnp.maximum(jnp.abs(den), jnp.exp(-m_t))

            b_last = b_row[:, C - 1:C]
            m_new = jnp.maximum(b_last + m_st, jnp.max(b_last - b_row + li_row, axis=1, keepdims=True))
            decay = jnp.exp(b_last + m_st - m_new)
            wk = jnp.exp(b_last - b_col + li_col - m_new)
            kw = kh.astype(F32) * wk
            ct_sc[st] = decay * ct + jnp.dot(kw.T.astype(BF16), vh, preferred_element_type=F32)
            n_sc[st:st + 1, :] = decay * n_sc[st:st + 1, :] + jnp.sum(kw, axis=0, keepdims=True)
            m_sc[st:st + 1, :] = jnp.broadcast_to(m_new, (1, LANES))

            cs = slice(h * M_V_DIM, (h + 1) * M_V_DIM)
            hn = _rms_rows(hout, gout_ref[:, cs])
            y_ref[bi, :, cs] = (hn * jax.nn.sigmoid(o_ref[bi, :, cs])).astype(BF16)


def _mlstm_core(q, k, v, o, gates, bias8, gout):
    B, seq_pad = q.shape[0], q.shape[1]
    nb = M_BATCH_ROWS
    C = M_CHUNK
    blk = lambda n: pl.BlockSpec((nb, C, n), lambda b, c: (b, c, 0))
    fixed = lambda b, c: (0, 0)
    return pl.pallas_call(
        _mlstm_kernel,
        grid=(B // nb, seq_pad // C),
        in_specs=[blk(512), blk(512), blk(1024), blk(1024), blk(LANES),
                  pl.BlockSpec((8, 1), fixed), pl.BlockSpec((1, D_MODEL), fixed)],
        out_specs=blk(D_MODEL),
        out_shape=jax.ShapeDtypeStruct((B, seq_pad, D_MODEL), BF16),
        scratch_shapes=[pltpu.VMEM((nb * M_HEADS, M_QK_DIM, M_V_DIM), F32),
                        pltpu.VMEM((nb * M_HEADS, M_QK_DIM), F32),
                        pltpu.VMEM((nb * M_HEADS, LANES), F32)],
        compiler_params=_cparams(("arbitrary", "arbitrary")),
        name="mlstm_core",
    )(q, k, v, o, gates, bias8, gout)


def _ffn_kernel(h_ref, y_ref, wo_ref, g_ref, wg_ref, wu_ref, wd_ref, o_ref, xn_sc, acc_sc):
    f = pl.program_id(1)

    @pl.when(f == 0)
    def _():
        x = h_ref[...] + jnp.dot(y_ref[...], wo_ref[...], preferred_element_type=F32)
        xn_sc[...] = _rms_rows(x, g_ref[...]).astype(BF16)
        acc_sc[...] = x

    xn = xn_sc[...]
    gate = jnp.dot(xn, wg_ref[...], preferred_element_type=F32)
    up = jnp.dot(xn, wu_ref[...], preferred_element_type=F32)
    act = (gate * jax.nn.sigmoid(gate) * up).astype(BF16)
    acc_sc[...] += jnp.dot(act, wd_ref[...], preferred_element_type=F32)

    @pl.when(f == pl.num_programs(1) - 1)
    def _():
        o_ref[...] = acc_sc[...]


def _ffn(h, y, wo, gain, wg, wu, wd, tm, tf):
    T = h.shape[0]
    return pl.pallas_call(
        _ffn_kernel,
        grid=(T // tm, D_FF // tf),
        in_specs=[pl.BlockSpec((tm, D_MODEL), lambda i, f: (i, 0)),
                  pl.BlockSpec((tm, D_MODEL), lambda i, f: (i, 0)),
                  pl.BlockSpec((D_MODEL, D_MODEL), lambda i, f: (0, 0)),
                  pl.BlockSpec((1, D_MODEL), lambda i, f: (0, 0)),
                  pl.BlockSpec((D_MODEL, tf), lambda i, f: (0, f)),
                  pl.BlockSpec((D_MODEL, tf), lambda i, f: (0, f)),
                  pl.BlockSpec((tf, D_MODEL), lambda i, f: (f, 0))],
        out_specs=pl.BlockSpec((tm, D_MODEL), lambda i, f: (i, 0)),
        out_shape=jax.ShapeDtypeStruct((T, D_MODEL), F32),
        scratch_shapes=[pltpu.VMEM((tm, D_MODEL), BF16), pltpu.VMEM((tm, D_MODEL), F32)],
        compiler_params=_cparams(("arbitrary", "arbitrary")),
        name="ffn_dense",
    )(h, y, wo, gain, wg, wu, wd)


def _moe_kernel(h_ref, g_ref, wr_ref, wg_ref, wu_ref, wd_ref, o_ref,
                xn_sc, xs_sc, ys_sc, posc_sc, posr_sc, comb_sc, cnt_sc):
    e = pl.program_id(1)
    f = pl.program_id(2)
    tm = h_ref.shape[0]
    SB = _moe_slot_block(tm)
    TB = _moe_token_block(tm)
    lane = lax.broadcasted_iota(jnp.int32, (tm, LANES), 1)

    @pl.when((e == 0) & (f == 0))
    def _():
        x = h_ref[...]
        xn = _rms_rows(x, g_ref[...])
        xn_sc[...] = xn.astype(BF16)
        o_ref[...] = x
        logits = jnp.dot(xn, wr_ref[...], preferred_element_type=F32, precision=lax.Precision.HIGHEST)
        logits = jnp.where(lane < N_EXPERTS, logits, -jnp.inf)
        m1 = jnp.max(logits, axis=1, keepdims=True)
        i1 = jnp.min(jnp.where(logits == m1, lane, LANES), axis=1, keepdims=True)
        rest = jnp.where(lane == i1, -jnp.inf, logits)
        m2 = jnp.max(rest, axis=1, keepdims=True)
        i2 = jnp.min(jnp.where(rest == m2, lane, LANES), axis=1, keepdims=True)
        e2 = jnp.exp(m2 - m1)
        g1 = 1.0 / (1.0 + e2)
        comb_sc[...] = jnp.where(lane == i1, g1, jnp.where(lane == i2, e2 * g1, 0.0))
        member = jnp.where(lane == i1, 1.0, jnp.where(lane == i2, 1.0, 0.0))
        tri = jnp.where(lax.broadcasted_iota(jnp.int32, (TB, TB), 0) >= lax.broadcasted_iota(jnp.int32, (TB, TB), 1),
                        1.0, 0.0).astype(BF16)
        carry = jnp.zeros((1, LANES), F32)
        for b in range(tm // TB):
            mb = member[b * TB:(b + 1) * TB]
            incl = jnp.dot(tri, mb.astype(BF16), preferred_element_type=F32)
            posc_sc[b * TB:(b + 1) * TB, :] = jnp.where(mb > 0.0, incl - 1.0 + carry, -1.0)
            carry = carry + incl[TB - 1:TB, :]
        cnt_sc[...] = carry
        for b in range(tm // LANES):
            posr_sc[:, b * LANES:(b + 1) * LANES] = posc_sc[b * LANES:(b + 1) * LANES, :].T

    lane_row = lax.broadcasted_iota(jnp.int32, (1, LANES), 1)
    count = jnp.sum(jnp.where(lane_row == e, cnt_sc[...], 0.0))
    nblk = ((count + (SB - 1.0)) * (1.0 / SB)).astype(jnp.int32)

    @pl.when(f == 0)
    def _():
        slots_of_tokens = posr_sc[pl.ds(e, 1), :]

        def body(r, carry):
            row0 = pl.multiple_of(r * SB, 16)
            slot = (r * SB + lax.broadcasted_iota(jnp.int32, (SB, tm), 0)).astype(F32)
            onehot = jnp.where(slots_of_tokens == slot, 1.0, 0.0).astype(BF16)
            xs_sc[pl.ds(row0, SB), :] = jnp.dot(onehot, xn_sc[...], preferred_element_type=F32).astype(BF16)
            ys_sc[pl.ds(row0, SB), :] = jnp.zeros((SB, D_MODEL), F32)
            return carry

        lax.fori_loop(0, nblk, body, 0)

    def ffn_rows(row0, nrows):
        xs = xs_sc[pl.ds(row0, nrows), :]
        gate = jnp.dot(xs, wg_ref[0], preferred_element_type=F32)
        up = jnp.dot(xs, wu_ref[0], preferred_element_type=F32)
        act = (gate * jax.nn.sigmoid(gate) * up).astype(BF16)
        ys_sc[pl.ds(row0, nrows), :] += jnp.dot(act, wd_ref[0], preferred_element_type=F32)

    def pair_body(r, carry):
        ffn_rows(pl.multiple_of(r * 2 * SB, 16), 2 * SB)
        return carry

    lax.fori_loop(0, nblk // 2, pair_body, 0)

    @pl.when(nblk % 2 == 1)
    def _():
        ffn_rows(pl.multiple_of((nblk - 1) * SB, 16), SB)

    @pl.when(f == pl.num_programs(2) - 1)
    def _():
        on_e = lane == e
        gate_e = jnp.sum(jnp.where(on_e, comb_sc[...], 0.0), axis=1, keepdims=True)
        slot_e = jnp.sum(jnp.where(on_e, posc_sc[...], 0.0), axis=1, keepdims=True)

        def body(r, carry):
            row0 = pl.multiple_of(r * SB, 16)
            ys = ys_sc[pl.ds(row0, SB), :].astype(BF16)
            slot = (r * SB + lax.broadcasted_iota(jnp.int32, (TB, SB), 1)).astype(F32)
            for tb in range(tm // TB):
                rows = slice(tb * TB, (tb + 1) * TB)
                onehot = jnp.where(slot_e[rows] == slot, 1.0, 0.0).astype(BF16)
                o_ref[rows, :] += gate_e[rows] * jnp.dot(onehot, ys, preferred_element_type=F32)
            return carry

        lax.fori_loop(0, nblk, body, 0)


def _moe_slot_block(tm):
    return MOE_SLOT_BLOCK if tm > MOE_SLOT_BLOCK else tm


def _moe_token_block(tm):
    return MOE_TOKEN_BLOCK if tm % MOE_TOKEN_BLOCK == 0 else tm


def _moe(h, gain, wr, wg, wu, wd, tm, tf):
    T = h.shape[0]
    sb = _moe_slot_block(tm)
    cap = -(-tm // sb) * sb
    return pl.pallas_call(
        _moe_kernel,
        grid=(T // tm, N_EXPERTS, D_FF // tf),
        in_specs=[pl.BlockSpec((tm, D_MODEL), lambda i, e, f: (i, 0), pipeline_mode=pl.Buffered(1)),
                  pl.BlockSpec((1, D_MODEL), lambda i, e, f: (0, 0)),
                  pl.BlockSpec((D_MODEL, LANES), lambda i, e, f: (0, 0)),
                  pl.BlockSpec((1, D_MODEL, tf), lambda i, e, f: (e, 0, f)),
                  pl.BlockSpec((1, D_MODEL, tf), lambda i, e, f: (e, 0, f)),
                  pl.BlockSpec((1, tf, D_MODEL), lambda i, e, f: (e, f, 0))],
        out_specs=pl.BlockSpec((tm, D_MODEL), lambda i, e, f: (i, 0)),
        out_shape=jax.ShapeDtypeStruct((T, D_MODEL), F32),
        scratch_shapes=[pltpu.VMEM((tm, D_MODEL), BF16), pltpu.VMEM((cap, D_MODEL), BF16),
                        pltpu.VMEM((cap, D_MODEL), F32), pltpu.VMEM((tm, LANES), F32),
                        pltpu.VMEM((LANES, tm), F32), pltpu.VMEM((tm, LANES), F32),
                        pltpu.VMEM((1, LANES), F32)],
        compiler_params=_cparams(("arbitrary", "arbitrary", "arbitrary")),
        name="moe_sparse",
    )(h, gain, wr, wg, wu, wd)


def _rope_tables(seq_pad):
    pos = jnp.arange(seq_pad, dtype=F32)[:, None]
    lane = np.arange(LANES)

    def table(head_dim):
        half = head_dim // 2
        inv = ROPE_THETA ** (-jnp.arange(half, dtype=F32) / half)
        d = lane % head_dim
        ang = pos * inv[d % half][None, :]
        return jnp.cos(ang), jnp.sin(ang), jnp.asarray(d < half)[None, :]

    c64, s64, lo64 = table(ATT_HEAD_DIM)
    c128, s128, lo128 = table(IDX_DIM)
    return (c64, jnp.where(lo64, -s64, 0.0), jnp.where(lo64, 0.0, s64), c128, jnp.where(lo128, -s128, s128))


def _pad_cols(w, n):
    return jnp.pad(w, ((0, 0), (0, n - w.shape[1])))


def kernel(x, meta, norm_mixer, norm_ffn, dsa_w_in, dsa_q_norm, dsa_k_norm, dsa_w_out, mlstm_w_in, mlstm_b_i,
           mlstm_b_f, mlstm_out_norm, mlstm_w_out, ffn_w_gate, ffn_w_up, ffn_w_down, moe_router, moe_w_gate,
           moe_w_up, moe_w_down):
    B, S, D = x.shape
    L = S + N_META
    top_k = min(TOPK_MAX, S // 4)
    seq_pad = max(-(-L // Q_BLOCK) * Q_BLOCK, KEY_CHUNK)
    T = B * seq_pad
    tm_proj = 384 if seq_pad % 384 == 0 else Q_BLOCK
    tm_ffn = 768 if T % 768 == 0 else Q_BLOCK
    tm_moe = 1536 if T % 1536 == 0 else Q_BLOCK
    tf = 512

    h = jnp.concatenate([jnp.broadcast_to(meta[None].astype(x.dtype), (B, N_META, D)), x,
                         jnp.zeros((B, seq_pad - L, D), x.dtype)], axis=1).reshape(T, D)
    tabs = _rope_tables(seq_pad)
    depth = norm_mixer.shape[0]

    for i in range(depth):
        j = i // 2
        gain_m = norm_mixer[i][None, :]
        gain_f = norm_ffn[i][None, :]
        if i % 2 == 0:
            w_in = _pad_cols(dsa_w_in[j], DSA_IN_PAD).astype(BF16)
            gq = jnp.tile(dsa_q_norm[j], 2)[None, :]
            gk = jnp.tile(dsa_k_norm[j], 2)[None, :]
            proj = _dsa_in_proj(h, gain_m, w_in, gq, gk, tabs, seq_pad, tm_proj)
            att = _dsa_core(*[a.reshape(B, seq_pad, a.shape[-1]) for a in proj], top_k)
            h = _ffn(h, att.reshape(T, D), dsa_w_out[j].astype(BF16), gain_f, ffn_w_gate[j].astype(BF16),
                     ffn_w_up[j].astype(BF16), ffn_w_down[j].astype(BF16), tm_ffn, tf)
        else:
            w_in = _pad_cols(mlstm_w_in[j], MLSTM_IN_PAD).astype(BF16)
            q, k, v, o, gates = _mlstm_in_proj(h, gain_m, w_in, tm_proj)
            bias8 = jnp.concatenate([mlstm_b_i[j], mlstm_b_f[j]])[:, None]
            y = _mlstm_core(*[a.reshape(B, seq_pad, a.shape[-1]) for a in (q, k, v, o, gates)], bias8,
                            mlstm_out_norm[j][None, :])
            h = _out_proj(h, y.reshape(T, D), mlstm_w_out[j].astype(BF16), tm_proj)
            wr = _pad_cols(moe_router[j], LANES)
            h = _moe(h, gain_f, wr, moe_w_gate[j].astype(BF16), moe_w_up[j].astype(BF16),
                     moe_w_down[j].astype(BF16), tm_moe, tf)
    return h.reshape(B, seq_pad, D)[:, N_META:L]
```

```python
import functools

import jax
import jax.numpy as jnp
import numpy as np
from jax import lax
from jax.experimental import pallas as pl
from jax.experimental.pallas import tpu as pltpu

F32 = jnp.float32
BF16 = jnp.bfloat16

D_MODEL = 1024
N_META = 16
RMS_EPS = 1e-6
ROPE_THETA = 10000.0
ATT_HEADS = 16
ATT_KV_HEADS = 4
ATT_HEAD_DIM = 64
ATT_GROUP = 4
IDX_HEADS = 8
IDX_DIM = 128
TOPK_MAX = 256
M_HEADS = 4
M_QK_DIM = 128
M_V_DIM = 256
GATE_CAP = 15.0
D_FF = 3584
N_EXPERTS = 8

LANES = 128
SUBLANES = 8
Q_BLOCK = 128
KEY_CHUNK = 256
M_CHUNK = 128
M_BATCH_ROWS = 1
MOE_SLOT_BLOCK = 208
MOE_TOKEN_BLOCK = 256
VMEM_LIMIT = 52 * 1024 * 1024
INT_MIN = -(2 ** 31)
INT_MAX = 2 ** 31 - 1
LOG2E = 1.4426950408889634

DSA_IN_PAD = 2816
MLSTM_IN_PAD = 3200


def _cparams(sem):
    return pltpu.CompilerParams(dimension_semantics=sem, vmem_limit_bytes=VMEM_LIMIT)


def _rms_rows(x, gain):
    ms = jnp.mean(x * x, axis=-1, keepdims=True)
    return x * lax.rsqrt(ms + RMS_EPS) * gain


def _dsa_in_kernel(h_ref, g_ref, w_ref, gq_ref, gk_ref, c64_ref, sa64_ref, sb64_ref, c128_ref, s128_ref,
                   smat_ref, q_ref, qsw_ref, k_ref, v_ref, iq_ref, ik_ref, iw_ref):
    xn = _rms_rows(h_ref[...], g_ref[...]).astype(BF16)
    z = jnp.dot(xn, w_ref[...], preferred_element_type=F32)
    c64, sa64, sb64 = c64_ref[...], sa64_ref[...], sb64_ref[...]
    c128, s128 = c128_ref[...], s128_ref[...]
    smat = smat_ref[...]

    def head_norm_rope(zc, gain):
        z2 = zc * zc
        hi = z2.astype(BF16)
        lo = (z2 - hi.astype(F32)).astype(BF16)
        ms = jnp.dot(hi, smat, preferred_element_type=F32) + jnp.dot(lo, smat, preferred_element_type=F32)
        y = zc * lax.rsqrt(ms + RMS_EPS) * gain
        return y * c64 + pltpu.roll(y, 96, 1) * sa64 + pltpu.roll(y, 32, 1) * sb64

    for c in range(8):
        r = head_norm_rope(z[:, c * LANES:(c + 1) * LANES], gq_ref[...]) * (ATT_HEAD_DIM ** -0.5 * LOG2E)
        q_ref[:, c * LANES:(c + 1) * LANES] = r.astype(BF16)
        qsw_ref[:, c * LANES:(c + 1) * LANES] = pltpu.roll(r, 64, 1).astype(BF16)
    for c in range(2):
        r = head_norm_rope(z[:, 1024 + c * LANES:1024 + (c + 1) * LANES], gk_ref[...])
        k_ref[:, c * LANES:(c + 1) * LANES] = r.astype(BF16)
    ones = jnp.ones((z.shape[0], LANES), BF16)
    for c in range(2):
        v_ref[:, 2 * c * LANES:(2 * c + 1) * LANES] = z[:, 1280 + c * LANES:1280 + (c + 1) * LANES].astype(BF16)
        v_ref[:, (2 * c + 1) * LANES:(2 * c + 2) * LANES] = ones
    for c in range(9):
        zc = z[:, 1536 + c * LANES:1536 + (c + 1) * LANES]
        r = (zc * c128 + pltpu.roll(zc, 64, 1) * s128).astype(BF16)
        if c < 8:
            iq_ref[:, c * LANES:(c + 1) * LANES] = r
        else:
            ik_ref[...] = r
    iw_ref[...] = z[:, 2688:2816] * (IDX_HEADS ** -0.5 * IDX_DIM ** -0.5)


def _dsa_in_proj(h, gain, w, gq, gk, tabs, seq_pad, tm):
    T = h.shape[0]
    nt = seq_pad // tm
    row = lambda i: (i, 0)
    fixed = lambda i: (0, 0)
    pos = lambda i: (i % nt, 0)
    tab_spec = pl.BlockSpec((tm, LANES), pos)
    smat = jnp.asarray(np.kron(np.eye(2), np.full((64, 64), 1.0 / 64)), BF16)
    outs = [(1024, BF16), (1024, BF16), (256, BF16), (512, BF16), (1024, BF16), (128, BF16), (128, F32)]
    return pl.pallas_call(
        _dsa_in_kernel,
        grid=(T // tm,),
        in_specs=[pl.BlockSpec((tm, D_MODEL), row), pl.BlockSpec((1, D_MODEL), fixed),
                  pl.BlockSpec((D_MODEL, DSA_IN_PAD), fixed), pl.BlockSpec((1, LANES), fixed),
                  pl.BlockSpec((1, LANES), fixed), tab_spec, tab_spec, tab_spec, tab_spec, tab_spec,
                  pl.BlockSpec((LANES, LANES), fixed)],
        out_specs=[pl.BlockSpec((tm, n), row) for n, _ in outs],
        out_shape=[jax.ShapeDtypeStruct((T, n), dt) for n, dt in outs],
        compiler_params=_cparams(("arbitrary",)),
        name="dsa_in_proj",
    )(h, gain, w, gq, gk, *tabs, smat)


def _dsa_core_kernel(q_ref, qsw_ref, k_ref, v_ref, iq_ref, ik_ref, iw_ref, o_ref,
                     iqt_sc, iwt_sc, qs_sc, keys_sc, bias_sc, cut_sc, m_sc, l_sc, acc_sc,
                     *, top_k, seq_pad):
    i = pl.program_id(1)
    QB, KC = Q_BLOCK, KEY_CHUNK
    nch = (i * QB + QB + KC - 1) // KC
    kf = float(top_k)
    lane1 = lax.broadcasted_iota(jnp.int32, (QB, LANES), 1)

    for h in range(IDX_HEADS):
        iqt_sc[:, h * QB:(h + 1) * QB] = iq_ref[0, :, h * LANES:(h + 1) * LANES].astype(F32).T.astype(BF16)
    iwt_sc[...] = iw_ref[0].T

    for g in range(ATT_KV_HEADS):
        for r in range(ATT_GROUP):
            h = g * ATT_GROUP + r
            src = q_ref if h % 2 == g % 2 else qsw_ref
            chunk = src[0, :, (h // 2) * LANES:(h // 2 + 1) * LANES].astype(F32)
            keep = (lane1 >= 64) if g % 2 else (lane1 < 64)
            qs_sc[g, r * QB:(r + 1) * QB, :] = jnp.where(keep, chunk, 0.0).astype(BF16)

    qpos = i * QB + lax.broadcasted_iota(jnp.int32, (KC, QB), 1)
    krow = lax.broadcasted_iota(jnp.int32, (KC, QB), 0)

    def chunk_start(j):
        return pl.multiple_of(jnp.minimum(j * KC, seq_pad - KC), LANES)

    def key_valid(j, start):
        kidx = start + krow
        return (kidx <= qpos) & (kidx >= j * KC)

    def for_chunks(fn):
        def body(j2, carry):
            fn(2 * j2)
            fn(2 * j2 + 1)
            return carry

        lax.fori_loop(0, nch // 2, body, 0)

        @pl.when(nch % 2 == 1)
        def _():
            fn(nch - 1)

    def score_chunk(j):
        start = chunk_start(j)
        s = jnp.dot(ik_ref[0, pl.ds(start, KC), :], iqt_sc[...], preferred_element_type=F32)
        acc = jnp.zeros((KC, QB), F32)
        for h in range(IDX_HEADS):
            acc = acc + jnp.maximum(s[:, h * QB:(h + 1) * QB], 0.0) * iwt_sc[h:h + 1, :]
        bits = pltpu.bitcast(acc, jnp.int32)
        key = jnp.where(bits >= 0, bits, bits ^ jnp.int32(INT_MAX))
        keys_sc[j] = jnp.where(key_valid(j, start), key, jnp.int32(INT_MIN))

    for_chunks(score_chunk)

    tiles = (KC // SUBLANES, SUBLANES, QB)
    krow3 = (lax.broadcasted_iota(jnp.int32, tiles, 0) * SUBLANES + lax.broadcasted_iota(jnp.int32, tiles, 1))

    keys_sc[nch] = jnp.full((KC, QB), INT_MIN, jnp.int32)

    def count(pred):
        def body(j2, acc):
            for j in (2 * j2, 2 * j2 + 1):
                kk = keys_sc[j].reshape(tiles)
                hit = jnp.where(pred(kk, chunk_start(j)), jnp.float32(1.0), jnp.float32(0.0))
                while hit.shape[0] > 1:
                    half = hit.shape[0] // 2
                    hit = hit[:half] + hit[half:]
                acc = acc + hit[0]
            return acc
        acc = lax.fori_loop(0, (nch + 1) // 2, body, jnp.zeros((SUBLANES, QB), F32))
        for shift in (4, 2, 1):
            acc = acc + pltpu.roll(acc, shift, 0)
        return acc

    zero = jnp.zeros((SUBLANES, QB), jnp.int32)
    v0 = jnp.where(count(lambda kk, base: kk >= zero) >= kf, zero, jnp.int32(INT_MIN))

    def bit_body(b, v):
        cand = v | lax.shift_left(jnp.int32(1), (30 - b).astype(jnp.int32))
        return jnp.where(count(lambda kk, base: kk >= cand) >= kf, cand, v)

    thr = lax.fori_loop(jnp.int32(0), jnp.int32(31), bit_body, v0)
    cut_sc[...] = jnp.full((SUBLANES, QB), INT_MAX, jnp.int32)
    n_ge = count(lambda kk, base: kk >= thr)

    @pl.when(jnp.max(n_ge) > kf)
    def _():
        need = kf - count(lambda kk, base: kk > thr)

        def cut_body(b, c):
            cand = c | lax.shift_left(jnp.int32(1), (12 - b).astype(jnp.int32))
            n = count(lambda kk, base: jnp.where(kk == thr, base + krow3, jnp.int32(INT_MAX)) < cand)
            return jnp.where(n < need, cand, c)

        cut_sc[...] = lax.fori_loop(jnp.int32(0), jnp.int32(13), cut_body, zero)

    thr1 = thr[0:1, :]
    cut1 = cut_sc[0:1, :]

    def bias_chunk(j):
        kk = keys_sc[j]
        start = chunk_start(j)
        sel = (kk > thr1) | ((kk == thr1) & (start + krow <= cut1))
        bt = jnp.where(sel & key_valid(j, start), 0.0, -1e30)
        bias_sc[j] = jnp.concatenate([bt[c * QB:(c + 1) * QB].T for c in range(KC // QB)], axis=1)

    for_chunks(bias_chunk)

    m_sc[...] = jnp.full(m_sc.shape, -1e9, F32)
    l_sc[...] = jnp.zeros(l_sc.shape, F32)
    acc_sc[...] = jnp.zeros(acc_sc.shape, F32)

    def att_step(start, width, bias, groups):
        bias = jnp.concatenate([bias] * ATT_GROUP, axis=0)
        for g in groups:
            cg = g // 2
            kj = k_ref[0, pl.ds(start, width), cg * LANES:(cg + 1) * LANES]
            vj = v_ref[0, pl.ds(start, width), cg * 2 * LANES:(cg + 1) * 2 * LANES]
            s = lax.dot_general(qs_sc[g], kj, (((1,), (1,)), ((), ())), preferred_element_type=F32) + bias
            m_old = m_sc[g]
            m_new = jnp.maximum(m_old, jnp.max(s, axis=1, keepdims=True))
            alpha = jnp.exp2(m_old - m_new)
            p = jnp.exp2(s - jnp.concatenate([m_new] * (width // LANES), axis=1)).astype(BF16)
            pv = jnp.dot(p, vj, preferred_element_type=F32)
            l_sc[g] = alpha * l_sc[g] + pv[:, LANES:]
            acc_sc[g] = alpha * acc_sc[g] + pv[:, :LANES]
            m_sc[g] = m_new

    npair = jnp.minimum(nch // 2, seq_pad // (2 * KC))

    def single_body(j, carry):
        att_step(chunk_start(j), KC, bias_sc[j], range(ATT_KV_HEADS))
        return carry

    def pair_step(j):
        bias = jnp.concatenate([bias_sc[2 * j], bias_sc[2 * j + 1]], axis=1)
        att_step(pl.multiple_of(j * 2 * KC, 2 * KC), 2 * KC, bias, range(ATT_KV_HEADS))

    def quad_body(j2, carry):
        pair_step(2 * j2)
        pair_step(2 * j2 + 1)
        return carry

    lax.fori_loop(0, npair // 2, quad_body, 0)

    @pl.when(npair % 2 == 1)
    def _():
        pair_step(npair - 1)

    lax.fori_loop(2 * npair, nch, single_body, 0)

    for g in range(ATT_KV_HEADS):
        og = acc_sc[g] / l_sc[g]
        for pair in range(2):
            even = og[(2 * pair) * QB:(2 * pair + 1) * QB]
            odd = og[(2 * pair + 1) * QB:(2 * pair + 2) * QB]
            if g % 2:
                even = pltpu.roll(even, 64, 1)
            else:
                odd = pltpu.roll(odd, 64, 1)
            c = 2 * g + pair
            o_ref[0, :, c * LANES:(c + 1) * LANES] = jnp.where(lane1 < 64, even, odd).astype(BF16)


def _dsa_core(q, qsw, k, v, iq, ik, iw, top_k):
    B, seq_pad = q.shape[0], q.shape[1]
    nqb = seq_pad // Q_BLOCK
    nkc = -(-seq_pad // KEY_CHUNK)
    rows = ATT_GROUP * Q_BLOCK
    qblock = lambda n: pl.BlockSpec((1, Q_BLOCK, n), lambda b, i: (b, i, 0))
    whole = lambda n: pl.BlockSpec((1, seq_pad, n), lambda b, i: (b, 0, 0))
    return pl.pallas_call(
        functools.partial(_dsa_core_kernel, top_k=top_k, seq_pad=seq_pad),
        grid=(B, nqb),
        in_specs=[qblock(D_MODEL), qblock(D_MODEL), whole(256), whole(512),
                  qblock(IDX_HEADS * IDX_DIM), whole(IDX_DIM), qblock(LANES)],
        out_specs=qblock(D_MODEL),
        out_shape=jax.ShapeDtypeStruct((B, seq_pad, D_MODEL), BF16),
        scratch_shapes=[
            pltpu.VMEM((IDX_DIM, IDX_HEADS * Q_BLOCK), BF16),
            pltpu.VMEM((LANES, Q_BLOCK), F32),
            pltpu.VMEM((ATT_KV_HEADS, rows, LANES), BF16),
            pltpu.VMEM((nkc + 1, KEY_CHUNK, Q_BLOCK), jnp.int32),
            pltpu.VMEM((nkc, Q_BLOCK, KEY_CHUNK), F32),
            pltpu.VMEM((SUBLANES, Q_BLOCK), jnp.int32),
            pltpu.VMEM((ATT_KV_HEADS, rows, LANES), F32),
            pltpu.VMEM((ATT_KV_HEADS, rows, LANES), F32),
            pltpu.VMEM((ATT_KV_HEADS, rows, LANES), F32),
        ],
        compiler_params=_cparams(("arbitrary", "arbitrary")),
        name="dsa_core",
    )(q, qsw, k, v, iq, ik, iw)


def _out_proj_kernel(h_ref, y_ref, w_ref, o_ref):
    o_ref[...] = h_ref[...] + jnp.dot(y_ref[...], w_ref[...], preferred_element_type=F32)


def _out_proj(h, y, w, tm):
    T = h.shape[0]
    row = lambda i: (i, 0)
    return pl.pallas_call(
        _out_proj_kernel,
        grid=(T // tm,),
        in_specs=[pl.BlockSpec((tm, D_MODEL), row), pl.BlockSpec((tm, D_MODEL), row),
                  pl.BlockSpec((D_MODEL, D_MODEL), lambda i: (0, 0))],
        out_specs=pl.BlockSpec((tm, D_MODEL), row),
        out_shape=jax.ShapeDtypeStruct((T, D_MODEL), F32),
        compiler_params=_cparams(("arbitrary",)),
        name="out_proj",
    )(h, y, w)


def _mlstm_in_kernel(h_ref, g_ref, w_ref, q_ref, k_ref, v_ref, o_ref, gate_ref):
    xn = _rms_rows(h_ref[...], g_ref[...]).astype(BF16)
    z = jnp.dot(xn, w_ref[...], preferred_element_type=F32)
    q_ref[...] = z[:, 0:512].astype(BF16)
    k_ref[...] = (z[:, 512:1024] * (M_QK_DIM ** -0.5)).astype(BF16)
    v_ref[...] = z[:, 1024:2048].astype(BF16)
    o_ref[...] = z[:, 2048:3072]
    gate_ref[...] = z[:, 3072:3200]


def _mlstm_in_proj(h, gain, w, tm):
    T = h.shape[0]
    row = lambda i: (i, 0)
    fixed = lambda i: (0, 0)
    outs = [(512, BF16), (512, BF16), (1024, BF16), (1024, F32), (128, F32)]
    return pl.pallas_call(
        _mlstm_in_kernel,
        grid=(T // tm,),
        in_specs=[pl.BlockSpec((tm, D_MODEL), row), pl.BlockSpec((1, D_MODEL), fixed),
                  pl.BlockSpec((D_MODEL, MLSTM_IN_PAD), fixed)],
        out_specs=[pl.BlockSpec((tm, n), row) for n, _ in outs],
        out_shape=[jax.ShapeDtypeStruct((T, n), dt) for n, dt in outs],
        compiler_params=_cparams(("arbitrary",)),
        name="mlstm_in_proj",
    )(h, gain, w)


def _mlstm_kernel(q_ref, k_ref, v_ref, o_ref, gate_ref, bias_ref, gout_ref, y_ref, ct_sc, n_sc, m_sc):
    C = M_CHUNK

    @pl.when(pl.program_id(1) == 0)
    def _():
        ct_sc[...] = jnp.zeros(ct_sc.shape, F32)
        n_sc[...] = jnp.zeros(n_sc.shape, F32)
        m_sc[...] = jnp.zeros(m_sc.shape, F32)

    lane8 = lax.broadcasted_iota(jnp.int32, (8, C), 1)
    t_idx = lax.broadcasted_iota(jnp.int32, (C, C), 0)
    s_idx = lax.broadcasted_iota(jnp.int32, (C, C), 1)
    for bi in range(q_ref.shape[0]):
        pre = gate_ref[bi].T[0:8, :] + bias_ref[...]
        capped = GATE_CAP * jnp.tanh(pre / GATE_CAP)
        log_f = -(jnp.maximum(-capped, 0.0) + jnp.log1p(jnp.exp(-jnp.abs(capped))))
        b = log_f
        sh = 1
        while sh < C:
            b = b + jnp.where(lane8 >= sh, pltpu.roll(b, sh, 1), 0.0)
            sh *= 2
        stacked = jnp.concatenate([b[4:8], capped[0:4]], axis=0)
        cols = jnp.concatenate([stacked, jnp.zeros((C - 8, C), F32)], axis=0).T

        for h in range(M_HEADS):
            st = bi * M_HEADS + h
            qh = q_ref[bi, :, h * M_QK_DIM:(h + 1) * M_QK_DIM]
            kh = k_ref[bi, :, h * M_QK_DIM:(h + 1) * M_QK_DIM]
            vh = v_ref[bi, :, h * M_V_DIM:(h + 1) * M_V_DIM]
            b_row, li_row = stacked[h:h + 1, :], stacked[4 + h:5 + h, :]
            b_col, li_col = cols[:, h:h + 1], cols[:, 4 + h:5 + h]
            m_st = m_sc[st:st + 1, 0:1]
            dmat = jnp.where(s_idx <= t_idx, b_col - b_row + li_row, -jnp.inf)
            inter = b_col + m_st
            m_t = jnp.maximum(inter, jnp.max(dmat, axis=1, keepdims=True))
            w_inter = jnp.exp(inter - m_t)
            qk = lax.dot_general(qh, kh, (((1,), (1,)), ((), ())), preferred_element_type=F32)
            s = qk * jnp.exp(dmat - m_t)
            ct = ct_sc[st]
            num = (w_inter * jnp.dot(qh, ct.astype(BF16), preferred_element_type=F32)
                   + jnp.dot(s.astype(BF16), vh, preferred_element_type=F32))
            qn = jnp.sum(qh.astype(F32) * n_sc[st:st + 1, :], axis=1, keepdims=True)
            den = w_inter * qn + jnp.sum(s, axis=1, keepdims=True)
            hout = num / jnp.maximum(jnp.abs(den), jnp.exp(-m_t))

            b_last = b_row[:, C - 1:C]
            m_new = jnp.maximum(b_last + m_st, jnp.max(b_last - b_row + li_row, axis=1, keepdims=True))
            decay = jnp.exp(b_last + m_st - m_new)
            wk = jnp.exp(b_last - b_col + li_col - m_new)
            kw = kh.astype(F32) * wk
            ct_sc[st] = decay * ct + jnp.dot(kw.T.astype(BF16), vh, preferred_element_type=F32)
            n_sc[st:st + 1, :] = decay * n_sc[st:st + 1, :] + jnp.sum(kw, axis=0, keepdims=True)
            m_sc[st:st + 1, :] = jnp.broadcast_to(m_new, (1, LANES))

            cs = slice(h * M_V_DIM, (h + 1) * M_V_DIM)
            hn = _rms_rows(hout, gout_ref[:, cs])
            y_ref[bi, :, cs] = (hn * jax.nn.sigmoid(o_ref[bi, :, cs])).astype(BF16)


def _mlstm_core(q, k, v, o, gates, bias8, gout):
    B, seq_pad = q.shape[0], q.shape[1]
    nb = M_BATCH_ROWS
    C = M_CHUNK
    blk = lambda n: pl.BlockSpec((nb, C, n), lambda b, c: (b, c, 0))
    fixed = lambda b, c: (0, 0)
    return pl.pallas_call(
        _mlstm_kernel,
        grid=(B // nb, seq_pad // C),
        in_specs=[blk(512), blk(512), blk(1024), blk(1024), blk(LANES),
                  pl.BlockSpec((8, 1), fixed), pl.BlockSpec((1, D_MODEL), fixed)],
        out_specs=blk(D_MODEL),
        out_shape=jax.ShapeDtypeStruct((B, seq_pad, D_MODEL), BF16),
        scratch_shapes=[pltpu.VMEM((nb * M_HEADS, M_QK_DIM, M_V_DIM), F32),
                        pltpu.VMEM((nb * M_HEADS, M_QK_DIM), F32),
                        pltpu.VMEM((nb * M_HEADS, LANES), F32)],
        compiler_params=_cparams(("arbitrary", "arbitrary")),
        name="mlstm_core",
    )(q, k, v, o, gates, bias8, gout)


def _ffn_kernel(h_ref, y_ref, wo_ref, g_ref, wg_ref, wu_ref, wd_ref, o_ref, xn_sc, acc_sc):
    f = pl.program_id(1)

    @pl.when(f == 0)
    def _():
        x = h_ref[...] + jnp.dot(y_ref[...], wo_ref[...], preferred_element_type=F32)
        xn_sc[...] = _rms_rows(x, g_ref[...]).astype(BF16)
        acc_sc[...] = x

    xn = xn_sc[...]
    gate = jnp.dot(xn, wg_ref[...], preferred_element_type=F32)
    up = jnp.dot(xn, wu_ref[...], preferred_element_type=F32)
    act = (gate * jax.nn.sigmoid(gate) * up).astype(BF16)
    acc_sc[...] += jnp.dot(act, wd_ref[...], preferred_element_type=F32)

    @pl.when(f == pl.num_programs(1) - 1)
    def _():
        o_ref[...] = acc_sc[...]


def _ffn(h, y, wo, gain, wg, wu, wd, tm, tf):
    T = h.shape[0]
    return pl.pallas_call(
        _ffn_kernel,
        grid=(T // tm, D_FF // tf),
        in_specs=[pl.BlockSpec((tm, D_MODEL), lambda i, f: (i, 0)),
                  pl.BlockSpec((tm, D_MODEL), lambda i, f: (i, 0)),
                  pl.BlockSpec((D_MODEL, D_MODEL), lambda i, f: (0, 0)),
                  pl.BlockSpec((1, D_MODEL), lambda i, f: (0, 0)),
                  pl.BlockSpec((D_MODEL, tf), lambda i, f: (0, f)),
                  pl.BlockSpec((D_MODEL, tf), lambda i, f: (0, f)),
                  pl.BlockSpec((tf, D_MODEL), lambda i, f: (f, 0))],
        out_specs=pl.BlockSpec((tm, D_MODEL), lambda i, f: (i, 0)),
        out_shape=jax.ShapeDtypeStruct((T, D_MODEL), F32),
        scratch_shapes=[pltpu.VMEM((tm, D_MODEL), BF16), pltpu.VMEM((tm, D_MODEL), F32)],
        compiler_params=_cparams(("arbitrary", "arbitrary")),
        name="ffn_dense",
    )(h, y, wo, gain, wg, wu, wd)


def _moe_kernel(h_ref, g_ref, wr_ref, wg_ref, wu_ref, wd_ref, o_ref,
                xn_sc, xs_sc, ys_sc, posc_sc, posr_sc, comb_sc, cnt_sc):
    e = pl.program_id(1)
    f = pl.program_id(2)
    tm = h_ref.shape[0]
    SB = _moe_slot_block(tm)
    TB = _moe_token_block(tm)
    lane = lax.broadcasted_iota(jnp.int32, (tm, LANES), 1)

    @pl.when((e == 0) & (f == 0))
    def _():
        x = h_ref[...]
        xn = _rms_rows(x, g_ref[...])
        xn_sc[...] = xn.astype(BF16)
        o_ref[...] = x
        logits = jnp.dot(xn, wr_ref[...], preferred_element_type=F32, precision=lax.Precision.HIGHEST)
        logits = jnp.where(lane < N_EXPERTS, logits, -jnp.inf)
        m1 = jnp.max(logits, axis=1, keepdims=True)
        i1 = jnp.min(jnp.where(logits == m1, lane, LANES), axis=1, keepdims=True)
        rest = jnp.where(lane == i1, -jnp.inf, logits)
        m2 = jnp.max(rest, axis=1, keepdims=True)
        i2 = jnp.min(jnp.where(rest == m2, lane, LANES), axis=1, keepdims=True)
        e2 = jnp.exp(m2 - m1)
        g1 = 1.0 / (1.0 + e2)
        comb_sc[...] = jnp.where(lane == i1, g1, jnp.where(lane == i2, e2 * g1, 0.0))
        member = jnp.where(lane == i1, 1.0, jnp.where(lane == i2, 1.0, 0.0))
        tri = jnp.where(lax.broadcasted_iota(jnp.int32, (TB, TB), 0) >= lax.broadcasted_iota(jnp.int32, (TB, TB), 1),
                        1.0, 0.0).astype(BF16)
        carry = jnp.zeros((1, LANES), F32)
        for b in range(tm // TB):
            mb = member[b * TB:(b + 1) * TB]
            incl = jnp.dot(tri, mb.astype(BF16), preferred_element_type=F32)
            posc_sc[b * TB:(b + 1) * TB, :] = jnp.where(mb > 0.0, incl - 1.0 + carry, -1.0)
            carry = carry + incl[TB - 1:TB, :]
        cnt_sc[...] = carry
        for b in range(tm // LANES):
            posr_sc[:, b * LANES:(b + 1) * LANES] = posc_sc[b * LANES:(b + 1) * LANES, :].T

    lane_row = lax.broadcasted_iota(jnp.int32, (1, LANES), 1)
    count = jnp.sum(jnp.where(lane_row == e, cnt_sc[...], 0.0))
    nblk = ((count + (SB - 1.0)) * (1.0 / SB)).astype(jnp.int32)

    @pl.when(f == 0)
    def _():
        slots_of_tokens = posr_sc[pl.ds(e, 1), :]

        def body(r, carry):
            row0 = pl.multiple_of(r * SB, 16)
            slot = (r * SB + lax.broadcasted_iota(jnp.int32, (SB, tm), 0)).astype(F32)
            onehot = jnp.where(slots_of_tokens == slot, 1.0, 0.0).astype(BF16)
            xs_sc[pl.ds(row0, SB), :] = jnp.dot(onehot, xn_sc[...], preferred_element_type=F32).astype(BF16)
            ys_sc[pl.ds(row0, SB), :] = jnp.zeros((SB, D_MODEL), F32)
            return carry

        lax.fori_loop(0, nblk, body, 0)

    def ffn_rows(row0, nrows):
        xs = xs_sc[pl.ds(row0, nrows), :]
        gate = jnp.dot(xs, wg_ref[0], preferred_element_type=F32)
        up = jnp.dot(xs, wu_ref[0], preferred_element_type=F32)
        act = (gate * jax.nn.sigmoid(gate) * up).astype(BF16)
        ys_sc[pl.ds(row0, nrows), :] += jnp.dot(act, wd_ref[0], preferred_element_type=F32)

    def pair_body(r, carry):
        ffn_rows(pl.multiple_of(r * 2 * SB, 16), 2 * SB)
        return carry

    lax.fori_loop(0, nblk // 2, pair_body, 0)

    @pl.when(nblk % 2 == 1)
    def _():
        ffn_rows(pl.multiple_of((nblk - 1) * SB, 16), SB)

    @pl.when(f == pl.num_programs(2) - 1)
    def _():
        on_e = lane == e
        gate_e = jnp.sum(jnp.where(on_e, comb_sc[...], 0.0), axis=1, keepdims=True)
        slot_e = jnp.sum(jnp.where(on_e, posc_sc[...], 0.0), axis=1, keepdims=True)

        def body(r, carry):
            row0 = pl.multiple_of(r * SB, 16)
            ys = ys_sc[pl.ds(row0, SB), :].astype(BF16)
            slot = (r * SB + lax.broadcasted_iota(jnp.int32, (TB, SB), 1)).astype(F32)
            for tb in range(tm // TB):
                rows = slice(tb * TB, (tb + 1) * TB)
                onehot = jnp.where(slot_e[rows] == slot, 1.0, 0.0).astype(BF16)
                o_ref[rows, :] += gate_e[rows] * jnp.dot(onehot, ys, preferred_element_type=F32)
            return carry

        lax.fori_loop(0, nblk, body, 0)


def _moe_slot_block(tm):
    return MOE_SLOT_BLOCK if tm > MOE_SLOT_BLOCK else tm


def _moe_token_block(tm):
    return MOE_TOKEN_BLOCK if tm % MOE_TOKEN_BLOCK == 0 else tm


def _moe(h, gain, wr, wg, wu, wd, tm, tf):
    T = h.shape[0]
    sb = _moe_slot_block(tm)
    cap = -(-tm // sb) * sb
    return pl.pallas_call(
        _moe_kernel,
        grid=(T // tm, N_EXPERTS, D_FF // tf),
        in_specs=[pl.BlockSpec((tm, D_MODEL), lambda i, e, f: (i, 0), pipeline_mode=pl.Buffered(1)),
                  pl.BlockSpec((1, D_MODEL), lambda i, e, f: (0, 0)),
                  pl.BlockSpec((D_MODEL, LANES), lambda i, e, f: (0, 0)),
                  pl.BlockSpec((1, D_MODEL, tf), lambda i, e, f: (e, 0, f)),
                  pl.BlockSpec((1, D_MODEL, tf), lambda i, e, f: (e, 0, f)),
                  pl.BlockSpec((1, tf, D_MODEL), lambda i, e, f: (e, f, 0))],
        out_specs=pl.BlockSpec((tm, D_MODEL), lambda i, e, f: (i, 0)),
        out_shape=jax.ShapeDtypeStruct((T, D_MODEL), F32),
        scratch_shapes=[pltpu.VMEM((tm, D_MODEL), BF16), pltpu.VMEM((cap, D_MODEL), BF16),
                        pltpu.VMEM((cap, D_MODEL), F32), pltpu.VMEM((tm, LANES), F32),
                        pltpu.VMEM((LANES, tm), F32), pltpu.VMEM((tm, LANES), F32),
                        pltpu.VMEM((1, LANES), F32)],
        compiler_params=_cparams(("arbitrary", "arbitrary", "arbitrary")),
        name="moe_sparse",
    )(h, gain, wr, wg, wu, wd)


def _rope_tables(seq_pad):
    pos = jnp.arange(seq_pad, dtype=F32)[:, None]
    lane = np.arange(LANES)

    def table(head_dim):
        half = head_dim // 2
        inv = ROPE_THETA ** (-jnp.arange(half, dtype=F32) / half)
        d = lane % head_dim
        ang = pos * inv[d % half][None, :]
        return jnp.cos(ang), jnp.sin(ang), jnp.asarray(d < half)[None, :]

    c64, s64, lo64 = table(ATT_HEAD_DIM)
    c128, s128, lo128 = table(IDX_DIM)
    return (c64, jnp.where(lo64, -s64, 0.0), jnp.where(lo64, 0.0, s64), c128, jnp.where(lo128, -s128, s128))


def _pad_cols(w, n):
    return jnp.pad(w, ((0, 0), (0, n - w.shape[1])))


def kernel(x, meta, norm_mixer, norm_ffn, dsa_w_in, dsa_q_norm, dsa_k_norm, dsa_w_out, mlstm_w_in, mlstm_b_i,
           mlstm_b_f, mlstm_out_norm, mlstm_w_out, ffn_w_gate, ffn_w_up, ffn_w_down, moe_router, moe_w_gate,
           moe_w_up, moe_w_down):
    B, S, D = x.shape
    L = S + N_META
    top_k = min(TOPK_MAX, S // 4)
    seq_pad = max(-(-L // Q_BLOCK) * Q_BLOCK, KEY_CHUNK)
    T = B * seq_pad
    tm_proj = 384 if seq_pad % 384 == 0 else Q_BLOCK
    tm_ffn = 768 if T % 768 == 0 else Q_BLOCK
    tm_moe = 1536 if T % 1536 == 0 else Q_BLOCK
    tf = 512

    h = jnp.concatenate([jnp.broadcast_to(meta[None].astype(x.dtype), (B, N_META, D)), x,
                         jnp.zeros((B, seq_pad - L, D), x.dtype)], axis=1).reshape(T, D)
    tabs = _rope_tables(seq_pad)
    depth = norm_mixer.shape[0]

    for i in range(depth):
        j = i // 2
        gain_m = norm_mixer[i][None, :]
        gain_f = norm_ffn[i][None, :]
        if i % 2 == 0:
            w_in = _pad_cols(dsa_w_in[j], DSA_IN_PAD).astype(BF16)
            gq = jnp.tile(dsa_q_norm[j], 2)[None, :]
            gk = jnp.tile(dsa_k_norm[j], 2)[None, :]
            proj = _dsa_in_proj(h, gain_m, w_in, gq, gk, tabs, seq_pad, tm_proj)
            att = _dsa_core(*[a.reshape(B, seq_pad, a.shape[-1]) for a in proj], top_k)
            h = _ffn(h, att.reshape(T, D), dsa_w_out[j].astype(BF16), gain_f, ffn_w_gate[j].astype(BF16),
                     ffn_w_up[j].astype(BF16), ffn_w_down[j].astype(BF16), tm_ffn, tf)
        else:
            w_in = _pad_cols(mlstm_w_in[j], MLSTM_IN_PAD).astype(BF16)
            q, k, v, o, gates = _mlstm_in_proj(h, gain_m, w_in, tm_proj)
            bias8 = jnp.concatenate([mlstm_b_i[j], mlstm_b_f[j]])[:, None]
            y = _mlstm_core(*[a.reshape(B, seq_pad, a.shape[-1]) for a in (q, k, v, o, gates)], bias8,
                            mlstm_out_norm[j][None, :])
            h = _out_proj(h, y.reshape(T, D), mlstm_w_out[j].astype(BF16), tm_proj)
            wr = _pad_cols(moe_router[j], LANES)
            h = _moe(h, gain_f, wr, moe_w_gate[j].astype(BF16), moe_w_up[j].astype(BF16),
                     moe_w_down[j].astype(BF16), tm_moe, tf)
    return h.reshape(B, seq_pad, D)[:, N_META:L]
```

```python
import functools

import jax
import jax.numpy as jnp
import numpy as np
from jax import lax
from jax.experimental import pallas as pl
from jax.experimental.pallas import tpu as pltpu

F32 = jnp.float32
BF16 = jnp.bfloat16

D_MODEL = 1024
N_META = 16
RMS_EPS = 1e-6
ROPE_THETA = 10000.0
ATT_HEADS = 16
ATT_KV_HEADS = 4
ATT_HEAD_DIM = 64
ATT_GROUP = 4
IDX_HEADS = 8
IDX_DIM = 128
TOPK_MAX = 256
M_HEADS = 4
M_QK_DIM = 128
M_V_DIM = 256
GATE_CAP = 15.0
D_FF = 3584
N_EXPERTS = 8

LANES = 128
SUBLANES = 8
Q_BLOCK = 128
KEY_CHUNK = 256
ATT_STEPS_PER_TRIP = 4
M_CHUNK = 128
M_BATCH_ROWS = 1
MOE_SLOT_BLOCK = 208
MOE_TOKEN_BLOCK = 256
VMEM_LIMIT = 52 * 1024 * 1024
INT_MIN = -(2 ** 31)
INT_MAX = 2 ** 31 - 1
LOG2E = 1.4426950408889634

DSA_IN_PAD = 2816
MLSTM_IN_PAD = 3200


def _cparams(sem):
    return pltpu.CompilerParams(dimension_semantics=sem, vmem_limit_bytes=VMEM_LIMIT)


def _rms_rows(x, gain):
    ms = jnp.mean(x * x, axis=-1, keepdims=True)
    return x * lax.rsqrt(ms + RMS_EPS) * gain


def _dsa_in_kernel(h_ref, g_ref, w_ref, gq_ref, gk_ref, c64_ref, sa64_ref, sb64_ref, c128_ref, s128_ref,
                   smat_ref, q_ref, qsw_ref, k_ref, v_ref, iq_ref, ik_ref, iw_ref):
    xn = _rms_rows(h_ref[...], g_ref[...]).astype(BF16)
    z = jnp.dot(xn, w_ref[...], preferred_element_type=F32)
    c64, sa64, sb64 = c64_ref[...], sa64_ref[...], sb64_ref[...]
    c128, s128 = c128_ref[...], s128_ref[...]
    smat = smat_ref[...]

    def head_norm_rope(zc, gain):
        z2 = zc * zc
        hi = z2.astype(BF16)
        lo = (z2 - hi.astype(F32)).astype(BF16)
        ms = jnp.dot(hi, smat, preferred_element_type=F32) + jnp.dot(lo, smat, preferred_element_type=F32)
        y = zc * lax.rsqrt(ms + RMS_EPS) * gain
        return y * c64 + pltpu.roll(y, 96, 1) * sa64 + pltpu.roll(y, 32, 1) * sb64

    for c in range(8):
        r = head_norm_rope(z[:, c * LANES:(c + 1) * LANES], gq_ref[...]) * (ATT_HEAD_DIM ** -0.5 * LOG2E)
        q_ref[:, c * LANES:(c + 1) * LANES] = r.astype(BF16)
        qsw_ref[:, c * LANES:(c + 1) * LANES] = pltpu.roll(r, 64, 1).astype(BF16)
    for c in range(2):
        r = head_norm_rope(z[:, 1024 + c * LANES:1024 + (c + 1) * LANES], gk_ref[...])
        k_ref[:, c * LANES:(c + 1) * LANES] = r.astype(BF16)
    ones = jnp.ones((z.shape[0], LANES), BF16)
    for c in range(2):
        v_ref[:, 2 * c * LANES:(2 * c + 1) * LANES] = z[:, 1280 + c * LANES:1280 + (c + 1) * LANES].astype(BF16)
        v_ref[:, (2 * c + 1) * LANES:(2 * c + 2) * LANES] = ones
    for c in range(9):
        zc = z[:, 1536 + c * LANES:1536 + (c + 1) * LANES]
        r = (zc * c128 + pltpu.roll(zc, 64, 1) * s128).astype(BF16)
        if c < 8:
            iq_ref[:, c * LANES:(c + 1) * LANES] = r
        else:
            ik_ref[...] = r
    iw_ref[...] = z[:, 2688:2816] * (IDX_HEADS ** -0.5 * IDX_DIM ** -0.5)


def _dsa_in_proj(h, gain, w, gq, gk, tabs, seq_pad, tm):
    T = h.shape[0]
    nt = seq_pad // tm
    row = lambda i: (i, 0)
    fixed = lambda i: (0, 0)
    pos = lambda i: (i % nt, 0)
    tab_spec = pl.BlockSpec((tm, LANES), pos)
    smat = jnp.asarray(np.kron(np.eye(2), np.full((64, 64), 1.0 / 64)), BF16)
    outs = [(1024, BF16), (1024, BF16), (256, BF16), (512, BF16), (1024, BF16), (128, BF16), (128, F32)]
    return pl.pallas_call(
        _dsa_in_kernel,
        grid=(T // tm,),
        in_specs=[pl.BlockSpec((tm, D_MODEL), row), pl.BlockSpec((1, D_MODEL), fixed),
                  pl.BlockSpec((D_MODEL, DSA_IN_PAD), fixed), pl.BlockSpec((1, LANES), fixed),
                  pl.BlockSpec((1, LANES), fixed), tab_spec, tab_spec, tab_spec, tab_spec, tab_spec,
                  pl.BlockSpec((LANES, LANES), fixed)],
        out_specs=[pl.BlockSpec((tm, n), row) for n, _ in outs],
        out_shape=[jax.ShapeDtypeStruct((T, n), dt) for n, dt in outs],
        compiler_params=_cparams(("arbitrary",)),
        name="dsa_in_proj",
    )(h, gain, w, gq, gk, *tabs, smat)


def _dsa_core_kernel(q_ref, qsw_ref, k_ref, v_ref, iq_ref, ik_ref, iw_ref, o_ref,
                     iqt_sc, iwt_sc, qs_sc, keys_sc, bias_sc, cut_sc, m_sc, l_sc, acc_sc,
                     *, top_k, seq_pad):
    i = pl.program_id(1)
    QB, KC = Q_BLOCK, KEY_CHUNK
    nch = (i * QB + QB + KC - 1) // KC
    kf = float(top_k)
    lane1 = lax.broadcasted_iota(jnp.int32, (QB, LANES), 1)

    for h in range(IDX_HEADS):
        iqt_sc[:, h * QB:(h + 1) * QB] = iq_ref[0, :, h * LANES:(h + 1) * LANES].astype(F32).T.astype(BF16)
    iwt_sc[...] = iw_ref[0].T

    for g in range(ATT_KV_HEADS):
        for r in range(ATT_GROUP):
            h = g * ATT_GROUP + r
            src = q_ref if h % 2 == g % 2 else qsw_ref
            chunk = src[0, :, (h // 2) * LANES:(h // 2 + 1) * LANES].astype(F32)
            keep = (lane1 >= 64) if g % 2 else (lane1 < 64)
            qs_sc[g, r * QB:(r + 1) * QB, :] = jnp.where(keep, chunk, 0.0).astype(BF16)

    qpos = i * QB + lax.broadcasted_iota(jnp.int32, (KC, QB), 1)
    krow = lax.broadcasted_iota(jnp.int32, (KC, QB), 0)

    def chunk_start(j):
        return pl.multiple_of(jnp.minimum(j * KC, seq_pad - KC), LANES)

    def key_valid(j, start):
        kidx = start + krow
        return (kidx <= qpos) & (kidx >= j * KC)

    def for_chunks(fn):
        def body(j2, carry):
            fn(2 * j2)
            fn(2 * j2 + 1)
            return carry

        lax.fori_loop(0, nch // 2, body, 0)

        @pl.when(nch % 2 == 1)
        def _():
            fn(nch - 1)

    def score_chunk(j):
        start = chunk_start(j)
        s = jnp.dot(ik_ref[0, pl.ds(start, KC), :], iqt_sc[...], preferred_element_type=F32)
        acc = jnp.zeros((KC, QB), F32)
        for h in range(IDX_HEADS):
            acc = acc + jnp.maximum(s[:, h * QB:(h + 1) * QB], 0.0) * iwt_sc[h:h + 1, :]
        bits = pltpu.bitcast(acc, jnp.int32)
        key = jnp.where(bits >= 0, bits, bits ^ jnp.int32(INT_MAX))
        keys_sc[j] = jnp.where(key_valid(j, start), key, jnp.int32(INT_MIN))

    for_chunks(score_chunk)

    tiles = (KC // SUBLANES, SUBLANES, QB)
    krow3 = (lax.broadcasted_iota(jnp.int32, tiles, 0) * SUBLANES + lax.broadcasted_iota(jnp.int32, tiles, 1))

    keys_sc[nch] = jnp.full((KC, QB), INT_MIN, jnp.int32)

    def count(pred):
        def body(j2, acc):
            for j in (2 * j2, 2 * j2 + 1):
                kk = keys_sc[j].reshape(tiles)
                hit = jnp.where(pred(kk, chunk_start(j)), jnp.float32(1.0), jnp.float32(0.0))
                while hit.shape[0] > 1:
                    half = hit.shape[0] // 2
                    hit = hit[:half] + hit[half:]
                acc = acc + hit[0]
            return acc
        acc = lax.fori_loop(0, (nch + 1) // 2, body, jnp.zeros((SUBLANES, QB), F32))
        for shift in (4, 2, 1):
            acc = acc + pltpu.roll(acc, shift, 0)
        return acc

    zero = jnp.zeros((SUBLANES, QB), jnp.int32)
    v0 = jnp.where(count(lambda kk, base: kk >= zero) >= kf, zero, jnp.int32(INT_MIN))

    def bit_body(b, v):
        cand = v | lax.shift_left(jnp.int32(1), (30 - b).astype(jnp.int32))
        return jnp.where(count(lambda kk, base: kk >= cand) >= kf, cand, v)

    thr = lax.fori_loop(jnp.int32(0), jnp.int32(31), bit_body, v0)
    cut_sc[...] = jnp.full((SUBLANES, QB), INT_MAX, jnp.int32)
    n_ge = count(lambda kk, base: kk >= thr)

    @pl.when(jnp.max(n_ge) > kf)
    def _():
        need = kf - count(lambda kk, base: kk > thr)

        def cut_body(b, c):
            cand = c | lax.shift_left(jnp.int32(1), (12 - b).astype(jnp.int32))
            n = count(lambda kk, base: jnp.where(kk == thr, base + krow3, jnp.int32(INT_MAX)) < cand)
            return jnp.where(n < need, cand, c)

        cut_sc[...] = lax.fori_loop(jnp.int32(0), jnp.int32(13), cut_body, zero)

    thr1 = thr[0:1, :]
    cut1 = cut_sc[0:1, :]

    def bias_chunk(j):
        kk = keys_sc[j]
        start = chunk_start(j)
        sel = (kk > thr1) | ((kk == thr1) & (start + krow <= cut1))
        bt = jnp.where(sel & key_valid(j, start), 0.0, -1e30)
        bias_sc[j] = jnp.concatenate([bt[c * QB:(c + 1) * QB].T for c in range(KC // QB)], axis=1)

    for_chunks(bias_chunk)

    m_sc[...] = jnp.full(m_sc.shape, -1e9, F32)
    l_sc[...] = jnp.zeros(l_sc.shape, F32)
    acc_sc[...] = jnp.zeros(acc_sc.shape, F32)

    def att_step(start, width, bias, groups):
        bias = jnp.concatenate([bias] * ATT_GROUP, axis=0)
        for g in groups:
            cg = g // 2
            kj = k_ref[0, pl.ds(start, width), cg * LANES:(cg + 1) * LANES]
            vj = v_ref[0, pl.ds(start, width), cg * 2 * LANES:(cg + 1) * 2 * LANES]
            s = lax.dot_general(qs_sc[g], kj, (((1,), (1,)), ((), ())), preferred_element_type=F32) + bias
            m_old = m_sc[g]
            m_new = jnp.maximum(m_old, jnp.max(s, axis=1, keepdims=True))
            alpha = jnp.exp2(m_old - m_new)
            p = jnp.exp2(s - jnp.concatenate([m_new] * (width // LANES), axis=1)).astype(BF16)
            pv = jnp.dot(p, vj, preferred_element_type=F32)
            l_sc[g] = alpha * l_sc[g] + pv[:, LANES:]
            acc_sc[g] = alpha * acc_sc[g] + pv[:, :LANES]
            m_sc[g] = m_new

    npair = jnp.minimum(nch // 2, seq_pad // (2 * KC))

    def single_body(j, carry):
        att_step(chunk_start(j), KC, bias_sc[j], range(ATT_KV_HEADS))
        return carry

    def pair_step(j):
        bias = jnp.concatenate([bias_sc[2 * j], bias_sc[2 * j + 1]], axis=1)
        att_step(pl.multiple_of(j * 2 * KC, 2 * KC), 2 * KC, bias, range(ATT_KV_HEADS))

    def multi_body(j4, carry):
        for t in range(ATT_STEPS_PER_TRIP):
            pair_step(ATT_STEPS_PER_TRIP * j4 + t)
        return carry

    lax.fori_loop(0, npair // ATT_STEPS_PER_TRIP, multi_body, 0)
    rest = npair % ATT_STEPS_PER_TRIP
    done = npair - rest
    n = ATT_STEPS_PER_TRIP // 2
    while n >= 1:
        @pl.when((rest & n) != 0)
        def _(n=n, done=done):
            for t in range(n):
                pair_step(done + t)

        done = done + (rest & n)
        n //= 2

    lax.fori_loop(2 * npair, nch, single_body, 0)

    for g in range(ATT_KV_HEADS):
        og = acc_sc[g] / l_sc[g]
        for pair in range(2):
            even = og[(2 * pair) * QB:(2 * pair + 1) * QB]
            odd = og[(2 * pair + 1) * QB:(2 * pair + 2) * QB]
            if g % 2:
                even = pltpu.roll(even, 64, 1)
            else:
                odd = pltpu.roll(odd, 64, 1)
            c = 2 * g + pair
            o_ref[0, :, c * LANES:(c + 1) * LANES] = jnp.where(lane1 < 64, even, odd).astype(BF16)


def _dsa_core(q, qsw, k, v, iq, ik, iw, top_k):
    B, seq_pad = q.shape[0], q.shape[1]
    nqb = seq_pad // Q_BLOCK
    nkc = -(-seq_pad // KEY_CHUNK)
    rows = ATT_GROUP * Q_BLOCK
    qblock = lambda n: pl.BlockSpec((1, Q_BLOCK, n), lambda b, i: (b, i, 0))
    whole = lambda n: pl.BlockSpec((1, seq_pad, n), lambda b, i: (b, 0, 0))
    return pl.pallas_call(
        functools.partial(_dsa_core_kernel, top_k=top_k, seq_pad=seq_pad),
        grid=(B, nqb),
        in_specs=[qblock(D_MODEL), qblock(D_MODEL), whole(256), whole(512),
                  qblock(IDX_HEADS * IDX_DIM), whole(IDX_DIM), qblock(LANES)],
        out_specs=qblock(D_MODEL),
        out_shape=jax.ShapeDtypeStruct((B, seq_pad, D_MODEL), BF16),
        scratch_shapes=[
            pltpu.VMEM((IDX_DIM, IDX_HEADS * Q_BLOCK), BF16),
            pltpu.VMEM((LANES, Q_BLOCK), F32),
            pltpu.VMEM((ATT_KV_HEADS, rows, LANES), BF16),
            pltpu.VMEM((nkc + 1, KEY_CHUNK, Q_BLOCK), jnp.int32),
            pltpu.VMEM((nkc, Q_BLOCK, KEY_CHUNK), F32),
            pltpu.VMEM((SUBLANES, Q_BLOCK), jnp.int32),
            pltpu.VMEM((ATT_KV_HEADS, rows, LANES), F32),
            pltpu.VMEM((ATT_KV_HEADS, rows, LANES), F32),
            pltpu.VMEM((ATT_KV_HEADS, rows, LANES), F32),
        ],
        compiler_params=_cparams(("arbitrary", "arbitrary")),
        name="dsa_core",
    )(q, qsw, k, v, iq, ik, iw)


def _out_proj_kernel(h_ref, y_ref, w_ref, o_ref):
    o_ref[...] = h_ref[...] + jnp.dot(y_ref[...], w_ref[...], preferred_element_type=F32)


def _out_proj(h, y, w, tm):
    T = h.shape[0]
    row = lambda i: (i, 0)
    return pl.pallas_call(
        _out_proj_kernel,
        grid=(T // tm,),
        in_specs=[pl.BlockSpec((tm, D_MODEL), row), pl.BlockSpec((tm, D_MODEL), row),
                  pl.BlockSpec((D_MODEL, D_MODEL), lambda i: (0, 0))],
        out_specs=pl.BlockSpec((tm, D_MODEL), row),
        out_shape=jax.ShapeDtypeStruct((T, D_MODEL), F32),
        compiler_params=_cparams(("arbitrary",)),
        name="out_proj",
    )(h, y, w)


def _mlstm_in_kernel(h_ref, g_ref, w_ref, q_ref, k_ref, v_ref, o_ref, gate_ref):
    xn = _rms_rows(h_ref[...], g_ref[...]).astype(BF16)
    z = jnp.dot(xn, w_ref[...], preferred_element_type=F32)
    q_ref[...] = z[:, 0:512].astype(BF16)
    k_ref[...] = (z[:, 512:1024] * (M_QK_DIM ** -0.5)).astype(BF16)
    v_ref[...] = z[:, 1024:2048].astype(BF16)
    o_ref[...] = z[:, 2048:3072]
    gate_ref[...] = z[:, 3072:3200]


def _mlstm_in_proj(h, gain, w, tm):
    T = h.shape[0]
    row = lambda i: (i, 0)
    fixed = lambda i: (0, 0)
    outs = [(512, BF16), (512, BF16), (1024, BF16), (1024, F32), (128, F32)]
    return pl.pallas_call(
        _mlstm_in_kernel,
        grid=(T // tm,),
        in_specs=[pl.BlockSpec((tm, D_MODEL), row), pl.BlockSpec((1, D_MODEL), fixed),
                  pl.BlockSpec((D_MODEL, MLSTM_IN_PAD), fixed)],
        out_specs=[pl.BlockSpec((tm, n), row) for n, _ in outs],
        out_shape=[jax.ShapeDtypeStruct((T, n), dt) for n, dt in outs],
        compiler_params=_cparams(("arbitrary",)),
        name="mlstm_in_proj",
    )(h, gain, w)


def _mlstm_kernel(q_ref, k_ref, v_ref, o_ref, gate_ref, bias_ref, gout_ref, y_ref, ct_sc, n_sc, m_sc):
    C = M_CHUNK

    @pl.when(pl.program_id(1) == 0)
    def _():
        ct_sc[...] = jnp.zeros(ct_sc.shape, F32)
        n_sc[...] = jnp.zeros(n_sc.shape, F32)
        m_sc[...] = jnp.zeros(m_sc.shape, F32)

    lane8 = lax.broadcasted_iota(jnp.int32, (8, C), 1)
    t_idx = lax.broadcasted_iota(jnp.int32, (C, C), 0)
    s_idx = lax.broadcasted_iota(jnp.int32, (C, C), 1)
    for bi in range(q_ref.shape[0]):
        pre = gate_ref[bi].T[0:8, :] + bias_ref[...]
        capped = GATE_CAP * jnp.tanh(pre / GATE_CAP)
        log_f = -(jnp.maximum(-capped, 0.0) + jnp.log1p(jnp.exp(-jnp.abs(capped))))
        b = log_f
        sh = 1
        while sh < C:
            b = b + jnp.where(lane8 >= sh, pltpu.roll(b, sh, 1), 0.0)
            sh *= 2
        stacked = jnp.concatenate([b[4:8], capped[0:4]], axis=0)
        cols = jnp.concatenate([stacked, jnp.zeros((C - 8, C), F32)], axis=0).T

        for h in range(M_HEADS):
            st = bi * M_HEADS + h
            qh = q_ref[bi, :, h * M_QK_DIM:(h + 1) * M_QK_DIM]
            kh = k_ref[bi, :, h * M_QK_DIM:(h + 1) * M_QK_DIM]
            vh = v_ref[bi, :, h * M_V_DIM:(h + 1) * M_V_DIM]
            b_row, li_row = stacked[h:h + 1, :], stacked[4 + h:5 + h, :]
            b_col, li_col = cols[:, h:h + 1], cols[:, 4 + h:5 + h]
            m_st = m_sc[st:st + 1, 0:1]
            dmat = jnp.where(s_idx <= t_idx, b_col - b_row + li_row, -jnp.inf)
            inter = b_col + m_st
            m_t = jnp.maximum(inter, jnp.max(dmat, axis=1, keepdims=True))
            w_inter = jnp.exp(inter - m_t)
            qk = lax.dot_general(qh, kh, (((1,), (1,)), ((), ())), preferred_element_type=F32)
            s = qk * jnp.exp(dmat - m_t)
            ct = ct_sc[st]
            num = (w_inter * jnp.dot(qh, ct.astype(BF16), preferred_element_type=F32)
                   + jnp.dot(s.astype(BF16), vh, preferred_element_type=F32))
            qn = jnp.sum(qh.astype(F32) * n_sc[st:st + 1, :], axis=1, keepdims=True)
            den = w_inter * qn + jnp.sum(s, axis=1, keepdims=True)
            hout = num / jnp.maximum(jnp.abs(den), jnp.exp(-m_t))

            b_last = b_row[:, C - 1:C]
            m_new = jnp.maximum(b_last + m_st, jnp.max(b_last - b_row + li_row, axis=1, keepdims=True))
            decay = jnp.exp(b_last + m_st - m_new)
            wk = jnp.exp(b_last - b_col + li_col - m_new)
            kw = kh.astype(F32) * wk
            ct_sc[st] = decay * ct + jnp.dot(kw.T.astype(BF16), vh, preferred_element_type=F32)
            n_sc[st:st + 1, :] = decay * n_sc[st:st + 1, :] + jnp.sum(kw, axis=0, keepdims=True)
            m_sc[st:st + 1, :] = jnp.broadcast_to(m_new, (1, LANES))

            cs = slice(h * M_V_DIM, (h + 1) * M_V_DIM)
            hn = _rms_rows(hout, gout_ref[:, cs])
            y_ref[bi, :, cs] = (hn * jax.nn.sigmoid(o_ref[bi, :, cs])).astype(BF16)


def _mlstm_core(q, k, v, o, gates, bias8, gout):
    B, seq_pad = q.shape[0], q.shape[1]
    nb = M_BATCH_ROWS
    C = M_CHUNK
    blk = lambda n: pl.BlockSpec((nb, C, n), lambda b, c: (b, c, 0))
    fixed = lambda b, c: (0, 0)
    return pl.pallas_call(
        _mlstm_kernel,
        grid=(B // nb, seq_pad // C),
        in_specs=[blk(512), blk(512), blk(1024), blk(1024), blk(LANES),
                  pl.BlockSpec((8, 1), fixed), pl.BlockSpec((1, D_MODEL), fixed)],
        out_specs=blk(D_MODEL),
        out_shape=jax.ShapeDtypeStruct((B, seq_pad, D_MODEL), BF16),
        scratch_shapes=[pltpu.VMEM((nb * M_HEADS, M_QK_DIM, M_V_DIM), F32),
                        pltpu.VMEM((nb * M_HEADS, M_QK_DIM), F32),
                        pltpu.VMEM((nb * M_HEADS, LANES), F32)],
        compiler_params=_cparams(("arbitrary", "arbitrary")),
        name="mlstm_core",
    )(q, k, v, o, gates, bias8, gout)


def _ffn_kernel(h_ref, y_ref, wo_ref, g_ref, wg_ref, wu_ref, wd_ref, o_ref, xn_sc, acc_sc):
    f = pl.program_id(1)

    @pl.when(f == 0)
    def _():
        x = h_ref[...] + jnp.dot(y_ref[...], wo_ref[...], preferred_element_type=F32)
        xn_sc[...] = _rms_rows(x, g_ref[...]).astype(BF16)
        acc_sc[...] = x

    xn = xn_sc[...]
    gate = jnp.dot(xn, wg_ref[...], preferred_element_type=F32)
    up = jnp.dot(xn, wu_ref[...], preferred_element_type=F32)
    act = (gate * jax.nn.sigmoid(gate) * up).astype(BF16)
    acc_sc[...] += jnp.dot(act, wd_ref[...], preferred_element_type=F32)

    @pl.when(f == pl.num_programs(1) - 1)
    def _():
        o_ref[...] = acc_sc[...]


def _ffn(h, y, wo, gain, wg, wu, wd, tm, tf):
    T = h.shape[0]
    return pl.pallas_call(
        _ffn_kernel,
        grid=(T // tm, D_FF // tf),
        in_specs=[pl.BlockSpec((tm, D_MODEL), lambda i, f: (i, 0)),
                  pl.BlockSpec((tm, D_MODEL), lambda i, f: (i, 0)),
                  pl.BlockSpec((D_MODEL, D_MODEL), lambda i, f: (0, 0)),
                  pl.BlockSpec((1, D_MODEL), lambda i, f: (0, 0)),
                  pl.BlockSpec((D_MODEL, tf), lambda i, f: (0, f)),
                  pl.BlockSpec((D_MODEL, tf), lambda i, f: (0, f)),
                  pl.BlockSpec((tf, D_MODEL), lambda i, f: (f, 0))],
        out_specs=pl.BlockSpec((tm, D_MODEL), lambda i, f: (i, 0)),
        out_shape=jax.ShapeDtypeStruct((T, D_MODEL), F32),
        scratch_shapes=[pltpu.VMEM((tm, D_MODEL), BF16), pltpu.VMEM((tm, D_MODEL), F32)],
        compiler_params=_cparams(("arbitrary", "arbitrary")),
        name="ffn_dense",
    )(h, y, wo, gain, wg, wu, wd)


def _moe_kernel(h_ref, g_ref, wr_ref, wg_ref, wu_ref, wd_ref, o_ref,
                xn_sc, xs_sc, ys_sc, posc_sc, posr_sc, comb_sc, cnt_sc):
    e = pl.program_id(1)
    f = pl.program_id(2)
    tm = h_ref.shape[0]
    SB = _moe_slot_block(tm)
    TB = _moe_token_block(tm)
    lane = lax.broadcasted_iota(jnp.int32, (tm, LANES), 1)

    @pl.when((e == 0) & (f == 0))
    def _():
        x = h_ref[...]
        xn = _rms_rows(x, g_ref[...])
        xn_sc[...] = xn.astype(BF16)
        o_ref[...] = x
        logits = jnp.dot(xn, wr_ref[...], preferred_element_type=F32, precision=lax.Precision.HIGHEST)
        logits = jnp.where(lane < N_EXPERTS, logits, -jnp.inf)
        m1 = jnp.max(logits, axis=1, keepdims=True)
        i1 = jnp.min(jnp.where(logits == m1, lane, LANES), axis=1, keepdims=True)
        rest = jnp.where(lane == i1, -jnp.inf, logits)
        m2 = jnp.max(rest, axis=1, keepdims=True)
        i2 = jnp.min(jnp.where(rest == m2, lane, LANES), axis=1, keepdims=True)
        e2 = jnp.exp(m2 - m1)
        g1 = 1.0 / (1.0 + e2)
        comb_sc[...] = jnp.where(lane == i1, g1, jnp.where(lane == i2, e2 * g1, 0.0))
        member = jnp.where(lane == i1, 1.0, jnp.where(lane == i2, 1.0, 0.0))
        tri = jnp.where(lax.broadcasted_iota(jnp.int32, (TB, TB), 0) >= lax.broadcasted_iota(jnp.int32, (TB, TB), 1),
                        1.0, 0.0).astype(BF16)
        carry = jnp.zeros((1, LANES), F32)
        for b in range(tm // TB):
            mb = member[b * TB:(b + 1) * TB]
            incl = jnp.dot(tri, mb.astype(BF16), preferred_element_type=F32)
            posc_sc[b * TB:(b + 1) * TB, :] = jnp.where(mb > 0.0, incl - 1.0 + carry, -1.0)
            carry = carry + incl[TB - 1:TB, :]
        cnt_sc[...] = carry
        for b in range(tm // LANES):
            posr_sc[:, b * LANES:(b + 1) * LANES] = posc_sc[b * LANES:(b + 1) * LANES, :].T

    lane_row = lax.broadcasted_iota(jnp.int32, (1, LANES), 1)
    count = jnp.sum(jnp.where(lane_row == e, cnt_sc[...], 0.0))
    nblk = ((count + (SB - 1.0)) * (1.0 / SB)).astype(jnp.int32)

    @pl.when(f == 0)
    def _():
        slots_of_tokens = posr_sc[pl.ds(e, 1), :]

        def body(r, carry):
            row0 = pl.multiple_of(r * SB, 16)
            slot = (r * SB + lax.broadcasted_iota(jnp.int32, (SB, tm), 0)).astype(F32)
            onehot = jnp.where(slots_of_tokens == slot, 1.0, 0.0).astype(BF16)
            xs_sc[pl.ds(row0, SB), :] = jnp.dot(onehot, xn_sc[...], preferred_element_type=F32).astype(BF16)
            ys_sc[pl.ds(row0, SB), :] = jnp.zeros((SB, D_MODEL), F32)
            return carry

        lax.fori_loop(0, nblk, body, 0)

    def ffn_rows(row0, nrows):
        xs = xs_sc[pl.ds(row0, nrows), :]
        gate = jnp.dot(xs, wg_ref[0], preferred_element_type=F32)
        up = jnp.dot(xs, wu_ref[0], preferred_element_type=F32)
        act = (gate * jax.nn.sigmoid(gate) * up).astype(BF16)
        ys_sc[pl.ds(row0, nrows), :] += jnp.dot(act, wd_ref[0], preferred_element_type=F32)

    def pair_body(r, carry):
        ffn_rows(pl.multiple_of(r * 2 * SB, 16), 2 * SB)
        return carry

    lax.fori_loop(0, nblk // 2, pair_body, 0)

    @pl.when(nblk % 2 == 1)
    def _():
        ffn_rows(pl.multiple_of((nblk - 1) * SB, 16), SB)

    @pl.when(f == pl.num_programs(2) - 1)
    def _():
        on_e = lane == e
        gate_e = jnp.sum(jnp.where(on_e, comb_sc[...], 0.0), axis=1, keepdims=True)
        slot_e = jnp.sum(jnp.where(on_e, posc_sc[...], 0.0), axis=1, keepdims=True)

        def body(r, carry):
            row0 = pl.multiple_of(r * SB, 16)
            ys = ys_sc[pl.ds(row0, SB), :].astype(BF16)
            slot = (r * SB + lax.broadcasted_iota(jnp.int32, (TB, SB), 1)).astype(F32)
            for tb in range(tm // TB):
                rows = slice(tb * TB, (tb + 1) * TB)
                onehot = jnp.where(slot_e[rows] == slot, 1.0, 0.0).astype(BF16)
                o_ref[rows, :] += gate_e[rows] * jnp.dot(onehot, ys, preferred_element_type=F32)
            return carry

        lax.fori_loop(0, nblk, body, 0)


def _moe_slot_block(tm):
    return MOE_SLOT_BLOCK if tm > MOE_SLOT_BLOCK else tm


def _moe_token_block(tm):
    return MOE_TOKEN_BLOCK if tm % MOE_TOKEN_BLOCK == 0 else tm


def _moe(h, gain, wr, wg, wu, wd, tm, tf):
    T = h.shape[0]
    sb = _moe_slot_block(tm)
    cap = -(-tm // sb) * sb
    return pl.pallas_call(
        _moe_kernel,
        grid=(T // tm, N_EXPERTS, D_FF // tf),
        in_specs=[pl.BlockSpec((tm, D_MODEL), lambda i, e, f: (i, 0), pipeline_mode=pl.Buffered(1)),
                  pl.BlockSpec((1, D_MODEL), lambda i, e, f: (0, 0)),
                  pl.BlockSpec((D_MODEL, LANES), lambda i, e, f: (0, 0)),
                  pl.BlockSpec((1, D_MODEL, tf), lambda i, e, f: (e, 0, f)),
                  pl.BlockSpec((1, D_MODEL, tf), lambda i, e, f: (e, 0, f)),
                  pl.BlockSpec((1, tf, D_MODEL), lambda i, e, f: (e, f, 0))],
        out_specs=pl.BlockSpec((tm, D_MODEL), lambda i, e, f: (i, 0)),
        out_shape=jax.ShapeDtypeStruct((T, D_MODEL), F32),
        scratch_shapes=[pltpu.VMEM((tm, D_MODEL), BF16), pltpu.VMEM((cap, D_MODEL), BF16),
                        pltpu.VMEM((cap, D_MODEL), F32), pltpu.VMEM((tm, LANES), F32),
                        pltpu.VMEM((LANES, tm), F32), pltpu.VMEM((tm, LANES), F32),
                        pltpu.VMEM((1, LANES), F32)],
        compiler_params=_cparams(("arbitrary", "arbitrary", "arbitrary")),
        name="moe_sparse",
    )(h, gain, wr, wg, wu, wd)


def _rope_tables(seq_pad):
    pos = jnp.arange(seq_pad, dtype=F32)[:, None]
    lane = np.arange(LANES)

    def table(head_dim):
        half = head_dim // 2
        inv = ROPE_THETA ** (-jnp.arange(half, dtype=F32) / half)
        d = lane % head_dim
        ang = pos * inv[d % half][None, :]
        return jnp.cos(ang), jnp.sin(ang), jnp.asarray(d < half)[None, :]

    c64, s64, lo64 = table(ATT_HEAD_DIM)
    c128, s128, lo128 = table(IDX_DIM)
    return (c64, jnp.where(lo64, -s64, 0.0), jnp.where(lo64, 0.0, s64), c128, jnp.where(lo128, -s128, s128))


def _pad_cols(w, n):
    return jnp.pad(w, ((0, 0), (0, n - w.shape[1])))


def kernel(x, meta, norm_mixer, norm_ffn, dsa_w_in, dsa_q_norm, dsa_k_norm, dsa_w_out, mlstm_w_in, mlstm_b_i,
           mlstm_b_f, mlstm_out_norm, mlstm_w_out, ffn_w_gate, ffn_w_up, ffn_w_down, moe_router, moe_w_gate,
           moe_w_up, moe_w_down):
    B, S, D = x.shape
    L = S + N_META
    top_k = min(TOPK_MAX, S // 4)
    seq_pad = max(-(-L // Q_BLOCK) * Q_BLOCK, KEY_CHUNK)
    T = B * seq_pad
    tm_proj = 384 if seq_pad % 384 == 0 else Q_BLOCK
    tm_ffn = 768 if T % 768 == 0 else Q_BLOCK
    tm_moe = 1536 if T % 1536 == 0 else Q_BLOCK
    tf = 512

    h = jnp.concatenate([jnp.broadcast_to(meta[None].astype(x.dtype), (B, N_META, D)), x,
                         jnp.zeros((B, seq_pad - L, D), x.dtype)], axis=1).reshape(T, D)
    tabs = _rope_tables(seq_pad)
    depth = norm_mixer.shape[0]

    for i in range(depth):
        j = i // 2
        gain_m = norm_mixer[i][None, :]
        gain_f = norm_ffn[i][None, :]
        if i % 2 == 0:
            w_in = _pad_cols(dsa_w_in[j], DSA_IN_PAD).astype(BF16)
            gq = jnp.tile(dsa_q_norm[j], 2)[None, :]
            gk = jnp.tile(dsa_k_norm[j], 2)[None, :]
            proj = _dsa_in_proj(h, gain_m, w_in, gq, gk, tabs, seq_pad, tm_proj)
            att = _dsa_core(*[a.reshape(B, seq_pad, a.shape[-1]) for a in proj], top_k)
            h = _ffn(h, att.reshape(T, D), dsa_w_out[j].astype(BF16), gain_f, ffn_w_gate[j].astype(BF16),
                     ffn_w_up[j].astype(BF16), ffn_w_down[j].astype(BF16), tm_ffn, tf)
        else:
            w_in = _pad_cols(mlstm_w_in[j], MLSTM_IN_PAD).astype(BF16)
            q, k, v, o, gates = _mlstm_in_proj(h, gain_m, w_in, tm_proj)
            bias8 = jnp.concatenate([mlstm_b_i[j], mlstm_b_f[j]])[:, None]
            y = _mlstm_core(*[a.reshape(B, seq_pad, a.shape[-1]) for a in (q, k, v, o, gates)], bias8,
                            mlstm_out_norm[j][None, :])
            h = _out_proj(h, y.reshape(T, D), mlstm_w_out[j].astype(BF16), tm_proj)
            wr = _pad_cols(moe_router[j], LANES)
            h = _moe(h, gain_f, wr, moe_w_gate[j].astype(BF16), moe_w_up[j].astype(BF16),
                     moe_w_down[j].astype(BF16), tm_moe, tf)
    return h.reshape(B, seq_pad, D)[:, N_META:L]
```

```python
import functools

import jax
import jax.numpy as jnp
import numpy as np
from jax import lax
from jax.experimental import pallas as pl
from jax.experimental.pallas import tpu as pltpu

F32 = jnp.float32
BF16 = jnp.bfloat16

D_MODEL = 1024
N_META = 16
RMS_EPS = 1e-6
ROPE_THETA = 10000.0
ATT_HEADS = 16
ATT_KV_HEADS = 4
ATT_HEAD_DIM = 64
ATT_GROUP = 4
IDX_HEADS = 8
IDX_DIM = 128
TOPK_MAX = 256
M_HEADS = 4
M_QK_DIM = 128
M_V_DIM = 256
GATE_CAP = 15.0
D_FF = 3584
N_EXPERTS = 8

LANES = 128
SUBLANES = 8
Q_BLOCK = 128
KEY_CHUNK = 256
ATT_STEPS_PER_TRIP = 4
M_CHUNK = 128
M_BATCH_ROWS = 1
MOE_SLOT_BLOCK = 208
MOE_TOKEN_BLOCK = 256
VMEM_LIMIT = 52 * 1024 * 1024
INT_MIN = -(2 ** 31)
INT_MAX = 2 ** 31 - 1
LOG2E = 1.4426950408889634

DSA_IN_PAD = 2816
MLSTM_IN_PAD = 3200


def _cparams(sem):
    return pltpu.CompilerParams(dimension_semantics=sem, vmem_limit_bytes=VMEM_LIMIT)


def _rms_rows(x, gain):
    ms = jnp.mean(x * x, axis=-1, keepdims=True)
    return x * lax.rsqrt(ms + RMS_EPS) * gain


def _dsa_in_kernel(h_ref, g_ref, w_ref, gq_ref, gk_ref, c64_ref, sa64_ref, sb64_ref, c128_ref, s128_ref,
                   smat_ref, q_ref, k_ref, v_ref, iq_ref, ik_ref, iw_ref):
    xn = _rms_rows(h_ref[...], g_ref[...]).astype(BF16)
    z = jnp.dot(xn, w_ref[...], preferred_element_type=F32)
    c64, sa64, sb64 = c64_ref[...], sa64_ref[...], sb64_ref[...]
    c128, s128 = c128_ref[...], s128_ref[...]
    smat = smat_ref[...]

    def head_norm_rope(zc, gain):
        z2 = zc * zc
        hi = z2.astype(BF16)
        lo = (z2 - hi.astype(F32)).astype(BF16)
        ms = jnp.dot(hi, smat, preferred_element_type=F32) + jnp.dot(lo, smat, preferred_element_type=F32)
        y = zc * lax.rsqrt(ms + RMS_EPS) * gain
        return y * c64 + pltpu.roll(y, 96, 1) * sa64 + pltpu.roll(y, 32, 1) * sb64

    nsub = z.shape[0] // Q_BLOCK
    lane = lax.broadcasted_iota(jnp.int32, (z.shape[0], LANES), 1)
    for c in range(8):
        r = head_norm_rope(z[:, c * LANES:(c + 1) * LANES], gq_ref[...]) * (ATT_HEAD_DIM ** -0.5 * LOG2E)
        swapped = pltpu.roll(r, 64, 1)
        for half in range(2):
            h = 2 * c + half
            g, rr = h // ATT_GROUP, h % ATT_GROUP
            keep = (lane >= 64) if g % 2 else (lane < 64)
            val = jnp.where(keep, r if half == g % 2 else swapped, 0.0).astype(BF16)
            for t in range(nsub):
                q_ref[t, g, rr * Q_BLOCK:(rr + 1) * Q_BLOCK, :] = val[t * Q_BLOCK:(t + 1) * Q_BLOCK]
    for c in range(2):
        r = head_norm_rope(z[:, 1024 + c * LANES:1024 + (c + 1) * LANES], gk_ref[...])
        k_ref[:, c * LANES:(c + 1) * LANES] = r.astype(BF16)
    ones = jnp.ones((z.shape[0], LANES), BF16)
    for c in range(2):
        v_ref[:, 2 * c * LANES:(2 * c + 1) * LANES] = z[:, 1280 + c * LANES:1280 + (c + 1) * LANES].astype(BF16)
        v_ref[:, (2 * c + 1) * LANES:(2 * c + 2) * LANES] = ones
    for c in range(9):
        zc = z[:, 1536 + c * LANES:1536 + (c + 1) * LANES]
        r = zc * c128 + pltpu.roll(zc, 64, 1) * s128
        if c < 8:
            for t in range(nsub):
                iq_ref[t, :, c * Q_BLOCK:(c + 1) * Q_BLOCK] = r[t * Q_BLOCK:(t + 1) * Q_BLOCK].T.astype(BF16)
        else:
            ik_ref[...] = r.astype(BF16)
    iw = z[:, 2688:2816] * (IDX_HEADS ** -0.5 * IDX_DIM ** -0.5)
    for t in range(nsub):
        iw_ref[t] = iw[t * Q_BLOCK:(t + 1) * Q_BLOCK].T


def _dsa_in_proj(h, gain, w, gq, gk, tabs, seq_pad, tm):
    T = h.shape[0]
    nt = seq_pad // tm
    row = lambda i: (i, 0)
    fixed = lambda i: (0, 0)
    pos = lambda i: (i % nt, 0)
    tab_spec = pl.BlockSpec((tm, LANES), pos)
    smat = jnp.asarray(np.kron(np.eye(2), np.full((64, 64), 1.0 / 64)), BF16)
    nsub = tm // Q_BLOCK
    rows = ATT_GROUP * Q_BLOCK
    outs = [((ATT_KV_HEADS, rows, LANES), BF16), (256, BF16), (512, BF16),
            ((IDX_DIM, IDX_HEADS * Q_BLOCK), BF16), (128, BF16), ((LANES, Q_BLOCK), F32)]

    def spec(o):
        if isinstance(o[0], int):
            return pl.BlockSpec((tm, o[0]), row)
        return pl.BlockSpec((nsub,) + o[0], lambda i: (i,) + (0,) * len(o[0]))

    def shape(o):
        if isinstance(o[0], int):
            return jax.ShapeDtypeStruct((T, o[0]), o[1])
        return jax.ShapeDtypeStruct((T // Q_BLOCK,) + o[0], o[1])

    return pl.pallas_call(
        _dsa_in_kernel,
        grid=(T // tm,),
        in_specs=[pl.BlockSpec((tm, D_MODEL), row), pl.BlockSpec((1, D_MODEL), fixed),
                  pl.BlockSpec((D_MODEL, DSA_IN_PAD), fixed), pl.BlockSpec((1, LANES), fixed),
                  pl.BlockSpec((1, LANES), fixed), tab_spec, tab_spec, tab_spec, tab_spec, tab_spec,
                  pl.BlockSpec((LANES, LANES), fixed)],
        out_specs=[spec(o) for o in outs],
        out_shape=[shape(o) for o in outs],
        compiler_params=_cparams(("arbitrary",)),
        name="dsa_in_proj",
    )(h, gain, w, gq, gk, *tabs, smat)


def _dsa_core_kernel(q_ref, k_ref, v_ref, iq_ref, ik_ref, iw_ref, o_ref,
                     keys_sc, bias_sc, cut_sc, m_sc, l_sc, acc_sc, *, top_k, seq_pad):
    i = pl.program_id(1)
    QB, KC = Q_BLOCK, KEY_CHUNK
    nch = (i * QB + QB + KC - 1) // KC
    kf = float(top_k)
    lane1 = lax.broadcasted_iota(jnp.int32, (QB, LANES), 1)

    qpos = i * QB + lax.broadcasted_iota(jnp.int32, (KC, QB), 1)
    krow = lax.broadcasted_iota(jnp.int32, (KC, QB), 0)

    def chunk_start(j):
        return pl.multiple_of(jnp.minimum(j * KC, seq_pad - KC), LANES)

    def key_valid(j, start):
        kidx = start + krow
        return (kidx <= qpos) & (kidx >= j * KC)

    def for_chunks(fn):
        def body(j2, carry):
            fn(2 * j2)
            fn(2 * j2 + 1)
            return carry

        lax.fori_loop(0, nch // 2, body, 0)

        @pl.when(nch % 2 == 1)
        def _():
            fn(nch - 1)

    def score_chunk(j):
        start = chunk_start(j)
        s = jnp.dot(ik_ref[0, pl.ds(start, KC), :], iq_ref[0], preferred_element_type=F32)
        acc = jnp.zeros((KC, QB), F32)
        for h in range(IDX_HEADS):
            acc = acc + jnp.maximum(s[:, h * QB:(h + 1) * QB], 0.0) * iw_ref[0, h:h + 1, :]
        bits = pltpu.bitcast(acc, jnp.int32)
        key = jnp.where(bits >= 0, bits, bits ^ jnp.int32(INT_MAX))
        keys_sc[j] = jnp.where(key_valid(j, start), key, jnp.int32(INT_MIN))

    for_chunks(score_chunk)

    tiles = (KC // SUBLANES, SUBLANES, QB)
    krow3 = (lax.broadcasted_iota(jnp.int32, tiles, 0) * SUBLANES + lax.broadcasted_iota(jnp.int32, tiles, 1))

    keys_sc[nch] = jnp.full((KC, QB), INT_MIN, jnp.int32)

    def count(pred):
        def body(j2, acc):
            for j in (2 * j2, 2 * j2 + 1):
                kk = keys_sc[j].reshape(tiles)
                hit = jnp.where(pred(kk, chunk_start(j)), jnp.float32(1.0), jnp.float32(0.0))
                while hit.shape[0] > 1:
                    half = hit.shape[0] // 2
                    hit = hit[:half] + hit[half:]
                acc = acc + hit[0]
            return acc
        acc = lax.fori_loop(0, (nch + 1) // 2, body, jnp.zeros((SUBLANES, QB), F32))
        for shift in (4, 2, 1):
            acc = acc + pltpu.roll(acc, shift, 0)
        return acc

    zero = jnp.zeros((SUBLANES, QB), jnp.int32)
    v0 = jnp.where(count(lambda kk, base: kk >= zero) >= kf, zero, jnp.int32(INT_MIN))

    def bit_body(b, v):
        cand = v | lax.shift_left(jnp.int32(1), (30 - b).astype(jnp.int32))
        return jnp.where(count(lambda kk, base: kk >= cand) >= kf, cand, v)

    thr = lax.fori_loop(jnp.int32(0), jnp.int32(31), bit_body, v0)
    cut_sc[...] = jnp.full((SUBLANES, QB), INT_MAX, jnp.int32)
    n_ge = count(lambda kk, base: kk >= thr)

    @pl.when(jnp.max(n_ge) > kf)
    def _():
        need = kf - count(lambda kk, base: kk > thr)

        def cut_body(b, c):
            cand = c | lax.shift_left(jnp.int32(1), (12 - b).astype(jnp.int32))
            n = count(lambda kk, base: jnp.where(kk == thr, base + krow3, jnp.int32(INT_MAX)) < cand)
            return jnp.where(n < need, cand, c)

        cut_sc[...] = lax.fori_loop(jnp.int32(0), jnp.int32(13), cut_body, zero)

    thr1 = thr[0:1, :]
    cut1 = cut_sc[0:1, :]

    def bias_chunk(j):
        kk = keys_sc[j]
        start = chunk_start(j)
        sel = (kk > thr1) | ((kk == thr1) & (start + krow <= cut1))
        bt = jnp.where(sel & key_valid(j, start), 0.0, -1e30)
        bias_sc[j] = jnp.concatenate([bt[c * QB:(c + 1) * QB].T for c in range(KC // QB)], axis=1)

    for_chunks(bias_chunk)

    m_sc[...] = jnp.full(m_sc.shape, -1e9, F32)
    l_sc[...] = jnp.zeros(l_sc.shape, F32)
    acc_sc[...] = jnp.zeros(acc_sc.shape, F32)

    def att_step(start, width, bias, groups):
        bias = jnp.concatenate([bias] * ATT_GROUP, axis=0)
        for g in groups:
            cg = g // 2
            kj = k_ref[0, pl.ds(start, width), cg * LANES:(cg + 1) * LANES]
            vj = v_ref[0, pl.ds(start, width), cg * 2 * LANES:(cg + 1) * 2 * LANES]
            s = lax.dot_general(q_ref[0, g], kj, (((1,), (1,)), ((), ())), preferred_element_type=F32) + bias
            m_old = m_sc[g]
            m_new = jnp.maximum(m_old, jnp.max(s, axis=1, keepdims=True))
            alpha = jnp.exp2(m_old - m_new)
            p = jnp.exp2(s - jnp.concatenate([m_new] * (width // LANES), axis=1)).astype(BF16)
            pv = jnp.dot(p, vj, preferred_element_type=F32)
            l_sc[g] = alpha * l_sc[g] + pv[:, LANES:]
            acc_sc[g] = alpha * acc_sc[g] + pv[:, :LANES]
            m_sc[g] = m_new

    npair = jnp.minimum(nch // 2, seq_pad // (2 * KC))

    def single_body(j, carry):
        att_step(chunk_start(j), KC, bias_sc[j], range(ATT_KV_HEADS))
        return carry

    def pair_step(j):
        bias = jnp.concatenate([bias_sc[2 * j], bias_sc[2 * j + 1]], axis=1)
        att_step(pl.multiple_of(j * 2 * KC, 2 * KC), 2 * KC, bias, range(ATT_KV_HEADS))

    def multi_body(j4, carry):
        for t in range(ATT_STEPS_PER_TRIP):
            pair_step(ATT_STEPS_PER_TRIP * j4 + t)
        return carry

    lax.fori_loop(0, npair // ATT_STEPS_PER_TRIP, multi_body, 0)
    rest = npair % ATT_STEPS_PER_TRIP
    done = npair - rest
    n = ATT_STEPS_PER_TRIP // 2
    while n >= 1:
        @pl.when((rest & n) != 0)
        def _(n=n, done=done):
            for t in range(n):
                pair_step(done + t)

        done = done + (rest & n)
        n //= 2

    lax.fori_loop(2 * npair, nch, single_body, 0)

    for g in range(ATT_KV_HEADS):
        og = acc_sc[g] / l_sc[g]
        for pair in range(2):
            even = og[(2 * pair) * QB:(2 * pair + 1) * QB]
            odd = og[(2 * pair + 1) * QB:(2 * pair + 2) * QB]
            if g % 2:
                even = pltpu.roll(even, 64, 1)
            else:
                odd = pltpu.roll(odd, 64, 1)
            c = 2 * g + pair
            o_ref[0, :, c * LANES:(c + 1) * LANES] = jnp.where(lane1 < 64, even, odd).astype(BF16)


def _dsa_core(q, k, v, iq, ik, iw, top_k):
    B, seq_pad = k.shape[0], k.shape[1]
    nqb = seq_pad // Q_BLOCK
    nkc = -(-seq_pad // KEY_CHUNK)
    rows = ATT_GROUP * Q_BLOCK
    qblock = lambda a: pl.BlockSpec((1,) + a.shape[1:], lambda b, i: (b * nqb + i,) + (0,) * (a.ndim - 1))
    whole = lambda n: pl.BlockSpec((1, seq_pad, n), lambda b, i: (b, 0, 0))
    return pl.pallas_call(
        functools.partial(_dsa_core_kernel, top_k=top_k, seq_pad=seq_pad),
        grid=(B, nqb),
        in_specs=[qblock(q), whole(256), whole(512), qblock(iq), whole(IDX_DIM), qblock(iw)],
        out_specs=pl.BlockSpec((1, Q_BLOCK, D_MODEL), lambda b, i: (b, i, 0)),
        out_shape=jax.ShapeDtypeStruct((B, seq_pad, D_MODEL), BF16),
        scratch_shapes=[
            pltpu.VMEM((nkc + 1, KEY_CHUNK, Q_BLOCK), jnp.int32),
            pltpu.VMEM((nkc, Q_BLOCK, KEY_CHUNK), F32),
            pltpu.VMEM((SUBLANES, Q_BLOCK), jnp.int32),
            pltpu.VMEM((ATT_KV_HEADS, rows, LANES), F32),
            pltpu.VMEM((ATT_KV_HEADS, rows, LANES), F32),
            pltpu.VMEM((ATT_KV_HEADS, rows, LANES), F32),
        ],
        compiler_params=_cparams(("arbitrary", "arbitrary")),
        name="dsa_core",
    )(q, k, v, iq, ik, iw)


def _out_proj_kernel(h_ref, y_ref, w_ref, o_ref):
    o_ref[...] = h_ref[...] + jnp.dot(y_ref[...], w_ref[...], preferred_element_type=F32)


def _out_proj(h, y, w, tm):
    T = h.shape[0]
    row = lambda i: (i, 0)
    return pl.pallas_call(
        _out_proj_kernel,
        grid=(T // tm,),
        in_specs=[pl.BlockSpec((tm, D_MODEL), row), pl.BlockSpec((tm, D_MODEL), row),
                  pl.BlockSpec((D_MODEL, D_MODEL), lambda i: (0, 0))],
        out_specs=pl.BlockSpec((tm, D_MODEL), row),
        out_shape=jax.ShapeDtypeStruct((T, D_MODEL), F32),
        compiler_params=_cparams(("arbitrary",)),
        name="out_proj",
    )(h, y, w)


def _mlstm_in_kernel(h_ref, g_ref, w_ref, q_ref, k_ref, v_ref, o_ref, gate_ref):
    xn = _rms_rows(h_ref[...], g_ref[...]).astype(BF16)
    z = jnp.dot(xn, w_ref[...], preferred_element_type=F32)
    q_ref[...] = z[:, 0:512].astype(BF16)
    k_ref[...] = (z[:, 512:1024] * (M_QK_DIM ** -0.5)).astype(BF16)
    v_ref[...] = z[:, 1024:2048].astype(BF16)
    o_ref[...] = z[:, 2048:3072]
    gate_ref[...] = z[:, 3072:3200]


def _mlstm_in_proj(h, gain, w, tm):
    T = h.shape[0]
    row = lambda i: (i, 0)
    fixed = lambda i: (0, 0)
    outs = [(512, BF16), (512, BF16), (1024, BF16), (1024, F32), (128, F32)]
    return pl.pallas_call(
        _mlstm_in_kernel,
        grid=(T // tm,),
        in_specs=[pl.BlockSpec((tm, D_MODEL), row), pl.BlockSpec((1, D_MODEL), fixed),
                  pl.BlockSpec((D_MODEL, MLSTM_IN_PAD), fixed)],
        out_specs=[pl.BlockSpec((tm, n), row) for n, _ in outs],
        out_shape=[jax.ShapeDtypeStruct((T, n), dt) for n, dt in outs],
        compiler_params=_cparams(("arbitrary",)),
        name="mlstm_in_proj",
    )(h, gain, w)


def _mlstm_kernel(q_ref, k_ref, v_ref, o_ref, gate_ref, bias_ref, gout_ref, y_ref, ct_sc, n_sc, m_sc):
    C = M_CHUNK

    @pl.when(pl.program_id(1) == 0)
    def _():
        ct_sc[...] = jnp.zeros(ct_sc.shape, F32)
        n_sc[...] = jnp.zeros(n_sc.shape, F32)
        m_sc[...] = jnp.zeros(m_sc.shape, F32)

    lane8 = lax.broadcasted_iota(jnp.int32, (8, C), 1)
    t_idx = lax.broadcasted_iota(jnp.int32, (C, C), 0)
    s_idx = lax.broadcasted_iota(jnp.int32, (C, C), 1)
    for bi in range(q_ref.shape[0]):
        pre = gate_ref[bi].T[0:8, :] + bias_ref[...]
        capped = GATE_CAP * jnp.tanh(pre / GATE_CAP)
        log_f = -(jnp.maximum(-capped, 0.0) + jnp.log1p(jnp.exp(-jnp.abs(capped))))
        b = log_f
        sh = 1
        while sh < C:
            b = b + jnp.where(lane8 >= sh, pltpu.roll(b, sh, 1), 0.0)
            sh *= 2
        stacked = jnp.concatenate([b[4:8], capped[0:4]], axis=0)
        cols = jnp.concatenate([stacked, jnp.zeros((C - 8, C), F32)], axis=0).T

        for h in range(M_HEADS):
            st = bi * M_HEADS + h
            qh = q_ref[bi, :, h * M_QK_DIM:(h + 1) * M_QK_DIM]
            kh = k_ref[bi, :, h * M_QK_DIM:(h + 1) * M_QK_DIM]
            vh = v_ref[bi, :, h * M_V_DIM:(h + 1) * M_V_DIM]
            b_row, li_row = stacked[h:h + 1, :], stacked[4 + h:5 + h, :]
            b_col, li_col = cols[:, h:h + 1], cols[:, 4 + h:5 + h]
            m_st = m_sc[st:st + 1, 0:1]
            dmat = jnp.where(s_idx <= t_idx, b_col - b_row + li_row, -jnp.inf)
            inter = b_col + m_st
            m_t = jnp.maximum(inter, jnp.max(dmat, axis=1, keepdims=True))
            w_inter = jnp.exp(inter - m_t)
            qk = lax.dot_general(qh, kh, (((1,), (1,)), ((), ())), preferred_element_type=F32)
            s = qk * jnp.exp(dmat - m_t)
            ct = ct_sc[st]
            num = (w_inter * jnp.dot(qh, ct.astype(BF16), preferred_element_type=F32)
                   + jnp.dot(s.astype(BF16), vh, preferred_element_type=F32))
            qn = jnp.sum(qh.astype(F32) * n_sc[st:st + 1, :], axis=1, keepdims=True)
            den = w_inter * qn + jnp.sum(s, axis=1, keepdims=True)
            hout = num / jnp.maximum(jnp.abs(den), jnp.exp(-m_t))

            b_last = b_row[:, C - 1:C]
            m_new = jnp.maximum(b_last + m_st, jnp.max(b_last - b_row + li_row, axis=1, keepdims=True))
            decay = jnp.exp(b_last + m_st - m_new)
            wk = jnp.exp(b_last - b_col + li_col - m_new)
            kw = kh.astype(F32) * wk
            ct_sc[st] = decay * ct + jnp.dot(kw.T.astype(BF16), vh, preferred_element_type=F32)
            n_sc[st:st + 1, :] = decay * n_sc[st:st + 1, :] + jnp.sum(kw, axis=0, keepdims=True)
            m_sc[st:st + 1, :] = jnp.broadcast_to(m_new, (1, LANES))

            cs = slice(h * M_V_DIM, (h + 1) * M_V_DIM)
            hn = _rms_rows(hout, gout_ref[:, cs])
            y_ref[bi, :, cs] = (hn * jax.nn.sigmoid(o_ref[bi, :, cs])).astype(BF16)


def _mlstm_core(q, k, v, o, gates, bias8, gout):
    B, seq_pad = q.shape[0], q.shape[1]
    nb = M_BATCH_ROWS
    C = M_CHUNK
    blk = lambda n: pl.BlockSpec((nb, C, n), lambda b, c: (b, c, 0))
    fixed = lambda b, c: (0, 0)
    return pl.pallas_call(
        _mlstm_kernel,
        grid=(B // nb, seq_pad // C),
        in_specs=[blk(512), blk(512), blk(1024), blk(1024), blk(LANES),
                  pl.BlockSpec((8, 1), fixed), pl.BlockSpec((1, D_MODEL), fixed)],
        out_specs=blk(D_MODEL),
        out_shape=jax.ShapeDtypeStruct((B, seq_pad, D_MODEL), BF16),
        scratch_shapes=[pltpu.VMEM((nb * M_HEADS, M_QK_DIM, M_V_DIM), F32),
                        pltpu.VMEM((nb * M_HEADS, M_QK_DIM), F32),
                        pltpu.VMEM((nb * M_HEADS, LANES), F32)],
        compiler_params=_cparams(("arbitrary", "arbitrary")),
        name="mlstm_core",
    )(q, k, v, o, gates, bias8, gout)


def _ffn_kernel(h_ref, y_ref, wo_ref, g_ref, wg_ref, wu_ref, wd_ref, o_ref, xn_sc, acc_sc):
    f = pl.program_id(1)

    @pl.when(f == 0)
    def _():
        x = h_ref[...] + jnp.dot(y_ref[...], wo_ref[...], preferred_element_type=F32)
        xn_sc[...] = _rms_rows(x, g_ref[...]).astype(BF16)
        acc_sc[...] = x

    xn = xn_sc[...]
    gate = jnp.dot(xn, wg_ref[...], preferred_element_type=F32)
    up = jnp.dot(xn, wu_ref[...], preferred_element_type=F32)
    act = (gate * jax.nn.sigmoid(gate) * up).astype(BF16)
    acc_sc[...] += jnp.dot(act, wd_ref[...], preferred_element_type=F32)

    @pl.when(f == pl.num_programs(1) - 1)
    def _():
        o_ref[...] = acc_sc[...]


def _ffn(h, y, wo, gain, wg, wu, wd, tm, tf):
    T = h.shape[0]
    return pl.pallas_call(
        _ffn_kernel,
        grid=(T // tm, D_FF // tf),
        in_specs=[pl.BlockSpec((tm, D_MODEL), lambda i, f: (i, 0)),
                  pl.BlockSpec((tm, D_MODEL), lambda i, f: (i, 0)),
                  pl.BlockSpec((D_MODEL, D_MODEL), lambda i, f: (0, 0)),
                  pl.BlockSpec((1, D_MODEL), lambda i, f: (0, 0)),
                  pl.BlockSpec((D_MODEL, tf), lambda i, f: (0, f)),
                  pl.BlockSpec((D_MODEL, tf), lambda i, f: (0, f)),
                  pl.BlockSpec((tf, D_MODEL), lambda i, f: (f, 0))],
        out_specs=pl.BlockSpec((tm, D_MODEL), lambda i, f: (i, 0)),
        out_shape=jax.ShapeDtypeStruct((T, D_MODEL), F32),
        scratch_shapes=[pltpu.VMEM((tm, D_MODEL), BF16), pltpu.VMEM((tm, D_MODEL), F32)],
        compiler_params=_cparams(("arbitrary", "arbitrary")),
        name="ffn_dense",
    )(h, y, wo, gain, wg, wu, wd)


def _moe_kernel(h_ref, g_ref, wr_ref, wg_ref, wu_ref, wd_ref, o_ref,
                xn_sc, xs_sc, ys_sc, posc_sc, posr_sc, comb_sc, cnt_sc):
    e = pl.program_id(1)
    f = pl.program_id(2)
    tm = h_ref.shape[0]
    SB = _moe_slot_block(tm)
    TB = _moe_token_block(tm)
    lane = lax.broadcasted_iota(jnp.int32, (tm, LANES), 1)

    @pl.when((e == 0) & (f == 0))
    def _():
        x = h_ref[...]
        xn = _rms_rows(x, g_ref[...])
        xn_sc[...] = xn.astype(BF16)
        o_ref[...] = x
        logits = jnp.dot(xn, wr_ref[...], preferred_element_type=F32, precision=lax.Precision.HIGHEST)
        logits = jnp.where(lane < N_EXPERTS, logits, -jnp.inf)
        m1 = jnp.max(logits, axis=1, keepdims=True)
        i1 = jnp.min(jnp.where(logits == m1, lane, LANES), axis=1, keepdims=True)
        rest = jnp.where(lane == i1, -jnp.inf, logits)
        m2 = jnp.max(rest, axis=1, keepdims=True)
        i2 = jnp.min(jnp.where(rest == m2, lane, LANES), axis=1, keepdims=True)
        e2 = jnp.exp(m2 - m1)
        g1 = 1.0 / (1.0 + e2)
        comb_sc[...] = jnp.where(lane == i1, g1, jnp.where(lane == i2, e2 * g1, 0.0))
        member = jnp.where(lane == i1, 1.0, jnp.where(lane == i2, 1.0, 0.0))
        tri = jnp.where(lax.broadcasted_iota(jnp.int32, (TB, TB), 0) >= lax.broadcasted_iota(jnp.int32, (TB, TB), 1),
                        1.0, 0.0).astype(BF16)
        carry = jnp.zeros((1, LANES), F32)
        for b in range(tm // TB):
            mb = member[b * TB:(b + 1) * TB]
            incl = jnp.dot(tri, mb.astype(BF16), preferred_element_type=F32)
            posc_sc[b * TB:(b + 1) * TB, :] = jnp.where(mb > 0.0, incl - 1.0 + carry, -1.0)
            carry = carry + incl[TB - 1:TB, :]
        cnt_sc[...] = carry
        for b in range(tm // LANES):
            posr_sc[:, b * LANES:(b + 1) * LANES] = posc_sc[b * LANES:(b + 1) * LANES, :].T

    lane_row = lax.broadcasted_iota(jnp.int32, (1, LANES), 1)
    count = jnp.sum(jnp.where(lane_row == e, cnt_sc[...], 0.0))
    nblk = ((count + (SB - 1.0)) * (1.0 / SB)).astype(jnp.int32)

    @pl.when(f == 0)
    def _():
        slots_of_tokens = posr_sc[pl.ds(e, 1), :]

        def body(r, carry):
            row0 = pl.multiple_of(r * SB, 16)
            slot = (r * SB + lax.broadcasted_iota(jnp.int32, (SB, tm), 0)).astype(F32)
            onehot = jnp.where(slots_of_tokens == slot, 1.0, 0.0).astype(BF16)
            xs_sc[pl.ds(row0, SB), :] = jnp.dot(onehot, xn_sc[...], preferred_element_type=F32).astype(BF16)
            ys_sc[pl.ds(row0, SB), :] = jnp.zeros((SB, D_MODEL), F32)
            return carry

        lax.fori_loop(0, nblk, body, 0)

    def ffn_rows(row0, nrows):
        xs = xs_sc[pl.ds(row0, nrows), :]
        gate = jnp.dot(xs, wg_ref[0], preferred_element_type=F32)
        up = jnp.dot(xs, wu_ref[0], preferred_element_type=F32)
        act = (gate * jax.nn.sigmoid(gate) * up).astype(BF16)
        ys_sc[pl.ds(row0, nrows), :] += jnp.dot(act, wd_ref[0], preferred_element_type=F32)

    def pair_body(r, carry):
        ffn_rows(pl.multiple_of(r * 2 * SB, 16), 2 * SB)
        return carry

    lax.fori_loop(0, nblk // 2, pair_body, 0)

    @pl.when(nblk % 2 == 1)
    def _():
        ffn_rows(pl.multiple_of((nblk - 1) * SB, 16), SB)

    @pl.when(f == pl.num_programs(2) - 1)
    def _():
        on_e = lane == e
        gate_e = jnp.sum(jnp.where(on_e, comb_sc[...], 0.0), axis=1, keepdims=True)
        slot_e = jnp.sum(jnp.where(on_e, posc_sc[...], 0.0), axis=1, keepdims=True)

        def body(r, carry):
            row0 = pl.multiple_of(r * SB, 16)
            ys = ys_sc[pl.ds(row0, SB), :].astype(BF16)
            slot = (r * SB + lax.broadcasted_iota(jnp.int32, (TB, SB), 1)).astype(F32)
            for tb in range(tm // TB):
                rows = slice(tb * TB, (tb + 1) * TB)
                onehot = jnp.where(slot_e[rows] == slot, 1.0, 0.0).astype(BF16)
                o_ref[rows, :] += gate_e[rows] * jnp.dot(onehot, ys, preferred_element_type=F32)
            return carry

        lax.fori_loop(0, nblk, body, 0)


def _moe_slot_block(tm):
    return MOE_SLOT_BLOCK if tm > MOE_SLOT_BLOCK else tm


def _moe_token_block(tm):
    return MOE_TOKEN_BLOCK if tm % MOE_TOKEN_BLOCK == 0 else tm


def _moe(h, gain, wr, wg, wu, wd, tm, tf):
    T = h.shape[0]
    sb = _moe_slot_block(tm)
    cap = -(-tm // sb) * sb
    return pl.pallas_call(
        _moe_kernel,
        grid=(T // tm, N_EXPERTS, D_FF // tf),
        in_specs=[pl.BlockSpec((tm, D_MODEL), lambda i, e, f: (i, 0), pipeline_mode=pl.Buffered(1)),
                  pl.BlockSpec((1, D_MODEL), lambda i, e, f: (0, 0)),
                  pl.BlockSpec((D_MODEL, LANES), lambda i, e, f: (0, 0)),
                  pl.BlockSpec((1, D_MODEL, tf), lambda i, e, f: (e, 0, f)),
                  pl.BlockSpec((1, D_MODEL, tf), lambda i, e, f: (e, 0, f)),
                  pl.BlockSpec((1, tf, D_MODEL), lambda i, e, f: (e, f, 0))],
        out_specs=pl.BlockSpec((tm, D_MODEL), lambda i, e, f: (i, 0)),
        out_shape=jax.ShapeDtypeStruct((T, D_MODEL), F32),
        scratch_shapes=[pltpu.VMEM((tm, D_MODEL), BF16), pltpu.VMEM((cap, D_MODEL), BF16),
                        pltpu.VMEM((cap, D_MODEL), F32), pltpu.VMEM((tm, LANES), F32),
                        pltpu.VMEM((LANES, tm), F32), pltpu.VMEM((tm, LANES), F32),
                        pltpu.VMEM((1, LANES), F32)],
        compiler_params=_cparams(("arbitrary", "arbitrary", "arbitrary")),
        name="moe_sparse",
    )(h, gain, wr, wg, wu, wd)


def _rope_tables(seq_pad):
    pos = jnp.arange(seq_pad, dtype=F32)[:, None]
    lane = np.arange(LANES)

    def table(head_dim):
        half = head_dim // 2
        inv = ROPE_THETA ** (-jnp.arange(half, dtype=F32) / half)
        d = lane % head_dim
        ang = pos * inv[d % half][None, :]
        return jnp.cos(ang), jnp.sin(ang), jnp.asarray(d < half)[None, :]

    c64, s64, lo64 = table(ATT_HEAD_DIM)
    c128, s128, lo128 = table(IDX_DIM)
    return (c64, jnp.where(lo64, -s64, 0.0), jnp.where(lo64, 0.0, s64), c128, jnp.where(lo128, -s128, s128))


def _pad_cols(w, n):
    return jnp.pad(w, ((0, 0), (0, n - w.shape[1])))


def kernel(x, meta, norm_mixer, norm_ffn, dsa_w_in, dsa_q_norm, dsa_k_norm, dsa_w_out, mlstm_w_in, mlstm_b_i,
           mlstm_b_f, mlstm_out_norm, mlstm_w_out, ffn_w_gate, ffn_w_up, ffn_w_down, moe_router, moe_w_gate,
           moe_w_up, moe_w_down):
    B, S, D = x.shape
    L = S + N_META
    top_k = min(TOPK_MAX, S // 4)
    seq_pad = max(-(-L // Q_BLOCK) * Q_BLOCK, KEY_CHUNK)
    T = B * seq_pad
    tm_proj = 384 if seq_pad % 384 == 0 else Q_BLOCK
    tm_ffn = 768 if T % 768 == 0 else Q_BLOCK
    tm_moe = 1536 if T % 1536 == 0 else Q_BLOCK
    tf = 512

    h = jnp.concatenate([jnp.broadcast_to(meta[None].astype(x.dtype), (B, N_META, D)), x,
                         jnp.zeros((B, seq_pad - L, D), x.dtype)], axis=1).reshape(T, D)
    tabs = _rope_tables(seq_pad)
    depth = norm_mixer.shape[0]

    for i in range(depth):
        j = i // 2
        gain_m = norm_mixer[i][None, :]
        gain_f = norm_ffn[i][None, :]
        if i % 2 == 0:
            w_in = _pad_cols(dsa_w_in[j], DSA_IN_PAD).astype(BF16)
            gq = jnp.tile(dsa_q_norm[j], 2)[None, :]
            gk = jnp.tile(dsa_k_norm[j], 2)[None, :]
            q, k, v, iq, ik, iw = _dsa_in_proj(h, gain_m, w_in, gq, gk, tabs, seq_pad, tm_proj)
            k, v, ik = [a.reshape(B, seq_pad, a.shape[-1]) for a in (k, v, ik)]
            att = _dsa_core(q, k, v, iq, ik, iw, top_k)
            h = _ffn(h, att.reshape(T, D), dsa_w_out[j].astype(BF16), gain_f, ffn_w_gate[j].astype(BF16),
                     ffn_w_up[j].astype(BF16), ffn_w_down[j].astype(BF16), tm_ffn, tf)
        else:
            w_in = _pad_cols(mlstm_w_in[j], MLSTM_IN_PAD).astype(BF16)
            q, k, v, o, gates = _mlstm_in_proj(h, gain_m, w_in, tm_proj)
            bias8 = jnp.concatenate([mlstm_b_i[j], mlstm_b_f[j]])[:, None]
            y = _mlstm_core(*[a.reshape(B, seq_pad, a.shape[-1]) for a in (q, k, v, o, gates)], bias8,
                            mlstm_out_norm[j][None, :])
            h = _out_proj(h, y.reshape(T, D), mlstm_w_out[j].astype(BF16), tm_proj)
            wr = _pad_cols(moe_router[j], LANES)
            h = _moe(h, gain_f, wr, moe_w_gate[j].astype(BF16), moe_w_up[j].astype(BF16),
                     moe_w_down[j].astype(BF16), tm_moe, tf)
    return h.reshape(B, seq_pad, D)[:, N_META:L]
```

```python
import functools

import jax
import jax.numpy as jnp
import numpy as np
from jax import lax
from jax.experimental import pallas as pl
from jax.experimental.pallas import tpu as pltpu

F32 = jnp.float32
BF16 = jnp.bfloat16

D_MODEL = 1024
N_META = 16
RMS_EPS = 1e-6
ROPE_THETA = 10000.0
ATT_HEADS = 16
ATT_KV_HEADS = 4
ATT_HEAD_DIM = 64
ATT_GROUP = 4
IDX_HEADS = 8
IDX_DIM = 128
TOPK_MAX = 256
M_HEADS = 4
M_QK_DIM = 128
M_V_DIM = 256
GATE_CAP = 15.0
D_FF = 3584
N_EXPERTS = 8

LANES = 128
SUBLANES = 8
Q_BLOCK = 128
KEY_CHUNK = 256
ATT_STEPS_PER_TRIP = 4
M_CHUNK = 128
M_BATCH_ROWS = 1
MOE_SLOT_BLOCK = 208
MOE_TOKEN_BLOCK = 256
VMEM_LIMIT = 52 * 1024 * 1024
INT_MIN = -(2 ** 31)
INT_MAX = 2 ** 31 - 1
LOG2E = 1.4426950408889634

DSA_IN_PAD = 2816
MLSTM_IN_PAD = 3200


def _cparams(sem):
    return pltpu.CompilerParams(dimension_semantics=sem, vmem_limit_bytes=VMEM_LIMIT)


def _rms_rows(x, gain):
    ms = jnp.mean(x * x, axis=-1, keepdims=True)
    return x * lax.rsqrt(ms + RMS_EPS) * gain


def _dsa_in_kernel(h_ref, g_ref, w_ref, gq_ref, gk_ref, c64_ref, sa64_ref, sb64_ref, c128_ref, s128_ref,
                   smat_ref, q_ref, k_ref, v_ref, iq_ref, ik_ref, iw_ref):
    xn = _rms_rows(h_ref[...], g_ref[...]).astype(BF16)
    z = jnp.dot(xn, w_ref[...], preferred_element_type=F32)
    c64, sa64, sb64 = c64_ref[...], sa64_ref[...], sb64_ref[...]
    c128, s128 = c128_ref[...], s128_ref[...]
    smat = smat_ref[...]

    def head_norm_rope(zc, gain):
        z2 = zc * zc
        hi = z2.astype(BF16)
        lo = (z2 - hi.astype(F32)).astype(BF16)
        ms = jnp.dot(hi, smat, preferred_element_type=F32) + jnp.dot(lo, smat, preferred_element_type=F32)
        y = zc * lax.rsqrt(ms + RMS_EPS) * gain
        return y * c64 + pltpu.roll(y, 96, 1) * sa64 + pltpu.roll(y, 32, 1) * sb64

    nsub = z.shape[0] // Q_BLOCK
    lane = lax.broadcasted_iota(jnp.int32, (z.shape[0], LANES), 1)
    for c in range(8):
        r = head_norm_rope(z[:, c * LANES:(c + 1) * LANES], gq_ref[...]) * (ATT_HEAD_DIM ** -0.5 * LOG2E)
        swapped = pltpu.roll(r, 64, 1)
        for half in range(2):
            h = 2 * c + half
            g, rr = h // ATT_GROUP, h % ATT_GROUP
            keep = (lane >= 64) if g % 2 else (lane < 64)
            val = jnp.where(keep, r if half == g % 2 else swapped, 0.0).astype(BF16)
            for t in range(nsub):
                q_ref[t, g, rr * Q_BLOCK:(rr + 1) * Q_BLOCK, :] = val[t * Q_BLOCK:(t + 1) * Q_BLOCK]
    for c in range(2):
        r = head_norm_rope(z[:, 1024 + c * LANES:1024 + (c + 1) * LANES], gk_ref[...])
        k_ref[:, c * LANES:(c + 1) * LANES] = r.astype(BF16)
    ones = jnp.ones((z.shape[0], LANES), BF16)
    for c in range(2):
        v_ref[:, 2 * c * LANES:(2 * c + 1) * LANES] = z[:, 1280 + c * LANES:1280 + (c + 1) * LANES].astype(BF16)
        v_ref[:, (2 * c + 1) * LANES:(2 * c + 2) * LANES] = ones
    for c in range(9):
        zc = z[:, 1536 + c * LANES:1536 + (c + 1) * LANES]
        r = zc * c128 + pltpu.roll(zc, 64, 1) * s128
        if c < 8:
            for t in range(nsub):
                iq_ref[t, :, c * Q_BLOCK:(c + 1) * Q_BLOCK] = r[t * Q_BLOCK:(t + 1) * Q_BLOCK].T.astype(BF16)
        else:
            ik_ref[...] = r.astype(BF16)
    iw = z[:, 2688:2816] * (IDX_HEADS ** -0.5 * IDX_DIM ** -0.5)
    for t in range(nsub):
        iw_ref[t] = iw[t * Q_BLOCK:(t + 1) * Q_BLOCK].T


def _dsa_in_proj(h, gain, w, gq, gk, tabs, seq_pad, tm):
    T = h.shape[0]
    nt = seq_pad // tm
    row = lambda i: (i, 0)
    fixed = lambda i: (0, 0)
    pos = lambda i: (i % nt, 0)
    tab_spec = pl.BlockSpec((tm, LANES), pos)
    smat = jnp.asarray(np.kron(np.eye(2), np.full((64, 64), 1.0 / 64)), BF16)
    nsub = tm // Q_BLOCK
    rows = ATT_GROUP * Q_BLOCK
    outs = [((ATT_KV_HEADS, rows, LANES), BF16), (256, BF16), (512, BF16),
            ((IDX_DIM, IDX_HEADS * Q_BLOCK), BF16), (128, BF16), ((LANES, Q_BLOCK), F32)]

    def spec(o):
        if isinstance(o[0], int):
            return pl.BlockSpec((tm, o[0]), row)
        return pl.BlockSpec((nsub,) + o[0], lambda i: (i,) + (0,) * len(o[0]))

    def shape(o):
        if isinstance(o[0], int):
            return jax.ShapeDtypeStruct((T, o[0]), o[1])
        return jax.ShapeDtypeStruct((T // Q_BLOCK,) + o[0], o[1])

    return pl.pallas_call(
        _dsa_in_kernel,
        grid=(T // tm,),
        in_specs=[pl.BlockSpec((tm, D_MODEL), row), pl.BlockSpec((1, D_MODEL), fixed),
                  pl.BlockSpec((D_MODEL, DSA_IN_PAD), fixed), pl.BlockSpec((1, LANES), fixed),
                  pl.BlockSpec((1, LANES), fixed), tab_spec, tab_spec, tab_spec, tab_spec, tab_spec,
                  pl.BlockSpec((LANES, LANES), fixed)],
        out_specs=[spec(o) for o in outs],
        out_shape=[shape(o) for o in outs],
        compiler_params=_cparams(("arbitrary",)),
        name="dsa_in_proj",
    )(h, gain, w, gq, gk, *tabs, smat)


def _dsa_core_kernel(q_ref, k_ref, v_ref, iq_ref, ik_ref, iw_ref, o_ref,
                     keys_sc, bias_sc, cut_sc, m_sc, l_sc, acc_sc, *, top_k, seq_pad):
    i = pl.program_id(1)
    QB, KC = Q_BLOCK, KEY_CHUNK
    nch = (i * QB + QB + KC - 1) // KC
    kf = float(top_k)
    lane1 = lax.broadcasted_iota(jnp.int32, (QB, LANES), 1)

    qpos = i * QB + lax.broadcasted_iota(jnp.int32, (KC, QB), 1)
    krow = lax.broadcasted_iota(jnp.int32, (KC, QB), 0)

    def chunk_start(j):
        return pl.multiple_of(jnp.minimum(j * KC, seq_pad - KC), LANES)

    def key_valid(j, start):
        kidx = start + krow
        return (kidx <= qpos) & (kidx >= j * KC)

    def for_chunks(fn):
        def body(j2, carry):
            fn(2 * j2)
            fn(2 * j2 + 1)
            return carry

        lax.fori_loop(0, nch // 2, body, 0)

        @pl.when(nch % 2 == 1)
        def _():
            fn(nch - 1)

    def score_chunk(j):
        start = chunk_start(j)
        s = jnp.dot(ik_ref[0, pl.ds(start, KC), :], iq_ref[0], preferred_element_type=F32)
        acc = jnp.zeros((KC, QB), F32)
        for h in range(IDX_HEADS):
            acc = acc + jnp.maximum(s[:, h * QB:(h + 1) * QB], 0.0) * iw_ref[0, h:h + 1, :]
        bits = pltpu.bitcast(acc, jnp.int32)
        key = jnp.where(bits >= 0, bits, bits ^ jnp.int32(INT_MAX))
        keys_sc[j] = jnp.where(key_valid(j, start), key, jnp.int32(INT_MIN))

    for_chunks(score_chunk)

    tiles = (KC // SUBLANES, SUBLANES, QB)
    krow3 = (lax.broadcasted_iota(jnp.int32, tiles, 0) * SUBLANES + lax.broadcasted_iota(jnp.int32, tiles, 1))

    keys_sc[nch] = jnp.full((KC, QB), INT_MIN, jnp.int32)

    def count(pred):
        def body(j2, acc):
            for j in (2 * j2, 2 * j2 + 1):
                kk = keys_sc[j].reshape(tiles)
                hit = jnp.where(pred(kk, chunk_start(j)), jnp.float32(1.0), jnp.float32(0.0))
                while hit.shape[0] > 1:
                    half = hit.shape[0] // 2
                    hit = hit[:half] + hit[half:]
                acc = acc + hit[0]
            return acc
        acc = lax.fori_loop(0, (nch + 1) // 2, body, jnp.zeros((SUBLANES, QB), F32))
        for shift in (4, 2, 1):
            acc = acc + pltpu.roll(acc, shift, 0)
        return acc

    zero = jnp.zeros((SUBLANES, QB), jnp.int32)
    v0 = jnp.where(count(lambda kk, base: kk >= zero) >= kf, zero, jnp.int32(INT_MIN))

    def bit_body(b, v):
        cand = v | lax.shift_left(jnp.int32(1), (30 - b).astype(jnp.int32))
        return jnp.where(count(lambda kk, base: kk >= cand) >= kf, cand, v)

    thr = lax.fori_loop(jnp.int32(0), jnp.int32(31), bit_body, v0)
    cut_sc[...] = jnp.full((SUBLANES, QB), INT_MAX, jnp.int32)
    n_ge = count(lambda kk, base: kk >= thr)

    @pl.when(jnp.max(n_ge) > kf)
    def _():
        need = kf - count(lambda kk, base: kk > thr)

        def cut_body(b, c):
            cand = c | lax.shift_left(jnp.int32(1), (12 - b).astype(jnp.int32))
            n = count(lambda kk, base: jnp.where(kk == thr, base + krow3, jnp.int32(INT_MAX)) < cand)
            return jnp.where(n < need, cand, c)

        cut_sc[...] = lax.fori_loop(jnp.int32(0), jnp.int32(13), cut_body, zero)

    thr1 = thr[0:1, :]
    cut1 = cut_sc[0:1, :]

    def bias_chunk(j):
        kk = keys_sc[j]
        start = chunk_start(j)
        sel = (kk > thr1) | ((kk == thr1) & (start + krow <= cut1))
        bt = jnp.where(sel & key_valid(j, start), 0.0, -1e30)
        bias_sc[j] = jnp.concatenate([bt[c * QB:(c + 1) * QB].T for c in range(KC // QB)], axis=1)

    for_chunks(bias_chunk)

    m_sc[...] = jnp.full(m_sc.shape, -1e9, F32)
    l_sc[...] = jnp.zeros(l_sc.shape, F32)
    acc_sc[...] = jnp.zeros(acc_sc.shape, F32)

    def att_step(start, width, bias, groups):
        bias = jnp.concatenate([bias] * ATT_GROUP, axis=0)
        for g in groups:
            cg = g // 2
            kj = k_ref[0, pl.ds(start, width), cg * LANES:(cg + 1) * LANES]
            vj = v_ref[0, pl.ds(start, width), cg * 2 * LANES:(cg + 1) * 2 * LANES]
            s = lax.dot_general(q_ref[0, g], kj, (((1,), (1,)), ((), ())), preferred_element_type=F32) + bias
            m_old = m_sc[g]
            m_new = jnp.maximum(m_old, jnp.max(s, axis=1, keepdims=True))
            alpha = jnp.exp2(m_old - m_new)
            p = jnp.exp2(s - jnp.concatenate([m_new] * (width // LANES), axis=1)).astype(BF16)
            pv = jnp.dot(p, vj, preferred_element_type=F32)
            l_sc[g] = alpha * l_sc[g] + pv[:, LANES:]
            acc_sc[g] = alpha * acc_sc[g] + pv[:, :LANES]
            m_sc[g] = m_new

    npair = jnp.minimum(nch // 2, seq_pad // (2 * KC))

    def single_body(j, carry):
        att_step(chunk_start(j), KC, bias_sc[j], range(ATT_KV_HEADS))
        return carry

    def pair_step(j):
        bias = jnp.concatenate([bias_sc[2 * j], bias_sc[2 * j + 1]], axis=1)
        att_step(pl.multiple_of(j * 2 * KC, 2 * KC), 2 * KC, bias, range(ATT_KV_HEADS))

    def multi_body(j4, carry):
        for t in range(ATT_STEPS_PER_TRIP):
            pair_step(ATT_STEPS_PER_TRIP * j4 + t)
        return carry

    lax.fori_loop(0, npair // ATT_STEPS_PER_TRIP, multi_body, 0)
    rest = npair % ATT_STEPS_PER_TRIP
    done = npair - rest
    n = ATT_STEPS_PER_TRIP // 2
    while n >= 1:
        @pl.when((rest & n) != 0)
        def _(n=n, done=done):
            for t in range(n):
                pair_step(done + t)

        done = done + (rest & n)
        n //= 2

    lax.fori_loop(2 * npair, nch, single_body, 0)

    for g in range(ATT_KV_HEADS):
        og = acc_sc[g] / l_sc[g]
        for pair in range(2):
            even = og[(2 * pair) * QB:(2 * pair + 1) * QB]
            odd = og[(2 * pair + 1) * QB:(2 * pair + 2) * QB]
            if g % 2:
                even = pltpu.roll(even, 64, 1)
            else:
                odd = pltpu.roll(odd, 64, 1)
            c = 2 * g + pair
            o_ref[0, :, c * LANES:(c + 1) * LANES] = jnp.where(lane1 < 64, even, odd).astype(BF16)


def _dsa_core(q, k, v, iq, ik, iw, top_k):
    B, seq_pad = k.shape[0], k.shape[1]
    nqb = seq_pad // Q_BLOCK
    nkc = -(-seq_pad // KEY_CHUNK)
    rows = ATT_GROUP * Q_BLOCK
    qblock = lambda a: pl.BlockSpec((1,) + a.shape[1:], lambda b, i: (b * nqb + i,) + (0,) * (a.ndim - 1))
    whole = lambda n: pl.BlockSpec((1, seq_pad, n), lambda b, i: (b, 0, 0))
    return pl.pallas_call(
        functools.partial(_dsa_core_kernel, top_k=top_k, seq_pad=seq_pad),
        grid=(B, nqb),
        in_specs=[qblock(q), whole(256), whole(512), qblock(iq), whole(IDX_DIM), qblock(iw)],
        out_specs=pl.BlockSpec((1, Q_BLOCK, D_MODEL), lambda b, i: (b, i, 0)),
        out_shape=jax.ShapeDtypeStruct((B, seq_pad, D_MODEL), BF16),
        scratch_shapes=[
            pltpu.VMEM((nkc + 1, KEY_CHUNK, Q_BLOCK), jnp.int32),
            pltpu.VMEM((nkc, Q_BLOCK, KEY_CHUNK), F32),
            pltpu.VMEM((SUBLANES, Q_BLOCK), jnp.int32),
            pltpu.VMEM((ATT_KV_HEADS, rows, LANES), F32),
            pltpu.VMEM((ATT_KV_HEADS, rows, LANES), F32),
            pltpu.VMEM((ATT_KV_HEADS, rows, LANES), F32),
        ],
        compiler_params=_cparams(("arbitrary", "arbitrary")),
        name="dsa_core",
    )(q, k, v, iq, ik, iw)


def _out_proj_kernel(h_ref, y_ref, w_ref, o_ref):
    o_ref[...] = h_ref[...] + jnp.dot(y_ref[...], w_ref[...], preferred_element_type=F32)


def _out_proj(h, y, w, tm):
    T = h.shape[0]
    row = lambda i: (i, 0)
    return pl.pallas_call(
        _out_proj_kernel,
        grid=(T // tm,),
        in_specs=[pl.BlockSpec((tm, D_MODEL), row), pl.BlockSpec((tm, D_MODEL), row),
                  pl.BlockSpec((D_MODEL, D_MODEL), lambda i: (0, 0))],
        out_specs=pl.BlockSpec((tm, D_MODEL), row),
        out_shape=jax.ShapeDtypeStruct((T, D_MODEL), F32),
        compiler_params=_cparams(("arbitrary",)),
        name="out_proj",
    )(h, y, w)


def _mlstm_in_kernel(h_ref, g_ref, w_ref, q_ref, k_ref, v_ref, o_ref, gate_ref):
    xn = _rms_rows(h_ref[...], g_ref[...]).astype(BF16)
    z = jnp.dot(xn, w_ref[...], preferred_element_type=F32)
    q_ref[...] = z[:, 0:512].astype(BF16)
    k_ref[...] = (z[:, 512:1024] * (M_QK_DIM ** -0.5)).astype(BF16)
    v_ref[...] = z[:, 1024:2048].astype(BF16)
    o_ref[...] = z[:, 2048:3072]
    gate_ref[...] = z[:, 3072:3200]


def _mlstm_in_proj(h, gain, w, tm):
    T = h.shape[0]
    row = lambda i: (i, 0)
    fixed = lambda i: (0, 0)
    outs = [(512, BF16), (512, BF16), (1024, BF16), (1024, F32), (128, F32)]
    return pl.pallas_call(
        _mlstm_in_kernel,
        grid=(T // tm,),
        in_specs=[pl.BlockSpec((tm, D_MODEL), row), pl.BlockSpec((1, D_MODEL), fixed),
                  pl.BlockSpec((D_MODEL, MLSTM_IN_PAD), fixed)],
        out_specs=[pl.BlockSpec((tm, n), row) for n, _ in outs],
        out_shape=[jax.ShapeDtypeStruct((T, n), dt) for n, dt in outs],
        compiler_params=_cparams(("arbitrary",)),
        name="mlstm_in_proj",
    )(h, gain, w)


def _mlstm_kernel(q_ref, k_ref, v_ref, o_ref, gate_ref, bias_ref, gout_ref, y_ref, ct_sc, n_sc, m_sc):
    C = M_CHUNK

    @pl.when(pl.program_id(1) == 0)
    def _():
        ct_sc[...] = jnp.zeros(ct_sc.shape, F32)
        n_sc[...] = jnp.zeros(n_sc.shape, F32)
        m_sc[...] = jnp.zeros(m_sc.shape, F32)

    lane8 = lax.broadcasted_iota(jnp.int32, (8, C), 1)
    t_idx = lax.broadcasted_iota(jnp.int32, (C, C), 0)
    s_idx = lax.broadcasted_iota(jnp.int32, (C, C), 1)
    for bi in range(q_ref.shape[0]):
        pre = gate_ref[bi].T[0:8, :] + bias_ref[...]
        capped = GATE_CAP * jnp.tanh(pre / GATE_CAP)
        log_f = -(jnp.maximum(-capped, 0.0) + jnp.log1p(jnp.exp(-jnp.abs(capped))))
        b = log_f
        sh = 1
        while sh < C:
            b = b + jnp.where(lane8 >= sh, pltpu.roll(b, sh, 1), 0.0)
            sh *= 2
        stacked = jnp.concatenate([b[4:8], capped[0:4]], axis=0)
        cols = jnp.concatenate([stacked, jnp.zeros((C - 8, C), F32)], axis=0).T

        for h in range(M_HEADS):
            st = bi * M_HEADS + h
            qh = q_ref[bi, :, h * M_QK_DIM:(h + 1) * M_QK_DIM]
            kh = k_ref[bi, :, h * M_QK_DIM:(h + 1) * M_QK_DIM]
            vh = v_ref[bi, :, h * M_V_DIM:(h + 1) * M_V_DIM]
            b_row, li_row = stacked[h:h + 1, :], stacked[4 + h:5 + h, :]
            b_col, li_col = cols[:, h:h + 1], cols[:, 4 + h:5 + h]
            m_st = m_sc[st:st + 1, 0:1]
            dmat = jnp.where(s_idx <= t_idx, b_col - b_row + li_row, -jnp.inf)
            inter = b_col + m_st
            m_t = jnp.maximum(inter, jnp.max(dmat, axis=1, keepdims=True))
            w_inter = jnp.exp(inter - m_t)
            qk = lax.dot_general(qh, kh, (((1,), (1,)), ((), ())), preferred_element_type=F32)
            s = qk * jnp.exp(dmat - m_t)
            ct = ct_sc[st]
            num = (w_inter * jnp.dot(qh, ct.astype(BF16), preferred_element_type=F32)
                   + jnp.dot(s.astype(BF16), vh, preferred_element_type=F32))
            qn = jnp.sum(qh.astype(F32) * n_sc[st:st + 1, :], axis=1, keepdims=True)
            den = w_inter * qn + jnp.sum(s, axis=1, keepdims=True)
            hout = num / jnp.maximum(jnp.abs(den), jnp.exp(-m_t))

            b_last = b_row[:, C - 1:C]
            m_new = jnp.maximum(b_last + m_st, jnp.max(b_last - b_row + li_row, axis=1, keepdims=True))
            decay = jnp.exp(b_last + m_st - m_new)
            wk = jnp.exp(b_last - b_col + li_col - m_new)
            kw = kh.astype(F32) * wk
            ct_sc[st] = decay * ct + jnp.dot(kw.T.astype(BF16), vh, preferred_element_type=F32)
            n_sc[st:st + 1, :] = decay * n_sc[st:st + 1, :] + jnp.sum(kw, axis=0, keepdims=True)
            m_sc[st:st + 1, :] = jnp.broadcast_to(m_new, (1, LANES))

            cs = slice(h * M_V_DIM, (h + 1) * M_V_DIM)
            hn = _rms_rows(hout, gout_ref[:, cs])
            y_ref[bi, :, cs] = (hn * jax.nn.sigmoid(o_ref[bi, :, cs])).astype(BF16)


def _mlstm_core(q, k, v, o, gates, bias8, gout):
    B, seq_pad = q.shape[0], q.shape[1]
    nb = M_BATCH_ROWS
    C = M_CHUNK
    blk = lambda n: pl.BlockSpec((nb, C, n), lambda b, c: (b, c, 0))
    fixed = lambda b, c: (0, 0)
    return pl.pallas_call(
        _mlstm_kernel,
        grid=(B // nb, seq_pad // C),
        in_specs=[blk(512), blk(512), blk(1024), blk(1024), blk(LANES),
                  pl.BlockSpec((8, 1), fixed), pl.BlockSpec((1, D_MODEL), fixed)],
        out_specs=blk(D_MODEL),
        out_shape=jax.ShapeDtypeStruct((B, seq_pad, D_MODEL), BF16),
        scratch_shapes=[pltpu.VMEM((nb * M_HEADS, M_QK_DIM, M_V_DIM), F32),
                        pltpu.VMEM((nb * M_HEADS, M_QK_DIM), F32),
                        pltpu.VMEM((nb * M_HEADS, LANES), F32)],
        compiler_params=_cparams(("arbitrary", "arbitrary")),
        name="mlstm_core",
    )(q, k, v, o, gates, bias8, gout)


def _ffn_kernel(h_ref, y_ref, wo_ref, g_ref, wg_ref, wu_ref, wd_ref, o_ref, xn_sc, acc_sc):
    f = pl.program_id(1)

    @pl.when(f == 0)
    def _():
        x = h_ref[...] + jnp.dot(y_ref[...], wo_ref[...], preferred_element_type=F32)
        xn_sc[...] = _rms_rows(x, g_ref[...]).astype(BF16)
        acc_sc[...] = x

    xn = xn_sc[...]
    gate = jnp.dot(xn, wg_ref[...], preferred_element_type=F32)
    up = jnp.dot(xn, wu_ref[...], preferred_element_type=F32)
    act = (gate * jax.nn.sigmoid(gate) * up).astype(BF16)
    acc_sc[...] += jnp.dot(act, wd_ref[...], preferred_element_type=F32)

    @pl.when(f == pl.num_programs(1) - 1)
    def _():
        o_ref[...] = acc_sc[...]


def _ffn(h, y, wo, gain, wg, wu, wd, tm, tf):
    T = h.shape[0]
    return pl.pallas_call(
        _ffn_kernel,
        grid=(T // tm, D_FF // tf),
        in_specs=[pl.BlockSpec((tm, D_MODEL), lambda i, f: (i, 0)),
                  pl.BlockSpec((tm, D_MODEL), lambda i, f: (i, 0)),
                  pl.BlockSpec((D_MODEL, D_MODEL), lambda i, f: (0, 0)),
                  pl.BlockSpec((1, D_MODEL), lambda i, f: (0, 0)),
                  pl.BlockSpec((D_MODEL, tf), lambda i, f: (0, f)),
                  pl.BlockSpec((D_MODEL, tf), lambda i, f: (0, f)),
                  pl.BlockSpec((tf, D_MODEL), lambda i, f: (f, 0))],
        out_specs=pl.BlockSpec((tm, D_MODEL), lambda i, f: (i, 0)),
        out_shape=jax.ShapeDtypeStruct((T, D_MODEL), F32),
        scratch_shapes=[pltpu.VMEM((tm, D_MODEL), BF16), pltpu.VMEM((tm, D_MODEL), F32)],
        compiler_params=_cparams(("arbitrary", "arbitrary")),
        name="ffn_dense",
    )(h, y, wo, gain, wg, wu, wd)


def _moe_kernel(h_ref, g_ref, wr_ref, wg_ref, wu_ref, wd_ref, o_ref,
                xn_sc, xs_sc, ys_sc, posc_sc, posr_sc, comb_sc, nblk_sc):
    e = pl.program_id(1)
    f = pl.program_id(2)
    tm = h_ref.shape[0]
    SB = _moe_slot_block(tm)
    TB = _moe_token_block(tm)
    lane = lax.broadcasted_iota(jnp.int32, (tm, LANES), 1)

    @pl.when((e == 0) & (f == 0))
    def _():
        x = h_ref[...]
        xn = _rms_rows(x, g_ref[...])
        xn_sc[...] = xn.astype(BF16)
        o_ref[...] = x
        logits = jnp.dot(xn, wr_ref[...], preferred_element_type=F32, precision=lax.Precision.HIGHEST)
        logits = jnp.where(lane < N_EXPERTS, logits, -jnp.inf)
        m1 = jnp.max(logits, axis=1, keepdims=True)
        i1 = jnp.min(jnp.where(logits == m1, lane, LANES), axis=1, keepdims=True)
        rest = jnp.where(lane == i1, -jnp.inf, logits)
        m2 = jnp.max(rest, axis=1, keepdims=True)
        i2 = jnp.min(jnp.where(rest == m2, lane, LANES), axis=1, keepdims=True)
        e2 = jnp.exp(m2 - m1)
        g1 = 1.0 / (1.0 + e2)
        comb_sc[...] = jnp.where(lane == i1, g1, jnp.where(lane == i2, e2 * g1, 0.0))
        member = jnp.where(lane == i1, 1.0, jnp.where(lane == i2, 1.0, 0.0))
        tri = jnp.where(lax.broadcasted_iota(jnp.int32, (TB, TB), 0) >= lax.broadcasted_iota(jnp.int32, (TB, TB), 1),
                        1.0, 0.0).astype(BF16)
        carry = jnp.zeros((1, LANES), F32)
        for b in range(tm // TB):
            mb = member[b * TB:(b + 1) * TB]
            incl = jnp.dot(tri, mb.astype(BF16), preferred_element_type=F32)
            posc_sc[b * TB:(b + 1) * TB, :] = jnp.where(mb > 0.0, incl - 1.0 + carry, -1.0)
            carry = carry + incl[TB - 1:TB, :]
        for b in range(tm // LANES):
            posr_sc[:, b * LANES:(b + 1) * LANES] = posc_sc[b * LANES:(b + 1) * LANES, :].T
        lane_row = lax.broadcasted_iota(jnp.int32, (1, LANES), 1)
        for x in range(N_EXPERTS):
            count = jnp.sum(jnp.where(lane_row == x, carry, 0.0))
            nblk_sc[x] = ((count + (SB - 1.0)) * (1.0 / SB)).astype(jnp.int32)

    nblk = nblk_sc[e]

    @pl.when(f == 0)
    def _():
        slots_of_tokens = posr_sc[pl.ds(e, 1), :]

        def body(r, carry):
            row0 = pl.multiple_of(r * SB, 16)
            slot = (r * SB + lax.broadcasted_iota(jnp.int32, (SB, tm), 0)).astype(F32)
            onehot = jnp.where(slots_of_tokens == slot, 1.0, 0.0).astype(BF16)
            xs_sc[pl.ds(row0, SB), :] = jnp.dot(onehot, xn_sc[...], preferred_element_type=F32).astype(BF16)
            ys_sc[pl.ds(row0, SB), :] = jnp.zeros((SB, D_MODEL), F32)
            return carry

        lax.fori_loop(0, nblk, body, 0)

    def ffn_rows(row0, nrows):
        xs = xs_sc[pl.ds(row0, nrows), :]
        gate = jnp.dot(xs, wg_ref[0], preferred_element_type=F32)
        up = jnp.dot(xs, wu_ref[0], preferred_element_type=F32)
        act = (gate * jax.nn.sigmoid(gate) * up).astype(BF16)
        ys_sc[pl.ds(row0, nrows), :] += jnp.dot(act, wd_ref[0], preferred_element_type=F32)

    def pair_body(r, carry):
        ffn_rows(pl.multiple_of(r * 2 * SB, 16), 2 * SB)
        return carry

    lax.fori_loop(0, nblk // 2, pair_body, 0)

    @pl.when(nblk % 2 == 1)
    def _():
        ffn_rows(pl.multiple_of((nblk - 1) * SB, 16), SB)

    @pl.when(f == pl.num_programs(2) - 1)
    def _():
        on_e = lane == e
        gate_e = jnp.sum(jnp.where(on_e, comb_sc[...], 0.0), axis=1, keepdims=True)
        slot_e = jnp.sum(jnp.where(on_e, posc_sc[...], 0.0), axis=1, keepdims=True)

        def body(r, carry):
            row0 = pl.multiple_of(r * SB, 16)
            ys = ys_sc[pl.ds(row0, SB), :].astype(BF16)
            slot = (r * SB + lax.broadcasted_iota(jnp.int32, (TB, SB), 1)).astype(F32)
            for tb in range(tm // TB):
                rows = slice(tb * TB, (tb + 1) * TB)
                onehot = jnp.where(slot_e[rows] == slot, 1.0, 0.0).astype(BF16)
                o_ref[rows, :] += gate_e[rows] * jnp.dot(onehot, ys, preferred_element_type=F32)
            return carry

        lax.fori_loop(0, nblk, body, 0)


def _moe_slot_block(tm):
    return MOE_SLOT_BLOCK if tm > MOE_SLOT_BLOCK else tm


def _moe_token_block(tm):
    return MOE_TOKEN_BLOCK if tm % MOE_TOKEN_BLOCK == 0 else tm


def _moe(h, gain, wr, wg, wu, wd, tm, tf):
    T = h.shape[0]
    sb = _moe_slot_block(tm)
    cap = -(-tm // sb) * sb
    return pl.pallas_call(
        _moe_kernel,
        grid=(T // tm, N_EXPERTS, D_FF // tf),
        in_specs=[pl.BlockSpec((tm, D_MODEL), lambda i, e, f: (i, 0), pipeline_mode=pl.Buffered(1)),
                  pl.BlockSpec((1, D_MODEL), lambda i, e, f: (0, 0)),
                  pl.BlockSpec((D_MODEL, LANES), lambda i, e, f: (0, 0)),
                  pl.BlockSpec((1, D_MODEL, tf), lambda i, e, f: (e, 0, f)),
                  pl.BlockSpec((1, D_MODEL, tf), lambda i, e, f: (e, 0, f)),
                  pl.BlockSpec((1, tf, D_MODEL), lambda i, e, f: (e, f, 0))],
        out_specs=pl.BlockSpec((tm, D_MODEL), lambda i, e, f: (i, 0)),
        out_shape=jax.ShapeDtypeStruct((T, D_MODEL), F32),
        scratch_shapes=[pltpu.VMEM((tm, D_MODEL), BF16), pltpu.VMEM((cap, D_MODEL), BF16),
                        pltpu.VMEM((cap, D_MODEL), F32), pltpu.VMEM((tm, LANES), F32),
                        pltpu.VMEM((LANES, tm), F32), pltpu.VMEM((tm, LANES), F32),
                        pltpu.SMEM((N_EXPERTS,), jnp.int32)],
        compiler_params=_cparams(("arbitrary", "arbitrary", "arbitrary")),
        name="moe_sparse",
    )(h, gain, wr, wg, wu, wd)


def _rope_tables(seq_pad):
    pos = jnp.arange(seq_pad, dtype=F32)[:, None]
    lane = np.arange(LANES)

    def table(head_dim):
        half = head_dim // 2
        inv = ROPE_THETA ** (-jnp.arange(half, dtype=F32) / half)
        d = lane % head_dim
        ang = pos * inv[d % half][None, :]
        return jnp.cos(ang), jnp.sin(ang), jnp.asarray(d < half)[None, :]

    c64, s64, lo64 = table(ATT_HEAD_DIM)
    c128, s128, lo128 = table(IDX_DIM)
    return (c64, jnp.where(lo64, -s64, 0.0), jnp.where(lo64, 0.0, s64), c128, jnp.where(lo128, -s128, s128))


def _pad_cols(w, n):
    return jnp.pad(w, ((0, 0), (0, n - w.shape[1])))


def kernel(x, meta, norm_mixer, norm_ffn, dsa_w_in, dsa_q_norm, dsa_k_norm, dsa_w_out, mlstm_w_in, mlstm_b_i,
           mlstm_b_f, mlstm_out_norm, mlstm_w_out, ffn_w_gate, ffn_w_up, ffn_w_down, moe_router, moe_w_gate,
           moe_w_up, moe_w_down):
    B, S, D = x.shape
    L = S + N_META
    top_k = min(TOPK_MAX, S // 4)
    seq_pad = max(-(-L // Q_BLOCK) * Q_BLOCK, KEY_CHUNK)
    T = B * seq_pad
    tm_proj = 384 if seq_pad % 384 == 0 else Q_BLOCK
    tm_ffn = 768 if T % 768 == 0 else Q_BLOCK
    tm_moe = 1536 if T % 1536 == 0 else Q_BLOCK
    tf = 512

    h = jnp.concatenate([jnp.broadcast_to(meta[None].astype(x.dtype), (B, N_META, D)), x,
                         jnp.zeros((B, seq_pad - L, D), x.dtype)], axis=1).reshape(T, D)
    tabs = _rope_tables(seq_pad)
    depth = norm_mixer.shape[0]

    for i in range(depth):
        j = i // 2
        gain_m = norm_mixer[i][None, :]
        gain_f = norm_ffn[i][None, :]
        if i % 2 == 0:
            w_in = _pad_cols(dsa_w_in[j], DSA_IN_PAD).astype(BF16)
            gq = jnp.tile(dsa_q_norm[j], 2)[None, :]
            gk = jnp.tile(dsa_k_norm[j], 2)[None, :]
            q, k, v, iq, ik, iw = _dsa_in_proj(h, gain_m, w_in, gq, gk, tabs, seq_pad, tm_proj)
            k, v, ik = [a.reshape(B, seq_pad, a.shape[-1]) for a in (k, v, ik)]
            att = _dsa_core(q, k, v, iq, ik, iw, top_k)
            h = _ffn(h, att.reshape(T, D), dsa_w_out[j].astype(BF16), gain_f, ffn_w_gate[j].astype(BF16),
                     ffn_w_up[j].astype(BF16), ffn_w_down[j].astype(BF16), tm_ffn, tf)
        else:
            w_in = _pad_cols(mlstm_w_in[j], MLSTM_IN_PAD).astype(BF16)
            q, k, v, o, gates = _mlstm_in_proj(h, gain_m, w_in, tm_proj)
            bias8 = jnp.concatenate([mlstm_b_i[j], mlstm_b_f[j]])[:, None]
            y = _mlstm_core(*[a.reshape(B, seq_pad, a.shape[-1]) for a in (q, k, v, o, gates)], bias8,
                            mlstm_out_norm[j][None, :])
            h = _out_proj(h, y.reshape(T, D), mlstm_w_out[j].astype(BF16), tm_proj)
            wr = _pad_cols(moe_router[j], LANES)
            h = _moe(h, gain_f, wr, moe_w_gate[j].astype(BF16), moe_w_up[j].astype(BF16),
                     moe_w_down[j].astype(BF16), tm_moe, tf)
    return h.reshape(B, seq_pad, D)[:, N_META:L]
```

```python
import functools

import jax
import jax.numpy as jnp
import numpy as np
from jax import lax
from jax.experimental import pallas as pl
from jax.experimental.pallas import tpu as pltpu

F32 = jnp.float32
BF16 = jnp.bfloat16

D_MODEL = 1024
N_META = 16
RMS_EPS = 1e-6
ROPE_THETA = 10000.0
ATT_HEADS = 16
ATT_KV_HEADS = 4
ATT_HEAD_DIM = 64
ATT_GROUP = 4
IDX_HEADS = 8
IDX_DIM = 128
TOPK_MAX = 256
M_HEADS = 4
M_QK_DIM = 128
M_V_DIM = 256
GATE_CAP = 15.0
D_FF = 3584
N_EXPERTS = 8

LANES = 128
SUBLANES = 8
Q_BLOCK = 128
KEY_CHUNK = 256
ATT_STEPS_PER_TRIP = 4
CHUNKS_PER_TRIP = 4
M_CHUNK = 128
M_BATCH_ROWS = 1
MOE_SLOT_BLOCK = 208
MOE_TOKEN_BLOCK = 256
VMEM_LIMIT = 52 * 1024 * 1024
INT_MIN = -(2 ** 31)
INT_MAX = 2 ** 31 - 1
LOG2E = 1.4426950408889634

DSA_IN_PAD = 2816
MLSTM_IN_PAD = 3200


def _cparams(sem):
    return pltpu.CompilerParams(dimension_semantics=sem, vmem_limit_bytes=VMEM_LIMIT)


def _unrolled_loop(n, fn, unroll):
    def body(t, carry):
        for u in range(unroll):
            fn(unroll * t + u)
        return carry

    lax.fori_loop(0, n // unroll, body, 0)
    rest = n % unroll
    done = n - rest
    run = unroll // 2
    while run >= 1:
        @pl.when((rest & run) != 0)
        def _(run=run, done=done):
            for u in range(run):
                fn(done + u)

        done = done + (rest & run)
        run //= 2


def _rms_rows(x, gain):
    ms = jnp.mean(x * x, axis=-1, keepdims=True)
    return x * lax.rsqrt(ms + RMS_EPS) * gain


def _dsa_in_kernel(h_ref, g_ref, w_ref, gq_ref, gk_ref, c64_ref, sa64_ref, sb64_ref, c128_ref, s128_ref,
                   smat_ref, q_ref, k_ref, v_ref, iq_ref, ik_ref, iw_ref):
    xn = _rms_rows(h_ref[...], g_ref[...]).astype(BF16)
    z = jnp.dot(xn, w_ref[...], preferred_element_type=F32)
    c64, sa64, sb64 = c64_ref[...], sa64_ref[...], sb64_ref[...]
    c128, s128 = c128_ref[...], s128_ref[...]
    smat = smat_ref[...]

    def head_norm_rope(zc, gain):
        z2 = zc * zc
        hi = z2.astype(BF16)
        lo = (z2 - hi.astype(F32)).astype(BF16)
        ms = jnp.dot(hi, smat, preferred_element_type=F32) + jnp.dot(lo, smat, preferred_element_type=F32)
        y = zc * lax.rsqrt(ms + RMS_EPS) * gain
        return y * c64 + pltpu.roll(y, 96, 1) * sa64 + pltpu.roll(y, 32, 1) * sb64

    nsub = z.shape[0] // Q_BLOCK
    lane = lax.broadcasted_iota(jnp.int32, (z.shape[0], LANES), 1)
    for c in range(8):
        r = head_norm_rope(z[:, c * LANES:(c + 1) * LANES], gq_ref[...]) * (ATT_HEAD_DIM ** -0.5 * LOG2E)
        swapped = pltpu.roll(r, 64, 1)
        for half in range(2):
            h = 2 * c + half
            g, rr = h // ATT_GROUP, h % ATT_GROUP
            keep = (lane >= 64) if g % 2 else (lane < 64)
            val = jnp.where(keep, r if half == g % 2 else swapped, 0.0).astype(BF16)
            for t in range(nsub):
                q_ref[t, g, rr * Q_BLOCK:(rr + 1) * Q_BLOCK, :] = val[t * Q_BLOCK:(t + 1) * Q_BLOCK]
    for c in range(2):
        r = head_norm_rope(z[:, 1024 + c * LANES:1024 + (c + 1) * LANES], gk_ref[...])
        k_ref[:, c * LANES:(c + 1) * LANES] = r.astype(BF16)
    ones = jnp.ones((z.shape[0], LANES), BF16)
    for c in range(2):
        v_ref[:, 2 * c * LANES:(2 * c + 1) * LANES] = z[:, 1280 + c * LANES:1280 + (c + 1) * LANES].astype(BF16)
        v_ref[:, (2 * c + 1) * LANES:(2 * c + 2) * LANES] = ones
    for c in range(9):
        zc = z[:, 1536 + c * LANES:1536 + (c + 1) * LANES]
        r = zc * c128 + pltpu.roll(zc, 64, 1) * s128
        if c < 8:
            for t in range(nsub):
                iq_ref[t, :, c * Q_BLOCK:(c + 1) * Q_BLOCK] = r[t * Q_BLOCK:(t + 1) * Q_BLOCK].T.astype(BF16)
        else:
            ik_ref[...] = r.astype(BF16)
    iw = z[:, 2688:2816] * (IDX_HEADS ** -0.5 * IDX_DIM ** -0.5)
    for t in range(nsub):
        iw_ref[t] = iw[t * Q_BLOCK:(t + 1) * Q_BLOCK].T


def _dsa_in_proj(h, gain, w, gq, gk, tabs, seq_pad, tm):
    T = h.shape[0]
    nt = seq_pad // tm
    row = lambda i: (i, 0)
    fixed = lambda i: (0, 0)
    pos = lambda i: (i % nt, 0)
    tab_spec = pl.BlockSpec((tm, LANES), pos)
    smat = jnp.asarray(np.kron(np.eye(2), np.full((64, 64), 1.0 / 64)), BF16)
    nsub = tm // Q_BLOCK
    rows = ATT_GROUP * Q_BLOCK
    outs = [((ATT_KV_HEADS, rows, LANES), BF16), (256, BF16), (512, BF16),
            ((IDX_DIM, IDX_HEADS * Q_BLOCK), BF16), (128, BF16), ((LANES, Q_BLOCK), F32)]

    def spec(o):
        if isinstance(o[0], int):
            return pl.BlockSpec((tm, o[0]), row)
        return pl.BlockSpec((nsub,) + o[0], lambda i: (i,) + (0,) * len(o[0]))

    def shape(o):
        if isinstance(o[0], int):
            return jax.ShapeDtypeStruct((T, o[0]), o[1])
        return jax.ShapeDtypeStruct((T // Q_BLOCK,) + o[0], o[1])

    return pl.pallas_call(
        _dsa_in_kernel,
        grid=(T // tm,),
        in_specs=[pl.BlockSpec((tm, D_MODEL), row), pl.BlockSpec((1, D_MODEL), fixed),
                  pl.BlockSpec((D_MODEL, DSA_IN_PAD), fixed), pl.BlockSpec((1, LANES), fixed),
                  pl.BlockSpec((1, LANES), fixed), tab_spec, tab_spec, tab_spec, tab_spec, tab_spec,
                  pl.BlockSpec((LANES, LANES), fixed)],
        out_specs=[spec(o) for o in outs],
        out_shape=[shape(o) for o in outs],
        compiler_params=_cparams(("arbitrary",)),
        name="dsa_in_proj",
    )(h, gain, w, gq, gk, *tabs, smat)


def _dsa_core_kernel(q_ref, k_ref, v_ref, iq_ref, ik_ref, iw_ref, o_ref,
                     keys_sc, bias_sc, cut_sc, m_sc, l_sc, acc_sc, *, top_k, seq_pad):
    i = pl.program_id(1)
    QB, KC = Q_BLOCK, KEY_CHUNK
    nch = (i * QB + QB + KC - 1) // KC
    kf = float(top_k)
    lane1 = lax.broadcasted_iota(jnp.int32, (QB, LANES), 1)

    qpos = i * QB + lax.broadcasted_iota(jnp.int32, (KC, QB), 1)
    krow = lax.broadcasted_iota(jnp.int32, (KC, QB), 0)

    def chunk_start(j):
        return pl.multiple_of(jnp.minimum(j * KC, seq_pad - KC), LANES)

    def key_valid(j, start):
        kidx = start + krow
        return (kidx <= qpos) & (kidx >= j * KC)

    def score_chunk(j):
        start = chunk_start(j)
        s = jnp.dot(ik_ref[0, pl.ds(start, KC), :], iq_ref[0], preferred_element_type=F32)
        acc = jnp.zeros((KC, QB), F32)
        for h in range(IDX_HEADS):
            acc = acc + jnp.maximum(s[:, h * QB:(h + 1) * QB], 0.0) * iw_ref[0, h:h + 1, :]
        bits = pltpu.bitcast(acc, jnp.int32)
        key = jnp.where(bits >= 0, bits, bits ^ jnp.int32(INT_MAX))
        keys_sc[j] = jnp.where(key_valid(j, start), key, jnp.int32(INT_MIN))

    _unrolled_loop(nch, score_chunk, CHUNKS_PER_TRIP)

    tiles = (KC // SUBLANES, SUBLANES, QB)
    krow3 = (lax.broadcasted_iota(jnp.int32, tiles, 0) * SUBLANES + lax.broadcasted_iota(jnp.int32, tiles, 1))

    keys_sc[nch] = jnp.full((KC, QB), INT_MIN, jnp.int32)

    def count(pred):
        def body(j2, acc):
            for j in (2 * j2, 2 * j2 + 1):
                kk = keys_sc[j].reshape(tiles)
                hit = jnp.where(pred(kk, chunk_start(j)), jnp.float32(1.0), jnp.float32(0.0))
                while hit.shape[0] > 1:
                    half = hit.shape[0] // 2
                    hit = hit[:half] + hit[half:]
                acc = acc + hit[0]
            return acc
        acc = lax.fori_loop(0, (nch + 1) // 2, body, jnp.zeros((SUBLANES, QB), F32))
        for shift in (4, 2, 1):
            acc = acc + pltpu.roll(acc, shift, 0)
        return acc

    zero = jnp.zeros((SUBLANES, QB), jnp.int32)
    v0 = jnp.where(count(lambda kk, base: kk >= zero) >= kf, zero, jnp.int32(INT_MIN))

    def bit_body(b, v):
        cand = v | lax.shift_left(jnp.int32(1), (30 - b).astype(jnp.int32))
        return jnp.where(count(lambda kk, base: kk >= cand) >= kf, cand, v)

    thr = lax.fori_loop(jnp.int32(0), jnp.int32(31), bit_body, v0)
    cut_sc[...] = jnp.full((SUBLANES, QB), INT_MAX, jnp.int32)
    n_ge = count(lambda kk, base: kk >= thr)

    @pl.when(jnp.max(n_ge) > kf)
    def _():
        need = kf - count(lambda kk, base: kk > thr)

        def cut_body(b, c):
            cand = c | lax.shift_left(jnp.int32(1), (12 - b).astype(jnp.int32))
            n = count(lambda kk, base: jnp.where(kk == thr, base + krow3, jnp.int32(INT_MAX)) < cand)
            return jnp.where(n < need, cand, c)

        cut_sc[...] = lax.fori_loop(jnp.int32(0), jnp.int32(13), cut_body, zero)

    thr1 = thr[0:1, :]
    cut1 = cut_sc[0:1, :]

    def bias_chunk(j):
        kk = keys_sc[j]
        start = chunk_start(j)
        sel = (kk > thr1) | ((kk == thr1) & (start + krow <= cut1))
        bt = jnp.where(sel & key_valid(j, start), 0.0, -1e30)
        bias_sc[j] = jnp.concatenate([bt[c * QB:(c + 1) * QB].T for c in range(KC // QB)], axis=1)

    _unrolled_loop(nch, bias_chunk, CHUNKS_PER_TRIP)

    m_sc[...] = jnp.full(m_sc.shape, -1e9, F32)
    l_sc[...] = jnp.zeros(l_sc.shape, F32)
    acc_sc[...] = jnp.zeros(acc_sc.shape, F32)

    def att_step(start, width, bias, groups):
        bias = jnp.concatenate([bias] * ATT_GROUP, axis=0)
        for g in groups:
            cg = g // 2
            kj = k_ref[0, pl.ds(start, width), cg * LANES:(cg + 1) * LANES]
            vj = v_ref[0, pl.ds(start, width), cg * 2 * LANES:(cg + 1) * 2 * LANES]
            s = lax.dot_general(q_ref[0, g], kj, (((1,), (1,)), ((), ())), preferred_element_type=F32) + bias
            m_old = m_sc[g]
            m_new = jnp.maximum(m_old, jnp.max(s, axis=1, keepdims=True))
            alpha = jnp.exp2(m_old - m_new)
            p = jnp.exp2(s - jnp.concatenate([m_new] * (width // LANES), axis=1)).astype(BF16)
            pv = jnp.dot(p, vj, preferred_element_type=F32)
            l_sc[g] = alpha * l_sc[g] + pv[:, LANES:]
            acc_sc[g] = alpha * acc_sc[g] + pv[:, :LANES]
            m_sc[g] = m_new

    npair = jnp.minimum(nch // 2, seq_pad // (2 * KC))

    def single_body(j, carry):
        att_step(chunk_start(j), KC, bias_sc[j], range(ATT_KV_HEADS))
        return carry

    def pair_step(j):
        bias = jnp.concatenate([bias_sc[2 * j], bias_sc[2 * j + 1]], axis=1)
        att_step(pl.multiple_of(j * 2 * KC, 2 * KC), 2 * KC, bias, range(ATT_KV_HEADS))

    _unrolled_loop(npair, pair_step, ATT_STEPS_PER_TRIP)
    lax.fori_loop(2 * npair, nch, single_body, 0)

    for g in range(ATT_KV_HEADS):
        og = acc_sc[g] / l_sc[g]
        for pair in range(2):
            even = og[(2 * pair) * QB:(2 * pair + 1) * QB]
            odd = og[(2 * pair + 1) * QB:(2 * pair + 2) * QB]
            if g % 2:
                even = pltpu.roll(even, 64, 1)
            else:
                odd = pltpu.roll(odd, 64, 1)
            c = 2 * g + pair
            o_ref[0, :, c * LANES:(c + 1) * LANES] = jnp.where(lane1 < 64, even, odd).astype(BF16)


def _dsa_core(q, k, v, iq, ik, iw, top_k):
    B, seq_pad = k.shape[0], k.shape[1]
    nqb = seq_pad // Q_BLOCK
    nkc = -(-seq_pad // KEY_CHUNK)
    rows = ATT_GROUP * Q_BLOCK
    qblock = lambda a: pl.BlockSpec((1,) + a.shape[1:], lambda b, i: (b * nqb + i,) + (0,) * (a.ndim - 1))
    whole = lambda n: pl.BlockSpec((1, seq_pad, n), lambda b, i: (b, 0, 0))
    return pl.pallas_call(
        functools.partial(_dsa_core_kernel, top_k=top_k, seq_pad=seq_pad),
        grid=(B, nqb),
        in_specs=[qblock(q), whole(256), whole(512), qblock(iq), whole(IDX_DIM), qblock(iw)],
        out_specs=pl.BlockSpec((1, Q_BLOCK, D_MODEL), lambda b, i: (b, i, 0)),
        out_shape=jax.ShapeDtypeStruct((B, seq_pad, D_MODEL), BF16),
        scratch_shapes=[
            pltpu.VMEM((nkc + 1, KEY_CHUNK, Q_BLOCK), jnp.int32),
            pltpu.VMEM((nkc, Q_BLOCK, KEY_CHUNK), F32),
            pltpu.VMEM((SUBLANES, Q_BLOCK), jnp.int32),
            pltpu.VMEM((ATT_KV_HEADS, rows, LANES), F32),
            pltpu.VMEM((ATT_KV_HEADS, rows, LANES), F32),
            pltpu.VMEM((ATT_KV_HEADS, rows, LANES), F32),
        ],
        compiler_params=_cparams(("arbitrary", "arbitrary")),
        name="dsa_core",
    )(q, k, v, iq, ik, iw)


def _out_proj_kernel(h_ref, y_ref, w_ref, o_ref):
    o_ref[...] = h_ref[...] + jnp.dot(y_ref[...], w_ref[...], preferred_element_type=F32)


def _out_proj(h, y, w, tm):
    T = h.shape[0]
    row = lambda i: (i, 0)
    return pl.pallas_call(
        _out_proj_kernel,
        grid=(T // tm,),
        in_specs=[pl.BlockSpec((tm, D_MODEL), row), pl.BlockSpec((tm, D_MODEL), row),
                  pl.BlockSpec((D_MODEL, D_MODEL), lambda i: (0, 0))],
        out_specs=pl.BlockSpec((tm, D_MODEL), row),
        out_shape=jax.ShapeDtypeStruct((T, D_MODEL), F32),
        compiler_params=_cparams(("arbitrary",)),
        name="out_proj",
    )(h, y, w)


def _mlstm_in_kernel(h_ref, g_ref, w_ref, q_ref, k_ref, v_ref, o_ref, gate_ref):
    xn = _rms_rows(h_ref[...], g_ref[...]).astype(BF16)
    z = jnp.dot(xn, w_ref[...], preferred_element_type=F32)
    q_ref[...] = z[:, 0:512].astype(BF16)
    k_ref[...] = (z[:, 512:1024] * (M_QK_DIM ** -0.5)).astype(BF16)
    v_ref[...] = z[:, 1024:2048].astype(BF16)
    o_ref[...] = z[:, 2048:3072]
    gate_ref[...] = z[:, 3072:3200]


def _mlstm_in_proj(h, gain, w, tm):
    T = h.shape[0]
    row = lambda i: (i, 0)
    fixed = lambda i: (0, 0)
    outs = [(512, BF16), (512, BF16), (1024, BF16), (1024, F32), (128, F32)]
    return pl.pallas_call(
        _mlstm_in_kernel,
        grid=(T // tm,),
        in_specs=[pl.BlockSpec((tm, D_MODEL), row), pl.BlockSpec((1, D_MODEL), fixed),
                  pl.BlockSpec((D_MODEL, MLSTM_IN_PAD), fixed)],
        out_specs=[pl.BlockSpec((tm, n), row) for n, _ in outs],
        out_shape=[jax.ShapeDtypeStruct((T, n), dt) for n, dt in outs],
        compiler_params=_cparams(("arbitrary",)),
        name="mlstm_in_proj",
    )(h, gain, w)


def _mlstm_kernel(q_ref, k_ref, v_ref, o_ref, gate_ref, bias_ref, gout_ref, y_ref, ct_sc, n_sc, m_sc):
    C = M_CHUNK

    @pl.when(pl.program_id(1) == 0)
    def _():
        ct_sc[...] = jnp.zeros(ct_sc.shape, F32)
        n_sc[...] = jnp.zeros(n_sc.shape, F32)
        m_sc[...] = jnp.zeros(m_sc.shape, F32)

    lane8 = lax.broadcasted_iota(jnp.int32, (8, C), 1)
    t_idx = lax.broadcasted_iota(jnp.int32, (C, C), 0)
    s_idx = lax.broadcasted_iota(jnp.int32, (C, C), 1)
    for bi in range(q_ref.shape[0]):
        pre = gate_ref[bi].T[0:8, :] + bias_ref[...]
        capped = GATE_CAP * jnp.tanh(pre / GATE_CAP)
        log_f = -(jnp.maximum(-capped, 0.0) + jnp.log1p(jnp.exp(-jnp.abs(capped))))
        b = log_f
        sh = 1
        while sh < C:
            b = b + jnp.where(lane8 >= sh, pltpu.roll(b, sh, 1), 0.0)
            sh *= 2
        stacked = jnp.concatenate([b[4:8], capped[0:4]], axis=0)
        cols = jnp.concatenate([stacked, jnp.zeros((C - 8, C), F32)], axis=0).T

        for h in range(M_HEADS):
            st = bi * M_HEADS + h
            qh = q_ref[bi, :, h * M_QK_DIM:(h + 1) * M_QK_DIM]
            kh = k_ref[bi, :, h * M_QK_DIM:(h + 1) * M_QK_DIM]
            vh = v_ref[bi, :, h * M_V_DIM:(h + 1) * M_V_DIM]
            b_row, li_row = stacked[h:h + 1, :], stacked[4 + h:5 + h, :]
            b_col, li_col = cols[:, h:h + 1], cols[:, 4 + h:5 + h]
            m_st = m_sc[st:st + 1, 0:1]
            dmat = jnp.where(s_idx <= t_idx, b_col - b_row + li_row, -jnp.inf)
            inter = b_col + m_st
            m_t = jnp.maximum(inter, jnp.max(dmat, axis=1, keepdims=True))
            w_inter = jnp.exp(inter - m_t)
            qk = lax.dot_general(qh, kh, (((1,), (1,)), ((), ())), preferred_element_type=F32)
            s = qk * jnp.exp(dmat - m_t)
            ct = ct_sc[st]
            num = (w_inter * jnp.dot(qh, ct.astype(BF16), preferred_element_type=F32)
                   + jnp.dot(s.astype(BF16), vh, preferred_element_type=F32))
            qn = jnp.sum(qh.astype(F32) * n_sc[st:st + 1, :], axis=1, keepdims=True)
            den = w_inter * qn + jnp.sum(s, axis=1, keepdims=True)
            hout = num / jnp.maximum(jnp.abs(den), jnp.exp(-m_t))

            b_last = b_row[:, C - 1:C]
            m_new = jnp.maximum(b_last + m_st, jnp.max(b_last - b_row + li_row, axis=1, keepdims=True))
            decay = jnp.exp(b_last + m_st - m_new)
            wk = jnp.exp(b_last - b_col + li_col - m_new)
            kw = kh.astype(F32) * wk
            ct_sc[st] = decay * ct + jnp.dot(kw.T.astype(BF16), vh, preferred_element_type=F32)
            n_sc[st:st + 1, :] = decay * n_sc[st:st + 1, :] + jnp.sum(kw, axis=0, keepdims=True)
            m_sc[st:st + 1, :] = jnp.broadcast_to(m_new, (1, LANES))

            cs = slice(h * M_V_DIM, (h + 1) * M_V_DIM)
            hn = _rms_rows(hout, gout_ref[:, cs])
            y_ref[bi, :, cs] = (hn * jax.nn.sigmoid(o_ref[bi, :, cs])).astype(BF16)


def _mlstm_core(q, k, v, o, gates, bias8, gout):
    B, seq_pad = q.shape[0], q.shape[1]
    nb = M_BATCH_ROWS
    C = M_CHUNK
    blk = lambda n: pl.BlockSpec((nb, C, n), lambda b, c: (b, c, 0))
    fixed = lambda b, c: (0, 0)
    return pl.pallas_call(
        _mlstm_kernel,
        grid=(B // nb, seq_pad // C),
        in_specs=[blk(512), blk(512), blk(1024), blk(1024), blk(LANES),
                  pl.BlockSpec((8, 1), fixed), pl.BlockSpec((1, D_MODEL), fixed)],
        out_specs=blk(D_MODEL),
        out_shape=jax.ShapeDtypeStruct((B, seq_pad, D_MODEL), BF16),
        scratch_shapes=[pltpu.VMEM((nb * M_HEADS, M_QK_DIM, M_V_DIM), F32),
                        pltpu.VMEM((nb * M_HEADS, M_QK_DIM), F32),
                        pltpu.VMEM((nb * M_HEADS, LANES), F32)],
        compiler_params=_cparams(("arbitrary", "arbitrary")),
        name="mlstm_core",
    )(q, k, v, o, gates, bias8, gout)


def _ffn_kernel(h_ref, y_ref, wo_ref, g_ref, wg_ref, wu_ref, wd_ref, o_ref, xn_sc, acc_sc):
    f = pl.program_id(1)

    @pl.when(f == 0)
    def _():
        x = h_ref[...] + jnp.dot(y_ref[...], wo_ref[...], preferred_element_type=F32)
        xn_sc[...] = _rms_rows(x, g_ref[...]).astype(BF16)
        acc_sc[...] = x

    xn = xn_sc[...]
    gate = jnp.dot(xn, wg_ref[...], preferred_element_type=F32)
    up = jnp.dot(xn, wu_ref[...], preferred_element_type=F32)
    act = (gate * jax.nn.sigmoid(gate) * up).astype(BF16)
    acc_sc[...] += jnp.dot(act, wd_ref[...], preferred_element_type=F32)

    @pl.when(f == pl.num_programs(1) - 1)
    def _():
        o_ref[...] = acc_sc[...]


def _ffn(h, y, wo, gain, wg, wu, wd, tm, tf):
    T = h.shape[0]
    return pl.pallas_call(
        _ffn_kernel,
        grid=(T // tm, D_FF // tf),
        in_specs=[pl.BlockSpec((tm, D_MODEL), lambda i, f: (i, 0)),
                  pl.BlockSpec((tm, D_MODEL), lambda i, f: (i, 0)),
                  pl.BlockSpec((D_MODEL, D_MODEL), lambda i, f: (0, 0)),
                  pl.BlockSpec((1, D_MODEL), lambda i, f: (0, 0)),
                  pl.BlockSpec((D_MODEL, tf), lambda i, f: (0, f)),
                  pl.BlockSpec((D_MODEL, tf), lambda i, f: (0, f)),
                  pl.BlockSpec((tf, D_MODEL), lambda i, f: (f, 0))],
        out_specs=pl.BlockSpec((tm, D_MODEL), lambda i, f: (i, 0)),
        out_shape=jax.ShapeDtypeStruct((T, D_MODEL), F32),
        scratch_shapes=[pltpu.VMEM((tm, D_MODEL), BF16), pltpu.VMEM((tm, D_MODEL), F32)],
        compiler_params=_cparams(("arbitrary", "arbitrary")),
        name="ffn_dense",
    )(h, y, wo, gain, wg, wu, wd)


def _moe_kernel(h_ref, g_ref, wr_ref, wg_ref, wu_ref, wd_ref, o_ref,
                xn_sc, xs_sc, ys_sc, posc_sc, posr_sc, comb_sc, nblk_sc):
    e = pl.program_id(1)
    f = pl.program_id(2)
    tm = h_ref.shape[0]
    SB = _moe_slot_block(tm)
    TB = _moe_token_block(tm)
    lane = lax.broadcasted_iota(jnp.int32, (tm, LANES), 1)

    @pl.when((e == 0) & (f == 0))
    def _():
        x = h_ref[...]
        xn = _rms_rows(x, g_ref[...])
        xn_sc[...] = xn.astype(BF16)
        o_ref[...] = x
        logits = jnp.dot(xn, wr_ref[...], preferred_element_type=F32, precision=lax.Precision.HIGHEST)
        logits = jnp.where(lane < N_EXPERTS, logits, -jnp.inf)
        m1 = jnp.max(logits, axis=1, keepdims=True)
        i1 = jnp.min(jnp.where(logits == m1, lane, LANES), axis=1, keepdims=True)
        rest = jnp.where(lane == i1, -jnp.inf, logits)
        m2 = jnp.max(rest, axis=1, keepdims=True)
        i2 = jnp.min(jnp.where(rest == m2, lane, LANES), axis=1, keepdims=True)
        e2 = jnp.exp(m2 - m1)
        g1 = 1.0 / (1.0 + e2)
        comb_sc[...] = jnp.where(lane == i1, g1, jnp.where(lane == i2, e2 * g1, 0.0))
        member = jnp.where(lane == i1, 1.0, jnp.where(lane == i2, 1.0, 0.0))
        tri = jnp.where(lax.broadcasted_iota(jnp.int32, (TB, TB), 0) >= lax.broadcasted_iota(jnp.int32, (TB, TB), 1),
                        1.0, 0.0).astype(BF16)
        carry = jnp.zeros((1, LANES), F32)
        for b in range(tm // TB):
            mb = member[b * TB:(b + 1) * TB]
            incl = jnp.dot(tri, mb.astype(BF16), preferred_element_type=F32)
            posc_sc[b * TB:(b + 1) * TB, :] = jnp.where(mb > 0.0, incl - 1.0 + carry, -1.0)
            carry = carry + incl[TB - 1:TB, :]
        for b in range(tm // LANES):
            posr_sc[:, b * LANES:(b + 1) * LANES] = posc_sc[b * LANES:(b + 1) * LANES, :].T
        lane_row = lax.broadcasted_iota(jnp.int32, (1, LANES), 1)
        for x in range(N_EXPERTS):
            count = jnp.sum(jnp.where(lane_row == x, carry, 0.0))
            nblk_sc[x] = ((count + (SB - 1.0)) * (1.0 / SB)).astype(jnp.int32)

    nblk = nblk_sc[e]

    @pl.when(f == 0)
    def _():
        slots_of_tokens = posr_sc[pl.ds(e, 1), :]

        def body(r, carry):
            row0 = pl.multiple_of(r * SB, 16)
            slot = (r * SB + lax.broadcasted_iota(jnp.int32, (SB, tm), 0)).astype(F32)
            onehot = jnp.where(slots_of_tokens == slot, 1.0, 0.0).astype(BF16)
            xs_sc[pl.ds(row0, SB), :] = jnp.dot(onehot, xn_sc[...], preferred_element_type=F32).astype(BF16)
            ys_sc[pl.ds(row0, SB), :] = jnp.zeros((SB, D_MODEL), F32)
            return carry

        lax.fori_loop(0, nblk, body, 0)

    def ffn_rows(row0, nrows):
        xs = xs_sc[pl.ds(row0, nrows), :]
        gate = jnp.dot(xs, wg_ref[0], preferred_element_type=F32)
        up = jnp.dot(xs, wu_ref[0], preferred_element_type=F32)
        act = (gate * jax.nn.sigmoid(gate) * up).astype(BF16)
        ys_sc[pl.ds(row0, nrows), :] += jnp.dot(act, wd_ref[0], preferred_element_type=F32)

    def pair_body(r, carry):
        ffn_rows(pl.multiple_of(r * 2 * SB, 16), 2 * SB)
        return carry

    lax.fori_loop(0, nblk // 2, pair_body, 0)

    @pl.when(nblk % 2 == 1)
    def _():
        ffn_rows(pl.multiple_of((nblk - 1) * SB, 16), SB)

    @pl.when(f == pl.num_programs(2) - 1)
    def _():
        on_e = lane == e
        gate_e = jnp.sum(jnp.where(on_e, comb_sc[...], 0.0), axis=1, keepdims=True)
        slot_e = jnp.sum(jnp.where(on_e, posc_sc[...], 0.0), axis=1, keepdims=True)

        def body(r, carry):
            row0 = pl.multiple_of(r * SB, 16)
            ys = ys_sc[pl.ds(row0, SB), :].astype(BF16)
            slot = (r * SB + lax.broadcasted_iota(jnp.int32, (TB, SB), 1)).astype(F32)
            for tb in range(tm // TB):
                rows = slice(tb * TB, (tb + 1) * TB)
                onehot = jnp.where(slot_e[rows] == slot, 1.0, 0.0).astype(BF16)
                o_ref[rows, :] += gate_e[rows] * jnp.dot(onehot, ys, preferred_element_type=F32)
            return carry

        lax.fori_loop(0, nblk, body, 0)


def _moe_slot_block(tm):
    return MOE_SLOT_BLOCK if tm > MOE_SLOT_BLOCK else tm


def _moe_token_block(tm):
    return MOE_TOKEN_BLOCK if tm % MOE_TOKEN_BLOCK == 0 else tm


def _moe(h, gain, wr, wg, wu, wd, tm, tf):
    T = h.shape[0]
    sb = _moe_slot_block(tm)
    cap = -(-tm // sb) * sb
    return pl.pallas_call(
        _moe_kernel,
        grid=(T // tm, N_EXPERTS, D_FF // tf),
        in_specs=[pl.BlockSpec((tm, D_MODEL), lambda i, e, f: (i, 0), pipeline_mode=pl.Buffered(1)),
                  pl.BlockSpec((1, D_MODEL), lambda i, e, f: (0, 0)),
                  pl.BlockSpec((D_MODEL, LANES), lambda i, e, f: (0, 0)),
                  pl.BlockSpec((1, D_MODEL, tf), lambda i, e, f: (e, 0, f)),
                  pl.BlockSpec((1, D_MODEL, tf), lambda i, e, f: (e, 0, f)),
                  pl.BlockSpec((1, tf, D_MODEL), lambda i, e, f: (e, f, 0))],
        out_specs=pl.BlockSpec((tm, D_MODEL), lambda i, e, f: (i, 0)),
        out_shape=jax.ShapeDtypeStruct((T, D_MODEL), F32),
        scratch_shapes=[pltpu.VMEM((tm, D_MODEL), BF16), pltpu.VMEM((cap, D_MODEL), BF16),
                        pltpu.VMEM((cap, D_MODEL), F32), pltpu.VMEM((tm, LANES), F32),
                        pltpu.VMEM((LANES, tm), F32), pltpu.VMEM((tm, LANES), F32),
                        pltpu.SMEM((N_EXPERTS,), jnp.int32)],
        compiler_params=_cparams(("arbitrary", "arbitrary", "arbitrary")),
        name="moe_sparse",
    )(h, gain, wr, wg, wu, wd)


def _rope_tables(seq_pad):
    pos = jnp.arange(seq_pad, dtype=F32)[:, None]
    lane = np.arange(LANES)

    def table(head_dim):
        half = head_dim // 2
        inv = ROPE_THETA ** (-jnp.arange(half, dtype=F32) / half)
        d = lane % head_dim
        ang = pos * inv[d % half][None, :]
        return jnp.cos(ang), jnp.sin(ang), jnp.asarray(d < half)[None, :]

    c64, s64, lo64 = table(ATT_HEAD_DIM)
    c128, s128, lo128 = table(IDX_DIM)
    return (c64, jnp.where(lo64, -s64, 0.0), jnp.where(lo64, 0.0, s64), c128, jnp.where(lo128, -s128, s128))


def _pad_cols(w, n):
    return jnp.pad(w, ((0, 0), (0, n - w.shape[1])))


def kernel(x, meta, norm_mixer, norm_ffn, dsa_w_in, dsa_q_norm, dsa_k_norm, dsa_w_out, mlstm_w_in, mlstm_b_i,
           mlstm_b_f, mlstm_out_norm, mlstm_w_out, ffn_w_gate, ffn_w_up, ffn_w_down, moe_router, moe_w_gate,
           moe_w_up, moe_w_down):
    B, S, D = x.shape
    L = S + N_META
    top_k = min(TOPK_MAX, S // 4)
    seq_pad = max(-(-L // Q_BLOCK) * Q_BLOCK, KEY_CHUNK)
    T = B * seq_pad
    tm_proj = 384 if seq_pad % 384 == 0 else Q_BLOCK
    tm_ffn = 768 if T % 768 == 0 else Q_BLOCK
    tm_moe = 1536 if T % 1536 == 0 else Q_BLOCK
    tf = 512

    h = jnp.concatenate([jnp.broadcast_to(meta[None].astype(x.dtype), (B, N_META, D)), x,
                         jnp.zeros((B, seq_pad - L, D), x.dtype)], axis=1).reshape(T, D)
    tabs = _rope_tables(seq_pad)
    depth = norm_mixer.shape[0]

    for i in range(depth):
        j = i // 2
        gain_m = norm_mixer[i][None, :]
        gain_f = norm_ffn[i][None, :]
        if i % 2 == 0:
            w_in = _pad_cols(dsa_w_in[j], DSA_IN_PAD).astype(BF16)
            gq = jnp.tile(dsa_q_norm[j], 2)[None, :]
            gk = jnp.tile(dsa_k_norm[j], 2)[None, :]
            q, k, v, iq, ik, iw = _dsa_in_proj(h, gain_m, w_in, gq, gk, tabs, seq_pad, tm_proj)
            k, v, ik = [a.reshape(B, seq_pad, a.shape[-1]) for a in (k, v, ik)]
            att = _dsa_core(q, k, v, iq, ik, iw, top_k)
            h = _ffn(h, att.reshape(T, D), dsa_w_out[j].astype(BF16), gain_f, ffn_w_gate[j].astype(BF16),
                     ffn_w_up[j].astype(BF16), ffn_w_down[j].astype(BF16), tm_ffn, tf)
        else:
            w_in = _pad_cols(mlstm_w_in[j], MLSTM_IN_PAD).astype(BF16)
            q, k, v, o, gates = _mlstm_in_proj(h, gain_m, w_in, tm_proj)
            bias8 = jnp.concatenate([mlstm_b_i[j], mlstm_b_f[j]])[:, None]
            y = _mlstm_core(*[a.reshape(B, seq_pad, a.shape[-1]) for a in (q, k, v, o, gates)], bias8,
                            mlstm_out_norm[j][None, :])
            h = _out_proj(h, y.reshape(T, D), mlstm_w_out[j].astype(BF16), tm_proj)
            wr = _pad_cols(moe_router[j], LANES)
            h = _moe(h, gain_f, wr, moe_w_gate[j].astype(BF16), moe_w_up[j].astype(BF16),
                     moe_w_down[j].astype(BF16), tm_moe, tf)
    return h.reshape(B, seq_pad, D)[:, N_META:L]
```

```python
import functools

import jax
import jax.numpy as jnp
import numpy as np
from jax import lax
from jax.experimental import pallas as pl
from jax.experimental.pallas import tpu as pltpu

F32 = jnp.float32
BF16 = jnp.bfloat16

D_MODEL = 1024
N_META = 16
RMS_EPS = 1e-6
ROPE_THETA = 10000.0
ATT_HEADS = 16
ATT_KV_HEADS = 4
ATT_HEAD_DIM = 64
ATT_GROUP = 4
IDX_HEADS = 8
IDX_DIM = 128
TOPK_MAX = 256
M_HEADS = 4
M_QK_DIM = 128
M_V_DIM = 256
GATE_CAP = 15.0
D_FF = 3584
N_EXPERTS = 8

LANES = 128
SUBLANES = 8
Q_BLOCK = 128
KEY_CHUNK = 256
ATT_STEPS_PER_TRIP = 4
CHUNKS_PER_TRIP = 4
M_CHUNK = 128
M_BATCH_ROWS = 1
MOE_SLOT_BLOCK = 208
MOE_TOKEN_BLOCK = 256
VMEM_LIMIT = 52 * 1024 * 1024
INT_MIN = -(2 ** 31)
INT_MAX = 2 ** 31 - 1
LOG2E = 1.4426950408889634

DSA_IN_PAD = 2816
MLSTM_IN_PAD = 3200


def _cparams(sem):
    return pltpu.CompilerParams(dimension_semantics=sem, vmem_limit_bytes=VMEM_LIMIT)


def _unrolled_loop(n, fn, unroll):
    def body(t, carry):
        for u in range(unroll):
            fn(unroll * t + u)
        return carry

    lax.fori_loop(0, n // unroll, body, 0)
    rest = n % unroll
    done = n - rest
    run = unroll // 2
    while run >= 1:
        @pl.when((rest & run) != 0)
        def _(run=run, done=done):
            for u in range(run):
                fn(done + u)

        done = done + (rest & run)
        run //= 2


def _rms_rows(x, gain):
    ms = jnp.mean(x * x, axis=-1, keepdims=True)
    return x * lax.rsqrt(ms + RMS_EPS) * gain


def _dsa_in_kernel(h_ref, g_ref, w_ref, gq_ref, gk_ref, c64_ref, sa64_ref, sb64_ref, c128_ref, s128_ref,
                   smat_ref, q_ref, k_ref, v_ref, iq_ref, ik_ref, iw_ref):
    xn = _rms_rows(h_ref[...], g_ref[...]).astype(BF16)
    z = jnp.dot(xn, w_ref[...], preferred_element_type=F32)
    c64, sa64, sb64 = c64_ref[...], sa64_ref[...], sb64_ref[...]
    c128, s128 = c128_ref[...], s128_ref[...]
    smat = smat_ref[...]

    def head_norm_rope(zc, gain):
        z2 = zc * zc
        hi = z2.astype(BF16)
        lo = (z2 - hi.astype(F32)).astype(BF16)
        ms = jnp.dot(hi, smat, preferred_element_type=F32) + jnp.dot(lo, smat, preferred_element_type=F32)
        y = zc * lax.rsqrt(ms + RMS_EPS) * gain
        return y * c64 + pltpu.roll(y, 96, 1) * sa64 + pltpu.roll(y, 32, 1) * sb64

    nsub = z.shape[0] // Q_BLOCK
    lane = lax.broadcasted_iota(jnp.int32, (z.shape[0], LANES), 1)
    for c in range(8):
        r = head_norm_rope(z[:, c * LANES:(c + 1) * LANES], gq_ref[...]) * (ATT_HEAD_DIM ** -0.5 * LOG2E)
        swapped = pltpu.roll(r, 64, 1)
        for half in range(2):
            h = 2 * c + half
            g, rr = h // ATT_GROUP, h % ATT_GROUP
            keep = (lane >= 64) if g % 2 else (lane < 64)
            val = jnp.where(keep, r if half == g % 2 else swapped, 0.0).astype(BF16)
            for t in range(nsub):
                q_ref[t, g, rr * Q_BLOCK:(rr + 1) * Q_BLOCK, :] = val[t * Q_BLOCK:(t + 1) * Q_BLOCK]
    for c in range(2):
        r = head_norm_rope(z[:, 1024 + c * LANES:1024 + (c + 1) * LANES], gk_ref[...])
        k_ref[:, c * LANES:(c + 1) * LANES] = r.astype(BF16)
    ones = jnp.ones((z.shape[0], LANES), BF16)
    for c in range(2):
        v_ref[:, 2 * c * LANES:(2 * c + 1) * LANES] = z[:, 1280 + c * LANES:1280 + (c + 1) * LANES].astype(BF16)
        v_ref[:, (2 * c + 1) * LANES:(2 * c + 2) * LANES] = ones
    for c in range(9):
        zc = z[:, 1536 + c * LANES:1536 + (c + 1) * LANES]
        r = zc * c128 + pltpu.roll(zc, 64, 1) * s128
        if c < 8:
            for t in range(nsub):
                iq_ref[t, :, c * Q_BLOCK:(c + 1) * Q_BLOCK] = r[t * Q_BLOCK:(t + 1) * Q_BLOCK].T.astype(BF16)
        else:
            ik_ref[...] = r.astype(BF16)
    iw = z[:, 2688:2816] * (IDX_HEADS ** -0.5 * IDX_DIM ** -0.5)
    for t in range(nsub):
        iw_ref[t] = iw[t * Q_BLOCK:(t + 1) * Q_BLOCK].T


def _dsa_in_proj(h, gain, w, gq, gk, tabs, seq_pad, tm):
    T = h.shape[0]
    nt = seq_pad // tm
    row = lambda i: (i, 0)
    fixed = lambda i: (0, 0)
    pos = lambda i: (i % nt, 0)
    tab_spec = pl.BlockSpec((tm, LANES), pos)
    smat = jnp.asarray(np.kron(np.eye(2), np.full((64, 64), 1.0 / 64)), BF16)
    nsub = tm // Q_BLOCK
    rows = ATT_GROUP * Q_BLOCK
    outs = [((ATT_KV_HEADS, rows, LANES), BF16), (256, BF16), (512, BF16),
            ((IDX_DIM, IDX_HEADS * Q_BLOCK), BF16), (128, BF16), ((LANES, Q_BLOCK), F32)]

    def spec(o):
        if isinstance(o[0], int):
            return pl.BlockSpec((tm, o[0]), row)
        return pl.BlockSpec((nsub,) + o[0], lambda i: (i,) + (0,) * len(o[0]))

    def shape(o):
        if isinstance(o[0], int):
            return jax.ShapeDtypeStruct((T, o[0]), o[1])
        return jax.ShapeDtypeStruct((T // Q_BLOCK,) + o[0], o[1])

    return pl.pallas_call(
        _dsa_in_kernel,
        grid=(T // tm,),
        in_specs=[pl.BlockSpec((tm, D_MODEL), row), pl.BlockSpec((1, D_MODEL), fixed),
                  pl.BlockSpec((D_MODEL, DSA_IN_PAD), fixed), pl.BlockSpec((1, LANES), fixed),
                  pl.BlockSpec((1, LANES), fixed), tab_spec, tab_spec, tab_spec, tab_spec, tab_spec,
                  pl.BlockSpec((LANES, LANES), fixed)],
        out_specs=[spec(o) for o in outs],
        out_shape=[shape(o) for o in outs],
        compiler_params=_cparams(("arbitrary",)),
        name="dsa_in_proj",
    )(h, gain, w, gq, gk, *tabs, smat)


def _dsa_core_kernel(q_ref, k_ref, v_ref, iq_ref, ik_ref, iw_ref, o_ref,
                     keys_sc, bias_sc, cut_sc, m_sc, l_sc, acc_sc, *, top_k, seq_pad):
    i = pl.program_id(1)
    QB, KC = Q_BLOCK, KEY_CHUNK
    nch = (i * QB + QB + KC - 1) // KC
    kf = float(top_k)
    lane1 = lax.broadcasted_iota(jnp.int32, (QB, LANES), 1)

    qpos = i * QB + lax.broadcasted_iota(jnp.int32, (KC, QB), 1)
    krow = lax.broadcasted_iota(jnp.int32, (KC, QB), 0)

    def chunk_start(j):
        return pl.multiple_of(jnp.minimum(j * KC, seq_pad - KC), LANES)

    def key_valid(j, start):
        kidx = start + krow
        return (kidx <= qpos) & (kidx >= j * KC)

    def score_chunk(j):
        start = chunk_start(j)
        s = jnp.dot(ik_ref[0, pl.ds(start, KC), :], iq_ref[0], preferred_element_type=F32)
        acc = jnp.zeros((KC, QB), F32)
        for h in range(IDX_HEADS):
            acc = acc + jnp.maximum(s[:, h * QB:(h + 1) * QB], 0.0) * iw_ref[0, h:h + 1, :]
        bits = pltpu.bitcast(acc, jnp.int32)
        key = jnp.where(bits >= 0, bits, bits ^ jnp.int32(INT_MAX))
        keys_sc[j] = jnp.where(key_valid(j, start), key, jnp.int32(INT_MIN))

    _unrolled_loop(nch, score_chunk, CHUNKS_PER_TRIP)

    tiles = (KC // SUBLANES, SUBLANES, QB)
    krow3 = (lax.broadcasted_iota(jnp.int32, tiles, 0) * SUBLANES + lax.broadcasted_iota(jnp.int32, tiles, 1))

    keys_sc[nch] = jnp.full((KC, QB), INT_MIN, jnp.int32)

    def count(pred):
        def body(j2, acc):
            for j in (2 * j2, 2 * j2 + 1):
                kk = keys_sc[j].reshape(tiles)
                hit = jnp.where(pred(kk, chunk_start(j)), jnp.float32(1.0), jnp.float32(0.0))
                while hit.shape[0] > 1:
                    half = hit.shape[0] // 2
                    hit = hit[:half] + hit[half:]
                acc = acc + hit[0]
            return acc
        acc = lax.fori_loop(0, (nch + 1) // 2, body, jnp.zeros((SUBLANES, QB), F32))
        for shift in (4, 2, 1):
            acc = acc + pltpu.roll(acc, shift, 0)
        return acc

    zero = jnp.zeros((SUBLANES, QB), jnp.int32)
    v0 = jnp.where(count(lambda kk, base: kk >= zero) >= kf, zero, jnp.int32(INT_MIN))

    def bit_body(b, v):
        cand = v | lax.shift_left(jnp.int32(1), (30 - b).astype(jnp.int32))
        return jnp.where(count(lambda kk, base: kk >= cand) >= kf, cand, v)

    thr = lax.fori_loop(jnp.int32(0), jnp.int32(31), bit_body, v0)
    cut_sc[...] = jnp.full((SUBLANES, QB), INT_MAX, jnp.int32)
    n_ge = count(lambda kk, base: kk >= thr)

    @pl.when(jnp.max(n_ge) > kf)
    def _():
        need = kf - count(lambda kk, base: kk > thr)

        def cut_body(b, c):
            cand = c | lax.shift_left(jnp.int32(1), (12 - b).astype(jnp.int32))
            n = count(lambda kk, base: jnp.where(kk == thr, base + krow3, jnp.int32(INT_MAX)) < cand)
            return jnp.where(n < need, cand, c)

        cut_sc[...] = lax.fori_loop(jnp.int32(0), jnp.int32(13), cut_body, zero)

    thr1 = thr[0:1, :]
    cut1 = cut_sc[0:1, :]

    def bias_chunk(j):
        kk = keys_sc[j]
        start = chunk_start(j)
        sel = (kk > thr1) | ((kk == thr1) & (start + krow <= cut1))
        bt = jnp.where(sel & key_valid(j, start), 0.0, -1e30)
        bias_sc[j] = jnp.concatenate([bt[c * QB:(c + 1) * QB].T for c in range(KC // QB)], axis=1)

    _unrolled_loop(nch, bias_chunk, CHUNKS_PER_TRIP)

    m_sc[...] = jnp.full(m_sc.shape, -1e9, F32)
    l_sc[...] = jnp.zeros(l_sc.shape, F32)
    acc_sc[...] = jnp.zeros(acc_sc.shape, F32)

    def att_step(start, width, bias, groups):
        bias = jnp.concatenate([bias] * ATT_GROUP, axis=0)
        for g in groups:
            cg = g // 2
            kj = k_ref[0, pl.ds(start, width), cg * LANES:(cg + 1) * LANES]
            vj = v_ref[0, pl.ds(start, width), cg * 2 * LANES:(cg + 1) * 2 * LANES]
            s = lax.dot_general(q_ref[0, g], kj, (((1,), (1,)), ((), ())), preferred_element_type=F32) + bias
            m_old = m_sc[g]
            m_new = jnp.maximum(m_old, jnp.max(s, axis=1, keepdims=True))
            alpha = jnp.exp2(m_old - m_new)
            p = jnp.exp2(s - jnp.concatenate([m_new] * (width // LANES), axis=1)).astype(BF16)
            pv = jnp.dot(p, vj, preferred_element_type=F32)
            l_sc[g] = alpha * l_sc[g] + pv[:, LANES:]
            acc_sc[g] = alpha * acc_sc[g] + pv[:, :LANES]
            m_sc[g] = m_new

    npair = jnp.minimum(nch // 2, seq_pad // (2 * KC))

    def single_body(j, carry):
        att_step(chunk_start(j), KC, bias_sc[j], range(ATT_KV_HEADS))
        return carry

    def pair_step(j):
        bias = jnp.concatenate([bias_sc[2 * j], bias_sc[2 * j + 1]], axis=1)
        att_step(pl.multiple_of(j * 2 * KC, 2 * KC), 2 * KC, bias, range(ATT_KV_HEADS))

    _unrolled_loop(npair, pair_step, ATT_STEPS_PER_TRIP)
    lax.fori_loop(2 * npair, nch, single_body, 0)

    for g in range(ATT_KV_HEADS):
        og = acc_sc[g] / l_sc[g]
        for pair in range(2):
            even = og[(2 * pair) * QB:(2 * pair + 1) * QB]
            odd = og[(2 * pair + 1) * QB:(2 * pair + 2) * QB]
            if g % 2:
                even = pltpu.roll(even, 64, 1)
            else:
                odd = pltpu.roll(odd, 64, 1)
            c = 2 * g + pair
            o_ref[0, :, c * LANES:(c + 1) * LANES] = jnp.where(lane1 < 64, even, odd).astype(BF16)


def _dsa_core(q, k, v, iq, ik, iw, top_k):
    B, seq_pad = k.shape[0], k.shape[1]
    nqb = seq_pad // Q_BLOCK
    nkc = -(-seq_pad // KEY_CHUNK)
    rows = ATT_GROUP * Q_BLOCK
    qblock = lambda a: pl.BlockSpec((1,) + a.shape[1:], lambda b, i: (b * nqb + i,) + (0,) * (a.ndim - 1))
    whole = lambda n: pl.BlockSpec((1, seq_pad, n), lambda b, i: (b, 0, 0))
    return pl.pallas_call(
        functools.partial(_dsa_core_kernel, top_k=top_k, seq_pad=seq_pad),
        grid=(B, nqb),
        in_specs=[qblock(q), whole(256), whole(512), qblock(iq), whole(IDX_DIM), qblock(iw)],
        out_specs=pl.BlockSpec((1, Q_BLOCK, D_MODEL), lambda b, i: (b, i, 0)),
        out_shape=jax.ShapeDtypeStruct((B, seq_pad, D_MODEL), BF16),
        scratch_shapes=[
            pltpu.VMEM((nkc + 1, KEY_CHUNK, Q_BLOCK), jnp.int32),
            pltpu.VMEM((nkc, Q_BLOCK, KEY_CHUNK), F32),
            pltpu.VMEM((SUBLANES, Q_BLOCK), jnp.int32),
            pltpu.VMEM((ATT_KV_HEADS, rows, LANES), F32),
            pltpu.VMEM((ATT_KV_HEADS, rows, LANES), F32),
            pltpu.VMEM((ATT_KV_HEADS, rows, LANES), F32),
        ],
        compiler_params=_cparams(("arbitrary", "arbitrary")),
        name="dsa_core",
    )(q, k, v, iq, ik, iw)


def _out_proj_kernel(h_ref, y_ref, w_ref, o_ref):
    o_ref[...] = h_ref[...] + jnp.dot(y_ref[...], w_ref[...], preferred_element_type=F32)


def _out_proj(h, y, w, tm):
    T = h.shape[0]
    row = lambda i: (i, 0)
    return pl.pallas_call(
        _out_proj_kernel,
        grid=(T // tm,),
        in_specs=[pl.BlockSpec((tm, D_MODEL), row), pl.BlockSpec((tm, D_MODEL), row),
                  pl.BlockSpec((D_MODEL, D_MODEL), lambda i: (0, 0))],
        out_specs=pl.BlockSpec((tm, D_MODEL), row),
        out_shape=jax.ShapeDtypeStruct((T, D_MODEL), F32),
        compiler_params=_cparams(("arbitrary",)),
        name="out_proj",
    )(h, y, w)


def _mlstm_in_kernel(h_ref, g_ref, w_ref, q_ref, k_ref, v_ref, o_ref, gate_ref):
    xn = _rms_rows(h_ref[...], g_ref[...]).astype(BF16)
    z = jnp.dot(xn, w_ref[...], preferred_element_type=F32)
    q_ref[...] = z[:, 0:512].astype(BF16)
    k_ref[...] = (z[:, 512:1024] * (M_QK_DIM ** -0.5)).astype(BF16)
    v_ref[...] = z[:, 1024:2048].astype(BF16)
    o_ref[...] = z[:, 2048:3072]
    gate_ref[...] = z[:, 3072:3200]


def _mlstm_in_proj(h, gain, w, tm):
    T = h.shape[0]
    row = lambda i: (i, 0)
    fixed = lambda i: (0, 0)
    outs = [(512, BF16), (512, BF16), (1024, BF16), (1024, F32), (128, F32)]
    return pl.pallas_call(
        _mlstm_in_kernel,
        grid=(T // tm,),
        in_specs=[pl.BlockSpec((tm, D_MODEL), row), pl.BlockSpec((1, D_MODEL), fixed),
                  pl.BlockSpec((D_MODEL, MLSTM_IN_PAD), fixed)],
        out_specs=[pl.BlockSpec((tm, n), row) for n, _ in outs],
        out_shape=[jax.ShapeDtypeStruct((T, n), dt) for n, dt in outs],
        compiler_params=_cparams(("arbitrary",)),
        name="mlstm_in_proj",
    )(h, gain, w)


def _mlstm_kernel(q_ref, k_ref, v_ref, o_ref, gate_ref, bias_ref, gout_ref, y_ref, ct_sc, n_sc, m_sc):
    C = M_CHUNK

    @pl.when(pl.program_id(1) == 0)
    def _():
        ct_sc[...] = jnp.zeros(ct_sc.shape, F32)
        n_sc[...] = jnp.zeros(n_sc.shape, F32)
        m_sc[...] = jnp.zeros(m_sc.shape, F32)

    lane8 = lax.broadcasted_iota(jnp.int32, (8, C), 1)
    t_idx = lax.broadcasted_iota(jnp.int32, (C, C), 0)
    s_idx = lax.broadcasted_iota(jnp.int32, (C, C), 1)
    for bi in range(q_ref.shape[0]):
        pre = gate_ref[bi].T[0:8, :] + bias_ref[...]
        capped = GATE_CAP * jnp.tanh(pre / GATE_CAP)
        log_f = -(jnp.maximum(-capped, 0.0) + jnp.log1p(jnp.exp(-jnp.abs(capped))))
        b = log_f
        sh = 1
        while sh < C:
            b = b + jnp.where(lane8 >= sh, pltpu.roll(b, sh, 1), 0.0)
            sh *= 2
        stacked = jnp.concatenate([b[4:8], capped[0:4]], axis=0)
        cols = jnp.concatenate([stacked, jnp.zeros((C - 8, C), F32)], axis=0).T

        for h in range(M_HEADS):
            st = bi * M_HEADS + h
            qh = q_ref[bi, :, h * M_QK_DIM:(h + 1) * M_QK_DIM]
            kh = k_ref[bi, :, h * M_QK_DIM:(h + 1) * M_QK_DIM]
            vh = v_ref[bi, :, h * M_V_DIM:(h + 1) * M_V_DIM]
            b_row, li_row = stacked[h:h + 1, :], stacked[4 + h:5 + h, :]
            b_col, li_col = cols[:, h:h + 1], cols[:, 4 + h:5 + h]
            m_st = m_sc[st:st + 1, 0:1]
            dmat = jnp.where(s_idx <= t_idx, b_col - b_row + li_row, -jnp.inf)
            inter = b_col + m_st
            m_t = jnp.maximum(inter, jnp.max(dmat, axis=1, keepdims=True))
            w_inter = jnp.exp(inter - m_t)
            qk = lax.dot_general(qh, kh, (((1,), (1,)), ((), ())), preferred_element_type=F32)
            s = qk * jnp.exp(dmat - m_t)
            ct = ct_sc[st]
            num = (w_inter * jnp.dot(qh, ct.astype(BF16), preferred_element_type=F32)
                   + jnp.dot(s.astype(BF16), vh, preferred_element_type=F32))
            qn = jnp.sum(qh.astype(F32) * n_sc[st:st + 1, :], axis=1, keepdims=True)
            den = w_inter * qn + jnp.sum(s, axis=1, keepdims=True)
            hout = num / jnp.maximum(jnp.abs(den), jnp.exp(-m_t))

            b_last = b_row[:, C - 1:C]
            m_new = jnp.maximum(b_last + m_st, jnp.max(b_last - b_row + li_row, axis=1, keepdims=True))
            decay = jnp.exp(b_last + m_st - m_new)
            wk = jnp.exp(b_last - b_col + li_col - m_new)
            kw = kh.astype(F32) * wk
            ct_sc[st] = decay * ct + jnp.dot(kw.T.astype(BF16), vh, preferred_element_type=F32)
            n_sc[st:st + 1, :] = decay * n_sc[st:st + 1, :] + jnp.sum(kw, axis=0, keepdims=True)
            m_sc[st:st + 1, :] = jnp.broadcast_to(m_new, (1, LANES))

            cs = slice(h * M_V_DIM, (h + 1) * M_V_DIM)
            hn = _rms_rows(hout, gout_ref[:, cs])
            y_ref[bi, :, cs] = (hn * jax.nn.sigmoid(o_ref[bi, :, cs])).astype(BF16)


def _mlstm_core(q, k, v, o, gates, bias8, gout):
    B, seq_pad = q.shape[0], q.shape[1]
    nb = M_BATCH_ROWS
    C = M_CHUNK
    blk = lambda n: pl.BlockSpec((nb, C, n), lambda b, c: (b, c, 0))
    fixed = lambda b, c: (0, 0)
    return pl.pallas_call(
        _mlstm_kernel,
        grid=(B // nb, seq_pad // C),
        in_specs=[blk(512), blk(512), blk(1024), blk(1024), blk(LANES),
                  pl.BlockSpec((8, 1), fixed), pl.BlockSpec((1, D_MODEL), fixed)],
        out_specs=blk(D_MODEL),
        out_shape=jax.ShapeDtypeStruct((B, seq_pad, D_MODEL), BF16),
        scratch_shapes=[pltpu.VMEM((nb * M_HEADS, M_QK_DIM, M_V_DIM), F32),
                        pltpu.VMEM((nb * M_HEADS, M_QK_DIM), F32),
                        pltpu.VMEM((nb * M_HEADS, LANES), F32)],
        compiler_params=_cparams(("arbitrary", "arbitrary")),
        name="mlstm_core",
    )(q, k, v, o, gates, bias8, gout)


def _ffn_kernel(h_ref, y_ref, wo_ref, g_ref, wg_ref, wu_ref, wd_ref, o_ref, xn_sc, acc_sc):
    f = pl.program_id(1)

    @pl.when(f == 0)
    def _():
        x = h_ref[...] + jnp.dot(y_ref[...], wo_ref[...], preferred_element_type=F32)
        xn_sc[...] = _rms_rows(x, g_ref[...]).astype(BF16)
        acc_sc[...] = x

    xn = xn_sc[...]
    gate = jnp.dot(xn, wg_ref[...], preferred_element_type=F32)
    up = jnp.dot(xn, wu_ref[...], preferred_element_type=F32)
    act = (gate * jax.nn.sigmoid(gate) * up).astype(BF16)
    acc_sc[...] += jnp.dot(act, wd_ref[...], preferred_element_type=F32)

    @pl.when(f == pl.num_programs(1) - 1)
    def _():
        o_ref[...] = acc_sc[...]


def _ffn(h, y, wo, gain, wg, wu, wd, tm, tf):
    T = h.shape[0]
    return pl.pallas_call(
        _ffn_kernel,
        grid=(T // tm, D_FF // tf),
        in_specs=[pl.BlockSpec((tm, D_MODEL), lambda i, f: (i, 0)),
                  pl.BlockSpec((tm, D_MODEL), lambda i, f: (i, 0)),
                  pl.BlockSpec((D_MODEL, D_MODEL), lambda i, f: (0, 0)),
                  pl.BlockSpec((1, D_MODEL), lambda i, f: (0, 0)),
                  pl.BlockSpec((D_MODEL, tf), lambda i, f: (0, f)),
                  pl.BlockSpec((D_MODEL, tf), lambda i, f: (0, f)),
                  pl.BlockSpec((tf, D_MODEL), lambda i, f: (f, 0))],
        out_specs=pl.BlockSpec((tm, D_MODEL), lambda i, f: (i, 0)),
        out_shape=jax.ShapeDtypeStruct((T, D_MODEL), F32),
        scratch_shapes=[pltpu.VMEM((tm, D_MODEL), BF16), pltpu.VMEM((tm, D_MODEL), F32)],
        compiler_params=_cparams(("arbitrary", "arbitrary")),
        name="ffn_dense",
    )(h, y, wo, gain, wg, wu, wd)


def _moe_kernel(h_ref, g_ref, wr_ref, wg_ref, wu_ref, wd_ref, o_ref,
                xn_sc, xs_sc, ys_sc, posc_sc, posr_sc, comb_sc, nblk_sc):
    e = pl.program_id(1)
    f = pl.program_id(2)
    tm = h_ref.shape[0]
    SB = _moe_slot_block(tm)
    TB = _moe_token_block(tm)
    lane = lax.broadcasted_iota(jnp.int32, (tm, LANES), 1)

    @pl.when((e == 0) & (f == 0))
    def _():
        x = h_ref[...]
        xn = _rms_rows(x, g_ref[...])
        xn_sc[...] = xn.astype(BF16)
        o_ref[...] = x
        logits = jnp.dot(xn, wr_ref[...], preferred_element_type=F32, precision=lax.Precision.HIGHEST)
        logits = jnp.where(lane < N_EXPERTS, logits, -jnp.inf)
        m1 = jnp.max(logits, axis=1, keepdims=True)
        i1 = jnp.min(jnp.where(logits == m1, lane, LANES), axis=1, keepdims=True)
        rest = jnp.where(lane == i1, -jnp.inf, logits)
        m2 = jnp.max(rest, axis=1, keepdims=True)
        i2 = jnp.min(jnp.where(rest == m2, lane, LANES), axis=1, keepdims=True)
        e2 = jnp.exp(m2 - m1)
        g1 = 1.0 / (1.0 + e2)
        comb_sc[...] = jnp.where(lane == i1, g1, jnp.where(lane == i2, e2 * g1, 0.0))
        member = jnp.where(lane == i1, 1.0, jnp.where(lane == i2, 1.0, 0.0))
        tri = jnp.where(lax.broadcasted_iota(jnp.int32, (TB, TB), 0) >= lax.broadcasted_iota(jnp.int32, (TB, TB), 1),
                        1.0, 0.0).astype(BF16)
        carry = jnp.zeros((1, LANES), F32)
        for b in range(tm // TB):
            mb = member[b * TB:(b + 1) * TB]
            incl = jnp.dot(tri, mb.astype(BF16), preferred_element_type=F32)
            posc_sc[b * TB:(b + 1) * TB, :] = jnp.where(mb > 0.0, incl - 1.0 + carry, -1.0)
            carry = carry + incl[TB - 1:TB, :]
        for b in range(tm // LANES):
            posr_sc[:, b * LANES:(b + 1) * LANES] = posc_sc[b * LANES:(b + 1) * LANES, :].T
        lane_row = lax.broadcasted_iota(jnp.int32, (1, LANES), 1)
        for x in range(N_EXPERTS):
            count = jnp.sum(jnp.where(lane_row == x, carry, 0.0))
            nblk_sc[x] = ((count + (SB - 1.0)) * (1.0 / SB)).astype(jnp.int32)

    nblk = nblk_sc[e]

    def for_slot_rows(fn):
        def pair_body(r, carry):
            fn(pl.multiple_of(r * 2 * SB, 16), 2 * SB)
            return carry

        lax.fori_loop(0, nblk // 2, pair_body, 0)

        @pl.when(nblk % 2 == 1)
        def _():
            fn(pl.multiple_of((nblk - 1) * SB, 16), SB)

    @pl.when(f == 0)
    def _():
        slots_of_tokens = posr_sc[pl.ds(e, 1), :]

        def compact_rows(row0, nrows):
            slot = (row0 + lax.broadcasted_iota(jnp.int32, (nrows, tm), 0)).astype(F32)
            onehot = jnp.where(slots_of_tokens == slot, 1.0, 0.0).astype(BF16)
            xs_sc[pl.ds(row0, nrows), :] = jnp.dot(onehot, xn_sc[...], preferred_element_type=F32).astype(BF16)
            ys_sc[pl.ds(row0, nrows), :] = jnp.zeros((nrows, D_MODEL), F32)

        for_slot_rows(compact_rows)

    def ffn_rows(row0, nrows):
        xs = xs_sc[pl.ds(row0, nrows), :]
        gate = jnp.dot(xs, wg_ref[0], preferred_element_type=F32)
        up = jnp.dot(xs, wu_ref[0], preferred_element_type=F32)
        act = (gate * jax.nn.sigmoid(gate) * up).astype(BF16)
        ys_sc[pl.ds(row0, nrows), :] += jnp.dot(act, wd_ref[0], preferred_element_type=F32)

    for_slot_rows(ffn_rows)

    @pl.when(f == pl.num_programs(2) - 1)
    def _():
        on_e = lane == e
        gate_e = jnp.sum(jnp.where(on_e, comb_sc[...], 0.0), axis=1, keepdims=True)
        slot_e = jnp.sum(jnp.where(on_e, posc_sc[...], 0.0), axis=1, keepdims=True)

        def scatter_rows(row0, nrows):
            ys = ys_sc[pl.ds(row0, nrows), :].astype(BF16)
            slot = (row0 + lax.broadcasted_iota(jnp.int32, (TB, nrows), 1)).astype(F32)
            for tb in range(tm // TB):
                rows = slice(tb * TB, (tb + 1) * TB)
                onehot = jnp.where(slot_e[rows] == slot, 1.0, 0.0).astype(BF16)
                o_ref[rows, :] += gate_e[rows] * jnp.dot(onehot, ys, preferred_element_type=F32)

        for_slot_rows(scatter_rows)


def _moe_slot_block(tm):
    return MOE_SLOT_BLOCK if tm > MOE_SLOT_BLOCK else tm


def _moe_token_block(tm):
    return MOE_TOKEN_BLOCK if tm % MOE_TOKEN_BLOCK == 0 else tm


def _moe(h, gain, wr, wg, wu, wd, tm, tf):
    T = h.shape[0]
    sb = _moe_slot_block(tm)
    cap = -(-tm // sb) * sb
    return pl.pallas_call(
        _moe_kernel,
        grid=(T // tm, N_EXPERTS, D_FF // tf),
        in_specs=[pl.BlockSpec((tm, D_MODEL), lambda i, e, f: (i, 0), pipeline_mode=pl.Buffered(1)),
                  pl.BlockSpec((1, D_MODEL), lambda i, e, f: (0, 0)),
                  pl.BlockSpec((D_MODEL, LANES), lambda i, e, f: (0, 0)),
                  pl.BlockSpec((1, D_MODEL, tf), lambda i, e, f: (e, 0, f)),
                  pl.BlockSpec((1, D_MODEL, tf), lambda i, e, f: (e, 0, f)),
                  pl.BlockSpec((1, tf, D_MODEL), lambda i, e, f: (e, f, 0))],
        out_specs=pl.BlockSpec((tm, D_MODEL), lambda i, e, f: (i, 0)),
        out_shape=jax.ShapeDtypeStruct((T, D_MODEL), F32),
        scratch_shapes=[pltpu.VMEM((tm, D_MODEL), BF16), pltpu.VMEM((cap, D_MODEL), BF16),
                        pltpu.VMEM((cap, D_MODEL), F32), pltpu.VMEM((tm, LANES), F32),
                        pltpu.VMEM((LANES, tm), F32), pltpu.VMEM((tm, LANES), F32),
                        pltpu.SMEM((N_EXPERTS,), jnp.int32)],
        compiler_params=_cparams(("arbitrary", "arbitrary", "arbitrary")),
        name="moe_sparse",
    )(h, gain, wr, wg, wu, wd)


def _copy_kernel(x_ref, o_ref):
    o_ref[...] = x_ref[...]


def _drop_meta(h3, seq):
    B, _, D = h3.shape
    rows = 512 if seq % 512 == 0 else seq
    return pl.pallas_call(
        _copy_kernel,
        grid=(B, seq // rows),
        in_specs=[pl.BlockSpec((pl.Element(1), pl.Element(rows), pl.Element(D)),
                               lambda b, i: (b, pl.multiple_of(N_META + i * rows, SUBLANES), 0))],
        out_specs=pl.BlockSpec((1, rows, D), lambda b, i: (b, i, 0)),
        out_shape=jax.ShapeDtypeStruct((B, seq, D), h3.dtype),
        compiler_params=_cparams(("arbitrary", "arbitrary")),
        name="drop_meta",
    )(h3)


def _rope_tables(seq_pad):
    pos = jnp.arange(seq_pad, dtype=F32)[:, None]
    lane = np.arange(LANES)

    def table(head_dim):
        half = head_dim // 2
        inv = ROPE_THETA ** (-jnp.arange(half, dtype=F32) / half)
        d = lane % head_dim
        ang = pos * inv[d % half][None, :]
        return jnp.cos(ang), jnp.sin(ang), jnp.asarray(d < half)[None, :]

    c64, s64, lo64 = table(ATT_HEAD_DIM)
    c128, s128, lo128 = table(IDX_DIM)
    return (c64, jnp.where(lo64, -s64, 0.0), jnp.where(lo64, 0.0, s64), c128, jnp.where(lo128, -s128, s128))


def _pad_cols(w, n):
    return jnp.pad(w, ((0, 0), (0, n - w.shape[1])))


def kernel(x, meta, norm_mixer, norm_ffn, dsa_w_in, dsa_q_norm, dsa_k_norm, dsa_w_out, mlstm_w_in, mlstm_b_i,
           mlstm_b_f, mlstm_out_norm, mlstm_w_out, ffn_w_gate, ffn_w_up, ffn_w_down, moe_router, moe_w_gate,
           moe_w_up, moe_w_down):
    B, S, D = x.shape
    L = S + N_META
    top_k = min(TOPK_MAX, S // 4)
    seq_pad = max(-(-L // Q_BLOCK) * Q_BLOCK, KEY_CHUNK)
    T = B * seq_pad
    tm_proj = 384 if seq_pad % 384 == 0 else Q_BLOCK
    tm_ffn = 768 if T % 768 == 0 else Q_BLOCK
    tm_moe = 1536 if T % 1536 == 0 else Q_BLOCK
    tf = 512

    h = jnp.concatenate([jnp.broadcast_to(meta[None].astype(x.dtype), (B, N_META, D)), x,
                         jnp.zeros((B, seq_pad - L, D), x.dtype)], axis=1).reshape(T, D)
    tabs = _rope_tables(seq_pad)
    depth = norm_mixer.shape[0]

    for i in range(depth):
        j = i // 2
        gain_m = norm_mixer[i][None, :]
        gain_f = norm_ffn[i][None, :]
        if i % 2 == 0:
            w_in = _pad_cols(dsa_w_in[j], DSA_IN_PAD).astype(BF16)
            gq = jnp.tile(dsa_q_norm[j], 2)[None, :]
            gk = jnp.tile(dsa_k_norm[j], 2)[None, :]
            q, k, v, iq, ik, iw = _dsa_in_proj(h, gain_m, w_in, gq, gk, tabs, seq_pad, tm_proj)
            k, v, ik = [a.reshape(B, seq_pad, a.shape[-1]) for a in (k, v, ik)]
            att = _dsa_core(q, k, v, iq, ik, iw, top_k)
            h = _ffn(h, att.reshape(T, D), dsa_w_out[j].astype(BF16), gain_f, ffn_w_gate[j].astype(BF16),
                     ffn_w_up[j].astype(BF16), ffn_w_down[j].astype(BF16), tm_ffn, tf)
        else:
            w_in = _pad_cols(mlstm_w_in[j], MLSTM_IN_PAD).astype(BF16)
            q, k, v, o, gates = _mlstm_in_proj(h, gain_m, w_in, tm_proj)
            bias8 = jnp.concatenate([mlstm_b_i[j], mlstm_b_f[j]])[:, None]
            y = _mlstm_core(*[a.reshape(B, seq_pad, a.shape[-1]) for a in (q, k, v, o, gates)], bias8,
                            mlstm_out_norm[j][None, :])
            h = _out_proj(h, y.reshape(T, D), mlstm_w_out[j].astype(BF16), tm_proj)
            wr = _pad_cols(moe_router[j], LANES)
            h = _moe(h, gain_f, wr, moe_w_gate[j].astype(BF16), moe_w_up[j].astype(BF16),
                     moe_w_down[j].astype(BF16), tm_moe, tf)
    return _drop_meta(h.reshape(B, seq_pad, D), S)
```

```python
import functools

import jax
import jax.numpy as jnp
import numpy as np
from jax import lax
from jax.experimental import pallas as pl
from jax.experimental.pallas import tpu as pltpu

F32 = jnp.float32
BF16 = jnp.bfloat16

D_MODEL = 1024
N_META = 16
RMS_EPS = 1e-6
ROPE_THETA = 10000.0
ATT_HEADS = 16
ATT_KV_HEADS = 4
ATT_HEAD_DIM = 64
ATT_GROUP = 4
IDX_HEADS = 8
IDX_DIM = 128
TOPK_MAX = 256
M_HEADS = 4
M_QK_DIM = 128
M_V_DIM = 256
GATE_CAP = 15.0
D_FF = 3584
N_EXPERTS = 8

LANES = 128
SUBLANES = 8
Q_BLOCK = 128
KEY_CHUNK = 256
ATT_STEPS_PER_TRIP = 4
CHUNKS_PER_TRIP = 4
M_CHUNK = 128
M_BATCH_ROWS = 1
MOE_SLOT_BLOCK = 208
MOE_TOKEN_BLOCK = 256
VMEM_LIMIT = 52 * 1024 * 1024
INT_MIN = -(2 ** 31)
INT_MAX = 2 ** 31 - 1
LOG2E = 1.4426950408889634

DSA_IN_PAD = 2816
MLSTM_IN_PAD = 3200


def _cparams(sem):
    return pltpu.CompilerParams(dimension_semantics=sem, vmem_limit_bytes=VMEM_LIMIT)


def _unrolled_loop(n, fn, unroll):
    def body(t, carry):
        for u in range(unroll):
            fn(unroll * t + u)
        return carry

    lax.fori_loop(0, n // unroll, body, 0)
    rest = n % unroll
    done = n - rest
    run = unroll // 2
    while run >= 1:
        @pl.when((rest & run) != 0)
        def _(run=run, done=done):
            for u in range(run):
                fn(done + u)

        done = done + (rest & run)
        run //= 2


def _rms_rows(x, gain):
    ms = jnp.mean(x * x, axis=-1, keepdims=True)
    return x * lax.rsqrt(ms + RMS_EPS) * gain


def _dsa_in_kernel(h_ref, g_ref, w_ref, gq_ref, gk_ref, c64_ref, sa64_ref, sb64_ref, c128_ref, s128_ref,
                   smat_ref, q_ref, k_ref, v_ref, iq_ref, ik_ref, iw_ref):
    xn = _rms_rows(h_ref[...], g_ref[...]).astype(BF16)
    z = jnp.dot(xn, w_ref[...], preferred_element_type=F32)
    c64, sa64, sb64 = c64_ref[...], sa64_ref[...], sb64_ref[...]
    c128, s128 = c128_ref[...], s128_ref[...]
    smat = smat_ref[...]

    def head_norm_rope(zc, gain):
        z2 = zc * zc
        hi = z2.astype(BF16)
        lo = (z2 - hi.astype(F32)).astype(BF16)
        ms = jnp.dot(hi, smat, preferred_element_type=F32) + jnp.dot(lo, smat, preferred_element_type=F32)
        y = zc * lax.rsqrt(ms + RMS_EPS) * gain
        return y * c64 + pltpu.roll(y, 96, 1) * sa64 + pltpu.roll(y, 32, 1) * sb64

    nsub = z.shape[0] // Q_BLOCK
    lane = lax.broadcasted_iota(jnp.int32, (z.shape[0], LANES), 1)
    for c in range(8):
        r = head_norm_rope(z[:, c * LANES:(c + 1) * LANES], gq_ref[...]) * (ATT_HEAD_DIM ** -0.5 * LOG2E)
        swapped = pltpu.roll(r, 64, 1)
        for half in range(2):
            h = 2 * c + half
            g, rr = h // ATT_GROUP, h % ATT_GROUP
            keep = (lane >= 64) if g % 2 else (lane < 64)
            val = jnp.where(keep, r if half == g % 2 else swapped, 0.0).astype(BF16)
            for t in range(nsub):
                q_ref[t, g, rr * Q_BLOCK:(rr + 1) * Q_BLOCK, :] = val[t * Q_BLOCK:(t + 1) * Q_BLOCK]
    for c in range(2):
        r = head_norm_rope(z[:, 1024 + c * LANES:1024 + (c + 1) * LANES], gk_ref[...])
        k_ref[:, c * LANES:(c + 1) * LANES] = r.astype(BF16)
    ones = jnp.ones((z.shape[0], LANES), BF16)
    for c in range(2):
        v_ref[:, 2 * c * LANES:(2 * c + 1) * LANES] = z[:, 1280 + c * LANES:1280 + (c + 1) * LANES].astype(BF16)
        v_ref[:, (2 * c + 1) * LANES:(2 * c + 2) * LANES] = ones
    for c in range(9):
        zc = z[:, 1536 + c * LANES:1536 + (c + 1) * LANES]
        r = zc * c128 + pltpu.roll(zc, 64, 1) * s128
        if c < 8:
            for t in range(nsub):
                iq_ref[t, :, c * Q_BLOCK:(c + 1) * Q_BLOCK] = r[t * Q_BLOCK:(t + 1) * Q_BLOCK].T.astype(BF16)
        else:
            ik_ref[...] = r.astype(BF16)
    iw = z[:, 2688:2816] * (IDX_HEADS ** -0.5 * IDX_DIM ** -0.5)
    for t in range(nsub):
        iw_ref[t] = iw[t * Q_BLOCK:(t + 1) * Q_BLOCK].T


def _dsa_in_proj(h, gain, w, gq, gk, tabs, seq_pad, tm):
    T = h.shape[0]
    nt = seq_pad // tm
    row = lambda i: (i, 0)
    fixed = lambda i: (0, 0)
    pos = lambda i: (i % nt, 0)
    tab_spec = pl.BlockSpec((tm, LANES), pos)
    smat = jnp.asarray(np.kron(np.eye(2), np.full((64, 64), 1.0 / 64)), BF16)
    nsub = tm // Q_BLOCK
    rows = ATT_GROUP * Q_BLOCK
    outs = [((ATT_KV_HEADS, rows, LANES), BF16), (256, BF16), (512, BF16),
            ((IDX_DIM, IDX_HEADS * Q_BLOCK), BF16), (128, BF16), ((LANES, Q_BLOCK), F32)]

    def spec(o):
        if isinstance(o[0], int):
            return pl.BlockSpec((tm, o[0]), row)
        return pl.BlockSpec((nsub,) + o[0], lambda i: (i,) + (0,) * len(o[0]))

    def shape(o):
        if isinstance(o[0], int):
            return jax.ShapeDtypeStruct((T, o[0]), o[1])
        return jax.ShapeDtypeStruct((T // Q_BLOCK,) + o[0], o[1])

    return pl.pallas_call(
        _dsa_in_kernel,
        grid=(T // tm,),
        in_specs=[pl.BlockSpec((tm, D_MODEL), row), pl.BlockSpec((1, D_MODEL), fixed),
                  pl.BlockSpec((D_MODEL, DSA_IN_PAD), fixed), pl.BlockSpec((1, LANES), fixed),
                  pl.BlockSpec((1, LANES), fixed), tab_spec, tab_spec, tab_spec, tab_spec, tab_spec,
                  pl.BlockSpec((LANES, LANES), fixed)],
        out_specs=[spec(o) for o in outs],
        out_shape=[shape(o) for o in outs],
        compiler_params=_cparams(("arbitrary",)),
        name="dsa_in_proj",
    )(h, gain, w, gq, gk, *tabs, smat)


def _dsa_core_kernel(q_ref, k_ref, v_ref, iq_ref, ik_ref, iw_ref, o_ref,
                     keys_sc, bias_sc, cut_sc, m_sc, l_sc, acc_sc, *, top_k, seq_pad):
    i = pl.program_id(1)
    QB, KC = Q_BLOCK, KEY_CHUNK
    nch = (i * QB + QB + KC - 1) // KC
    kf = float(top_k)
    lane1 = lax.broadcasted_iota(jnp.int32, (QB, LANES), 1)

    qpos = i * QB + lax.broadcasted_iota(jnp.int32, (KC, QB), 1)
    krow = lax.broadcasted_iota(jnp.int32, (KC, QB), 0)

    def chunk_start(j):
        return pl.multiple_of(jnp.minimum(j * KC, seq_pad - KC), LANES)

    def key_valid(j, start):
        kidx = start + krow
        return (kidx <= qpos) & (kidx >= j * KC)

    def score_chunk(j):
        start = chunk_start(j)
        s = jnp.dot(ik_ref[0, pl.ds(start, KC), :], iq_ref[0], preferred_element_type=F32)
        acc = jnp.zeros((KC, QB), F32)
        for h in range(IDX_HEADS):
            acc = acc + jnp.maximum(s[:, h * QB:(h + 1) * QB], 0.0) * iw_ref[0, h:h + 1, :]
        bits = pltpu.bitcast(acc, jnp.int32)
        key = jnp.where(bits >= 0, bits, bits ^ jnp.int32(INT_MAX))
        keys_sc[j] = jnp.where(key_valid(j, start), key, jnp.int32(INT_MIN))

    _unrolled_loop(nch, score_chunk, CHUNKS_PER_TRIP)

    tiles = (KC // SUBLANES, SUBLANES, QB)
    krow3 = (lax.broadcasted_iota(jnp.int32, tiles, 0) * SUBLANES + lax.broadcasted_iota(jnp.int32, tiles, 1))

    keys_sc[nch] = jnp.full((KC, QB), INT_MIN, jnp.int32)

    def count(pred):
        def body(j2, acc):
            for j in (2 * j2, 2 * j2 + 1):
                kk = keys_sc[j].reshape(tiles)
                hit = jnp.where(pred(kk, chunk_start(j)), jnp.float32(1.0), jnp.float32(0.0))
                while hit.shape[0] > 1:
                    half = hit.shape[0] // 2
                    hit = hit[:half] + hit[half:]
                acc = acc + hit[0]
            return acc
        acc = lax.fori_loop(0, (nch + 1) // 2, body, jnp.zeros((SUBLANES, QB), F32))
        for shift in (4, 2, 1):
            acc = acc + pltpu.roll(acc, shift, 0)
        return acc

    zero = jnp.zeros((SUBLANES, QB), jnp.int32)
    v0 = jnp.where(count(lambda kk, base: kk >= zero) >= kf, zero, jnp.int32(INT_MIN))

    def bit_body(b, v):
        cand = v | lax.shift_left(jnp.int32(1), (30 - b).astype(jnp.int32))
        return jnp.where(count(lambda kk, base: kk >= cand) >= kf, cand, v)

    thr = lax.fori_loop(jnp.int32(0), jnp.int32(31), bit_body, v0)
    cut_sc[...] = jnp.full((SUBLANES, QB), INT_MAX, jnp.int32)
    n_ge = count(lambda kk, base: kk >= thr)

    @pl.when(jnp.max(n_ge) > kf)
    def _():
        need = kf - count(lambda kk, base: kk > thr)

        def cut_body(b, c):
            cand = c | lax.shift_left(jnp.int32(1), (12 - b).astype(jnp.int32))
            n = count(lambda kk, base: jnp.where(kk == thr, base + krow3, jnp.int32(INT_MAX)) < cand)
            return jnp.where(n < need, cand, c)

        cut_sc[...] = lax.fori_loop(jnp.int32(0), jnp.int32(13), cut_body, zero)

    thr1 = thr[0:1, :]
    cut1 = cut_sc[0:1, :]

    def bias_chunk(j):
        kk = keys_sc[j]
        start = chunk_start(j)
        sel = (kk > thr1) | ((kk == thr1) & (start + krow <= cut1))
        bt = jnp.where(sel & key_valid(j, start), 0.0, -1e30)
        bias_sc[j] = jnp.concatenate([bt[c * QB:(c + 1) * QB].T for c in range(KC // QB)], axis=1)

    _unrolled_loop(nch, bias_chunk, CHUNKS_PER_TRIP)

    m_sc[...] = jnp.full(m_sc.shape, -1e9, F32)
    l_sc[...] = jnp.zeros(l_sc.shape, F32)
    acc_sc[...] = jnp.zeros(acc_sc.shape, F32)

    def att_step(start, width, bias, groups):
        bias = jnp.concatenate([bias] * ATT_GROUP, axis=0)
        for g in groups:
            cg = g // 2
            kj = k_ref[0, pl.ds(start, width), cg * LANES:(cg + 1) * LANES]
            vj = v_ref[0, pl.ds(start, width), cg * 2 * LANES:(cg + 1) * 2 * LANES]
            s = lax.dot_general(q_ref[0, g], kj, (((1,), (1,)), ((), ())), preferred_element_type=F32) + bias
            m_old = m_sc[g]
            m_new = jnp.maximum(m_old, jnp.max(s, axis=1, keepdims=True))
            alpha = jnp.exp2(m_old - m_new)
            p = jnp.exp2(s - jnp.concatenate([m_new] * (width // LANES), axis=1)).astype(BF16)
            pv = jnp.dot(p, vj, preferred_element_type=F32)
            l_sc[g] = alpha * l_sc[g] + pv[:, LANES:]
            acc_sc[g] = alpha * acc_sc[g] + pv[:, :LANES]
            m_sc[g] = m_new

    npair = jnp.minimum(nch // 2, seq_pad // (2 * KC))

    def single_body(j, carry):
        att_step(chunk_start(j), KC, bias_sc[j], range(ATT_KV_HEADS))
        return carry

    def pair_step(j):
        bias = jnp.concatenate([bias_sc[2 * j], bias_sc[2 * j + 1]], axis=1)
        att_step(pl.multiple_of(j * 2 * KC, 2 * KC), 2 * KC, bias, range(ATT_KV_HEADS))

    _unrolled_loop(npair, pair_step, ATT_STEPS_PER_TRIP)
    lax.fori_loop(2 * npair, nch, single_body, 0)

    for g in range(ATT_KV_HEADS):
        og = acc_sc[g] / l_sc[g]
        for pair in range(2):
            even = og[(2 * pair) * QB:(2 * pair + 1) * QB]
            odd = og[(2 * pair + 1) * QB:(2 * pair + 2) * QB]
            if g % 2:
                even = pltpu.roll(even, 64, 1)
            else:
                odd = pltpu.roll(odd, 64, 1)
            c = 2 * g + pair
            o_ref[0, :, c * LANES:(c + 1) * LANES] = jnp.where(lane1 < 64, even, odd).astype(BF16)


def _dsa_core(q, k, v, iq, ik, iw, top_k):
    B, seq_pad = k.shape[0], k.shape[1]
    nqb = seq_pad // Q_BLOCK
    nkc = -(-seq_pad // KEY_CHUNK)
    rows = ATT_GROUP * Q_BLOCK
    qblock = lambda a: pl.BlockSpec((1,) + a.shape[1:], lambda b, i: (b * nqb + i,) + (0,) * (a.ndim - 1))
    whole = lambda n: pl.BlockSpec((1, seq_pad, n), lambda b, i: (b, 0, 0))
    return pl.pallas_call(
        functools.partial(_dsa_core_kernel, top_k=top_k, seq_pad=seq_pad),
        grid=(B, nqb),
        in_specs=[qblock(q), whole(256), whole(512), qblock(iq), whole(IDX_DIM), qblock(iw)],
        out_specs=pl.BlockSpec((1, Q_BLOCK, D_MODEL), lambda b, i: (b, i, 0)),
        out_shape=jax.ShapeDtypeStruct((B, seq_pad, D_MODEL), BF16),
        scratch_shapes=[
            pltpu.VMEM((nkc + 1, KEY_CHUNK, Q_BLOCK), jnp.int32),
            pltpu.VMEM((nkc, Q_BLOCK, KEY_CHUNK), F32),
            pltpu.VMEM((SUBLANES, Q_BLOCK), jnp.int32),
            pltpu.VMEM((ATT_KV_HEADS, rows, LANES), F32),
            pltpu.VMEM((ATT_KV_HEADS, rows, LANES), F32),
            pltpu.VMEM((ATT_KV_HEADS, rows, LANES), F32),
        ],
        compiler_params=_cparams(("arbitrary", "arbitrary")),
        name="dsa_core",
    )(q, k, v, iq, ik, iw)


def _out_proj_kernel(h_ref, y_ref, w_ref, o_ref):
    o_ref[...] = h_ref[...] + jnp.dot(y_ref[...], w_ref[...], preferred_element_type=F32)


def _out_proj(h, y, w, tm):
    T = h.shape[0]
    row = lambda i: (i, 0)
    return pl.pallas_call(
        _out_proj_kernel,
        grid=(T // tm,),
        in_specs=[pl.BlockSpec((tm, D_MODEL), row), pl.BlockSpec((tm, D_MODEL), row),
                  pl.BlockSpec((D_MODEL, D_MODEL), lambda i: (0, 0))],
        out_specs=pl.BlockSpec((tm, D_MODEL), row),
        out_shape=jax.ShapeDtypeStruct((T, D_MODEL), F32),
        compiler_params=_cparams(("arbitrary",)),
        name="out_proj",
    )(h, y, w)


def _mlstm_in_kernel(h_ref, g_ref, w_ref, q_ref, k_ref, v_ref, o_ref, gate_ref):
    xn = _rms_rows(h_ref[...], g_ref[...]).astype(BF16)
    z = jnp.dot(xn, w_ref[...], preferred_element_type=F32)
    q_ref[...] = z[:, 0:512].astype(BF16)
    k_ref[...] = (z[:, 512:1024] * (M_QK_DIM ** -0.5)).astype(BF16)
    v_ref[...] = z[:, 1024:2048].astype(BF16)
    o_ref[...] = z[:, 2048:3072]
    gate_ref[...] = z[:, 3072:3200]


def _mlstm_in_proj(h, gain, w, tm):
    T = h.shape[0]
    row = lambda i: (i, 0)
    fixed = lambda i: (0, 0)
    outs = [(512, BF16), (512, BF16), (1024, BF16), (1024, F32), (128, F32)]
    return pl.pallas_call(
        _mlstm_in_kernel,
        grid=(T // tm,),
        in_specs=[pl.BlockSpec((tm, D_MODEL), row), pl.BlockSpec((1, D_MODEL), fixed),
                  pl.BlockSpec((D_MODEL, MLSTM_IN_PAD), fixed)],
        out_specs=[pl.BlockSpec((tm, n), row) for n, _ in outs],
        out_shape=[jax.ShapeDtypeStruct((T, n), dt) for n, dt in outs],
        compiler_params=_cparams(("arbitrary",)),
        name="mlstm_in_proj",
    )(h, gain, w)


def _mlstm_kernel(q_ref, k_ref, v_ref, o_ref, gate_ref, bias_ref, gout_ref, y_ref, ct_sc, n_sc, m_sc):
    C = M_CHUNK

    @pl.when(pl.program_id(1) == 0)
    def _():
        ct_sc[...] = jnp.zeros(ct_sc.shape, F32)
        n_sc[...] = jnp.zeros(n_sc.shape, F32)
        m_sc[...] = jnp.zeros(m_sc.shape, F32)

    lane8 = lax.broadcasted_iota(jnp.int32, (8, C), 1)
    t_idx = lax.broadcasted_iota(jnp.int32, (C, C), 0)
    s_idx = lax.broadcasted_iota(jnp.int32, (C, C), 1)
    for bi in range(q_ref.shape[0]):
        pre = gate_ref[bi].T[0:8, :] + bias_ref[...]
        capped = GATE_CAP * jnp.tanh(pre / GATE_CAP)
        log_f = -(jnp.maximum(-capped, 0.0) + jnp.log1p(jnp.exp(-jnp.abs(capped))))
        b = log_f
        sh = 1
        while sh < C:
            b = b + jnp.where(lane8 >= sh, pltpu.roll(b, sh, 1), 0.0)
            sh *= 2
        stacked = jnp.concatenate([b[4:8], capped[0:4]], axis=0)
        cols = jnp.concatenate([stacked, jnp.zeros((C - 8, C), F32)], axis=0).T

        for h in range(M_HEADS):
            st = bi * M_HEADS + h
            qh = q_ref[bi, :, h * M_QK_DIM:(h + 1) * M_QK_DIM]
            kh = k_ref[bi, :, h * M_QK_DIM:(h + 1) * M_QK_DIM]
            vh = v_ref[bi, :, h * M_V_DIM:(h + 1) * M_V_DIM]
            b_row, li_row = stacked[h:h + 1, :], stacked[4 + h:5 + h, :]
            b_col, li_col = cols[:, h:h + 1], cols[:, 4 + h:5 + h]
            m_st = m_sc[st:st + 1, 0:1]
            dmat = jnp.where(s_idx <= t_idx, b_col - b_row + li_row, -jnp.inf)
            inter = b_col + m_st
            m_t = jnp.maximum(inter, jnp.max(dmat, axis=1, keepdims=True))
            w_inter = jnp.exp(inter - m_t)
            qk = lax.dot_general(qh, kh, (((1,), (1,)), ((), ())), preferred_element_type=F32)
            s = qk * jnp.exp(dmat - m_t)
            ct = ct_sc[st]
            num = (w_inter * jnp.dot(qh, ct.astype(BF16), preferred_element_type=F32)
                   + jnp.dot(s.astype(BF16), vh, preferred_element_type=F32))
            qn = jnp.sum(qh.astype(F32) * n_sc[st:st + 1, :], axis=1, keepdims=True)
            den = w_inter * qn + jnp.sum(s, axis=1, keepdims=True)
            hout = num / jnp.maximum(jnp.abs(den), jnp.exp(-m_t))

            b_last = b_row[:, C - 1:C]
            m_new = jnp.maximum(b_last + m_st, jnp.max(b_last - b_row + li_row, axis=1, keepdims=True))
            decay = jnp.exp(b_last + m_st - m_new)
            wk = jnp.exp(b_last - b_col + li_col - m_new)
            kw = kh.astype(F32) * wk
            ct_sc[st] = decay * ct + jnp.dot(kw.T.astype(BF16), vh, preferred_element_type=F32)
            n_sc[st:st + 1, :] = decay * n_sc[st:st + 1, :] + jnp.sum(kw, axis=0, keepdims=True)
            m_sc[st:st + 1, :] = jnp.broadcast_to(m_new, (1, LANES))

            cs = slice(h * M_V_DIM, (h + 1) * M_V_DIM)
            hn = _rms_rows(hout, gout_ref[:, cs])
            y_ref[bi, :, cs] = (hn * jax.nn.sigmoid(o_ref[bi, :, cs])).astype(BF16)


def _mlstm_core(q, k, v, o, gates, bias8, gout):
    B, seq_pad = q.shape[0], q.shape[1]
    nb = M_BATCH_ROWS
    C = M_CHUNK
    blk = lambda n: pl.BlockSpec((nb, C, n), lambda b, c: (b, c, 0))
    fixed = lambda b, c: (0, 0)
    return pl.pallas_call(
        _mlstm_kernel,
        grid=(B // nb, seq_pad // C),
        in_specs=[blk(512), blk(512), blk(1024), blk(1024), blk(LANES),
                  pl.BlockSpec((8, 1), fixed), pl.BlockSpec((1, D_MODEL), fixed)],
        out_specs=blk(D_MODEL),
        out_shape=jax.ShapeDtypeStruct((B, seq_pad, D_MODEL), BF16),
        scratch_shapes=[pltpu.VMEM((nb * M_HEADS, M_QK_DIM, M_V_DIM), F32),
                        pltpu.VMEM((nb * M_HEADS, M_QK_DIM), F32),
                        pltpu.VMEM((nb * M_HEADS, LANES), F32)],
        compiler_params=_cparams(("arbitrary", "arbitrary")),
        name="mlstm_core",
    )(q, k, v, o, gates, bias8, gout)


def _ffn_kernel(h_ref, y_ref, wo_ref, g_ref, wg_ref, wu_ref, wd_ref, o_ref, xn_sc, acc_sc):
    f = pl.program_id(1)

    @pl.when(f == 0)
    def _():
        x = h_ref[...] + jnp.dot(y_ref[...], wo_ref[...], preferred_element_type=F32)
        xn_sc[...] = _rms_rows(x, g_ref[...]).astype(BF16)
        acc_sc[...] = x

    xn = xn_sc[...]
    gate = jnp.dot(xn, wg_ref[0], preferred_element_type=F32)
    up = jnp.dot(xn, wu_ref[0], preferred_element_type=F32)
    act = (gate * jax.nn.sigmoid(gate) * up).astype(BF16)
    acc_sc[...] += jnp.dot(act, wd_ref[0], preferred_element_type=F32)

    @pl.when(f == pl.num_programs(1) - 1)
    def _():
        o_ref[...] = acc_sc[...]


def _ffn(h, y, wo, gain, wg, wu, wd, layer, tm, tf):
    T = h.shape[0]
    return pl.pallas_call(
        _ffn_kernel,
        grid=(T // tm, D_FF // tf),
        in_specs=[pl.BlockSpec((tm, D_MODEL), lambda i, f: (i, 0)),
                  pl.BlockSpec((tm, D_MODEL), lambda i, f: (i, 0)),
                  pl.BlockSpec((D_MODEL, D_MODEL), lambda i, f: (0, 0)),
                  pl.BlockSpec((1, D_MODEL), lambda i, f: (0, 0)),
                  pl.BlockSpec((1, D_MODEL, tf), lambda i, f: (layer, 0, f)),
                  pl.BlockSpec((1, D_MODEL, tf), lambda i, f: (layer, 0, f)),
                  pl.BlockSpec((1, tf, D_MODEL), lambda i, f: (layer, f, 0))],
        out_specs=pl.BlockSpec((tm, D_MODEL), lambda i, f: (i, 0)),
        out_shape=jax.ShapeDtypeStruct((T, D_MODEL), F32),
        scratch_shapes=[pltpu.VMEM((tm, D_MODEL), BF16), pltpu.VMEM((tm, D_MODEL), F32)],
        compiler_params=_cparams(("arbitrary", "arbitrary")),
        name="ffn_dense",
    )(h, y, wo, gain, wg, wu, wd)


def _moe_kernel(h_ref, g_ref, wr_ref, wg_ref, wu_ref, wd_ref, o_ref,
                xn_sc, xs_sc, ys_sc, posc_sc, posr_sc, comb_sc, nblk_sc):
    e = pl.program_id(1)
    f = pl.program_id(2)
    tm = h_ref.shape[0]
    SB = _moe_slot_block(tm)
    TB = _moe_token_block(tm)
    lane = lax.broadcasted_iota(jnp.int32, (tm, LANES), 1)

    @pl.when((e == 0) & (f == 0))
    def _():
        x = h_ref[...]
        xn = _rms_rows(x, g_ref[...])
        xn_sc[...] = xn.astype(BF16)
        o_ref[...] = x
        logits = jnp.dot(xn, wr_ref[...], preferred_element_type=F32, precision=lax.Precision.HIGHEST)
        logits = jnp.where(lane < N_EXPERTS, logits, -jnp.inf)
        m1 = jnp.max(logits, axis=1, keepdims=True)
        i1 = jnp.min(jnp.where(logits == m1, lane, LANES), axis=1, keepdims=True)
        rest = jnp.where(lane == i1, -jnp.inf, logits)
        m2 = jnp.max(rest, axis=1, keepdims=True)
        i2 = jnp.min(jnp.where(rest == m2, lane, LANES), axis=1, keepdims=True)
        e2 = jnp.exp(m2 - m1)
        g1 = 1.0 / (1.0 + e2)
        comb_sc[...] = jnp.where(lane == i1, g1, jnp.where(lane == i2, e2 * g1, 0.0))
        member = jnp.where(lane == i1, 1.0, jnp.where(lane == i2, 1.0, 0.0))
        tri = jnp.where(lax.broadcasted_iota(jnp.int32, (TB, TB), 0) >= lax.broadcasted_iota(jnp.int32, (TB, TB), 1),
                        1.0, 0.0).astype(BF16)
        carry = jnp.zeros((1, LANES), F32)
        for b in range(tm // TB):
            mb = member[b * TB:(b + 1) * TB]
            incl = jnp.dot(tri, mb.astype(BF16), preferred_element_type=F32)
            posc_sc[b * TB:(b + 1) * TB, :] = jnp.where(mb > 0.0, incl - 1.0 + carry, -1.0)
            carry = carry + incl[TB - 1:TB, :]
        for b in range(tm // LANES):
            posr_sc[:, b * LANES:(b + 1) * LANES] = posc_sc[b * LANES:(b + 1) * LANES, :].T
        lane_row = lax.broadcasted_iota(jnp.int32, (1, LANES), 1)
        for x in range(N_EXPERTS):
            count = jnp.sum(jnp.where(lane_row == x, carry, 0.0))
            nblk_sc[x] = ((count + (SB - 1.0)) * (1.0 / SB)).astype(jnp.int32)

    nblk = nblk_sc[e]

    def for_slot_rows(fn):
        def pair_body(r, carry):
            fn(pl.multiple_of(r * 2 * SB, 16), 2 * SB)
            return carry

        lax.fori_loop(0, nblk // 2, pair_body, 0)

        @pl.when(nblk % 2 == 1)
        def _():
            fn(pl.multiple_of((nblk - 1) * SB, 16), SB)

    @pl.when(f == 0)
    def _():
        slots_of_tokens = posr_sc[pl.ds(e, 1), :]

        def compact_rows(row0, nrows):
            slot = (row0 + lax.broadcasted_iota(jnp.int32, (nrows, tm), 0)).astype(F32)
            onehot = jnp.where(slots_of_tokens == slot, 1.0, 0.0).astype(BF16)
            xs_sc[pl.ds(row0, nrows), :] = jnp.dot(onehot, xn_sc[...], preferred_element_type=F32).astype(BF16)
            ys_sc[pl.ds(row0, nrows), :] = jnp.zeros((nrows, D_MODEL), F32)

        for_slot_rows(compact_rows)

    def ffn_rows(row0, nrows):
        xs = xs_sc[pl.ds(row0, nrows), :]
        gate = jnp.dot(xs, wg_ref[0], preferred_element_type=F32)
        up = jnp.dot(xs, wu_ref[0], preferred_element_type=F32)
        act = (gate * jax.nn.sigmoid(gate) * up).astype(BF16)
        ys_sc[pl.ds(row0, nrows), :] += jnp.dot(act, wd_ref[0], preferred_element_type=F32)

    for_slot_rows(ffn_rows)

    @pl.when(f == pl.num_programs(2) - 1)
    def _():
        on_e = lane == e
        gate_e = jnp.sum(jnp.where(on_e, comb_sc[...], 0.0), axis=1, keepdims=True)
        slot_e = jnp.sum(jnp.where(on_e, posc_sc[...], 0.0), axis=1, keepdims=True)

        def scatter_rows(row0, nrows):
            ys = ys_sc[pl.ds(row0, nrows), :].astype(BF16)
            slot = (row0 + lax.broadcasted_iota(jnp.int32, (TB, nrows), 1)).astype(F32)
            for tb in range(tm // TB):
                rows = slice(tb * TB, (tb + 1) * TB)
                onehot = jnp.where(slot_e[rows] == slot, 1.0, 0.0).astype(BF16)
                o_ref[rows, :] += gate_e[rows] * jnp.dot(onehot, ys, preferred_element_type=F32)

        for_slot_rows(scatter_rows)


def _moe_slot_block(tm):
    return MOE_SLOT_BLOCK if tm > MOE_SLOT_BLOCK else tm


def _moe_token_block(tm):
    return MOE_TOKEN_BLOCK if tm % MOE_TOKEN_BLOCK == 0 else tm


def _moe(h, gain, wr, wg, wu, wd, layer, tm, tf):
    T = h.shape[0]
    sb = _moe_slot_block(tm)
    cap = -(-tm // sb) * sb
    return pl.pallas_call(
        _moe_kernel,
        grid=(T // tm, N_EXPERTS, D_FF // tf),
        in_specs=[pl.BlockSpec((tm, D_MODEL), lambda i, e, f: (i, 0), pipeline_mode=pl.Buffered(1)),
                  pl.BlockSpec((1, D_MODEL), lambda i, e, f: (0, 0)),
                  pl.BlockSpec((D_MODEL, LANES), lambda i, e, f: (0, 0)),
                  pl.BlockSpec((1, D_MODEL, tf), lambda i, e, f: (layer * N_EXPERTS + e, 0, f)),
                  pl.BlockSpec((1, D_MODEL, tf), lambda i, e, f: (layer * N_EXPERTS + e, 0, f)),
                  pl.BlockSpec((1, tf, D_MODEL), lambda i, e, f: (layer * N_EXPERTS + e, f, 0))],
        out_specs=pl.BlockSpec((tm, D_MODEL), lambda i, e, f: (i, 0)),
        out_shape=jax.ShapeDtypeStruct((T, D_MODEL), F32),
        scratch_shapes=[pltpu.VMEM((tm, D_MODEL), BF16), pltpu.VMEM((cap, D_MODEL), BF16),
                        pltpu.VMEM((cap, D_MODEL), F32), pltpu.VMEM((tm, LANES), F32),
                        pltpu.VMEM((LANES, tm), F32), pltpu.VMEM((tm, LANES), F32),
                        pltpu.SMEM((N_EXPERTS,), jnp.int32)],
        compiler_params=_cparams(("arbitrary", "arbitrary", "arbitrary")),
        name="moe_sparse",
    )(h, gain, wr, wg, wu, wd)


def _copy_kernel(x_ref, o_ref):
    o_ref[...] = x_ref[...]


def _drop_meta(h3, seq):
    B, _, D = h3.shape
    rows = 512 if seq % 512 == 0 else seq
    return pl.pallas_call(
        _copy_kernel,
        grid=(B, seq // rows),
        in_specs=[pl.BlockSpec((pl.Element(1), pl.Element(rows), pl.Element(D)),
                               lambda b, i: (b, pl.multiple_of(N_META + i * rows, SUBLANES), 0))],
        out_specs=pl.BlockSpec((1, rows, D), lambda b, i: (b, i, 0)),
        out_shape=jax.ShapeDtypeStruct((B, seq, D), h3.dtype),
        compiler_params=_cparams(("arbitrary", "arbitrary")),
        name="drop_meta",
    )(h3)


def _rope_tables(seq_pad):
    pos = jnp.arange(seq_pad, dtype=F32)[:, None]
    lane = np.arange(LANES)

    def table(head_dim):
        half = head_dim // 2
        inv = ROPE_THETA ** (-jnp.arange(half, dtype=F32) / half)
        d = lane % head_dim
        ang = pos * inv[d % half][None, :]
        return jnp.cos(ang), jnp.sin(ang), jnp.asarray(d < half)[None, :]

    c64, s64, lo64 = table(ATT_HEAD_DIM)
    c128, s128, lo128 = table(IDX_DIM)
    return (c64, jnp.where(lo64, -s64, 0.0), jnp.where(lo64, 0.0, s64), c128, jnp.where(lo128, -s128, s128))


def _pad_cols(w, n):
    return jnp.pad(w, ((0, 0), (0, n - w.shape[1])))


def kernel(x, meta, norm_mixer, norm_ffn, dsa_w_in, dsa_q_norm, dsa_k_norm, dsa_w_out, mlstm_w_in, mlstm_b_i,
           mlstm_b_f, mlstm_out_norm, mlstm_w_out, ffn_w_gate, ffn_w_up, ffn_w_down, moe_router, moe_w_gate,
           moe_w_up, moe_w_down):
    B, S, D = x.shape
    L = S + N_META
    top_k = min(TOPK_MAX, S // 4)
    seq_pad = max(-(-L // Q_BLOCK) * Q_BLOCK, KEY_CHUNK)
    T = B * seq_pad
    tm_proj = 384 if seq_pad % 384 == 0 else Q_BLOCK
    tm_ffn = 768 if T % 768 == 0 else Q_BLOCK
    tm_moe = 1536 if T % 1536 == 0 else Q_BLOCK
    tf = 512

    h = jnp.concatenate([jnp.broadcast_to(meta[None].astype(x.dtype), (B, N_META, D)), x,
                         jnp.zeros((B, seq_pad - L, D), x.dtype)], axis=1).reshape(T, D)
    tabs = _rope_tables(seq_pad)
    depth = norm_mixer.shape[0]
    ffn_w = [w.astype(BF16) for w in (ffn_w_gate, ffn_w_up, ffn_w_down)]
    moe_w = [w.astype(BF16).reshape((-1,) + w.shape[2:]) for w in (moe_w_gate, moe_w_up, moe_w_down)]

    for i in range(depth):
        j = i // 2
        gain_m = norm_mixer[i][None, :]
        gain_f = norm_ffn[i][None, :]
        if i % 2 == 0:
            w_in = _pad_cols(dsa_w_in[j], DSA_IN_PAD).astype(BF16)
            gq = jnp.tile(dsa_q_norm[j], 2)[None, :]
            gk = jnp.tile(dsa_k_norm[j], 2)[None, :]
            q, k, v, iq, ik, iw = _dsa_in_proj(h, gain_m, w_in, gq, gk, tabs, seq_pad, tm_proj)
            k, v, ik = [a.reshape(B, seq_pad, a.shape[-1]) for a in (k, v, ik)]
            att = _dsa_core(q, k, v, iq, ik, iw, top_k)
            h = _ffn(h, att.reshape(T, D), dsa_w_out[j].astype(BF16), gain_f, *ffn_w, j, tm_ffn, tf)
        else:
            w_in = _pad_cols(mlstm_w_in[j], MLSTM_IN_PAD).astype(BF16)
            q, k, v, o, gates = _mlstm_in_proj(h, gain_m, w_in, tm_proj)
            bias8 = jnp.concatenate([mlstm_b_i[j], mlstm_b_f[j]])[:, None]
            y = _mlstm_core(*[a.reshape(B, seq_pad, a.shape[-1]) for a in (q, k, v, o, gates)], bias8,
                            mlstm_out_norm[j][None, :])
            h = _out_proj(h, y.reshape(T, D), mlstm_w_out[j].astype(BF16), tm_proj)
            wr = _pad_cols(moe_router[j], LANES)
            h = _moe(h, gain_f, wr, *moe_w, j, tm_moe, tf)
    return _drop_meta(h.reshape(B, seq_pad, D), S)
```

```python
import functools

import jax
import jax.numpy as jnp
import numpy as np
from jax import lax
from jax.experimental import pallas as pl
from jax.experimental.pallas import tpu as pltpu

F32 = jnp.float32
BF16 = jnp.bfloat16

D_MODEL = 1024
N_META = 16
RMS_EPS = 1e-6
ROPE_THETA = 10000.0
ATT_HEADS = 16
ATT_KV_HEADS = 4
ATT_HEAD_DIM = 64
ATT_GROUP = 4
IDX_HEADS = 8
IDX_DIM = 128
TOPK_MAX = 256
M_HEADS = 4
M_QK_DIM = 128
M_V_DIM = 256
GATE_CAP = 15.0
D_FF = 3584
N_EXPERTS = 8

LANES = 128
SUBLANES = 8
Q_BLOCK = 128
KEY_CHUNK = 256
ATT_STEPS_PER_TRIP = 4
CHUNKS_PER_TRIP = 4
M_CHUNK = 128
M_BATCH_ROWS = 1
MOE_SLOT_BLOCK = 208
MOE_TOKEN_BLOCK = 256
VMEM_LIMIT = 52 * 1024 * 1024
INT_MIN = -(2 ** 31)
F32_LOWEST = -3.4028234663852886e38
INT_MAX = 2 ** 31 - 1
LOG2E = 1.4426950408889634

DSA_IN_PAD = 2816
MLSTM_IN_PAD = 3200


def _cparams(sem):
    return pltpu.CompilerParams(dimension_semantics=sem, vmem_limit_bytes=VMEM_LIMIT)


def _unrolled_loop(n, fn, unroll):
    def body(t, carry):
        for u in range(unroll):
            fn(unroll * t + u)
        return carry

    lax.fori_loop(0, n // unroll, body, 0)
    rest = n % unroll
    done = n - rest
    run = unroll // 2
    while run >= 1:
        @pl.when((rest & run) != 0)
        def _(run=run, done=done):
            for u in range(run):
                fn(done + u)

        done = done + (rest & run)
        run //= 2


def _rms_rows(x, gain):
    ms = jnp.mean(x * x, axis=-1, keepdims=True)
    return x * lax.rsqrt(ms + RMS_EPS) * gain


def _dsa_in_kernel(h_ref, g_ref, w_ref, gq_ref, gk_ref, c64_ref, sa64_ref, sb64_ref, c128_ref, s128_ref,
                   smat_ref, q_ref, k_ref, v_ref, iq_ref, ik_ref, iw_ref):
    xn = _rms_rows(h_ref[...], g_ref[...]).astype(BF16)
    z = jnp.dot(xn, w_ref[...], preferred_element_type=F32)
    c64, sa64, sb64 = c64_ref[...], sa64_ref[...], sb64_ref[...]
    c128, s128 = c128_ref[...], s128_ref[...]
    smat = smat_ref[...]

    def head_norm_rope(zc, gain):
        z2 = zc * zc
        hi = z2.astype(BF16)
        lo = (z2 - hi.astype(F32)).astype(BF16)
        ms = jnp.dot(hi, smat, preferred_element_type=F32) + jnp.dot(lo, smat, preferred_element_type=F32)
        y = zc * lax.rsqrt(ms + RMS_EPS) * gain
        return y * c64 + pltpu.roll(y, 96, 1) * sa64 + pltpu.roll(y, 32, 1) * sb64

    nsub = z.shape[0] // Q_BLOCK
    lane = lax.broadcasted_iota(jnp.int32, (z.shape[0], LANES), 1)
    for c in range(8):
        r = head_norm_rope(z[:, c * LANES:(c + 1) * LANES], gq_ref[...]) * (ATT_HEAD_DIM ** -0.5 * LOG2E)
        swapped = pltpu.roll(r, 64, 1)
        for half in range(2):
            h = 2 * c + half
            g, rr = h // ATT_GROUP, h % ATT_GROUP
            keep = (lane >= 64) if g % 2 else (lane < 64)
            val = jnp.where(keep, r if half == g % 2 else swapped, 0.0).astype(BF16)
            for t in range(nsub):
                q_ref[t, g, rr * Q_BLOCK:(rr + 1) * Q_BLOCK, :] = val[t * Q_BLOCK:(t + 1) * Q_BLOCK]
    for c in range(2):
        r = head_norm_rope(z[:, 1024 + c * LANES:1024 + (c + 1) * LANES], gk_ref[...])
        k_ref[:, c * LANES:(c + 1) * LANES] = r.astype(BF16)
    ones = jnp.ones((z.shape[0], LANES), BF16)
    for c in range(2):
        v_ref[:, 2 * c * LANES:(2 * c + 1) * LANES] = z[:, 1280 + c * LANES:1280 + (c + 1) * LANES].astype(BF16)
        v_ref[:, (2 * c + 1) * LANES:(2 * c + 2) * LANES] = ones
    for c in range(9):
        zc = z[:, 1536 + c * LANES:1536 + (c + 1) * LANES]
        r = zc * c128 + pltpu.roll(zc, 64, 1) * s128
        if c < 8:
            for t in range(nsub):
                iq_ref[t, :, c * Q_BLOCK:(c + 1) * Q_BLOCK] = r[t * Q_BLOCK:(t + 1) * Q_BLOCK].T.astype(BF16)
        else:
            ik_ref[...] = r.astype(BF16)
    iw = z[:, 2688:2816] * (IDX_HEADS ** -0.5 * IDX_DIM ** -0.5)
    for t in range(nsub):
        iw_ref[t] = iw[t * Q_BLOCK:(t + 1) * Q_BLOCK].T


def _dsa_in_proj(h, gain, w, gq, gk, tabs, seq_pad, tm):
    T = h.shape[0]
    nt = seq_pad // tm
    row = lambda i: (i, 0)
    fixed = lambda i: (0, 0)
    pos = lambda i: (i % nt, 0)
    tab_spec = pl.BlockSpec((tm, LANES), pos)
    smat = jnp.asarray(np.kron(np.eye(2), np.full((64, 64), 1.0 / 64)), BF16)
    nsub = tm // Q_BLOCK
    rows = ATT_GROUP * Q_BLOCK
    outs = [((ATT_KV_HEADS, rows, LANES), BF16), (256, BF16), (512, BF16),
            ((IDX_DIM, IDX_HEADS * Q_BLOCK), BF16), (128, BF16), ((LANES, Q_BLOCK), F32)]

    def spec(o):
        if isinstance(o[0], int):
            return pl.BlockSpec((tm, o[0]), row)
        return pl.BlockSpec((nsub,) + o[0], lambda i: (i,) + (0,) * len(o[0]))

    def shape(o):
        if isinstance(o[0], int):
            return jax.ShapeDtypeStruct((T, o[0]), o[1])
        return jax.ShapeDtypeStruct((T // Q_BLOCK,) + o[0], o[1])

    return pl.pallas_call(
        _dsa_in_kernel,
        grid=(T // tm,),
        in_specs=[pl.BlockSpec((tm, D_MODEL), row), pl.BlockSpec((1, D_MODEL), fixed),
                  pl.BlockSpec((D_MODEL, DSA_IN_PAD), fixed), pl.BlockSpec((1, LANES), fixed),
                  pl.BlockSpec((1, LANES), fixed), tab_spec, tab_spec, tab_spec, tab_spec, tab_spec,
                  pl.BlockSpec((LANES, LANES), fixed)],
        out_specs=[spec(o) for o in outs],
        out_shape=[shape(o) for o in outs],
        compiler_params=_cparams(("arbitrary",)),
        name="dsa_in_proj",
    )(h, gain, w, gq, gk, *tabs, smat)


def _dsa_core_kernel(q_ref, k_ref, v_ref, iq_ref, ik_ref, iw_ref, o_ref,
                     keys_sc, bias_sc, cut_sc, m_sc, l_sc, acc_sc, *, top_k, seq_pad):
    i = pl.program_id(1)
    QB, KC = Q_BLOCK, KEY_CHUNK
    nch = (i * QB + QB + KC - 1) // KC
    kf = float(top_k)
    lane1 = lax.broadcasted_iota(jnp.int32, (QB, LANES), 1)

    qpos = i * QB + lax.broadcasted_iota(jnp.int32, (KC, QB), 1)
    krow = lax.broadcasted_iota(jnp.int32, (KC, QB), 0)

    def chunk_start(j):
        return pl.multiple_of(jnp.minimum(j * KC, seq_pad - KC), LANES)

    def key_valid(j, start):
        kidx = start + krow
        return (kidx <= qpos) & (kidx >= j * KC)

    def score_chunk(j):
        start = chunk_start(j)
        s = jnp.dot(ik_ref[0, pl.ds(start, KC), :], iq_ref[0], preferred_element_type=F32)
        acc = jnp.zeros((KC, QB), F32)
        for h in range(IDX_HEADS):
            acc = acc + jnp.maximum(s[:, h * QB:(h + 1) * QB], 0.0) * iw_ref[0, h:h + 1, :]
        keys_sc[j] = jnp.where(key_valid(j, start), acc, F32_LOWEST)

    _unrolled_loop(nch, score_chunk, CHUNKS_PER_TRIP)

    tiles = (KC // SUBLANES, SUBLANES, QB)
    krow3 = (lax.broadcasted_iota(jnp.int32, tiles, 0) * SUBLANES + lax.broadcasted_iota(jnp.int32, tiles, 1))

    keys_sc[nch] = jnp.full((KC, QB), F32_LOWEST, F32)

    def count(pred):
        def body(j2, acc):
            for j in (2 * j2, 2 * j2 + 1):
                kk = keys_sc[j].reshape(tiles)
                hit = jnp.where(pred(kk, chunk_start(j)), jnp.float32(1.0), jnp.float32(0.0))
                while hit.shape[0] > 1:
                    half = hit.shape[0] // 2
                    hit = hit[:half] + hit[half:]
                acc = acc + hit[0]
            return acc
        acc = lax.fori_loop(0, (nch + 1) // 2, body, jnp.zeros((SUBLANES, QB), F32))
        for shift in (4, 2, 1):
            acc = acc + pltpu.roll(acc, shift, 0)
        return acc

    def as_score(code):
        return pltpu.bitcast(jnp.where(code >= 0, code, code ^ jnp.int32(INT_MAX)), F32)

    zero = jnp.zeros((SUBLANES, QB), jnp.int32)
    v0 = jnp.where(count(lambda kk, base: kk >= 0.0) >= kf, zero, jnp.int32(INT_MIN))

    def bit_body(b, v):
        code = v | lax.shift_left(jnp.int32(1), (30 - b).astype(jnp.int32))
        cand = as_score(code)
        return jnp.where(count(lambda kk, base: kk >= cand) >= kf, code, v)

    thr = as_score(lax.fori_loop(jnp.int32(0), jnp.int32(31), bit_body, v0))
    cut_sc[...] = jnp.full((SUBLANES, QB), INT_MAX, jnp.int32)
    n_ge = count(lambda kk, base: kk >= thr)

    @pl.when(jnp.max(n_ge) > kf)
    def _():
        need = kf - count(lambda kk, base: kk > thr)

        def cut_body(b, c):
            cand = c | lax.shift_left(jnp.int32(1), (12 - b).astype(jnp.int32))
            n = count(lambda kk, base: jnp.where(kk == thr, base + krow3, jnp.int32(INT_MAX)) < cand)
            return jnp.where(n < need, cand, c)

        cut_sc[...] = lax.fori_loop(jnp.int32(0), jnp.int32(13), cut_body, zero)

    thr1 = thr[0:1, :]
    cut1 = cut_sc[0:1, :]

    def bias_chunk(j):
        kk = keys_sc[j]
        start = chunk_start(j)
        sel = (kk > thr1) | ((kk == thr1) & (start + krow <= cut1))
        bt = jnp.where(sel & key_valid(j, start), 0.0, -1e30)
        bias_sc[j] = jnp.concatenate([bt[c * QB:(c + 1) * QB].T for c in range(KC // QB)], axis=1)

    _unrolled_loop(nch, bias_chunk, CHUNKS_PER_TRIP)

    m_sc[...] = jnp.full(m_sc.shape, -1e9, F32)
    l_sc[...] = jnp.zeros(l_sc.shape, F32)
    acc_sc[...] = jnp.zeros(acc_sc.shape, F32)

    def att_step(start, width, bias, groups):
        bias = jnp.concatenate([bias] * ATT_GROUP, axis=0)
        for g in groups:
            cg = g // 2
            kj = k_ref[0, pl.ds(start, width), cg * LANES:(cg + 1) * LANES]
            vj = v_ref[0, pl.ds(start, width), cg * 2 * LANES:(cg + 1) * 2 * LANES]
            s = lax.dot_general(q_ref[0, g], kj, (((1,), (1,)), ((), ())), preferred_element_type=F32) + bias
            m_old = m_sc[g]
            m_new = jnp.maximum(m_old, jnp.max(s, axis=1, keepdims=True))
            alpha = jnp.exp2(m_old - m_new)
            p = jnp.exp2(s - jnp.concatenate([m_new] * (width // LANES), axis=1)).astype(BF16)
            pv = jnp.dot(p, vj, preferred_element_type=F32)
            l_sc[g] = alpha * l_sc[g] + pv[:, LANES:]
            acc_sc[g] = alpha * acc_sc[g] + pv[:, :LANES]
            m_sc[g] = m_new

    npair = jnp.minimum(nch // 2, seq_pad // (2 * KC))

    def single_body(j, carry):
        att_step(chunk_start(j), KC, bias_sc[j], range(ATT_KV_HEADS))
        return carry

    def pair_step(j):
        bias = jnp.concatenate([bias_sc[2 * j], bias_sc[2 * j + 1]], axis=1)
        att_step(pl.multiple_of(j * 2 * KC, 2 * KC), 2 * KC, bias, range(ATT_KV_HEADS))

    _unrolled_loop(npair, pair_step, ATT_STEPS_PER_TRIP)
    lax.fori_loop(2 * npair, nch, single_body, 0)

    for g in range(ATT_KV_HEADS):
        og = acc_sc[g] / l_sc[g]
        for pair in range(2):
            even = og[(2 * pair) * QB:(2 * pair + 1) * QB]
            odd = og[(2 * pair + 1) * QB:(2 * pair + 2) * QB]
            if g % 2:
                even = pltpu.roll(even, 64, 1)
            else:
                odd = pltpu.roll(odd, 64, 1)
            c = 2 * g + pair
            o_ref[0, :, c * LANES:(c + 1) * LANES] = jnp.where(lane1 < 64, even, odd).astype(BF16)


def _dsa_core(q, k, v, iq, ik, iw, top_k):
    B, seq_pad = k.shape[0], k.shape[1]
    nqb = seq_pad // Q_BLOCK
    nkc = -(-seq_pad // KEY_CHUNK)
    rows = ATT_GROUP * Q_BLOCK
    qblock = lambda a: pl.BlockSpec((1,) + a.shape[1:], lambda b, i: (b * nqb + i,) + (0,) * (a.ndim - 1))
    whole = lambda n: pl.BlockSpec((1, seq_pad, n), lambda b, i: (b, 0, 0))
    return pl.pallas_call(
        functools.partial(_dsa_core_kernel, top_k=top_k, seq_pad=seq_pad),
        grid=(B, nqb),
        in_specs=[qblock(q), whole(256), whole(512), qblock(iq), whole(IDX_DIM), qblock(iw)],
        out_specs=pl.BlockSpec((1, Q_BLOCK, D_MODEL), lambda b, i: (b, i, 0)),
        out_shape=jax.ShapeDtypeStruct((B, seq_pad, D_MODEL), BF16),
        scratch_shapes=[
            pltpu.VMEM((nkc + 1, KEY_CHUNK, Q_BLOCK), F32),
            pltpu.VMEM((nkc, Q_BLOCK, KEY_CHUNK), F32),
            pltpu.VMEM((SUBLANES, Q_BLOCK), jnp.int32),
            pltpu.VMEM((ATT_KV_HEADS, rows, LANES), F32),
            pltpu.VMEM((ATT_KV_HEADS, rows, LANES), F32),
            pltpu.VMEM((ATT_KV_HEADS, rows, LANES), F32),
        ],
        compiler_params=_cparams(("arbitrary", "arbitrary")),
        name="dsa_core",
    )(q, k, v, iq, ik, iw)


def _mlstm_in_kernel(h_ref, g_ref, w_ref, q_ref, k_ref, v_ref, o_ref, gate_ref):
    xn = _rms_rows(h_ref[...], g_ref[...]).astype(BF16)
    z = jnp.dot(xn, w_ref[...], preferred_element_type=F32)
    q_ref[...] = z[:, 0:512].astype(BF16)
    k_ref[...] = (z[:, 512:1024] * (M_QK_DIM ** -0.5)).astype(BF16)
    v_ref[...] = z[:, 1024:2048].astype(BF16)
    o_ref[...] = z[:, 2048:3072]
    gate_ref[...] = z[:, 3072:3200]


def _mlstm_in_proj(h, gain, w, tm):
    T = h.shape[0]
    row = lambda i: (i, 0)
    fixed = lambda i: (0, 0)
    outs = [(512, BF16), (512, BF16), (1024, BF16), (1024, F32), (128, F32)]
    return pl.pallas_call(
        _mlstm_in_kernel,
        grid=(T // tm,),
        in_specs=[pl.BlockSpec((tm, D_MODEL), row), pl.BlockSpec((1, D_MODEL), fixed),
                  pl.BlockSpec((D_MODEL, MLSTM_IN_PAD), fixed)],
        out_specs=[pl.BlockSpec((tm, n), row) for n, _ in outs],
        out_shape=[jax.ShapeDtypeStruct((T, n), dt) for n, dt in outs],
        compiler_params=_cparams(("arbitrary",)),
        name="mlstm_in_proj",
    )(h, gain, w)


def _mlstm_kernel(q_ref, k_ref, v_ref, o_ref, gate_ref, h_ref, bias_ref, gout_ref, wo_ref, hout_ref,
                  ct_sc, n_sc, m_sc, y_sc):
    C = M_CHUNK

    @pl.when(pl.program_id(1) == 0)
    def _():
        ct_sc[...] = jnp.zeros(ct_sc.shape, F32)
        n_sc[...] = jnp.zeros(n_sc.shape, F32)
        m_sc[...] = jnp.zeros(m_sc.shape, F32)

    lane8 = lax.broadcasted_iota(jnp.int32, (8, C), 1)
    t_idx = lax.broadcasted_iota(jnp.int32, (C, C), 0)
    s_idx = lax.broadcasted_iota(jnp.int32, (C, C), 1)
    for bi in range(q_ref.shape[0]):
        pre = gate_ref[bi].T[0:8, :] + bias_ref[...]
        capped = GATE_CAP * jnp.tanh(pre / GATE_CAP)
        log_f = -(jnp.maximum(-capped, 0.0) + jnp.log1p(jnp.exp(-jnp.abs(capped))))
        b = log_f
        sh = 1
        while sh < C:
            b = b + jnp.where(lane8 >= sh, pltpu.roll(b, sh, 1), 0.0)
            sh *= 2
        stacked = jnp.concatenate([b[4:8], capped[0:4]], axis=0)
        cols = jnp.concatenate([stacked, jnp.zeros((C - 8, C), F32)], axis=0).T

        for h in range(M_HEADS):
            st = bi * M_HEADS + h
            qh = q_ref[bi, :, h * M_QK_DIM:(h + 1) * M_QK_DIM]
            kh = k_ref[bi, :, h * M_QK_DIM:(h + 1) * M_QK_DIM]
            vh = v_ref[bi, :, h * M_V_DIM:(h + 1) * M_V_DIM]
            b_row, li_row = stacked[h:h + 1, :], stacked[4 + h:5 + h, :]
            b_col, li_col = cols[:, h:h + 1], cols[:, 4 + h:5 + h]
            m_st = m_sc[st:st + 1, 0:1]
            dmat = jnp.where(s_idx <= t_idx, b_col - b_row + li_row, -jnp.inf)
            inter = b_col + m_st
            m_t = jnp.maximum(inter, jnp.max(dmat, axis=1, keepdims=True))
            w_inter = jnp.exp(inter - m_t)
            qk = lax.dot_general(qh, kh, (((1,), (1,)), ((), ())), preferred_element_type=F32)
            s = qk * jnp.exp(dmat - m_t)
            ct = ct_sc[st]
            num = (w_inter * jnp.dot(qh, ct.astype(BF16), preferred_element_type=F32)
                   + jnp.dot(s.astype(BF16), vh, preferred_element_type=F32))
            qn = jnp.sum(qh.astype(F32) * n_sc[st:st + 1, :], axis=1, keepdims=True)
            den = w_inter * qn + jnp.sum(s, axis=1, keepdims=True)
            hout = num / jnp.maximum(jnp.abs(den), jnp.exp(-m_t))

            b_last = b_row[:, C - 1:C]
            m_new = jnp.maximum(b_last + m_st, jnp.max(b_last - b_row + li_row, axis=1, keepdims=True))
            decay = jnp.exp(b_last + m_st - m_new)
            wk = jnp.exp(b_last - b_col + li_col - m_new)
            kw = kh.astype(F32) * wk
            ct_sc[st] = decay * ct + jnp.dot(kw.T.astype(BF16), vh, preferred_element_type=F32)
            n_sc[st:st + 1, :] = decay * n_sc[st:st + 1, :] + jnp.sum(kw, axis=0, keepdims=True)
            m_sc[st:st + 1, :] = jnp.broadcast_to(m_new, (1, LANES))

            cs = slice(h * M_V_DIM, (h + 1) * M_V_DIM)
            hn = _rms_rows(hout, gout_ref[:, cs])
            y_sc[:, cs] = (hn * jax.nn.sigmoid(o_ref[bi, :, cs])).astype(BF16)

        hout_ref[bi] = h_ref[bi] + jnp.dot(y_sc[...], wo_ref[...], preferred_element_type=F32)


def _mlstm_core(q, k, v, o, gates, h, bias8, gout, wo):
    B, seq_pad = q.shape[0], q.shape[1]
    nb = M_BATCH_ROWS
    C = M_CHUNK
    blk = lambda n: pl.BlockSpec((nb, C, n), lambda b, c: (b, c, 0))
    fixed = lambda b, c: (0, 0)
    return pl.pallas_call(
        _mlstm_kernel,
        grid=(B // nb, seq_pad // C),
        in_specs=[blk(512), blk(512), blk(1024), blk(1024), blk(LANES), blk(D_MODEL),
                  pl.BlockSpec((8, 1), fixed), pl.BlockSpec((1, D_MODEL), fixed),
                  pl.BlockSpec((D_MODEL, D_MODEL), fixed)],
        out_specs=blk(D_MODEL),
        out_shape=jax.ShapeDtypeStruct((B, seq_pad, D_MODEL), F32),
        scratch_shapes=[pltpu.VMEM((nb * M_HEADS, M_QK_DIM, M_V_DIM), F32),
                        pltpu.VMEM((nb * M_HEADS, M_QK_DIM), F32),
                        pltpu.VMEM((nb * M_HEADS, LANES), F32),
                        pltpu.VMEM((C, D_MODEL), BF16)],
        compiler_params=_cparams(("arbitrary", "arbitrary")),
        name="mlstm_core",
    )(q, k, v, o, gates, h, bias8, gout, wo)


def _ffn_kernel(h_ref, y_ref, wo_ref, g_ref, wg_ref, wu_ref, wd_ref, o_ref, xn_sc, acc_sc):
    f = pl.program_id(1)

    @pl.when(f == 0)
    def _():
        x = h_ref[...] + jnp.dot(y_ref[...], wo_ref[...], preferred_element_type=F32)
        xn_sc[...] = _rms_rows(x, g_ref[...]).astype(BF16)
        acc_sc[...] = x

    xn = xn_sc[...]
    gate = jnp.dot(xn, wg_ref[0], preferred_element_type=F32)
    up = jnp.dot(xn, wu_ref[0], preferred_element_type=F32)
    act = (gate * jax.nn.sigmoid(gate) * up).astype(BF16)
    acc_sc[...] += jnp.dot(act, wd_ref[0], preferred_element_type=F32)

    @pl.when(f == pl.num_programs(1) - 1)
    def _():
        o_ref[...] = acc_sc[...]


def _ffn(h, y, wo, gain, wg, wu, wd, layer, tm, tf):
    T = h.shape[0]
    return pl.pallas_call(
        _ffn_kernel,
        grid=(T // tm, D_FF // tf),
        in_specs=[pl.BlockSpec((tm, D_MODEL), lambda i, f: (i, 0)),
                  pl.BlockSpec((tm, D_MODEL), lambda i, f: (i, 0)),
                  pl.BlockSpec((D_MODEL, D_MODEL), lambda i, f: (0, 0)),
                  pl.BlockSpec((1, D_MODEL), lambda i, f: (0, 0)),
                  pl.BlockSpec((1, D_MODEL, tf), lambda i, f: (layer, 0, f)),
                  pl.BlockSpec((1, D_MODEL, tf), lambda i, f: (layer, 0, f)),
                  pl.BlockSpec((1, tf, D_MODEL), lambda i, f: (layer, f, 0))],
        out_specs=pl.BlockSpec((tm, D_MODEL), lambda i, f: (i, 0)),
        out_shape=jax.ShapeDtypeStruct((T, D_MODEL), F32),
        scratch_shapes=[pltpu.VMEM((tm, D_MODEL), BF16), pltpu.VMEM((tm, D_MODEL), F32)],
        compiler_params=_cparams(("arbitrary", "arbitrary")),
        name="ffn_dense",
    )(h, y, wo, gain, wg, wu, wd)


def _moe_kernel(h_ref, g_ref, wr_ref, wg_ref, wu_ref, wd_ref, o_ref,
                xn_sc, xs_sc, ys_sc, posc_sc, posr_sc, comb_sc, nblk_sc):
    e = pl.program_id(1)
    f = pl.program_id(2)
    tm = h_ref.shape[0]
    SB = _moe_slot_block(tm)
    TB = _moe_token_block(tm)
    lane = lax.broadcasted_iota(jnp.int32, (tm, LANES), 1)

    @pl.when((e == 0) & (f == 0))
    def _():
        x = h_ref[...]
        xn = _rms_rows(x, g_ref[...])
        xn_sc[...] = xn.astype(BF16)
        o_ref[...] = x
        logits = jnp.dot(xn, wr_ref[...], preferred_element_type=F32, precision=lax.Precision.HIGHEST)
        logits = jnp.where(lane < N_EXPERTS, logits, -jnp.inf)
        m1 = jnp.max(logits, axis=1, keepdims=True)
        i1 = jnp.min(jnp.where(logits == m1, lane, LANES), axis=1, keepdims=True)
        rest = jnp.where(lane == i1, -jnp.inf, logits)
        m2 = jnp.max(rest, axis=1, keepdims=True)
        i2 = jnp.min(jnp.where(rest == m2, lane, LANES), axis=1, keepdims=True)
        e2 = jnp.exp(m2 - m1)
        g1 = 1.0 / (1.0 + e2)
        comb_sc[...] = jnp.where(lane == i1, g1, jnp.where(lane == i2, e2 * g1, 0.0))
        member = jnp.where(lane == i1, 1.0, jnp.where(lane == i2, 1.0, 0.0))
        tri = jnp.where(lax.broadcasted_iota(jnp.int32, (TB, TB), 0) >= lax.broadcasted_iota(jnp.int32, (TB, TB), 1),
                        1.0, 0.0).astype(BF16)
        carry = jnp.zeros((1, LANES), F32)
        for b in range(tm // TB):
            mb = member[b * TB:(b + 1) * TB]
            incl = jnp.dot(tri, mb.astype(BF16), preferred_element_type=F32)
            posc_sc[b * TB:(b + 1) * TB, :] = jnp.where(mb > 0.0, incl - 1.0 + carry, -1.0)
            carry = carry + incl[TB - 1:TB, :]
        for b in range(tm // LANES):
            posr_sc[:, b * LANES:(b + 1) * LANES] = posc_sc[b * LANES:(b + 1) * LANES, :].T
        lane_row = lax.broadcasted_iota(jnp.int32, (1, LANES), 1)
        for x in range(N_EXPERTS):
            count = jnp.sum(jnp.where(lane_row == x, carry, 0.0))
            nblk_sc[x] = ((count + (SB - 1.0)) * (1.0 / SB)).astype(jnp.int32)

    nblk = nblk_sc[e]

    def for_slot_rows(fn):
        def pair_body(r, carry):
            fn(pl.multiple_of(r * 2 * SB, 16), 2 * SB)
            return carry

        lax.fori_loop(0, nblk // 2, pair_body, 0)

        @pl.when(nblk % 2 == 1)
        def _():
            fn(pl.multiple_of((nblk - 1) * SB, 16), SB)

    @pl.when(f == 0)
    def _():
        slots_of_tokens = posr_sc[pl.ds(e, 1), :]

        def compact_rows(row0, nrows):
            slot = (row0 + lax.broadcasted_iota(jnp.int32, (nrows, tm), 0)).astype(F32)
            onehot = jnp.where(slots_of_tokens == slot, 1.0, 0.0).astype(BF16)
            xs_sc[pl.ds(row0, nrows), :] = jnp.dot(onehot, xn_sc[...], preferred_element_type=F32).astype(BF16)
            ys_sc[pl.ds(row0, nrows), :] = jnp.zeros((nrows, D_MODEL), F32)

        for_slot_rows(compact_rows)

    def ffn_rows(row0, nrows):
        xs = xs_sc[pl.ds(row0, nrows), :]
        gate = jnp.dot(xs, wg_ref[0], preferred_element_type=F32)
        up = jnp.dot(xs, wu_ref[0], preferred_element_type=F32)
        act = (gate * jax.nn.sigmoid(gate) * up).astype(BF16)
        ys_sc[pl.ds(row0, nrows), :] += jnp.dot(act, wd_ref[0], preferred_element_type=F32)

    for_slot_rows(ffn_rows)

    @pl.when(f == pl.num_programs(2) - 1)
    def _():
        on_e = lane == e
        gate_e = jnp.sum(jnp.where(on_e, comb_sc[...], 0.0), axis=1, keepdims=True)
        slot_e = jnp.sum(jnp.where(on_e, posc_sc[...], 0.0), axis=1, keepdims=True)

        def scatter_rows(row0, nrows):
            ys = ys_sc[pl.ds(row0, nrows), :].astype(BF16)
            slot = (row0 + lax.broadcasted_iota(jnp.int32, (TB, nrows), 1)).astype(F32)
            for tb in range(tm // TB):
                rows = slice(tb * TB, (tb + 1) * TB)
                onehot = jnp.where(slot_e[rows] == slot, 1.0, 0.0).astype(BF16)
                o_ref[rows, :] += gate_e[rows] * jnp.dot(onehot, ys, preferred_element_type=F32)

        for_slot_rows(scatter_rows)


def _moe_slot_block(tm):
    return MOE_SLOT_BLOCK if tm > MOE_SLOT_BLOCK else tm


def _moe_token_block(tm):
    return MOE_TOKEN_BLOCK if tm % MOE_TOKEN_BLOCK == 0 else tm


def _moe(h, gain, wr, wg, wu, wd, layer, tm, tf):
    T = h.shape[0]
    sb = _moe_slot_block(tm)
    cap = -(-tm // sb) * sb
    return pl.pallas_call(
        _moe_kernel,
        grid=(T // tm, N_EXPERTS, D_FF // tf),
        in_specs=[pl.BlockSpec((tm, D_MODEL), lambda i, e, f: (i, 0), pipeline_mode=pl.Buffered(1)),
                  pl.BlockSpec((1, D_MODEL), lambda i, e, f: (0, 0)),
                  pl.BlockSpec((D_MODEL, LANES), lambda i, e, f: (0, 0)),
                  pl.BlockSpec((1, D_MODEL, tf), lambda i, e, f: (layer * N_EXPERTS + e, 0, f)),
                  pl.BlockSpec((1, D_MODEL, tf), lambda i, e, f: (layer * N_EXPERTS + e, 0, f)),
                  pl.BlockSpec((1, tf, D_MODEL), lambda i, e, f: (layer * N_EXPERTS + e, f, 0))],
        out_specs=pl.BlockSpec((tm, D_MODEL), lambda i, e, f: (i, 0)),
        out_shape=jax.ShapeDtypeStruct((T, D_MODEL), F32),
        scratch_shapes=[pltpu.VMEM((tm, D_MODEL), BF16), pltpu.VMEM((cap, D_MODEL), BF16),
                        pltpu.VMEM((cap, D_MODEL), F32), pltpu.VMEM((tm, LANES), F32),
                        pltpu.VMEM((LANES, tm), F32), pltpu.VMEM((tm, LANES), F32),
                        pltpu.SMEM((N_EXPERTS,), jnp.int32)],
        compiler_params=_cparams(("arbitrary", "arbitrary", "arbitrary")),
        name="moe_sparse",
    )(h, gain, wr, wg, wu, wd)


def _copy_kernel(x_ref, o_ref):
    o_ref[...] = x_ref[...]


def _drop_meta(h3, seq):
    B, _, D = h3.shape
    rows = 512 if seq % 512 == 0 else seq
    return pl.pallas_call(
        _copy_kernel,
        grid=(B, seq // rows),
        in_specs=[pl.BlockSpec((pl.Element(1), pl.Element(rows), pl.Element(D)),
                               lambda b, i: (b, pl.multiple_of(N_META + i * rows, SUBLANES), 0))],
        out_specs=pl.BlockSpec((1, rows, D), lambda b, i: (b, i, 0)),
        out_shape=jax.ShapeDtypeStruct((B, seq, D), h3.dtype),
        compiler_params=_cparams(("arbitrary", "arbitrary")),
        name="drop_meta",
    )(h3)


def _rope_tables(seq_pad):
    pos = jnp.arange(seq_pad, dtype=F32)[:, None]
    lane = np.arange(LANES)

    def table(head_dim):
        half = head_dim // 2
        inv = ROPE_THETA ** (-jnp.arange(half, dtype=F32) / half)
        d = lane % head_dim
        ang = pos * inv[d % half][None, :]
        return jnp.cos(ang), jnp.sin(ang), jnp.asarray(d < half)[None, :]

    c64, s64, lo64 = table(ATT_HEAD_DIM)
    c128, s128, lo128 = table(IDX_DIM)
    return (c64, jnp.where(lo64, -s64, 0.0), jnp.where(lo64, 0.0, s64), c128, jnp.where(lo128, -s128, s128))


def _pad_cols(w, n):
    return jnp.pad(w, ((0, 0), (0, n - w.shape[1])))


def kernel(x, meta, norm_mixer, norm_ffn, dsa_w_in, dsa_q_norm, dsa_k_norm, dsa_w_out, mlstm_w_in, mlstm_b_i,
           mlstm_b_f, mlstm_out_norm, mlstm_w_out, ffn_w_gate, ffn_w_up, ffn_w_down, moe_router, moe_w_gate,
           moe_w_up, moe_w_down):
    B, S, D = x.shape
    L = S + N_META
    top_k = min(TOPK_MAX, S // 4)
    seq_pad = max(-(-L // Q_BLOCK) * Q_BLOCK, KEY_CHUNK)
    T = B * seq_pad
    tm_proj = 384 if seq_pad % 384 == 0 else Q_BLOCK
    tm_ffn = 768 if T % 768 == 0 else Q_BLOCK
    tm_moe = 1536 if T % 1536 == 0 else Q_BLOCK
    tf = 512

    h = jnp.concatenate([jnp.broadcast_to(meta[None].astype(x.dtype), (B, N_META, D)), x,
                         jnp.zeros((B, seq_pad - L, D), x.dtype)], axis=1).reshape(T, D)
    tabs = _rope_tables(seq_pad)
    depth = norm_mixer.shape[0]
    ffn_w = [w.astype(BF16) for w in (ffn_w_gate, ffn_w_up, ffn_w_down)]
    moe_w = [w.astype(BF16).reshape((-1,) + w.shape[2:]) for w in (moe_w_gate, moe_w_up, moe_w_down)]

    for i in range(depth):
        j = i // 2
        gain_m = norm_mixer[i][None, :]
        gain_f = norm_ffn[i][None, :]
        if i % 2 == 0:
            w_in = _pad_cols(dsa_w_in[j], DSA_IN_PAD).astype(BF16)
            gq = jnp.tile(dsa_q_norm[j], 2)[None, :]
            gk = jnp.tile(dsa_k_norm[j], 2)[None, :]
            q, k, v, iq, ik, iw = _dsa_in_proj(h, gain_m, w_in, gq, gk, tabs, seq_pad, tm_proj)
            k, v, ik = [a.reshape(B, seq_pad, a.shape[-1]) for a in (k, v, ik)]
            att = _dsa_core(q, k, v, iq, ik, iw, top_k)
            h = _ffn(h, att.reshape(T, D), dsa_w_out[j].astype(BF16), gain_f, *ffn_w, j, tm_ffn, tf)
        else:
            w_in = _pad_cols(mlstm_w_in[j], MLSTM_IN_PAD).astype(BF16)
            q, k, v, o, gates = _mlstm_in_proj(h, gain_m, w_in, tm_proj)
            bias8 = jnp.concatenate([mlstm_b_i[j], mlstm_b_f[j]])[:, None]
            h = _mlstm_core(*[a.reshape(B, seq_pad, a.shape[-1]) for a in (q, k, v, o, gates, h)], bias8,
                            mlstm_out_norm[j][None, :], mlstm_w_out[j].astype(BF16)).reshape(T, D)
            wr = _pad_cols(moe_router[j], LANES)
            h = _moe(h, gain_f, wr, *moe_w, j, tm_moe, tf)
    return _drop_meta(h.reshape(B, seq_pad, D), S)
```

```python
import functools

import jax
import jax.numpy as jnp
import numpy as np
from jax import lax
from jax.experimental import pallas as pl
from jax.experimental.pallas import tpu as pltpu

F32 = jnp.float32
BF16 = jnp.bfloat16

D_MODEL = 1024
N_META = 16
RMS_EPS = 1e-6
ROPE_THETA = 10000.0
ATT_HEADS = 16
ATT_KV_HEADS = 4
ATT_HEAD_DIM = 64
ATT_GROUP = 4
IDX_HEADS = 8
IDX_DIM = 128
TOPK_MAX = 256
M_HEADS = 4
M_QK_DIM = 128
M_V_DIM = 256
GATE_CAP = 15.0
D_FF = 3584
N_EXPERTS = 8

LANES = 128
SUBLANES = 8
Q_BLOCK = 128
KEY_CHUNK = 256
ATT_STEPS_PER_TRIP = 4
CHUNKS_PER_TRIP = 4
M_CHUNK = 128
M_BATCH_ROWS = 1
MOE_SLOT_BLOCK = 208
MOE_TOKEN_BLOCK = 256
VMEM_LIMIT = 52 * 1024 * 1024
INT_MIN = -(2 ** 31)
F32_LOWEST = -3.4028234663852886e38
INT_MAX = 2 ** 31 - 1
LOG2E = 1.4426950408889634

DSA_IN_PAD = 2816
MLSTM_IN_PAD = 3200


def _cparams(sem):
    return pltpu.CompilerParams(dimension_semantics=sem, vmem_limit_bytes=VMEM_LIMIT)


def _unrolled_loop(n, fn, unroll):
    def body(t, carry):
        for u in range(unroll):
            fn(unroll * t + u)
        return carry

    lax.fori_loop(0, n // unroll, body, 0)
    rest = n % unroll
    done = n - rest
    run = unroll // 2
    while run >= 1:
        @pl.when((rest & run) != 0)
        def _(run=run, done=done):
            for u in range(run):
                fn(done + u)

        done = done + (rest & run)
        run //= 2


def _rms_rows(x, gain):
    ms = jnp.mean(x * x, axis=-1, keepdims=True)
    return x * lax.rsqrt(ms + RMS_EPS) * gain


def _dsa_in_kernel(h_ref, g_ref, w_ref, gq_ref, gk_ref, c64_ref, sa64_ref, sb64_ref, c128_ref, s128_ref,
                   smat_ref, q_ref, k_ref, v_ref, iq_ref, ik_ref, iw_ref):
    xn = _rms_rows(h_ref[...], g_ref[...]).astype(BF16)
    z = jnp.dot(xn, w_ref[...], preferred_element_type=F32)
    c64, sa64, sb64 = c64_ref[...], sa64_ref[...], sb64_ref[...]
    c128, s128 = c128_ref[...], s128_ref[...]
    smat = smat_ref[...]

    def head_norm_rope(zc, gain):
        z2 = zc * zc
        hi = z2.astype(BF16)
        lo = (z2 - hi.astype(F32)).astype(BF16)
        ms = jnp.dot(hi, smat, preferred_element_type=F32) + jnp.dot(lo, smat, preferred_element_type=F32)
        y = zc * lax.rsqrt(ms + RMS_EPS) * gain
        return y * c64 + pltpu.roll(y, 96, 1) * sa64 + pltpu.roll(y, 32, 1) * sb64

    nsub = z.shape[0] // Q_BLOCK
    lane = lax.broadcasted_iota(jnp.int32, (z.shape[0], LANES), 1)
    for c in range(8):
        r = head_norm_rope(z[:, c * LANES:(c + 1) * LANES], gq_ref[...]) * (ATT_HEAD_DIM ** -0.5 * LOG2E)
        swapped = pltpu.roll(r, 64, 1)
        for half in range(2):
            h = 2 * c + half
            g, rr = h // ATT_GROUP, h % ATT_GROUP
            keep = (lane >= 64) if g % 2 else (lane < 64)
            val = jnp.where(keep, r if half == g % 2 else swapped, 0.0).astype(BF16)
            for t in range(nsub):
                q_ref[t, g, rr * Q_BLOCK:(rr + 1) * Q_BLOCK, :] = val[t * Q_BLOCK:(t + 1) * Q_BLOCK]
    for c in range(2):
        r = head_norm_rope(z[:, 1024 + c * LANES:1024 + (c + 1) * LANES], gk_ref[...])
        k_ref[:, c * LANES:(c + 1) * LANES] = r.astype(BF16)
    ones = jnp.ones((z.shape[0], LANES), BF16)
    for c in range(2):
        v_ref[:, 2 * c * LANES:(2 * c + 1) * LANES] = z[:, 1280 + c * LANES:1280 + (c + 1) * LANES].astype(BF16)
        v_ref[:, (2 * c + 1) * LANES:(2 * c + 2) * LANES] = ones
    for c in range(9):
        zc = z[:, 1536 + c * LANES:1536 + (c + 1) * LANES]
        r = zc * c128 + pltpu.roll(zc, 64, 1) * s128
        if c < 8:
            for t in range(nsub):
                iq_ref[t, :, c * Q_BLOCK:(c + 1) * Q_BLOCK] = r[t * Q_BLOCK:(t + 1) * Q_BLOCK].T.astype(BF16)
        else:
            ik_ref[...] = r.astype(BF16)
    iw = z[:, 2688:2816] * (IDX_HEADS ** -0.5 * IDX_DIM ** -0.5)
    for t in range(nsub):
        iw_ref[t] = iw[t * Q_BLOCK:(t + 1) * Q_BLOCK].T


def _dsa_in_proj(h, gain, w, gq, gk, tabs, seq_pad, tm):
    T = h.shape[0]
    nt = seq_pad // tm
    row = lambda i: (i, 0)
    fixed = lambda i: (0, 0)
    pos = lambda i: (i % nt, 0)
    tab_spec = pl.BlockSpec((tm, LANES), pos)
    smat = jnp.asarray(np.kron(np.eye(2), np.full((64, 64), 1.0 / 64)), BF16)
    nsub = tm // Q_BLOCK
    rows = ATT_GROUP * Q_BLOCK
    outs = [((ATT_KV_HEADS, rows, LANES), BF16), (256, BF16), (512, BF16),
            ((IDX_DIM, IDX_HEADS * Q_BLOCK), BF16), (128, BF16), ((LANES, Q_BLOCK), F32)]

    def spec(o):
        if isinstance(o[0], int):
            return pl.BlockSpec((tm, o[0]), row)
        return pl.BlockSpec((nsub,) + o[0], lambda i: (i,) + (0,) * len(o[0]))

    def shape(o):
        if isinstance(o[0], int):
            return jax.ShapeDtypeStruct((T, o[0]), o[1])
        return jax.ShapeDtypeStruct((T // Q_BLOCK,) + o[0], o[1])

    return pl.pallas_call(
        _dsa_in_kernel,
        grid=(T // tm,),
        in_specs=[pl.BlockSpec((tm, D_MODEL), row), pl.BlockSpec((1, D_MODEL), fixed),
                  pl.BlockSpec((D_MODEL, DSA_IN_PAD), fixed), pl.BlockSpec((1, LANES), fixed),
                  pl.BlockSpec((1, LANES), fixed), tab_spec, tab_spec, tab_spec, tab_spec, tab_spec,
                  pl.BlockSpec((LANES, LANES), fixed)],
        out_specs=[spec(o) for o in outs],
        out_shape=[shape(o) for o in outs],
        compiler_params=_cparams(("arbitrary",)),
        name="dsa_in_proj",
    )(h, gain, w, gq, gk, *tabs, smat)


def _dsa_core_kernel(q_ref, k_ref, v_ref, iq_ref, ik_ref, iw_ref, o_ref,
                     keys_sc, bias_sc, cut_sc, m_sc, l_sc, acc_sc, *, top_k, seq_pad):
    i = pl.program_id(1)
    QB, KC = Q_BLOCK, KEY_CHUNK
    nch = (i * QB + QB + KC - 1) // KC
    kf = float(top_k)
    lane1 = lax.broadcasted_iota(jnp.int32, (QB, LANES), 1)

    qpos = i * QB + lax.broadcasted_iota(jnp.int32, (KC, QB), 1)
    krow = lax.broadcasted_iota(jnp.int32, (KC, QB), 0)

    def chunk_start(j):
        return pl.multiple_of(jnp.minimum(j * KC, seq_pad - KC), LANES)

    def key_valid(j, start):
        kidx = start + krow
        return (kidx <= qpos) & (kidx >= j * KC)

    def score_chunk(j):
        start = chunk_start(j)
        s = jnp.dot(ik_ref[0, pl.ds(start, KC), :], iq_ref[0], preferred_element_type=F32)
        acc = jnp.zeros((KC, QB), F32)
        for h in range(IDX_HEADS):
            acc = acc + jnp.maximum(s[:, h * QB:(h + 1) * QB], 0.0) * iw_ref[0, h:h + 1, :]
        keys_sc[j] = jnp.where(key_valid(j, start), acc, F32_LOWEST)

    _unrolled_loop(nch, score_chunk, CHUNKS_PER_TRIP)

    tiles = (KC // SUBLANES, SUBLANES, QB)
    krow3 = (lax.broadcasted_iota(jnp.int32, tiles, 0) * SUBLANES + lax.broadcasted_iota(jnp.int32, tiles, 1))

    keys_sc[nch] = jnp.full((KC, QB), F32_LOWEST, F32)

    def count(pred):
        def body(j2, acc):
            for j in (2 * j2, 2 * j2 + 1):
                kk = keys_sc[j].reshape(tiles)
                hit = jnp.where(pred(kk, chunk_start(j)), jnp.float32(1.0), jnp.float32(0.0))
                while hit.shape[0] > 1:
                    half = hit.shape[0] // 2
                    hit = hit[:half] + hit[half:]
                acc = acc + hit[0]
            return acc
        acc = lax.fori_loop(0, (nch + 1) // 2, body, jnp.zeros((SUBLANES, QB), F32))
        for shift in (4, 2, 1):
            acc = acc + pltpu.roll(acc, shift, 0)
        return acc

    def as_score(code):
        bits = jnp.where(code >= 0, code, code ^ jnp.int32(INT_MAX))
        expo = lax.shift_right_logical(bits, jnp.int32(23)) & jnp.int32(0xFF)
        mant = 1.0 + (bits & jnp.int32(0x7FFFFF)).astype(F32) * (2.0 ** -23)
        scale = jnp.where(expo >= 128, jnp.float32(2.0), jnp.float32(2.0 ** -64))
        for i in range(7):
            scale = scale * jnp.where((expo & jnp.int32(1 << i)) != 0, jnp.float32(2.0 ** (1 << i)), jnp.float32(1.0))
        scale = jnp.where(expo >= 128, scale, scale * jnp.float32(2.0 ** -63))
        mag = jnp.where(expo == 0, jnp.float32(0.0), jnp.where(expo == 255, jnp.float32(jnp.inf), mant * scale))
        return jnp.where(bits < 0, -mag, mag)

    zero = jnp.zeros((SUBLANES, QB), jnp.int32)
    v0 = jnp.where(count(lambda kk, base: kk >= 0.0) >= kf, zero, jnp.int32(INT_MIN))

    def bit_body(b, v):
        code = v | lax.shift_left(jnp.int32(1), (30 - b).astype(jnp.int32))
        cand = as_score(code)
        return jnp.where(count(lambda kk, base: kk >= cand) >= kf, code, v)

    thr = as_score(lax.fori_loop(jnp.int32(0), jnp.int32(31), bit_body, v0))
    cut_sc[...] = jnp.full((SUBLANES, QB), INT_MAX, jnp.int32)
    n_ge = count(lambda kk, base: kk >= thr)

    @pl.when(jnp.max(n_ge) > kf)
    def _():
        need = kf - count(lambda kk, base: kk > thr)

        def cut_body(b, c):
            cand = c | lax.shift_left(jnp.int32(1), (12 - b).astype(jnp.int32))
            n = count(lambda kk, base: jnp.where(kk == thr, base + krow3, jnp.int32(INT_MAX)) < cand)
            return jnp.where(n < need, cand, c)

        cut_sc[...] = lax.fori_loop(jnp.int32(0), jnp.int32(13), cut_body, zero)

    thr1 = thr[0:1, :]
    cut1 = cut_sc[0:1, :]

    def bias_chunk(j):
        kk = keys_sc[j]
        start = chunk_start(j)
        sel = (kk > thr1) | ((kk == thr1) & (start + krow <= cut1))
        bt = jnp.where(sel & key_valid(j, start), 0.0, -1e30)
        bias_sc[j] = jnp.concatenate([bt[c * QB:(c + 1) * QB].T for c in range(KC // QB)], axis=1)

    _unrolled_loop(nch, bias_chunk, CHUNKS_PER_TRIP)

    m_sc[...] = jnp.full(m_sc.shape, -1e9, F32)
    l_sc[...] = jnp.zeros(l_sc.shape, F32)
    acc_sc[...] = jnp.zeros(acc_sc.shape, F32)

    def att_step(start, width, bias, groups):
        bias = jnp.concatenate([bias] * ATT_GROUP, axis=0)
        for g in groups:
            cg = g // 2
            kj = k_ref[0, pl.ds(start, width), cg * LANES:(cg + 1) * LANES]
            vj = v_ref[0, pl.ds(start, width), cg * 2 * LANES:(cg + 1) * 2 * LANES]
            s = lax.dot_general(q_ref[0, g], kj, (((1,), (1,)), ((), ())), preferred_element_type=F32) + bias
            m_old = m_sc[g]
            m_new = jnp.maximum(m_old, jnp.max(s, axis=1, keepdims=True))
            alpha = jnp.exp2(m_old - m_new)
            p = jnp.exp2(s - jnp.concatenate([m_new] * (width // LANES), axis=1)).astype(BF16)
            pv = jnp.dot(p, vj, preferred_element_type=F32)
            l_sc[g] = alpha * l_sc[g] + pv[:, LANES:]
            acc_sc[g] = alpha * acc_sc[g] + pv[:, :LANES]
            m_sc[g] = m_new

    npair = jnp.minimum(nch // 2, seq_pad // (2 * KC))

    def single_body(j, carry):
        att_step(chunk_start(j), KC, bias_sc[j], range(ATT_KV_HEADS))
        return carry

    def pair_step(j):
        bias = jnp.concatenate([bias_sc[2 * j], bias_sc[2 * j + 1]], axis=1)
        att_step(pl.multiple_of(j * 2 * KC, 2 * KC), 2 * KC, bias, range(ATT_KV_HEADS))

    _unrolled_loop(npair, pair_step, ATT_STEPS_PER_TRIP)
    lax.fori_loop(2 * npair, nch, single_body, 0)

    for g in range(ATT_KV_HEADS):
        og = acc_sc[g] / l_sc[g]
        for pair in range(2):
            even = og[(2 * pair) * QB:(2 * pair + 1) * QB]
            odd = og[(2 * pair + 1) * QB:(2 * pair + 2) * QB]
            if g % 2:
                even = pltpu.roll(even, 64, 1)
            else:
                odd = pltpu.roll(odd, 64, 1)
            c = 2 * g + pair
            o_ref[0, :, c * LANES:(c + 1) * LANES] = jnp.where(lane1 < 64, even, odd).astype(BF16)


def _dsa_core(q, k, v, iq, ik, iw, top_k):
    B, seq_pad = k.shape[0], k.shape[1]
    nqb = seq_pad // Q_BLOCK
    nkc = -(-seq_pad // KEY_CHUNK)
    rows = ATT_GROUP * Q_BLOCK
    qblock = lambda a: pl.BlockSpec((1,) + a.shape[1:], lambda b, i: (b * nqb + i,) + (0,) * (a.ndim - 1))
    whole = lambda n: pl.BlockSpec((1, seq_pad, n), lambda b, i: (b, 0, 0))
    return pl.pallas_call(
        functools.partial(_dsa_core_kernel, top_k=top_k, seq_pad=seq_pad),
        grid=(B, nqb),
        in_specs=[qblock(q), whole(256), whole(512), qblock(iq), whole(IDX_DIM), qblock(iw)],
        out_specs=pl.BlockSpec((1, Q_BLOCK, D_MODEL), lambda b, i: (b, i, 0)),
        out_shape=jax.ShapeDtypeStruct((B, seq_pad, D_MODEL), BF16),
        scratch_shapes=[
            pltpu.VMEM((nkc + 1, KEY_CHUNK, Q_BLOCK), F32),
            pltpu.VMEM((nkc, Q_BLOCK, KEY_CHUNK), F32),
            pltpu.VMEM((SUBLANES, Q_BLOCK), jnp.int32),
            pltpu.VMEM((ATT_KV_HEADS, rows, LANES), F32),
            pltpu.VMEM((ATT_KV_HEADS, rows, LANES), F32),
            pltpu.VMEM((ATT_KV_HEADS, rows, LANES), F32),
        ],
        compiler_params=_cparams(("arbitrary", "arbitrary")),
        name="dsa_core",
    )(q, k, v, iq, ik, iw)


def _mlstm_in_kernel(h_ref, g_ref, w_ref, q_ref, k_ref, v_ref, o_ref, gate_ref):
    xn = _rms_rows(h_ref[...], g_ref[...]).astype(BF16)
    z = jnp.dot(xn, w_ref[...], preferred_element_type=F32)
    q_ref[...] = z[:, 0:512].astype(BF16)
    k_ref[...] = (z[:, 512:1024] * (M_QK_DIM ** -0.5)).astype(BF16)
    v_ref[...] = z[:, 1024:2048].astype(BF16)
    o_ref[...] = z[:, 2048:3072]
    gate_ref[...] = z[:, 3072:3200]


def _mlstm_in_proj(h, gain, w, tm):
    T = h.shape[0]
    row = lambda i: (i, 0)
    fixed = lambda i: (0, 0)
    outs = [(512, BF16), (512, BF16), (1024, BF16), (1024, F32), (128, F32)]
    return pl.pallas_call(
        _mlstm_in_kernel,
        grid=(T // tm,),
        in_specs=[pl.BlockSpec((tm, D_MODEL), row), pl.BlockSpec((1, D_MODEL), fixed),
                  pl.BlockSpec((D_MODEL, MLSTM_IN_PAD), fixed)],
        out_specs=[pl.BlockSpec((tm, n), row) for n, _ in outs],
        out_shape=[jax.ShapeDtypeStruct((T, n), dt) for n, dt in outs],
        compiler_params=_cparams(("arbitrary",)),
        name="mlstm_in_proj",
    )(h, gain, w)


def _mlstm_kernel(q_ref, k_ref, v_ref, o_ref, gate_ref, h_ref, bias_ref, gout_ref, wo_ref, hout_ref,
                  ct_sc, n_sc, m_sc, y_sc):
    C = M_CHUNK

    @pl.when(pl.program_id(1) == 0)
    def _():
        ct_sc[...] = jnp.zeros(ct_sc.shape, F32)
        n_sc[...] = jnp.zeros(n_sc.shape, F32)
        m_sc[...] = jnp.zeros(m_sc.shape, F32)

    lane8 = lax.broadcasted_iota(jnp.int32, (8, C), 1)
    t_idx = lax.broadcasted_iota(jnp.int32, (C, C), 0)
    s_idx = lax.broadcasted_iota(jnp.int32, (C, C), 1)
    for bi in range(q_ref.shape[0]):
        pre = gate_ref[bi].T[0:8, :] + bias_ref[...]
        capped = GATE_CAP * jnp.tanh(pre / GATE_CAP)
        log_f = -(jnp.maximum(-capped, 0.0) + jnp.log1p(jnp.exp(-jnp.abs(capped))))
        b = log_f
        sh = 1
        while sh < C:
            b = b + jnp.where(lane8 >= sh, pltpu.roll(b, sh, 1), 0.0)
            sh *= 2
        stacked = jnp.concatenate([b[4:8], capped[0:4]], axis=0)
        cols = jnp.concatenate([stacked, jnp.zeros((C - 8, C), F32)], axis=0).T

        for h in range(M_HEADS):
            st = bi * M_HEADS + h
            qh = q_ref[bi, :, h * M_QK_DIM:(h + 1) * M_QK_DIM]
            kh = k_ref[bi, :, h * M_QK_DIM:(h + 1) * M_QK_DIM]
            vh = v_ref[bi, :, h * M_V_DIM:(h + 1) * M_V_DIM]
            b_row, li_row = stacked[h:h + 1, :], stacked[4 + h:5 + h, :]
            b_col, li_col = cols[:, h:h + 1], cols[:, 4 + h:5 + h]
            m_st = m_sc[st:st + 1, 0:1]
            dmat = jnp.where(s_idx <= t_idx, b_col - b_row + li_row, -jnp.inf)
            inter = b_col + m_st
            m_t = jnp.maximum(inter, jnp.max(dmat, axis=1, keepdims=True))
            w_inter = jnp.exp(inter - m_t)
            qk = lax.dot_general(qh, kh, (((1,), (1,)), ((), ())), preferred_element_type=F32)
            s = qk * jnp.exp(dmat - m_t)
            ct = ct_sc[st]
            num = (w_inter * jnp.dot(qh, ct.astype(BF16), preferred_element_type=F32)
                   + jnp.dot(s.astype(BF16), vh, preferred_element_type=F32))
            qn = jnp.sum(qh.astype(F32) * n_sc[st:st + 1, :], axis=1, keepdims=True)
            den = w_inter * qn + jnp.sum(s, axis=1, keepdims=True)
            hout = num / jnp.maximum(jnp.abs(den), jnp.exp(-m_t))

            b_last = b_row[:, C - 1:C]
            m_new = jnp.maximum(b_last + m_st, jnp.max(b_last - b_row + li_row, axis=1, keepdims=True))
            decay = jnp.exp(b_last + m_st - m_new)
            wk = jnp.exp(b_last - b_col + li_col - m_new)
            kw = kh.astype(F32) * wk
            ct_sc[st] = decay * ct + jnp.dot(kw.T.astype(BF16), vh, preferred_element_type=F32)
            n_sc[st:st + 1, :] = decay * n_sc[st:st + 1, :] + jnp.sum(kw, axis=0, keepdims=True)
            m_sc[st:st + 1, :] = jnp.broadcast_to(m_new, (1, LANES))

            cs = slice(h * M_V_DIM, (h + 1) * M_V_DIM)
            hn = _rms_rows(hout, gout_ref[:, cs])
            y_sc[:, cs] = (hn * jax.nn.sigmoid(o_ref[bi, :, cs])).astype(BF16)

        hout_ref[bi] = h_ref[bi] + jnp.dot(y_sc[...], wo_ref[...], preferred_element_type=F32)


def _mlstm_core(q, k, v, o, gates, h, bias8, gout, wo):
    B, seq_pad = q.shape[0], q.shape[1]
    nb = M_BATCH_ROWS
    C = M_CHUNK
    blk = lambda n: pl.BlockSpec((nb, C, n), lambda b, c: (b, c, 0))
    fixed = lambda b, c: (0, 0)
    return pl.pallas_call(
        _mlstm_kernel,
        grid=(B // nb, seq_pad // C),
        in_specs=[blk(512), blk(512), blk(1024), blk(1024), blk(LANES), blk(D_MODEL),
                  pl.BlockSpec((8, 1), fixed), pl.BlockSpec((1, D_MODEL), fixed),
                  pl.BlockSpec((D_MODEL, D_MODEL), fixed)],
        out_specs=blk(D_MODEL),
        out_shape=jax.ShapeDtypeStruct((B, seq_pad, D_MODEL), F32),
        scratch_shapes=[pltpu.VMEM((nb * M_HEADS, M_QK_DIM, M_V_DIM), F32),
                        pltpu.VMEM((nb * M_HEADS, M_QK_DIM), F32),
                        pltpu.VMEM((nb * M_HEADS, LANES), F32),
                        pltpu.VMEM((C, D_MODEL), BF16)],
        compiler_params=_cparams(("arbitrary", "arbitrary")),
        name="mlstm_core",
    )(q, k, v, o, gates, h, bias8, gout, wo)


def _ffn_kernel(h_ref, y_ref, wo_ref, g_ref, wg_ref, wu_ref, wd_ref, o_ref, xn_sc, acc_sc):
    f = pl.program_id(1)

    @pl.when(f == 0)
    def _():
        x = h_ref[...] + jnp.dot(y_ref[...], wo_ref[...], preferred_element_type=F32)
        xn_sc[...] = _rms_rows(x, g_ref[...]).astype(BF16)
        acc_sc[...] = x

    xn = xn_sc[...]
    gate = jnp.dot(xn, wg_ref[0], preferred_element_type=F32)
    up = jnp.dot(xn, wu_ref[0], preferred_element_type=F32)
    act = (gate * jax.nn.sigmoid(gate) * up).astype(BF16)
    acc_sc[...] += jnp.dot(act, wd_ref[0], preferred_element_type=F32)

    @pl.when(f == pl.num_programs(1) - 1)
    def _():
        o_ref[...] = acc_sc[...]


def _ffn(h, y, wo, gain, wg, wu, wd, layer, tm, tf):
    T = h.shape[0]
    return pl.pallas_call(
        _ffn_kernel,
        grid=(T // tm, D_FF // tf),
        in_specs=[pl.BlockSpec((tm, D_MODEL), lambda i, f: (i, 0)),
                  pl.BlockSpec((tm, D_MODEL), lambda i, f: (i, 0)),
                  pl.BlockSpec((D_MODEL, D_MODEL), lambda i, f: (0, 0)),
                  pl.BlockSpec((1, D_MODEL), lambda i, f: (0, 0)),
                  pl.BlockSpec((1, D_MODEL, tf), lambda i, f: (layer, 0, f)),
                  pl.BlockSpec((1, D_MODEL, tf), lambda i, f: (layer, 0, f)),
                  pl.BlockSpec((1, tf, D_MODEL), lambda i, f: (layer, f, 0))],
        out_specs=pl.BlockSpec((tm, D_MODEL), lambda i, f: (i, 0)),
        out_shape=jax.ShapeDtypeStruct((T, D_MODEL), F32),
        scratch_shapes=[pltpu.VMEM((tm, D_MODEL), BF16), pltpu.VMEM((tm, D_MODEL), F32)],
        compiler_params=_cparams(("arbitrary", "arbitrary")),
        name="ffn_dense",
    )(h, y, wo, gain, wg, wu, wd)


def _moe_kernel(h_ref, g_ref, wr_ref, wg_ref, wu_ref, wd_ref, o_ref,
                xn_sc, xs_sc, ys_sc, posc_sc, posr_sc, comb_sc, nblk_sc):
    e = pl.program_id(1)
    f = pl.program_id(2)
    tm = h_ref.shape[0]
    SB = _moe_slot_block(tm)
    TB = _moe_token_block(tm)
    lane = lax.broadcasted_iota(jnp.int32, (tm, LANES), 1)

    @pl.when((e == 0) & (f == 0))
    def _():
        x = h_ref[...]
        xn = _rms_rows(x, g_ref[...])
        xn_sc[...] = xn.astype(BF16)
        o_ref[...] = x
        logits = jnp.dot(xn, wr_ref[...], preferred_element_type=F32, precision=lax.Precision.HIGHEST)
        logits = jnp.where(lane < N_EXPERTS, logits, -jnp.inf)
        m1 = jnp.max(logits, axis=1, keepdims=True)
        i1 = jnp.min(jnp.where(logits == m1, lane, LANES), axis=1, keepdims=True)
        rest = jnp.where(lane == i1, -jnp.inf, logits)
        m2 = jnp.max(rest, axis=1, keepdims=True)
        i2 = jnp.min(jnp.where(rest == m2, lane, LANES), axis=1, keepdims=True)
        e2 = jnp.exp(m2 - m1)
        g1 = 1.0 / (1.0 + e2)
        comb_sc[...] = jnp.where(lane == i1, g1, jnp.where(lane == i2, e2 * g1, 0.0))
        member = jnp.where(lane == i1, 1.0, jnp.where(lane == i2, 1.0, 0.0))
        tri = jnp.where(lax.broadcasted_iota(jnp.int32, (TB, TB), 0) >= lax.broadcasted_iota(jnp.int32, (TB, TB), 1),
                        1.0, 0.0).astype(BF16)
        carry = jnp.zeros((1, LANES), F32)
        for b in range(tm // TB):
            mb = member[b * TB:(b + 1) * TB]
            incl = jnp.dot(tri, mb.astype(BF16), preferred_element_type=F32)
            posc_sc[b * TB:(b + 1) * TB, :] = jnp.where(mb > 0.0, incl - 1.0 + carry, -1.0)
            carry = carry + incl[TB - 1:TB, :]
        for b in range(tm // LANES):
            posr_sc[:, b * LANES:(b + 1) * LANES] = posc_sc[b * LANES:(b + 1) * LANES, :].T
        lane_row = lax.broadcasted_iota(jnp.int32, (1, LANES), 1)
        for x in range(N_EXPERTS):
            count = jnp.sum(jnp.where(lane_row == x, carry, 0.0))
            nblk_sc[x] = ((count + (SB - 1.0)) * (1.0 / SB)).astype(jnp.int32)

    nblk = nblk_sc[e]

    def for_slot_rows(fn):
        def pair_body(r, carry):
            fn(pl.multiple_of(r * 2 * SB, 16), 2 * SB)
            return carry

        lax.fori_loop(0, nblk // 2, pair_body, 0)

        @pl.when(nblk % 2 == 1)
        def _():
            fn(pl.multiple_of((nblk - 1) * SB, 16), SB)

    @pl.when(f == 0)
    def _():
        slots_of_tokens = posr_sc[pl.ds(e, 1), :]

        def compact_rows(row0, nrows):
            slot = (row0 + lax.broadcasted_iota(jnp.int32, (nrows, tm), 0)).astype(F32)
            onehot = jnp.where(slots_of_tokens == slot, 1.0, 0.0).astype(BF16)
            xs_sc[pl.ds(row0, nrows), :] = jnp.dot(onehot, xn_sc[...], preferred_element_type=F32).astype(BF16)
            ys_sc[pl.ds(row0, nrows), :] = jnp.zeros((nrows, D_MODEL), F32)

        for_slot_rows(compact_rows)

    def ffn_rows(row0, nrows):
        xs = xs_sc[pl.ds(row0, nrows), :]
        gate = jnp.dot(xs, wg_ref[0], preferred_element_type=F32)
        up = jnp.dot(xs, wu_ref[0], preferred_element_type=F32)
        act = (gate * jax.nn.sigmoid(gate) * up).astype(BF16)
        ys_sc[pl.ds(row0, nrows), :] += jnp.dot(act, wd_ref[0], preferred_element_type=F32)

    for_slot_rows(ffn_rows)

    @pl.when(f == pl.num_programs(2) - 1)
    def _():
        on_e = lane == e
        gate_e = jnp.sum(jnp.where(on_e, comb_sc[...], 0.0), axis=1, keepdims=True)
        slot_e = jnp.sum(jnp.where(on_e, posc_sc[...], 0.0), axis=1, keepdims=True)

        def scatter_rows(row0, nrows):
            ys = ys_sc[pl.ds(row0, nrows), :].astype(BF16)
            slot = (row0 + lax.broadcasted_iota(jnp.int32, (TB, nrows), 1)).astype(F32)
            for tb in range(tm // TB):
                rows = slice(tb * TB, (tb + 1) * TB)
                onehot = jnp.where(slot_e[rows] == slot, 1.0, 0.0).astype(BF16)
                o_ref[rows, :] += gate_e[rows] * jnp.dot(onehot, ys, preferred_element_type=F32)

        for_slot_rows(scatter_rows)


def _moe_slot_block(tm):
    return MOE_SLOT_BLOCK if tm > MOE_SLOT_BLOCK else tm


def _moe_token_block(tm):
    return MOE_TOKEN_BLOCK if tm % MOE_TOKEN_BLOCK == 0 else tm


def _moe(h, gain, wr, wg, wu, wd, layer, tm, tf):
    T = h.shape[0]
    sb = _moe_slot_block(tm)
    cap = -(-tm // sb) * sb
    return pl.pallas_call(
        _moe_kernel,
        grid=(T // tm, N_EXPERTS, D_FF // tf),
        in_specs=[pl.BlockSpec((tm, D_MODEL), lambda i, e, f: (i, 0), pipeline_mode=pl.Buffered(1)),
                  pl.BlockSpec((1, D_MODEL), lambda i, e, f: (0, 0)),
                  pl.BlockSpec((D_MODEL, LANES), lambda i, e, f: (0, 0)),
                  pl.BlockSpec((1, D_MODEL, tf), lambda i, e, f: (layer * N_EXPERTS + e, 0, f)),
                  pl.BlockSpec((1, D_MODEL, tf), lambda i, e, f: (layer * N_EXPERTS + e, 0, f)),
                  pl.BlockSpec((1, tf, D_MODEL), lambda i, e, f: (layer * N_EXPERTS + e, f, 0))],
        out_specs=pl.BlockSpec((tm, D_MODEL), lambda i, e, f: (i, 0)),
        out_shape=jax.ShapeDtypeStruct((T, D_MODEL), F32),
        scratch_shapes=[pltpu.VMEM((tm, D_MODEL), BF16), pltpu.VMEM((cap, D_MODEL), BF16),
                        pltpu.VMEM((cap, D_MODEL), F32), pltpu.VMEM((tm, LANES), F32),
                        pltpu.VMEM((LANES, tm), F32), pltpu.VMEM((tm, LANES), F32),
                        pltpu.SMEM((N_EXPERTS,), jnp.int32)],
        compiler_params=_cparams(("arbitrary", "arbitrary", "arbitrary")),
        name="moe_sparse",
    )(h, gain, wr, wg, wu, wd)


def _copy_kernel(x_ref, o_ref):
    o_ref[...] = x_ref[...]


def _drop_meta(h3, seq):
    B, _, D = h3.shape
    rows = 512 if seq % 512 == 0 else seq
    return pl.pallas_call(
        _copy_kernel,
        grid=(B, seq // rows),
        in_specs=[pl.BlockSpec((pl.Element(1), pl.Element(rows), pl.Element(D)),
                               lambda b, i: (b, pl.multiple_of(N_META + i * rows, SUBLANES), 0))],
        out_specs=pl.BlockSpec((1, rows, D), lambda b, i: (b, i, 0)),
        out_shape=jax.ShapeDtypeStruct((B, seq, D), h3.dtype),
        compiler_params=_cparams(("arbitrary", "arbitrary")),
        name="drop_meta",
    )(h3)


def _rope_tables(seq_pad):
    pos = jnp.arange(seq_pad, dtype=F32)[:, None]
    lane = np.arange(LANES)

    def table(head_dim):
        half = head_dim // 2
        inv = ROPE_THETA ** (-jnp.arange(half, dtype=F32) / half)
        d = lane % head_dim
        ang = pos * inv[d % half][None, :]
        return jnp.cos(ang), jnp.sin(ang), jnp.asarray(d < half)[None, :]

    c64, s64, lo64 = table(ATT_HEAD_DIM)
    c128, s128, lo128 = table(IDX_DIM)
    return (c64, jnp.where(lo64, -s64, 0.0), jnp.where(lo64, 0.0, s64), c128, jnp.where(lo128, -s128, s128))


def _pad_cols(w, n):
    return jnp.pad(w, ((0, 0), (0, n - w.shape[1])))


def kernel(x, meta, norm_mixer, norm_ffn, dsa_w_in, dsa_q_norm, dsa_k_norm, dsa_w_out, mlstm_w_in, mlstm_b_i,
           mlstm_b_f, mlstm_out_norm, mlstm_w_out, ffn_w_gate, ffn_w_up, ffn_w_down, moe_router, moe_w_gate,
           moe_w_up, moe_w_down):
    B, S, D = x.shape
    L = S + N_META
    top_k = min(TOPK_MAX, S // 4)
    seq_pad = max(-(-L // Q_BLOCK) * Q_BLOCK, KEY_CHUNK)
    T = B * seq_pad
    tm_proj = 384 if seq_pad % 384 == 0 else Q_BLOCK
    tm_ffn = 768 if T % 768 == 0 else Q_BLOCK
    tm_moe = 1536 if T % 1536 == 0 else Q_BLOCK
    tf = 512

    h = jnp.concatenate([jnp.broadcast_to(meta[None].astype(x.dtype), (B, N_META, D)), x,
                         jnp.zeros((B, seq_pad - L, D), x.dtype)], axis=1).reshape(T, D)
    tabs = _rope_tables(seq_pad)
    depth = norm_mixer.shape[0]
    ffn_w = [w.astype(BF16) for w in (ffn_w_gate, ffn_w_up, ffn_w_down)]
    moe_w = [w.astype(BF16).reshape((-1,) + w.shape[2:]) for w in (moe_w_gate, moe_w_up, moe_w_down)]

    for i in range(depth):
        j = i // 2
        gain_m = norm_mixer[i][None, :]
        gain_f = norm_ffn[i][None, :]
        if i % 2 == 0:
            w_in = _pad_cols(dsa_w_in[j], DSA_IN_PAD).astype(BF16)
            gq = jnp.tile(dsa_q_norm[j], 2)[None, :]
            gk = jnp.tile(dsa_k_norm[j], 2)[None, :]
            q, k, v, iq, ik, iw = _dsa_in_proj(h, gain_m, w_in, gq, gk, tabs, seq_pad, tm_proj)
            k, v, ik = [a.reshape(B, seq_pad, a.shape[-1]) for a in (k, v, ik)]
            att = _dsa_core(q, k, v, iq, ik, iw, top_k)
            h = _ffn(h, att.reshape(T, D), dsa_w_out[j].astype(BF16), gain_f, *ffn_w, j, tm_ffn, tf)
        else:
            w_in = _pad_cols(mlstm_w_in[j], MLSTM_IN_PAD).astype(BF16)
            q, k, v, o, gates = _mlstm_in_proj(h, gain_m, w_in, tm_proj)
            bias8 = jnp.concatenate([mlstm_b_i[j], mlstm_b_f[j]])[:, None]
            h = _mlstm_core(*[a.reshape(B, seq_pad, a.shape[-1]) for a in (q, k, v, o, gates, h)], bias8,
                            mlstm_out_norm[j][None, :], mlstm_w_out[j].astype(BF16)).reshape(T, D)
            wr = _pad_cols(moe_router[j], LANES)
            h = _moe(h, gain_f, wr, *moe_w, j, tm_moe, tf)
    return _drop_meta(h.reshape(B, seq_pad, D), S)
```

```python
import functools

import jax
import jax.numpy as jnp
import numpy as np
from jax import lax
from jax.experimental import pallas as pl
from jax.experimental.pallas import tpu as pltpu

F32 = jnp.float32
BF16 = jnp.bfloat16

D_MODEL = 1024
N_META = 16
RMS_EPS = 1e-6
ROPE_THETA = 10000.0
ATT_HEADS = 16
ATT_KV_HEADS = 4
ATT_HEAD_DIM = 64
ATT_GROUP = 4
IDX_HEADS = 8
IDX_DIM = 128
TOPK_MAX = 256
M_HEADS = 4
M_QK_DIM = 128
M_V_DIM = 256
GATE_CAP = 15.0
D_FF = 3584
N_EXPERTS = 8

LANES = 128
SUBLANES = 8
Q_BLOCK = 128
KEY_CHUNK = 256
ATT_STEPS_PER_TRIP = 4
CHUNKS_PER_TRIP = 4
M_CHUNK = 128
M_BATCH_ROWS = 1
MOE_SLOT_BLOCK = 208
MOE_TOKEN_BLOCK = 256
VMEM_LIMIT = 52 * 1024 * 1024
INT_MIN = -(2 ** 31)
INT_MAX = 2 ** 31 - 1
LOG2E = 1.4426950408889634

DSA_IN_PAD = 2816
MLSTM_IN_PAD = 3200


def _cparams(sem):
    return pltpu.CompilerParams(dimension_semantics=sem, vmem_limit_bytes=VMEM_LIMIT)


def _unrolled_loop(n, fn, unroll):
    def body(t, carry):
        for u in range(unroll):
            fn(unroll * t + u)
        return carry

    lax.fori_loop(0, n // unroll, body, 0)
    rest = n % unroll
    done = n - rest
    run = unroll // 2
    while run >= 1:
        @pl.when((rest & run) != 0)
        def _(run=run, done=done):
            for u in range(run):
                fn(done + u)

        done = done + (rest & run)
        run //= 2


def _rms_rows(x, gain):
    ms = jnp.mean(x * x, axis=-1, keepdims=True)
    return x * lax.rsqrt(ms + RMS_EPS) * gain


def _dsa_in_kernel(h_ref, g_ref, w_ref, gq_ref, gk_ref, c64_ref, sa64_ref, sb64_ref, c128_ref, s128_ref,
                   smat_ref, q_ref, k_ref, v_ref, iq_ref, ik_ref, iw_ref):
    xn = _rms_rows(h_ref[...], g_ref[...]).astype(BF16)
    z = jnp.dot(xn, w_ref[...], preferred_element_type=F32)
    c64, sa64, sb64 = c64_ref[...], sa64_ref[...], sb64_ref[...]
    c128, s128 = c128_ref[...], s128_ref[...]
    smat = smat_ref[...]

    def head_norm_rope(zc, gain):
        z2 = zc * zc
        hi = z2.astype(BF16)
        lo = (z2 - hi.astype(F32)).astype(BF16)
        ms = jnp.dot(hi, smat, preferred_element_type=F32) + jnp.dot(lo, smat, preferred_element_type=F32)
        y = zc * lax.rsqrt(ms + RMS_EPS) * gain
        return y * c64 + pltpu.roll(y, 96, 1) * sa64 + pltpu.roll(y, 32, 1) * sb64

    nsub = z.shape[0] // Q_BLOCK
    lane = lax.broadcasted_iota(jnp.int32, (z.shape[0], LANES), 1)
    for c in range(8):
        r = head_norm_rope(z[:, c * LANES:(c + 1) * LANES], gq_ref[...]) * (ATT_HEAD_DIM ** -0.5 * LOG2E)
        swapped = pltpu.roll(r, 64, 1)
        for half in range(2):
            h = 2 * c + half
            g, rr = h // ATT_GROUP, h % ATT_GROUP
            keep = (lane >= 64) if g % 2 else (lane < 64)
            val = jnp.where(keep, r if half == g % 2 else swapped, 0.0).astype(BF16)
            for t in range(nsub):
                q_ref[t, g, rr * Q_BLOCK:(rr + 1) * Q_BLOCK, :] = val[t * Q_BLOCK:(t + 1) * Q_BLOCK]
    for c in range(2):
        r = head_norm_rope(z[:, 1024 + c * LANES:1024 + (c + 1) * LANES], gk_ref[...])
        k_ref[:, c * LANES:(c + 1) * LANES] = r.astype(BF16)
    ones = jnp.ones((z.shape[0], LANES), BF16)
    for c in range(2):
        v_ref[:, 2 * c * LANES:(2 * c + 1) * LANES] = z[:, 1280 + c * LANES:1280 + (c + 1) * LANES].astype(BF16)
        v_ref[:, (2 * c + 1) * LANES:(2 * c + 2) * LANES] = ones
    for c in range(9):
        zc = z[:, 1536 + c * LANES:1536 + (c + 1) * LANES]
        r = zc * c128 + pltpu.roll(zc, 64, 1) * s128
        if c < 8:
            for t in range(nsub):
                iq_ref[t, :, c * Q_BLOCK:(c + 1) * Q_BLOCK] = r[t * Q_BLOCK:(t + 1) * Q_BLOCK].T.astype(BF16)
        else:
            ik_ref[...] = r.astype(BF16)
    iw = z[:, 2688:2816] * (IDX_HEADS ** -0.5 * IDX_DIM ** -0.5)
    for t in range(nsub):
        iw_ref[t] = iw[t * Q_BLOCK:(t + 1) * Q_BLOCK].T


def _dsa_in_proj(h, gain, w, gq, gk, tabs, seq_pad, tm):
    T = h.shape[0]
    nt = seq_pad // tm
    row = lambda i: (i, 0)
    fixed = lambda i: (0, 0)
    pos = lambda i: (i % nt, 0)
    tab_spec = pl.BlockSpec((tm, LANES), pos)
    smat = jnp.asarray(np.kron(np.eye(2), np.full((64, 64), 1.0 / 64)), BF16)
    nsub = tm // Q_BLOCK
    rows = ATT_GROUP * Q_BLOCK
    outs = [((ATT_KV_HEADS, rows, LANES), BF16), (256, BF16), (512, BF16),
            ((IDX_DIM, IDX_HEADS * Q_BLOCK), BF16), (128, BF16), ((LANES, Q_BLOCK), F32)]

    def spec(o):
        if isinstance(o[0], int):
            return pl.BlockSpec((tm, o[0]), row)
        return pl.BlockSpec((nsub,) + o[0], lambda i: (i,) + (0,) * len(o[0]))

    def shape(o):
        if isinstance(o[0], int):
            return jax.ShapeDtypeStruct((T, o[0]), o[1])
        return jax.ShapeDtypeStruct((T // Q_BLOCK,) + o[0], o[1])

    return pl.pallas_call(
        _dsa_in_kernel,
        grid=(T // tm,),
        in_specs=[pl.BlockSpec((tm, D_MODEL), row), pl.BlockSpec((1, D_MODEL), fixed),
                  pl.BlockSpec((D_MODEL, DSA_IN_PAD), fixed), pl.BlockSpec((1, LANES), fixed),
                  pl.BlockSpec((1, LANES), fixed), tab_spec, tab_spec, tab_spec, tab_spec, tab_spec,
                  pl.BlockSpec((LANES, LANES), fixed)],
        out_specs=[spec(o) for o in outs],
        out_shape=[shape(o) for o in outs],
        compiler_params=_cparams(("arbitrary",)),
        name="dsa_in_proj",
    )(h, gain, w, gq, gk, *tabs, smat)


def _dsa_core_kernel(q_ref, k_ref, v_ref, iq_ref, ik_ref, iw_ref, o_ref,
                     keys_sc, bias_sc, cut_sc, m_sc, l_sc, acc_sc, *, top_k, seq_pad):
    i = pl.program_id(1)
    QB, KC = Q_BLOCK, KEY_CHUNK
    nch = (i * QB + QB + KC - 1) // KC
    kf = float(top_k)
    lane1 = lax.broadcasted_iota(jnp.int32, (QB, LANES), 1)

    qpos = i * QB + lax.broadcasted_iota(jnp.int32, (KC, QB), 1)
    krow = lax.broadcasted_iota(jnp.int32, (KC, QB), 0)

    def chunk_start(j):
        return pl.multiple_of(jnp.minimum(j * KC, seq_pad - KC), LANES)

    def key_valid(j, start):
        kidx = start + krow
        return (kidx <= qpos) & (kidx >= j * KC)

    def score_chunk(j):
        start = chunk_start(j)
        s = jnp.dot(ik_ref[0, pl.ds(start, KC), :], iq_ref[0], preferred_element_type=F32)
        acc = jnp.zeros((KC, QB), F32)
        for h in range(IDX_HEADS):
            acc = acc + jnp.maximum(s[:, h * QB:(h + 1) * QB], 0.0) * iw_ref[0, h:h + 1, :]
        bits = pltpu.bitcast(acc, jnp.int32)
        key = jnp.where(bits >= 0, bits, bits ^ jnp.int32(INT_MAX))
        keys_sc[j] = jnp.where(key_valid(j, start), key, jnp.int32(INT_MIN))

    _unrolled_loop(nch, score_chunk, CHUNKS_PER_TRIP)

    tiles = (KC // SUBLANES, SUBLANES, QB)
    krow3 = (lax.broadcasted_iota(jnp.int32, tiles, 0) * SUBLANES + lax.broadcasted_iota(jnp.int32, tiles, 1))

    keys_sc[nch] = jnp.full((KC, QB), INT_MIN, jnp.int32)

    def count(pred):
        def body(j2, acc):
            for j in (2 * j2, 2 * j2 + 1):
                kk = keys_sc[j].reshape(tiles)
                hit = jnp.where(pred(kk, chunk_start(j)), jnp.float32(1.0), jnp.float32(0.0))
                while hit.shape[0] > 1:
                    half = hit.shape[0] // 2
                    hit = hit[:half] + hit[half:]
                acc = acc + hit[0]
            return acc
        acc = lax.fori_loop(0, (nch + 1) // 2, body, jnp.zeros((SUBLANES, QB), F32))
        for shift in (4, 2, 1):
            acc = acc + pltpu.roll(acc, shift, 0)
        return acc

    zero = jnp.zeros((SUBLANES, QB), jnp.int32)
    v0 = jnp.where(count(lambda kk, base: kk >= zero) >= kf, zero, jnp.int32(INT_MIN))

    def bit_body(b, v):
        cand = v | lax.shift_left(jnp.int32(1), (30 - b).astype(jnp.int32))
        return jnp.where(count(lambda kk, base: kk >= cand) >= kf, cand, v)

    thr = lax.fori_loop(jnp.int32(0), jnp.int32(31), bit_body, v0)
    cut_sc[...] = jnp.full((SUBLANES, QB), INT_MAX, jnp.int32)
    n_ge = count(lambda kk, base: kk >= thr)

    @pl.when(jnp.max(n_ge) > kf)
    def _():
        need = kf - count(lambda kk, base: kk > thr)

        def cut_body(b, c):
            cand = c | lax.shift_left(jnp.int32(1), (12 - b).astype(jnp.int32))
            n = count(lambda kk, base: jnp.where(kk == thr, base + krow3, jnp.int32(INT_MAX)) < cand)
            return jnp.where(n < need, cand, c)

        cut_sc[...] = lax.fori_loop(jnp.int32(0), jnp.int32(13), cut_body, zero)

    thr1 = thr[0:1, :]
    cut1 = cut_sc[0:1, :]

    def bias_chunk(j):
        kk = keys_sc[j]
        start = chunk_start(j)
        sel = (kk > thr1) | ((kk == thr1) & (start + krow <= cut1))
        bt = jnp.where(sel & key_valid(j, start), 0.0, -1e30)
        bias_sc[j] = jnp.concatenate([bt[c * QB:(c + 1) * QB].T for c in range(KC // QB)], axis=1)

    _unrolled_loop(nch, bias_chunk, CHUNKS_PER_TRIP)

    m_sc[...] = jnp.full(m_sc.shape, -1e9, F32)
    l_sc[...] = jnp.zeros(l_sc.shape, F32)
    acc_sc[...] = jnp.zeros(acc_sc.shape, F32)

    def att_step(start, width, bias, groups):
        bias = jnp.concatenate([bias] * ATT_GROUP, axis=0)
        for g in groups:
            cg = g // 2
            kj = k_ref[0, pl.ds(start, width), cg * LANES:(cg + 1) * LANES]
            vj = v_ref[0, pl.ds(start, width), cg * 2 * LANES:(cg + 1) * 2 * LANES]
            s = lax.dot_general(q_ref[0, g], kj, (((1,), (1,)), ((), ())), preferred_element_type=F32) + bias
            m_old = m_sc[g]
            m_new = jnp.maximum(m_old, jnp.max(s, axis=1, keepdims=True))
            alpha = jnp.exp2(m_old - m_new)
            p = jnp.exp2(s - jnp.concatenate([m_new] * (width // LANES), axis=1)).astype(BF16)
            pv = jnp.dot(p, vj, preferred_element_type=F32)
            l_sc[g] = alpha * l_sc[g] + pv[:, LANES:]
            acc_sc[g] = alpha * acc_sc[g] + pv[:, :LANES]
            m_sc[g] = m_new

    npair = jnp.minimum(nch // 2, seq_pad // (2 * KC))

    def single_body(j, carry):
        att_step(chunk_start(j), KC, bias_sc[j], range(ATT_KV_HEADS))
        return carry

    def pair_step(j):
        bias = jnp.concatenate([bias_sc[2 * j], bias_sc[2 * j + 1]], axis=1)
        att_step(pl.multiple_of(j * 2 * KC, 2 * KC), 2 * KC, bias, range(ATT_KV_HEADS))

    _unrolled_loop(npair, pair_step, ATT_STEPS_PER_TRIP)
    lax.fori_loop(2 * npair, nch, single_body, 0)

    for g in range(ATT_KV_HEADS):
        og = acc_sc[g] / l_sc[g]
        for pair in range(2):
            even = og[(2 * pair) * QB:(2 * pair + 1) * QB]
            odd = og[(2 * pair + 1) * QB:(2 * pair + 2) * QB]
            if g % 2:
                even = pltpu.roll(even, 64, 1)
            else:
                odd = pltpu.roll(odd, 64, 1)
            c = 2 * g + pair
            o_ref[0, :, c * LANES:(c + 1) * LANES] = jnp.where(lane1 < 64, even, odd).astype(BF16)


def _dsa_core(q, k, v, iq, ik, iw, top_k):
    B, seq_pad = k.shape[0], k.shape[1]
    nqb = seq_pad // Q_BLOCK
    nkc = -(-seq_pad // KEY_CHUNK)
    rows = ATT_GROUP * Q_BLOCK
    qblock = lambda a: pl.BlockSpec((1,) + a.shape[1:], lambda b, i: (b * nqb + i,) + (0,) * (a.ndim - 1))
    whole = lambda n: pl.BlockSpec((1, seq_pad, n), lambda b, i: (b, 0, 0))
    return pl.pallas_call(
        functools.partial(_dsa_core_kernel, top_k=top_k, seq_pad=seq_pad),
        grid=(B, nqb),
        in_specs=[qblock(q), whole(256), whole(512), qblock(iq), whole(IDX_DIM), qblock(iw)],
        out_specs=pl.BlockSpec((1, Q_BLOCK, D_MODEL), lambda b, i: (b, i, 0)),
        out_shape=jax.ShapeDtypeStruct((B, seq_pad, D_MODEL), BF16),
        scratch_shapes=[
            pltpu.VMEM((nkc + 1, KEY_CHUNK, Q_BLOCK), jnp.int32),
            pltpu.VMEM((nkc, Q_BLOCK, KEY_CHUNK), F32),
            pltpu.VMEM((SUBLANES, Q_BLOCK), jnp.int32),
            pltpu.VMEM((ATT_KV_HEADS, rows, LANES), F32),
            pltpu.VMEM((ATT_KV_HEADS, rows, LANES), F32),
            pltpu.VMEM((ATT_KV_HEADS, rows, LANES), F32),
        ],
        compiler_params=_cparams(("arbitrary", "arbitrary")),
        name="dsa_core",
    )(q, k, v, iq, ik, iw)


def _mlstm_in_kernel(h_ref, g_ref, w_ref, q_ref, k_ref, v_ref, o_ref, gate_ref):
    xn = _rms_rows(h_ref[...], g_ref[...]).astype(BF16)
    z = jnp.dot(xn, w_ref[...], preferred_element_type=F32)
    q_ref[...] = z[:, 0:512].astype(BF16)
    k_ref[...] = (z[:, 512:1024] * (M_QK_DIM ** -0.5)).astype(BF16)
    v_ref[...] = z[:, 1024:2048].astype(BF16)
    o_ref[...] = z[:, 2048:3072]
    gate_ref[...] = z[:, 3072:3200]


def _mlstm_in_proj(h, gain, w, tm):
    T = h.shape[0]
    row = lambda i: (i, 0)
    fixed = lambda i: (0, 0)
    outs = [(512, BF16), (512, BF16), (1024, BF16), (1024, F32), (128, F32)]
    return pl.pallas_call(
        _mlstm_in_kernel,
        grid=(T // tm,),
        in_specs=[pl.BlockSpec((tm, D_MODEL), row), pl.BlockSpec((1, D_MODEL), fixed),
                  pl.BlockSpec((D_MODEL, MLSTM_IN_PAD), fixed)],
        out_specs=[pl.BlockSpec((tm, n), row) for n, _ in outs],
        out_shape=[jax.ShapeDtypeStruct((T, n), dt) for n, dt in outs],
        compiler_params=_cparams(("arbitrary",)),
        name="mlstm_in_proj",
    )(h, gain, w)


def _mlstm_kernel(q_ref, k_ref, v_ref, o_ref, gate_ref, h_ref, bias_ref, gout_ref, wo_ref, hout_ref,
                  ct_sc, n_sc, m_sc, y_sc):
    C = M_CHUNK

    @pl.when(pl.program_id(1) == 0)
    def _():
        ct_sc[...] = jnp.zeros(ct_sc.shape, F32)
        n_sc[...] = jnp.zeros(n_sc.shape, F32)
        m_sc[...] = jnp.zeros(m_sc.shape, F32)

    lane8 = lax.broadcasted_iota(jnp.int32, (8, C), 1)
    t_idx = lax.broadcasted_iota(jnp.int32, (C, C), 0)
    s_idx = lax.broadcasted_iota(jnp.int32, (C, C), 1)
    for bi in range(q_ref.shape[0]):
        pre = gate_ref[bi].T[0:8, :] + bias_ref[...]
        capped = GATE_CAP * jnp.tanh(pre / GATE_CAP)
        log_f = -(jnp.maximum(-capped, 0.0) + jnp.log1p(jnp.exp(-jnp.abs(capped))))
        b = log_f
        sh = 1
        while sh < C:
            b = b + jnp.where(lane8 >= sh, pltpu.roll(b, sh, 1), 0.0)
            sh *= 2
        stacked = jnp.concatenate([b[4:8], capped[0:4]], axis=0)
        cols = jnp.concatenate([stacked, jnp.zeros((C - 8, C), F32)], axis=0).T

        for h in range(M_HEADS):
            st = bi * M_HEADS + h
            qh = q_ref[bi, :, h * M_QK_DIM:(h + 1) * M_QK_DIM]
            kh = k_ref[bi, :, h * M_QK_DIM:(h + 1) * M_QK_DIM]
            vh = v_ref[bi, :, h * M_V_DIM:(h + 1) * M_V_DIM]
            b_row, li_row = stacked[h:h + 1, :], stacked[4 + h:5 + h, :]
            b_col, li_col = cols[:, h:h + 1], cols[:, 4 + h:5 + h]
            m_st = m_sc[st:st + 1, 0:1]
            dmat = jnp.where(s_idx <= t_idx, b_col - b_row + li_row, -jnp.inf)
            inter = b_col + m_st
            m_t = jnp.maximum(inter, jnp.max(dmat, axis=1, keepdims=True))
            w_inter = jnp.exp(inter - m_t)
            qk = lax.dot_general(qh, kh, (((1,), (1,)), ((), ())), preferred_element_type=F32)
            s = qk * jnp.exp(dmat - m_t)
            ct = ct_sc[st]
            num = (w_inter * jnp.dot(qh, ct.astype(BF16), preferred_element_type=F32)
                   + jnp.dot(s.astype(BF16), vh, preferred_element_type=F32))
            qn = jnp.sum(qh.astype(F32) * n_sc[st:st + 1, :], axis=1, keepdims=True)
            den = w_inter * qn + jnp.sum(s, axis=1, keepdims=True)
            hout = num / jnp.maximum(jnp.abs(den), jnp.exp(-m_t))

            b_last = b_row[:, C - 1:C]
            m_new = jnp.maximum(b_last + m_st, jnp.max(b_last - b_row + li_row, axis=1, keepdims=True))
            decay = jnp.exp(b_last + m_st - m_new)
            wk = jnp.exp(b_last - b_col + li_col - m_new)
            kw = kh.astype(F32) * wk
            ct_sc[st] = decay * ct + jnp.dot(kw.T.astype(BF16), vh, preferred_element_type=F32)
            n_sc[st:st + 1, :] = decay * n_sc[st:st + 1, :] + jnp.sum(kw, axis=0, keepdims=True)
            m_sc[st:st + 1, :] = jnp.broadcast_to(m_new, (1, LANES))

            cs = slice(h * M_V_DIM, (h + 1) * M_V_DIM)
            hn = _rms_rows(hout, gout_ref[:, cs])
            y_sc[:, cs] = (hn * jax.nn.sigmoid(o_ref[bi, :, cs])).astype(BF16)

        hout_ref[bi] = h_ref[bi] + jnp.dot(y_sc[...], wo_ref[...], preferred_element_type=F32)


def _mlstm_core(q, k, v, o, gates, h, bias8, gout, wo):
    B, seq_pad = q.shape[0], q.shape[1]
    nb = M_BATCH_ROWS
    C = M_CHUNK
    blk = lambda n: pl.BlockSpec((nb, C, n), lambda b, c: (b, c, 0))
    fixed = lambda b, c: (0, 0)
    return pl.pallas_call(
        _mlstm_kernel,
        grid=(B // nb, seq_pad // C),
        in_specs=[blk(512), blk(512), blk(1024), blk(1024), blk(LANES), blk(D_MODEL),
                  pl.BlockSpec((8, 1), fixed), pl.BlockSpec((1, D_MODEL), fixed),
                  pl.BlockSpec((D_MODEL, D_MODEL), fixed)],
        out_specs=blk(D_MODEL),
        out_shape=jax.ShapeDtypeStruct((B, seq_pad, D_MODEL), F32),
        scratch_shapes=[pltpu.VMEM((nb * M_HEADS, M_QK_DIM, M_V_DIM), F32),
                        pltpu.VMEM((nb * M_HEADS, M_QK_DIM), F32),
                        pltpu.VMEM((nb * M_HEADS, LANES), F32),
                        pltpu.VMEM((C, D_MODEL), BF16)],
        compiler_params=_cparams(("arbitrary", "arbitrary")),
        name="mlstm_core",
    )(q, k, v, o, gates, h, bias8, gout, wo)


def _ffn_kernel(h_ref, y_ref, wo_ref, g_ref, wg_ref, wu_ref, wd_ref, o_ref, xn_sc, acc_sc):
    f = pl.program_id(1)

    @pl.when(f == 0)
    def _():
        x = h_ref[...] + jnp.dot(y_ref[...], wo_ref[...], preferred_element_type=F32)
        xn_sc[...] = _rms_rows(x, g_ref[...]).astype(BF16)
        acc_sc[...] = x

    xn = xn_sc[...]
    gate = jnp.dot(xn, wg_ref[0], preferred_element_type=F32)
    up = jnp.dot(xn, wu_ref[0], preferred_element_type=F32)
    act = (gate * jax.nn.sigmoid(gate) * up).astype(BF16)
    acc_sc[...] += jnp.dot(act, wd_ref[0], preferred_element_type=F32)

    @pl.when(f == pl.num_programs(1) - 1)
    def _():
        o_ref[...] = acc_sc[...]


def _ffn(h, y, wo, gain, wg, wu, wd, layer, tm, tf):
    T = h.shape[0]
    return pl.pallas_call(
        _ffn_kernel,
        grid=(T // tm, D_FF // tf),
        in_specs=[pl.BlockSpec((tm, D_MODEL), lambda i, f: (i, 0)),
                  pl.BlockSpec((tm, D_MODEL), lambda i, f: (i, 0)),
                  pl.BlockSpec((D_MODEL, D_MODEL), lambda i, f: (0, 0)),
                  pl.BlockSpec((1, D_MODEL), lambda i, f: (0, 0)),
                  pl.BlockSpec((1, D_MODEL, tf), lambda i, f: (layer, 0, f)),
                  pl.BlockSpec((1, D_MODEL, tf), lambda i, f: (layer, 0, f)),
                  pl.BlockSpec((1, tf, D_MODEL), lambda i, f: (layer, f, 0))],
        out_specs=pl.BlockSpec((tm, D_MODEL), lambda i, f: (i, 0)),
        out_shape=jax.ShapeDtypeStruct((T, D_MODEL), F32),
        scratch_shapes=[pltpu.VMEM((tm, D_MODEL), BF16), pltpu.VMEM((tm, D_MODEL), F32)],
        compiler_params=_cparams(("arbitrary", "arbitrary")),
        name="ffn_dense",
    )(h, y, wo, gain, wg, wu, wd)


def _moe_kernel(h_ref, g_ref, wr_ref, wg_ref, wu_ref, wd_ref, o_ref,
                xn_sc, xs_sc, ys_sc, posc_sc, posr_sc, comb_sc, nblk_sc):
    e = pl.program_id(1)
    f = pl.program_id(2)
    tm = h_ref.shape[0]
    SB = _moe_slot_block(tm)
    TB = _moe_token_block(tm)
    lane = lax.broadcasted_iota(jnp.int32, (tm, LANES), 1)

    @pl.when((e == 0) & (f == 0))
    def _():
        x = h_ref[...]
        xn = _rms_rows(x, g_ref[...])
        xh = xn.astype(BF16)
        xn_sc[...] = xh
        o_ref[...] = x
        xl = (xn - xh.astype(F32)).astype(BF16)
        wr = wr_ref[...]
        wh = wr.astype(BF16)
        wl = (wr - wh.astype(F32)).astype(BF16)
        logits = (jnp.dot(xh, wh, preferred_element_type=F32) + jnp.dot(xh, wl, preferred_element_type=F32)
                  + jnp.dot(xl, wh, preferred_element_type=F32))
        logits = jnp.where(lane < N_EXPERTS, logits, -jnp.inf)
        m1 = jnp.max(logits, axis=1, keepdims=True)
        i1 = jnp.min(jnp.where(logits == m1, lane, LANES), axis=1, keepdims=True)
        rest = jnp.where(lane == i1, -jnp.inf, logits)
        m2 = jnp.max(rest, axis=1, keepdims=True)
        i2 = jnp.min(jnp.where(rest == m2, lane, LANES), axis=1, keepdims=True)
        e2 = jnp.exp(m2 - m1)
        g1 = 1.0 / (1.0 + e2)
        comb_sc[...] = jnp.where(lane == i1, g1, jnp.where(lane == i2, e2 * g1, 0.0))
        member = jnp.where(lane == i1, 1.0, jnp.where(lane == i2, 1.0, 0.0))
        tri = jnp.where(lax.broadcasted_iota(jnp.int32, (TB, TB), 0) >= lax.broadcasted_iota(jnp.int32, (TB, TB), 1),
                        1.0, 0.0).astype(BF16)
        carry = jnp.zeros((1, LANES), F32)
        for b in range(tm // TB):
            mb = member[b * TB:(b + 1) * TB]
            incl = jnp.dot(tri, mb.astype(BF16), preferred_element_type=F32)
            posc_sc[b * TB:(b + 1) * TB, :] = jnp.where(mb > 0.0, incl - 1.0 + carry, -1.0)
            carry = carry + incl[TB - 1:TB, :]
        for b in range(tm // LANES):
            posr_sc[:, b * LANES:(b + 1) * LANES] = posc_sc[b * LANES:(b + 1) * LANES, :].T
        lane_row = lax.broadcasted_iota(jnp.int32, (1, LANES), 1)
        for x in range(N_EXPERTS):
            count = jnp.sum(jnp.where(lane_row == x, carry, 0.0))
            nblk_sc[x] = ((count + (SB - 1.0)) * (1.0 / SB)).astype(jnp.int32)

    nblk = nblk_sc[e]

    def for_slot_rows(fn):
        def pair_body(r, carry):
            fn(pl.multiple_of(r * 2 * SB, 16), 2 * SB)
            return carry

        lax.fori_loop(0, nblk // 2, pair_body, 0)

        @pl.when(nblk % 2 == 1)
        def _():
            fn(pl.multiple_of((nblk - 1) * SB, 16), SB)

    @pl.when(f == 0)
    def _():
        slots_of_tokens = posr_sc[pl.ds(e, 1), :]

        def compact_rows(row0, nrows):
            slot = (row0 + lax.broadcasted_iota(jnp.int32, (nrows, tm), 0)).astype(F32)
            onehot = jnp.where(slots_of_tokens == slot, 1.0, 0.0).astype(BF16)
            xs_sc[pl.ds(row0, nrows), :] = jnp.dot(onehot, xn_sc[...], preferred_element_type=F32).astype(BF16)
            ys_sc[pl.ds(row0, nrows), :] = jnp.zeros((nrows, D_MODEL), F32)

        for_slot_rows(compact_rows)

    def ffn_rows(row0, nrows):
        xs = xs_sc[pl.ds(row0, nrows), :]
        gate = jnp.dot(xs, wg_ref[0], preferred_element_type=F32)
        up = jnp.dot(xs, wu_ref[0], preferred_element_type=F32)
        act = (gate * jax.nn.sigmoid(gate) * up).astype(BF16)
        ys_sc[pl.ds(row0, nrows), :] += jnp.dot(act, wd_ref[0], preferred_element_type=F32)

    for_slot_rows(ffn_rows)

    @pl.when(f == pl.num_programs(2) - 1)
    def _():
        on_e = lane == e
        gate_e = jnp.sum(jnp.where(on_e, comb_sc[...], 0.0), axis=1, keepdims=True)
        slot_e = jnp.sum(jnp.where(on_e, posc_sc[...], 0.0), axis=1, keepdims=True)

        def scatter_rows(row0, nrows):
            ys = ys_sc[pl.ds(row0, nrows), :].astype(BF16)
            slot = (row0 + lax.broadcasted_iota(jnp.int32, (TB, nrows), 1)).astype(F32)
            for tb in range(tm // TB):
                rows = slice(tb * TB, (tb + 1) * TB)
                onehot = jnp.where(slot_e[rows] == slot, 1.0, 0.0).astype(BF16)
                o_ref[rows, :] += gate_e[rows] * jnp.dot(onehot, ys, preferred_element_type=F32)

        for_slot_rows(scatter_rows)


def _moe_slot_block(tm):
    return MOE_SLOT_BLOCK if tm > MOE_SLOT_BLOCK else tm


def _moe_token_block(tm):
    return MOE_TOKEN_BLOCK if tm % MOE_TOKEN_BLOCK == 0 else tm


def _moe(h, gain, wr, wg, wu, wd, layer, tm, tf):
    T = h.shape[0]
    sb = _moe_slot_block(tm)
    cap = -(-tm // sb) * sb
    return pl.pallas_call(
        _moe_kernel,
        grid=(T // tm, N_EXPERTS, D_FF // tf),
        in_specs=[pl.BlockSpec((tm, D_MODEL), lambda i, e, f: (i, 0), pipeline_mode=pl.Buffered(1)),
                  pl.BlockSpec((1, D_MODEL), lambda i, e, f: (0, 0)),
                  pl.BlockSpec((D_MODEL, LANES), lambda i, e, f: (0, 0)),
                  pl.BlockSpec((1, D_MODEL, tf), lambda i, e, f: (layer * N_EXPERTS + e, 0, f)),
                  pl.BlockSpec((1, D_MODEL, tf), lambda i, e, f: (layer * N_EXPERTS + e, 0, f)),
                  pl.BlockSpec((1, tf, D_MODEL), lambda i, e, f: (layer * N_EXPERTS + e, f, 0))],
        out_specs=pl.BlockSpec((tm, D_MODEL), lambda i, e, f: (i, 0)),
        out_shape=jax.ShapeDtypeStruct((T, D_MODEL), F32),
        scratch_shapes=[pltpu.VMEM((tm, D_MODEL), BF16), pltpu.VMEM((cap, D_MODEL), BF16),
                        pltpu.VMEM((cap, D_MODEL), F32), pltpu.VMEM((tm, LANES), F32),
                        pltpu.VMEM((LANES, tm), F32), pltpu.VMEM((tm, LANES), F32),
                        pltpu.SMEM((N_EXPERTS,), jnp.int32)],
        compiler_params=_cparams(("arbitrary", "arbitrary", "arbitrary")),
        name="moe_sparse",
    )(h, gain, wr, wg, wu, wd)


def _copy_kernel(x_ref, o_ref):
    o_ref[...] = x_ref[...]


def _drop_meta(h3, seq):
    B, _, D = h3.shape
    rows = 512 if seq % 512 == 0 else seq
    return pl.pallas_call(
        _copy_kernel,
        grid=(B, seq // rows),
        in_specs=[pl.BlockSpec((pl.Element(1), pl.Element(rows), pl.Element(D)),
                               lambda b, i: (b, pl.multiple_of(N_META + i * rows, SUBLANES), 0))],
        out_specs=pl.BlockSpec((1, rows, D), lambda b, i: (b, i, 0)),
        out_shape=jax.ShapeDtypeStruct((B, seq, D), h3.dtype),
        compiler_params=_cparams(("arbitrary", "arbitrary")),
        name="drop_meta",
    )(h3)


def _rope_tables(seq_pad):
    pos = jnp.arange(seq_pad, dtype=F32)[:, None]
    lane = np.arange(LANES)

    def table(head_dim):
        half = head_dim // 2
        inv = ROPE_THETA ** (-jnp.arange(half, dtype=F32) / half)
        d = lane % head_dim
        ang = pos * inv[d % half][None, :]
        return jnp.cos(ang), jnp.sin(ang), jnp.asarray(d < half)[None, :]

    c64, s64, lo64 = table(ATT_HEAD_DIM)
    c128, s128, lo128 = table(IDX_DIM)
    return (c64, jnp.where(lo64, -s64, 0.0), jnp.where(lo64, 0.0, s64), c128, jnp.where(lo128, -s128, s128))


def _pad_cols(w, n):
    return jnp.pad(w, ((0, 0), (0, n - w.shape[1])))


def kernel(x, meta, norm_mixer, norm_ffn, dsa_w_in, dsa_q_norm, dsa_k_norm, dsa_w_out, mlstm_w_in, mlstm_b_i,
           mlstm_b_f, mlstm_out_norm, mlstm_w_out, ffn_w_gate, ffn_w_up, ffn_w_down, moe_router, moe_w_gate,
           moe_w_up, moe_w_down):
    B, S, D = x.shape
    L = S + N_META
    top_k = min(TOPK_MAX, S // 4)
    seq_pad = max(-(-L // Q_BLOCK) * Q_BLOCK, KEY_CHUNK)
    T = B * seq_pad
    tm_proj = 384 if seq_pad % 384 == 0 else Q_BLOCK
    tm_ffn = 768 if T % 768 == 0 else Q_BLOCK
    tm_moe = 1536 if T % 1536 == 0 else Q_BLOCK
    tf = 512

    h = jnp.concatenate([jnp.broadcast_to(meta[None].astype(x.dtype), (B, N_META, D)), x,
                         jnp.zeros((B, seq_pad - L, D), x.dtype)], axis=1).reshape(T, D)
    tabs = _rope_tables(seq_pad)
    depth = norm_mixer.shape[0]
    ffn_w = [w.astype(BF16) for w in (ffn_w_gate, ffn_w_up, ffn_w_down)]
    moe_w = [w.astype(BF16).reshape((-1,) + w.shape[2:]) for w in (moe_w_gate, moe_w_up, moe_w_down)]

    for i in range(depth):
        j = i // 2
        gain_m = norm_mixer[i][None, :]
        gain_f = norm_ffn[i][None, :]
        if i % 2 == 0:
            w_in = _pad_cols(dsa_w_in[j], DSA_IN_PAD).astype(BF16)
            gq = jnp.tile(dsa_q_norm[j], 2)[None, :]
            gk = jnp.tile(dsa_k_norm[j], 2)[None, :]
            q, k, v, iq, ik, iw = _dsa_in_proj(h, gain_m, w_in, gq, gk, tabs, seq_pad, tm_proj)
            k, v, ik = [a.reshape(B, seq_pad, a.shape[-1]) for a in (k, v, ik)]
            att = _dsa_core(q, k, v, iq, ik, iw, top_k)
            h = _ffn(h, att.reshape(T, D), dsa_w_out[j].astype(BF16), gain_f, *ffn_w, j, tm_ffn, tf)
        else:
            w_in = _pad_cols(mlstm_w_in[j], MLSTM_IN_PAD).astype(BF16)
            q, k, v, o, gates = _mlstm_in_proj(h, gain_m, w_in, tm_proj)
            bias8 = jnp.concatenate([mlstm_b_i[j], mlstm_b_f[j]])[:, None]
            h = _mlstm_core(*[a.reshape(B, seq_pad, a.shape[-1]) for a in (q, k, v, o, gates, h)], bias8,
                            mlstm_out_norm[j][None, :], mlstm_w_out[j].astype(BF16)).reshape(T, D)
            wr = _pad_cols(moe_router[j], LANES)
            h = _moe(h, gain_f, wr, *moe_w, j, tm_moe, tf)
    return _drop_meta(h.reshape(B, seq_pad, D), S)
```
